```python
import math
import jax, jax.numpy as jnp
from jax import lax
import numpy as np

D_MODEL = 1024
BATCH = 8
SEQ = 8192
DEPTH = 4

HEAD_DIM = 64
N_HEADS = 8
N_KV_HEADS = 2
GQA_GROUP = N_HEADS // N_KV_HEADS
ATTN_WIDTH = N_HEADS * HEAD_DIM
KV_WIDTH = N_KV_HEADS * HEAD_DIM
CONV_WIDTH = D_MODEL - ATTN_WIDTH
MIX_WIDTH = ATTN_WIDTH + CONV_WIDTH
IN_WIDTH = ATTN_WIDTH + 2 * KV_WIDTH + 2 * CONV_WIDTH
WINDOW = 128
BLOCK = 128
CONV_KERNEL = 31
NUM_BUCKETS = 32
MAX_DISTANCE = 128
D_FF = 4 * D_MODEL
EPS = 1e-6
NEG = -1e30

kernel_name = "hymba_conformer_swa_sink_hybrid"


def rms_norm(x, g):
    xf = x.astype(jnp.float32)
    y = xf * lax.rsqrt(jnp.mean(xf * xf, axis=-1, keepdims=True) + EPS)
    return (y * g.astype(jnp.float32)).astype(x.dtype)


def layer_norm(x, g, b):
    xf = x.astype(jnp.float32)
    mu = jnp.mean(xf, axis=-1, keepdims=True)
    var = jnp.mean(jnp.square(xf - mu), axis=-1, keepdims=True)
    y = (xf - mu) * lax.rsqrt(var + EPS)
    return (y * g.astype(jnp.float32) + b.astype(jnp.float32)).astype(x.dtype)


def t5_causal_bucket(n):
    n = np.asarray(n)
    max_exact = NUM_BUCKETS // 2
    large = max_exact + (np.log(np.maximum(n, 1) / max_exact)
                         / np.log(MAX_DISTANCE / max_exact)
                         * (NUM_BUCKETS - max_exact)).astype(np.int32)
    large = np.minimum(large, NUM_BUCKETS - 1)
    return np.where(n < max_exact, n, large).astype(np.int32)


def band_structure(seq_len):
    n_blocks = seq_len // BLOCK
    qi = np.arange(BLOCK)[:, None]
    kj = np.arange(2 * BLOCK)[None, :]
    dist = qi + BLOCK - kj
    in_window = (dist >= 0) & (dist < WINDOW)
    bucket = t5_causal_bucket(np.clip(dist, 0, None))
    k_abs = (np.arange(n_blocks)[:, None] - 1) * BLOCK + np.arange(2 * BLOCK)[None, :]
    valid = in_window[None, :, :] & (k_abs >= 0)[:, None, :]
    return bucket, valid


def sliding_window_gqa(q, k, v, sinks, rel_bias):
    B, T = q.shape[0], q.shape[1]
    nb = T // BLOCK
    bucket, valid = band_structure(T)
    bias = jnp.transpose(rel_bias[bucket].astype(jnp.float32), (2, 0, 1))
    bias = bias.reshape(N_KV_HEADS, GQA_GROUP, BLOCK, 2 * BLOCK)

    qb = q.reshape(B, nb, BLOCK, N_KV_HEADS, GQA_GROUP, HEAD_DIM)

    def band(t):
        tp = jnp.pad(t, ((0, 0), (BLOCK, 0), (0, 0), (0, 0)))
        prev = tp[:, :T].reshape(B, nb, BLOCK, N_KV_HEADS, HEAD_DIM)
        cur = t.reshape(B, nb, BLOCK, N_KV_HEADS, HEAD_DIM)
        return jnp.concatenate([prev, cur], axis=2)

    kb, vb = band(k), band(v)
    scale = 1.0 / math.sqrt(HEAD_DIM)
    s = jnp.einsum('bnqkgd,bnskd->bnkgqs', qb, kb).astype(jnp.float32) * scale
    s = s + bias[None, None]
    s = jnp.where(jnp.asarray(valid)[None, :, None, None, :, :], s, NEG)
    sink = sinks.astype(jnp.float32).reshape(N_KV_HEADS, GQA_GROUP)[None, None, :, :, None, None]
    m = jnp.maximum(jnp.max(s, axis=-1, keepdims=True), sink)
    p = jnp.exp(s - m)
    p = p / (jnp.sum(p, axis=-1, keepdims=True) + jnp.exp(sink - m))
    o = jnp.einsum('bnkgqs,bnskd->bnqkgd', p.astype(v.dtype), vb)
    return o.reshape(B, T, ATTN_WIDTH)


def conformer_conv(u, gate, conv_w, conv_b, ln_g, ln_b):
    h = u * jax.nn.sigmoid(gate)
    hp = jnp.pad(h, ((0, 0), (CONV_KERNEL - 1, 0), (0, 0)))
    y = lax.conv_general_dilated(
        hp, conv_w[:, None, :].astype(h.dtype), window_strides=(1,), padding='VALID',
        dimension_numbers=('NWC', 'WIO', 'NWC'), feature_group_count=CONV_WIDTH)
    y = y + conv_b
    y = layer_norm(y, ln_g, ln_b)
    return jax.nn.silu(y)


def _fwd_setup_inputs(seed: int = 0) -> dict:
    key = jax.random.key(seed)
    ks = jax.random.split(key, 20)
    f32 = jnp.float32
    nrm = lambda k, shape, s: (jax.random.normal(k, shape, f32) * s).astype(f32)
    return {
        "x": nrm(ks[0], (BATCH, SEQ, D_MODEL), 1.0),
        "rel_bias": nrm(ks[1], (NUM_BUCKETS, N_HEADS), 0.5),
        "norm_mix_g": 1.0 + nrm(ks[2], (DEPTH, D_MODEL), 0.02),
        "w_in": nrm(ks[3], (DEPTH, D_MODEL, IN_WIDTH), D_MODEL ** -0.5),
        "q_norm_g": 1.0 + nrm(ks[4], (DEPTH, HEAD_DIM), 0.02),
        "k_norm_g": 1.0 + nrm(ks[5], (DEPTH, HEAD_DIM), 0.02),
        "sinks": nrm(ks[6], (DEPTH, N_HEADS), 0.5),
        "conv_w": nrm(ks[7], (DEPTH, CONV_KERNEL, CONV_WIDTH), CONV_KERNEL ** -0.5),
        "conv_b": nrm(ks[8], (DEPTH, CONV_WIDTH), 0.02),
        "conv_ln_g": 1.0 + nrm(ks[9], (DEPTH, CONV_WIDTH), 0.02),
        "conv_ln_b": nrm(ks[10], (DEPTH, CONV_WIDTH), 0.02),
        "attn_out_g": 1.0 + nrm(ks[11], (DEPTH, ATTN_WIDTH), 0.02),
        "conv_out_g": 1.0 + nrm(ks[12], (DEPTH, CONV_WIDTH), 0.02),
        "w_out": nrm(ks[13], (DEPTH, MIX_WIDTH, D_MODEL), (MIX_WIDTH * 2 * DEPTH) ** -0.5),
        "norm_mlp_g": 1.0 + nrm(ks[14], (DEPTH, D_MODEL), 0.02),
        "w_mlp_up": nrm(ks[15], (DEPTH, D_MODEL, D_FF), D_MODEL ** -0.5),
        "w_mlp_down": nrm(ks[16], (DEPTH, D_FF, D_MODEL), (D_FF * 2 * DEPTH) ** -0.5),
    }


def _fwd_reference(x, rel_bias, norm_mix_g, w_in, q_norm_g, k_norm_g, sinks, conv_w, conv_b,
              conv_ln_g, conv_ln_b, attn_out_g, conv_out_g, w_out, norm_mlp_g,
              w_mlp_up, w_mlp_down):
    B, T, _ = x.shape
    splits = [ATTN_WIDTH, ATTN_WIDTH + KV_WIDTH, ATTN_WIDTH + 2 * KV_WIDTH,
              ATTN_WIDTH + 2 * KV_WIDTH + CONV_WIDTH]
    for l in range(DEPTH):
        h = rms_norm(x, norm_mix_g[l])
        z = h @ w_in[l]
        q, k, v, u, gate = jnp.split(z, splits, axis=-1)
        q = rms_norm(q.reshape(B, T, N_HEADS, HEAD_DIM), q_norm_g[l])
        k = rms_norm(k.reshape(B, T, N_KV_HEADS, HEAD_DIM), k_norm_g[l])
        v = v.reshape(B, T, N_KV_HEADS, HEAD_DIM)
        a = sliding_window_gqa(q, k, v, sinks[l], rel_bias)
        c = conformer_conv(u, gate, conv_w[l], conv_b[l], conv_ln_g[l], conv_ln_b[l])
        mix = jnp.concatenate([rms_norm(a, attn_out_g[l]), rms_norm(c, conv_out_g[l])], axis=-1)
        x = x + mix @ w_out[l]
        h = rms_norm(x, norm_mlp_g[l])
        x = x + jnp.square(jax.nn.relu(h @ w_mlp_up[l])) @ w_mlp_down[l]
    return x


import jax as _jax
import jax.numpy as _jnp

TWIN_FORMAT = 'train_step'
FWD_PARAMS = ['x', 'rel_bias', 'norm_mix_g', 'w_in', 'q_norm_g', 'k_norm_g', 'sinks', 'conv_w', 'conv_b', 'conv_ln_g', 'conv_ln_b', 'attn_out_g', 'conv_out_g', 'w_out', 'norm_mlp_g', 'w_mlp_up', 'w_mlp_down']
TWIN_WEIGHTS = ['rel_bias', 'norm_mix_g', 'w_in', 'q_norm_g', 'k_norm_g', 'sinks', 'conv_w', 'conv_b', 'conv_ln_g', 'conv_ln_b', 'attn_out_g', 'conv_out_g', 'w_out', 'norm_mlp_g', 'w_mlp_up', 'w_mlp_down']
TWIN_DIFF_INPUT = 'x'
TWIN_INPUTS = ['x', 'rel_bias', 'norm_mix_g', 'w_in', 'q_norm_g', 'k_norm_g', 'sinks', 'conv_w', 'conv_b', 'conv_ln_g', 'conv_ln_b', 'attn_out_g', 'conv_out_g', 'w_out', 'norm_mlp_g', 'w_mlp_up', 'w_mlp_down', 'loss_target', 'm_rel_bias', 'm_norm_mix_g', 'm_w_in', 'm_q_norm_g', 'm_k_norm_g', 'm_sinks', 'm_conv_w', 'm_conv_b', 'm_conv_ln_g', 'm_conv_ln_b', 'm_attn_out_g', 'm_conv_out_g', 'm_w_out', 'm_norm_mlp_g', 'm_w_mlp_up', 'm_w_mlp_down', 'v_rel_bias', 'v_norm_mix_g', 'v_w_in', 'v_q_norm_g', 'v_k_norm_g', 'v_sinks', 'v_conv_w', 'v_conv_b', 'v_conv_ln_g', 'v_conv_ln_b', 'v_attn_out_g', 'v_conv_out_g', 'v_w_out', 'v_norm_mlp_g', 'v_w_mlp_up', 'v_w_mlp_down']
TWIN_OUTPUTS = ['loss', 'grad_x', 'grad_rel_bias', 'grad_norm_mix_g', 'grad_w_in', 'grad_q_norm_g', 'grad_k_norm_g', 'grad_sinks', 'grad_conv_w', 'grad_conv_b', 'grad_conv_ln_g', 'grad_conv_ln_b', 'grad_attn_out_g', 'grad_conv_out_g', 'grad_w_out', 'grad_norm_mlp_g', 'grad_w_mlp_up', 'grad_w_mlp_down', 'delta_rel_bias', 'delta_norm_mix_g', 'delta_w_in', 'delta_q_norm_g', 'delta_k_norm_g', 'delta_sinks', 'delta_conv_w', 'delta_conv_b', 'delta_conv_ln_g', 'delta_conv_ln_b', 'delta_attn_out_g', 'delta_conv_out_g', 'delta_w_out', 'delta_norm_mlp_g', 'delta_w_mlp_up', 'delta_w_mlp_down', 'new_m_rel_bias', 'new_m_norm_mix_g', 'new_m_w_in', 'new_m_q_norm_g', 'new_m_k_norm_g', 'new_m_sinks', 'new_m_conv_w', 'new_m_conv_b', 'new_m_conv_ln_g', 'new_m_conv_ln_b', 'new_m_attn_out_g', 'new_m_conv_out_g', 'new_m_w_out', 'new_m_norm_mlp_g', 'new_m_w_mlp_up', 'new_m_w_mlp_down', 'new_v_rel_bias', 'new_v_norm_mix_g', 'new_v_w_in', 'new_v_q_norm_g', 'new_v_k_norm_g', 'new_v_sinks', 'new_v_conv_w', 'new_v_conv_b', 'new_v_conv_ln_g', 'new_v_conv_ln_b', 'new_v_attn_out_g', 'new_v_conv_out_g', 'new_v_w_out', 'new_v_norm_mlp_g', 'new_v_w_mlp_up', 'new_v_w_mlp_down']
TWIN_LEAF_KINDS = {'loss': 'loss', 'grad_x': 'grad_x', 'grad_rel_bias': 'grad_w', 'grad_norm_mix_g': 'grad_w', 'grad_w_in': 'grad_w', 'grad_q_norm_g': 'grad_w', 'grad_k_norm_g': 'grad_w', 'grad_sinks': 'grad_w', 'grad_conv_w': 'grad_w', 'grad_conv_b': 'grad_w', 'grad_conv_ln_g': 'grad_w', 'grad_conv_ln_b': 'grad_w', 'grad_attn_out_g': 'grad_w', 'grad_conv_out_g': 'grad_w', 'grad_w_out': 'grad_w', 'grad_norm_mlp_g': 'grad_w', 'grad_w_mlp_up': 'grad_w', 'grad_w_mlp_down': 'grad_w', 'delta_rel_bias': 'delta_w', 'delta_norm_mix_g': 'delta_w', 'delta_w_in': 'delta_w', 'delta_q_norm_g': 'delta_w', 'delta_k_norm_g': 'delta_w', 'delta_sinks': 'delta_w', 'delta_conv_w': 'delta_w', 'delta_conv_b': 'delta_w', 'delta_conv_ln_g': 'delta_w', 'delta_conv_ln_b': 'delta_w', 'delta_attn_out_g': 'delta_w', 'delta_conv_out_g': 'delta_w', 'delta_w_out': 'delta_w', 'delta_norm_mlp_g': 'delta_w', 'delta_w_mlp_up': 'delta_w', 'delta_w_mlp_down': 'delta_w', 'new_m_rel_bias': 'new_m', 'new_m_norm_mix_g': 'new_m', 'new_m_w_in': 'new_m', 'new_m_q_norm_g': 'new_m', 'new_m_k_norm_g': 'new_m', 'new_m_sinks': 'new_m', 'new_m_conv_w': 'new_m', 'new_m_conv_b': 'new_m', 'new_m_conv_ln_g': 'new_m', 'new_m_conv_ln_b': 'new_m', 'new_m_attn_out_g': 'new_m', 'new_m_conv_out_g': 'new_m', 'new_m_w_out': 'new_m', 'new_m_norm_mlp_g': 'new_m', 'new_m_w_mlp_up': 'new_m', 'new_m_w_mlp_down': 'new_m', 'new_v_rel_bias': 'new_v', 'new_v_norm_mix_g': 'new_v', 'new_v_w_in': 'new_v', 'new_v_q_norm_g': 'new_v', 'new_v_k_norm_g': 'new_v', 'new_v_sinks': 'new_v', 'new_v_conv_w': 'new_v', 'new_v_conv_b': 'new_v', 'new_v_conv_ln_g': 'new_v', 'new_v_conv_ln_b': 'new_v', 'new_v_attn_out_g': 'new_v', 'new_v_conv_out_g': 'new_v', 'new_v_w_out': 'new_v', 'new_v_norm_mlp_g': 'new_v', 'new_v_w_mlp_up': 'new_v', 'new_v_w_mlp_down': 'new_v'}


def _forward(args):
    return _fwd_reference(*[args[k] for k in FWD_PARAMS])


def _output_shape():
    def fwd():
        inp = _fwd_setup_inputs(0)
        return _fwd_reference(*[inp[k] for k in FWD_PARAMS])
    out = _jax.eval_shape(fwd)
    return out.shape, out.dtype

N_MICROBATCH = 1
ADAM_LR = 0.001
ADAM_B1 = 0.9
ADAM_B2 = 0.999
ADAM_EPS = 1e-08
ADAM_WD = 0.01
ADAM_STEP = 10
PER_EXAMPLE_BATCH_AXIS = {'x': 0, 'loss_target': 0}
SHARED_INPUTS = []
_WEIGHT_DTYPES = {'rel_bias': _jnp.float32, 'norm_mix_g': _jnp.float32, 'w_in': _jnp.float32, 'q_norm_g': _jnp.float32, 'k_norm_g': _jnp.float32, 'sinks': _jnp.float32, 'conv_w': _jnp.float32, 'conv_b': _jnp.float32, 'conv_ln_g': _jnp.float32, 'conv_ln_b': _jnp.float32, 'attn_out_g': _jnp.float32, 'conv_out_g': _jnp.float32, 'w_out': _jnp.float32, 'norm_mlp_g': _jnp.float32, 'w_mlp_up': _jnp.float32, 'w_mlp_down': _jnp.float32}
MOMENT_SCALE = {'rel_bias': 5.851488e-01, 'norm_mix_g': 7.866703e+00, 'w_in': 5.879638e+00, 'q_norm_g': 1.275319e+00, 'k_norm_g': 1.272267e+00, 'sinks': 2.985620e-01, 'conv_w': 3.562067e+00, 'conv_b': 3.028355e+01, 'conv_ln_g': 1.126312e+01, 'conv_ln_b': 1.600742e+01, 'attn_out_g': 1.402665e+01, 'conv_out_g': 1.149710e+01, 'w_out': 2.408499e+01, 'norm_mlp_g': 2.552258e+01, 'w_mlp_up': 2.218825e+00, 'w_mlp_down': 2.789885e+01}


def _to_microbatches(a, axis):
    t = _jnp.moveaxis(a, axis, 0)
    t = t.reshape((N_MICROBATCH, t.shape[0] // N_MICROBATCH) + t.shape[1:])
    return _jnp.moveaxis(t, 1, axis + 1)


def setup_inputs(seed: int = 0) -> dict:
    inp = _fwd_setup_inputs(seed)
    key = _jax.random.fold_in(_jax.random.key(seed), 7919)
    shape, _ = _output_shape()
    out = dict(inp)
    out["loss_target"] = _jax.random.normal(_jax.random.fold_in(key, 0), shape, _jnp.float32)
    for i, name in enumerate(TWIN_WEIGHTS):
        w = inp[name].astype(_jnp.float32)
        if MOMENT_SCALE is None:
            s = _jnp.sqrt(_jnp.mean(_jnp.square(w)) + 1e-30)
        else:
            s = MOMENT_SCALE[name]
        km, kv = _jax.random.split(_jax.random.fold_in(key, i + 1))
        out[name] = w
        out["m_" + name] = s * _jax.random.normal(km, w.shape, _jnp.float32)
        out["v_" + name] = (s * s) * _jax.random.uniform(kv, w.shape, _jnp.float32, 0.5, 1.5)
    if N_MICROBATCH > 1:
        for name, axis in PER_EXAMPLE_BATCH_AXIS.items():
            out[name] = _to_microbatches(out[name], axis)
    return {'x': out['x'], 'rel_bias': out['rel_bias'], 'norm_mix_g': out['norm_mix_g'], 'w_in': out['w_in'], 'q_norm_g': out['q_norm_g'], 'k_norm_g': out['k_norm_g'], 'sinks': out['sinks'], 'conv_w': out['conv_w'], 'conv_b': out['conv_b'], 'conv_ln_g': out['conv_ln_g'], 'conv_ln_b': out['conv_ln_b'], 'attn_out_g': out['attn_out_g'], 'conv_out_g': out['conv_out_g'], 'w_out': out['w_out'], 'norm_mlp_g': out['norm_mlp_g'], 'w_mlp_up': out['w_mlp_up'], 'w_mlp_down': out['w_mlp_down'], 'loss_target': out['loss_target'], 'm_rel_bias': out['m_rel_bias'], 'm_norm_mix_g': out['m_norm_mix_g'], 'm_w_in': out['m_w_in'], 'm_q_norm_g': out['m_q_norm_g'], 'm_k_norm_g': out['m_k_norm_g'], 'm_sinks': out['m_sinks'], 'm_conv_w': out['m_conv_w'], 'm_conv_b': out['m_conv_b'], 'm_conv_ln_g': out['m_conv_ln_g'], 'm_conv_ln_b': out['m_conv_ln_b'], 'm_attn_out_g': out['m_attn_out_g'], 'm_conv_out_g': out['m_conv_out_g'], 'm_w_out': out['m_w_out'], 'm_norm_mlp_g': out['m_norm_mlp_g'], 'm_w_mlp_up': out['m_w_mlp_up'], 'm_w_mlp_down': out['m_w_mlp_down'], 'v_rel_bias': out['v_rel_bias'], 'v_norm_mix_g': out['v_norm_mix_g'], 'v_w_in': out['v_w_in'], 'v_q_norm_g': out['v_q_norm_g'], 'v_k_norm_g': out['v_k_norm_g'], 'v_sinks': out['v_sinks'], 'v_conv_w': out['v_conv_w'], 'v_conv_b': out['v_conv_b'], 'v_conv_ln_g': out['v_conv_ln_g'], 'v_conv_ln_b': out['v_conv_ln_b'], 'v_attn_out_g': out['v_attn_out_g'], 'v_conv_out_g': out['v_conv_out_g'], 'v_w_out': out['v_w_out'], 'v_norm_mlp_g': out['v_norm_mlp_g'], 'v_w_mlp_up': out['v_w_mlp_up'], 'v_w_mlp_down': out['v_w_mlp_down']}


def _loss(weights, diff, rest, loss_target):
    with _jax.named_scope("forward"):
        args = {**rest, TWIN_DIFF_INPUT: diff, **{k: w.astype(_WEIGHT_DTYPES[k]) for k, w in weights.items()}}
        y = _forward(args)
    with _jax.named_scope("loss_head"):
        err = _jnp.square(y.astype(_jnp.float32) - loss_target)
        return 0.5 * _jnp.sum(_jnp.mean(err, axis=-1)) if err.ndim else 0.5 * err


def _adamw(w, g, m, v):
    m = ADAM_B1 * m + (1.0 - ADAM_B1) * g
    v = ADAM_B2 * v + (1.0 - ADAM_B2) * _jnp.square(g)
    m_hat = m / (1.0 - ADAM_B1 ** ADAM_STEP)
    v_hat = v / (1.0 - ADAM_B2 ** ADAM_STEP)
    delta = -ADAM_LR * (m_hat / (_jnp.sqrt(v_hat) + ADAM_EPS) + ADAM_WD * w)
    return delta, m, v


def reference(x, rel_bias, norm_mix_g, w_in, q_norm_g, k_norm_g, sinks, conv_w, conv_b, conv_ln_g, conv_ln_b, attn_out_g, conv_out_g, w_out, norm_mlp_g, w_mlp_up, w_mlp_down, loss_target, m_rel_bias, m_norm_mix_g, m_w_in, m_q_norm_g, m_k_norm_g, m_sinks, m_conv_w, m_conv_b, m_conv_ln_g, m_conv_ln_b, m_attn_out_g, m_conv_out_g, m_w_out, m_norm_mlp_g, m_w_mlp_up, m_w_mlp_down, v_rel_bias, v_norm_mix_g, v_w_in, v_q_norm_g, v_k_norm_g, v_sinks, v_conv_w, v_conv_b, v_conv_ln_g, v_conv_ln_b, v_attn_out_g, v_conv_out_g, v_w_out, v_norm_mlp_g, v_w_mlp_up, v_w_mlp_down):
    given = dict(x=x, rel_bias=rel_bias, norm_mix_g=norm_mix_g, w_in=w_in, q_norm_g=q_norm_g, k_norm_g=k_norm_g, sinks=sinks, conv_w=conv_w, conv_b=conv_b, conv_ln_g=conv_ln_g, conv_ln_b=conv_ln_b, attn_out_g=attn_out_g, conv_out_g=conv_out_g, w_out=w_out, norm_mlp_g=norm_mlp_g, w_mlp_up=w_mlp_up, w_mlp_down=w_mlp_down, loss_target=loss_target, m_rel_bias=m_rel_bias, m_norm_mix_g=m_norm_mix_g, m_w_in=m_w_in, m_q_norm_g=m_q_norm_g, m_k_norm_g=m_k_norm_g, m_sinks=m_sinks, m_conv_w=m_conv_w, m_conv_b=m_conv_b, m_conv_ln_g=m_conv_ln_g, m_conv_ln_b=m_conv_ln_b, m_attn_out_g=m_attn_out_g, m_conv_out_g=m_conv_out_g, m_w_out=m_w_out, m_norm_mlp_g=m_norm_mlp_g, m_w_mlp_up=m_w_mlp_up, m_w_mlp_down=m_w_mlp_down, v_rel_bias=v_rel_bias, v_norm_mix_g=v_norm_mix_g, v_w_in=v_w_in, v_q_norm_g=v_q_norm_g, v_k_norm_g=v_k_norm_g, v_sinks=v_sinks, v_conv_w=v_conv_w, v_conv_b=v_conv_b, v_conv_ln_g=v_conv_ln_g, v_conv_ln_b=v_conv_ln_b, v_attn_out_g=v_attn_out_g, v_conv_out_g=v_conv_out_g, v_w_out=v_w_out, v_norm_mlp_g=v_norm_mlp_g, v_w_mlp_up=v_w_mlp_up, v_w_mlp_down=v_w_mlp_down)
    weights = {n: given[n] for n in TWIN_WEIGHTS}
    shared = {n: given[n] for n in SHARED_INPUTS}
    per_example = {n: given[n] for n in ['x']}
    grad_fn = _jax.value_and_grad(_loss, argnums=(0, 1))

    def one_microbatch(ex, loss_target):
        ex = dict(ex)
        diff = ex.pop(TWIN_DIFF_INPUT)
        return grad_fn(weights, diff, {**shared, **ex}, loss_target)

    if N_MICROBATCH == 1:
        loss, (grad_w, grad_x) = one_microbatch(per_example, given["loss_target"])
    else:
        def body(carry, xs):
            loss_sum, grad_sum = carry
            l_k, (gw_k, gx_k) = one_microbatch(xs[0], xs[1])
            with _jax.named_scope("update"):
                return (loss_sum + l_k, _jax.tree.map(_jnp.add, grad_sum, gw_k)), gx_k

        init = (_jnp.zeros((), _jnp.float32), _jax.tree.map(_jnp.zeros_like, weights))
        (loss, grad_w), grad_x = _jax.lax.scan(body, init, (per_example, given["loss_target"]))
    with _jax.named_scope("update"):
        delta_w, new_m, new_v = {}, {}, {}
        for n in TWIN_WEIGHTS:
            delta_w[n], new_m[n], new_v[n] = _adamw(weights[n], grad_w[n], given["m_" + n], given["v_" + n])
    return (loss, grad_x, *[grad_w[n] for n in TWIN_WEIGHTS], *[delta_w[n] for n in TWIN_WEIGHTS],
            *[new_m[n] for n in TWIN_WEIGHTS], *[new_v[n] for n in TWIN_WEIGHTS])
```

```python
import math

import numpy as np
import jax
import jax.numpy as jnp
from jax import lax
from jax.experimental import pallas as pl
from jax.experimental.pallas import tpu as pltpu

F32 = jnp.float32
BF16 = jnp.bfloat16

D_MODEL = 1024
DEPTH = 4
HEAD_DIM = 64
N_HEADS = 8
N_KV_HEADS = 2
GQA_GROUP = N_HEADS // N_KV_HEADS
ATTN_WIDTH = N_HEADS * HEAD_DIM
KV_WIDTH = N_KV_HEADS * HEAD_DIM
CONV_WIDTH = D_MODEL - ATTN_WIDTH
MIX_WIDTH = ATTN_WIDTH + CONV_WIDTH
IN_WIDTH = ATTN_WIDTH + 2 * KV_WIDTH + 2 * CONV_WIDTH
BLOCK = 128
CONV_KERNEL = 31
CONV_ROWS = 32
HALO = 32
CONV_CH = 256
NUM_BUCKETS = 32
MAX_DISTANCE = 128
D_FF = 4 * D_MODEL
EPS = 1e-6
NEG = -1e30
SCALE = 1.0 / math.sqrt(HEAD_DIM)

ADAM_LR = 0.001
ADAM_B1 = 0.9
ADAM_B2 = 0.999
ADAM_EPS = 1e-08
ADAM_WD = 0.01
ADAM_STEP = 10

N_CHIPS = 4
N_DEV = 8
LANES = 128
VMEM_LIMIT = 52 * 1024 * 1024

Q0, K0, V0, U0, G0 = 0, ATTN_WIDTH, ATTN_WIDTH + KV_WIDTH, ATTN_WIDTH + 2 * KV_WIDTH, ATTN_WIDTH + 2 * KV_WIDTH + CONV_WIDTH

NT = (((1,), (1,)), ((), ()))
TN = (((0,), (0,)), ((), ()))
MESH = pl.DeviceIdType.MESH
ANY = pl.BlockSpec(memory_space=pl.ANY)


def _params(sem=None):
    return pltpu.CompilerParams(dimension_semantics=sem, vmem_limit_bytes=VMEM_LIMIT)


def _chunk(n):
    for c in range(1024, 0, -LANES):
        if n % c == 0:
            return c
    raise ValueError(n)


def _tile(t, want):
    return min(t, want)


def _t5_bucket(n):
    n = np.asarray(n)
    max_exact = NUM_BUCKETS // 2
    large = max_exact + (np.log(np.maximum(n, 1) / max_exact) / np.log(MAX_DISTANCE / max_exact)
                         * (NUM_BUCKETS - max_exact)).astype(np.int32)
    large = np.minimum(large, NUM_BUCKETS - 1)
    return np.where(n < max_exact, n, large).astype(np.int32)


def _band():
    qi = np.arange(BLOCK)[:, None]
    kj = np.arange(2 * BLOCK)[None, :]
    dist = qi + BLOCK - kj
    in_window = (dist >= 0) & (dist < BLOCK)
    bucket = _t5_bucket(np.clip(dist, 0, None))
    return bucket, in_window


def _norm_matmul(x, g, w_all, l, out_dtype, name):
    T, D = x.shape
    N = w_all.shape[2]
    TM = _tile(T, 512)
    CH = _chunk(N)

    def body(x_ref, g_ref, w_ref, h_ref, z_ref):
        xv = x_ref[...]
        r = lax.rsqrt(jnp.mean(xv * xv, axis=-1, keepdims=True) + EPS)
        h = (xv * r * g_ref[...]).astype(BF16)
        h_ref[...] = h
        for c0 in range(0, N, CH):
            z_ref[:, c0:c0 + CH] = jnp.dot(h, w_ref[:, c0:c0 + CH], preferred_element_type=F32).astype(z_ref.dtype)

    return pl.pallas_call(
        body, name=name, grid=(T // TM,),
        in_specs=[pl.BlockSpec((TM, D), lambda i: (i, 0)),
                  pl.BlockSpec((1, D), lambda i: (0, 0)),
                  pl.BlockSpec((None, D, N), lambda i: (l, 0, 0))],
        out_specs=[pl.BlockSpec((TM, D), lambda i: (i, 0)),
                   pl.BlockSpec((TM, N), lambda i: (i, 0))],
        out_shape=[jax.ShapeDtypeStruct((T, D), BF16), jax.ShapeDtypeStruct((T, N), out_dtype)],
        compiler_params=_params(("parallel",)),
    )(x, g, w_all)


def _matmul_res(a, w_all, l, res, relu2, name):
    T, K = a.shape
    N = w_all.shape[2]
    TM = _tile(T, 512)
    CH = _chunk(K)

    def body(a_ref, w_ref, res_ref, o_ref):
        acc = res_ref[...]
        for k0 in range(0, K, CH):
            av = a_ref[:, k0:k0 + CH]
            if relu2:
                av = jnp.square(jnp.maximum(av.astype(F32), 0.0)).astype(BF16)
            acc = acc + jnp.dot(av, w_ref[k0:k0 + CH, :], preferred_element_type=F32)
        o_ref[...] = acc

    return pl.pallas_call(
        body, name=name, grid=(T // TM,),
        in_specs=[pl.BlockSpec((TM, K), lambda i: (i, 0)),
                  pl.BlockSpec((None, K, N), lambda i: (l, 0, 0)),
                  pl.BlockSpec((TM, N), lambda i: (i, 0))],
        out_specs=pl.BlockSpec((TM, N), lambda i: (i, 0)),
        out_shape=jax.ShapeDtypeStruct((T, N), F32),
        compiler_params=_params(("parallel",)),
    )(a, w_all, res)


def _head_norm(t, g):
    r = lax.rsqrt(jnp.mean(t * t, axis=-1, keepdims=True) + EPS)
    that = t * r
    return that * g, that, r


def _softmax_sink(s, sink):
    m = jnp.maximum(jnp.max(s, axis=-1, keepdims=True), sink)
    p = jnp.exp(s - m)
    es = jnp.exp(sink - m)
    den = jnp.sum(p, axis=-1, keepdims=True) + es
    return p / den, es / den


def _band_ok(n):
    col = lax.broadcasted_iota(jnp.int32, (BLOCK, 2 * BLOCK), 1)
    return jnp.logical_or(n > 0, col >= BLOCK)


def _attn_fwd(z, biasm, sinks_l, qg, kg):
    T = z.shape[0]
    nb = T // BLOCK
    kb, vb = K0 // KV_WIDTH, V0 // KV_WIDTH

    def body(sk_ref, q_ref, kc_ref, kp_ref, vc_ref, vp_ref, b_ref, qg_ref, kg_ref, a_ref):
        n = pl.program_id(0)
        ok = _band_ok(n)
        kcat = jnp.concatenate([kp_ref[...], kc_ref[...]], axis=0)
        vcat = jnp.concatenate([vp_ref[...], vc_ref[...]], axis=0).astype(BF16)
        qv = q_ref[...]
        for kvh in range(N_KV_HEADS):
            kn, _, _ = _head_norm(kcat[:, kvh * HEAD_DIM:(kvh + 1) * HEAD_DIM], kg_ref[...])
            kn = kn.astype(BF16)
            vh = vcat[:, kvh * HEAD_DIM:(kvh + 1) * HEAD_DIM]
            for g in range(GQA_GROUP):
                h = kvh * GQA_GROUP + g
                qn, _, _ = _head_norm(qv[:, h * HEAD_DIM:(h + 1) * HEAD_DIM], qg_ref[...])
                s = lax.dot_general(qn.astype(BF16), kn, NT, preferred_element_type=F32) * SCALE + b_ref[h]
                s = jnp.where(ok, s, NEG)
                p, _ = _softmax_sink(s, sk_ref[h])
                a_ref[:, h * HEAD_DIM:(h + 1) * HEAD_DIM] = jnp.dot(p.astype(BF16), vh, preferred_element_type=F32)

    cur = lambda n: n
    prev = lambda n: jnp.maximum(n - 1, 0)
    return pl.pallas_call(
        body, name="attn_fwd", grid=(nb,),
        in_specs=[pl.BlockSpec(memory_space=pltpu.SMEM),
                  pl.BlockSpec((BLOCK, ATTN_WIDTH), lambda n: (n, 0)),
                  pl.BlockSpec((BLOCK, KV_WIDTH), lambda n: (cur(n), kb)),
                  pl.BlockSpec((BLOCK, KV_WIDTH), lambda n: (prev(n), kb)),
                  pl.BlockSpec((BLOCK, KV_WIDTH), lambda n: (cur(n), vb)),
                  pl.BlockSpec((BLOCK, KV_WIDTH), lambda n: (prev(n), vb)),
                  pl.BlockSpec((N_HEADS, BLOCK, 2 * BLOCK), lambda n: (0, 0, 0)),
                  pl.BlockSpec((1, HEAD_DIM), lambda n: (0, 0)),
                  pl.BlockSpec((1, HEAD_DIM), lambda n: (0, 0))],
        out_specs=pl.BlockSpec((BLOCK, ATTN_WIDTH), lambda n: (n, 0)),
        out_shape=jax.ShapeDtypeStruct((T, ATTN_WIDTH), F32),
        compiler_params=_params(("parallel",)),
    )(sinks_l, z, z, z, z, z, biasm, qg, kg)


def _conv_fwd(z, cw, cb):
    T = z.shape[0]
    TC = _tile(T, 512)
    ub, gb = U0 // CONV_CH, G0 // CONV_CH
    hpt = TC // HALO

    def body(u_ref, g_ref, up_ref, gp_ref, w_ref, b_ref, y_ref, hp_ref):
        i = pl.program_id(0)
        hp_ref[pl.ds(0, HALO), :] = jnp.where(i > 0, up_ref[...] * jax.nn.sigmoid(gp_ref[...]), 0.0)
        hp_ref[pl.ds(HALO, TC), :] = u_ref[...] * jax.nn.sigmoid(g_ref[...])
        acc = jnp.zeros((TC, CONV_CH), F32) + b_ref[...]
        for j in range(CONV_KERNEL):
            acc = acc + hp_ref[pl.ds(HALO - (CONV_KERNEL - 1) + j, TC), :] * w_ref[pl.ds(j, 1), :]
        y_ref[...] = acc

    prev = lambda i: jnp.maximum(i * hpt - 1, 0)
    return pl.pallas_call(
        body, name="conv_fwd", grid=(T // TC, CONV_WIDTH // CONV_CH),
        in_specs=[pl.BlockSpec((TC, CONV_CH), lambda i, j: (i, ub + j)),
                  pl.BlockSpec((TC, CONV_CH), lambda i, j: (i, gb + j)),
                  pl.BlockSpec((HALO, CONV_CH), lambda i, j: (prev(i), ub + j)),
                  pl.BlockSpec((HALO, CONV_CH), lambda i, j: (prev(i), gb + j)),
                  pl.BlockSpec((CONV_ROWS, CONV_CH), lambda i, j: (0, j)),
                  pl.BlockSpec((1, CONV_CH), lambda i, j: (0, j))],
        out_specs=pl.BlockSpec((TC, CONV_CH), lambda i, j: (i, j)),
        out_shape=jax.ShapeDtypeStruct((T, CONV_WIDTH), F32),
        scratch_shapes=[pltpu.VMEM((TC + HALO, CONV_CH), F32)],
        compiler_params=_params(("parallel", "parallel")),
    )(z, z, z, z, cw, cb)


def _ln_silu(y, ln_g, ln_b):
    mu = jnp.mean(y, axis=-1, keepdims=True)
    yc = y - mu
    var = jnp.mean(yc * yc, axis=-1, keepdims=True)
    rstd = lax.rsqrt(var + EPS)
    yhat = yc * rstd
    yn = yhat * ln_g + ln_b
    sg = jax.nn.sigmoid(yn)
    return yn * sg, yn, sg, yhat, rstd


def _mix_norm(a, y, ln_g, ln_b, ag, cg):
    T = a.shape[0]
    TM = _tile(T, 512)

    def body(a_ref, y_ref, lg_ref, lb_ref, ag_ref, cg_ref, o_ref):
        av = a_ref[...]
        ra = lax.rsqrt(jnp.mean(av * av, axis=-1, keepdims=True) + EPS)
        o_ref[:, :ATTN_WIDTH] = (av * ra * ag_ref[...]).astype(BF16)
        c, _, _, _, _ = _ln_silu(y_ref[...], lg_ref[...], lb_ref[...])
        rc = lax.rsqrt(jnp.mean(c * c, axis=-1, keepdims=True) + EPS)
        o_ref[:, ATTN_WIDTH:] = (c * rc * cg_ref[...]).astype(BF16)

    vec = pl.BlockSpec((1, CONV_WIDTH), lambda i: (0, 0))
    return pl.pallas_call(
        body, name="mix_norm", grid=(T // TM,),
        in_specs=[pl.BlockSpec((TM, ATTN_WIDTH), lambda i: (i, 0)),
                  pl.BlockSpec((TM, CONV_WIDTH), lambda i: (i, 0)), vec, vec, vec, vec],
        out_specs=pl.BlockSpec((TM, MIX_WIDTH), lambda i: (i, 0)),
        out_shape=jax.ShapeDtypeStruct((T, MIX_WIDTH), BF16),
        compiler_params=_params(("parallel",)),
    )(a, y, ln_g, ln_b, ag, cg)


def _loss_grad(y, tgt):
    T, D = y.shape
    TM = _tile(T, 512)
    nt = T // TM

    def body(y_ref, t_ref, part_ref, dy_ref):
        diff = y_ref[...] - t_ref[...]
        dy_ref[...] = diff / D
        tok = jnp.mean(diff * diff, axis=-1, keepdims=True)
        part_ref[...] = jnp.zeros((1, LANES), F32) + 0.5 * jnp.sum(tok)

    return pl.pallas_call(
        body, name="loss_grad", grid=(nt,),
        in_specs=[pl.BlockSpec((TM, D), lambda i: (i, 0)), pl.BlockSpec((TM, D), lambda i: (i, 0))],
        out_specs=[pl.BlockSpec((None, 1, LANES), lambda i: (i, 0, 0)), pl.BlockSpec((TM, D), lambda i: (i, 0))],
        out_shape=[jax.ShapeDtypeStruct((nt, 1, LANES), F32), jax.ShapeDtypeStruct((T, D), F32)],
        compiler_params=_params(("parallel",)),
    )(y, tgt)


def _dact(g, w_all, l, up):
    T, N = g.shape
    K = w_all.shape[1]
    TM = _tile(T, 512)
    CH = _chunk(K)

    def body(g_ref, w_ref, up_ref, o_ref):
        gv = g_ref[...].astype(BF16)
        for k0 in range(0, K, CH):
            da = lax.dot_general(gv, w_ref[k0:k0 + CH, :], NT, preferred_element_type=F32)
            upv = up_ref[:, k0:k0 + CH].astype(F32)
            o_ref[:, k0:k0 + CH] = (da * (2.0 * jnp.maximum(upv, 0.0))).astype(BF16)

    return pl.pallas_call(
        body, name="mlp_dact", grid=(T // TM,),
        in_specs=[pl.BlockSpec((TM, N), lambda i: (i, 0)),
                  pl.BlockSpec((None, K, N), lambda i: (l, 0, 0)),
                  pl.BlockSpec((TM, K), lambda i: (i, 0))],
        out_specs=pl.BlockSpec((TM, K), lambda i: (i, 0)),
        out_shape=jax.ShapeDtypeStruct((T, K), BF16),
        compiler_params=_params(("parallel",)),
    )(g, w_all, up)


def _matmul_tn(a, b, relu2, buf, buf_shape, out_block, out_index, tm, tn, name):
    T, M = a.shape
    N = b.shape[1]
    TK = _tile(T, 1024)
    nk = T // TK

    def body(*refs):
        a_ref, b_ref = refs[0], refs[1]
        o_ref = refs[-1]
        k = pl.program_id(2)
        av = a_ref[...]
        if relu2:
            av = jnp.square(jnp.maximum(av.astype(F32), 0.0)).astype(BF16)
        c = lax.dot_general(av, b_ref[...].astype(BF16), TN, preferred_element_type=F32).reshape(o_ref.shape)

        @pl.when(k == 0)
        def _():
            o_ref[...] = c

        @pl.when(k > 0)
        def _():
            o_ref[...] += c

    in_specs = [pl.BlockSpec((TK, tm), lambda i, j, k: (k, i)), pl.BlockSpec((TK, tn), lambda i, j, k: (k, j))]
    args = [a, b]
    aliases = {}
    if buf is not None:
        in_specs.append(ANY)
        args.append(buf)
        aliases = {2: 0}
    return pl.pallas_call(
        body, name=name, grid=(M // tm, N // tn, nk),
        in_specs=in_specs,
        out_specs=pl.BlockSpec(out_block, lambda i, j, k: out_index(i, j)),
        out_shape=jax.ShapeDtypeStruct(buf_shape, F32),
        input_output_aliases=aliases,
        compiler_params=_params(("parallel", "parallel", "arbitrary")),
    )(*args)


def _matmul_nt_normbwd(dz, w_all, l, x, gvec, gres, name):
    T, K = dz.shape
    D = x.shape[1]
    TM = _tile(T, 512)
    CH = _chunk(K)

    def body(dz_ref, w_ref, x_ref, gv_ref, gr_ref, o_ref, dg_ref):
        i = pl.program_id(0)
        dh = jnp.zeros((TM, D), F32)
        for k0 in range(0, K, CH):
            dh = dh + lax.dot_general(dz_ref[:, k0:k0 + CH], w_ref[:, k0:k0 + CH], NT, preferred_element_type=F32)
        xv = x_ref[...]
        r = lax.rsqrt(jnp.mean(xv * xv, axis=-1, keepdims=True) + EPS)
        xhat = xv * r
        dg = jnp.sum(dh * xhat, axis=0, keepdims=True)

        @pl.when(i == 0)
        def _():
            dg_ref[...] = dg

        @pl.when(i > 0)
        def _():
            dg_ref[...] += dg

        wv = dh * gv_ref[...]
        o_ref[...] = gr_ref[...] + r * (wv - xhat * jnp.mean(wv * xhat, axis=-1, keepdims=True))

    return pl.pallas_call(
        body, name=name, grid=(T // TM,),
        in_specs=[pl.BlockSpec((TM, K), lambda i: (i, 0)),
                  pl.BlockSpec((None, D, K), lambda i: (l, 0, 0)),
                  pl.BlockSpec((TM, D), lambda i: (i, 0)),
                  pl.BlockSpec((1, D), lambda i: (0, 0)),
                  pl.BlockSpec((TM, D), lambda i: (i, 0))],
        out_specs=[pl.BlockSpec((TM, D), lambda i: (i, 0)), pl.BlockSpec((1, D), lambda i: (0, 0))],
        out_shape=[jax.ShapeDtypeStruct((T, D), F32), jax.ShapeDtypeStruct((1, D), F32)],
        compiler_params=_params(("arbitrary",)),
    )(dz, w_all, x, gvec, gres)


def _mix_bwd(g1, w_all, l, a, y, ln_g, ln_b, ag, cg):
    T, D = g1.shape
    TM = _tile(T, 512)

    def body(g_ref, w_ref, a_ref, y_ref, lg_ref, lb_ref, ag_ref, cg_ref, da_ref, dy_ref, sm_ref):
        i = pl.program_id(0)
        dmix = lax.dot_general(g_ref[...].astype(BF16), w_ref[...], NT, preferred_element_type=F32)
        dma, dmc = dmix[:, :ATTN_WIDTH], dmix[:, ATTN_WIDTH:]
        av = a_ref[...]
        ra = lax.rsqrt(jnp.mean(av * av, axis=-1, keepdims=True) + EPS)
        ahat = av * ra
        d_ag = jnp.sum(dma * ahat, axis=0, keepdims=True)
        wa = dma * ag_ref[...]
        da_ref[...] = ra * (wa - ahat * jnp.mean(wa * ahat, axis=-1, keepdims=True))

        c, yn, sg, yhat, rstd = _ln_silu(y_ref[...], lg_ref[...], lb_ref[...])
        rc = lax.rsqrt(jnp.mean(c * c, axis=-1, keepdims=True) + EPS)
        chat = c * rc
        d_cg = jnp.sum(dmc * chat, axis=0, keepdims=True)
        wc = dmc * cg_ref[...]
        dc = rc * (wc - chat * jnp.mean(wc * chat, axis=-1, keepdims=True))
        dyn = dc * (sg * (1.0 + yn * (1.0 - sg)))
        d_lg = jnp.sum(dyn * yhat, axis=0, keepdims=True)
        d_lb = jnp.sum(dyn, axis=0, keepdims=True)
        dyh = dyn * lg_ref[...]
        dy = rstd * (dyh - jnp.mean(dyh, axis=-1, keepdims=True) - yhat * jnp.mean(dyh * yhat, axis=-1, keepdims=True))
        dy_ref[...] = dy
        d_cb = jnp.sum(dy, axis=0, keepdims=True)
        sums = jnp.concatenate([d_ag, d_cg, d_lg, d_lb, d_cb, jnp.zeros((3, CONV_WIDTH), F32)], axis=0)

        @pl.when(i == 0)
        def _():
            sm_ref[...] = sums

        @pl.when(i > 0)
        def _():
            sm_ref[...] += sums

    vec = pl.BlockSpec((1, CONV_WIDTH), lambda i: (0, 0))
    return pl.pallas_call(
        body, name="mix_bwd", grid=(T // TM,),
        in_specs=[pl.BlockSpec((TM, D), lambda i: (i, 0)),
                  pl.BlockSpec((None, MIX_WIDTH, D), lambda i: (l, 0, 0)),
                  pl.BlockSpec((TM, ATTN_WIDTH), lambda i: (i, 0)),
                  pl.BlockSpec((TM, CONV_WIDTH), lambda i: (i, 0)), vec, vec, vec, vec],
        out_specs=[pl.BlockSpec((TM, ATTN_WIDTH), lambda i: (i, 0)),
                   pl.BlockSpec((TM, CONV_WIDTH), lambda i: (i, 0)),
                   pl.BlockSpec((8, CONV_WIDTH), lambda i: (0, 0))],
        out_shape=[jax.ShapeDtypeStruct((T, ATTN_WIDTH), F32), jax.ShapeDtypeStruct((T, CONV_WIDTH), F32),
                   jax.ShapeDtypeStruct((8, CONV_WIDTH), F32)],
        compiler_params=_params(("arbitrary",)),
    )(g1, w_all, a, y, ln_g, ln_b, ag, cg)


def _conv_bwd(dy, z, cw):
    T = z.shape[0]
    TC = _tile(T, 512)
    nt = T // TC
    ub, gb = U0 // CONV_CH, G0 // CONV_CH
    nch = CONV_WIDTH // CONV_CH
    hpt = TC // HALO

    def body(dy_ref, dyn_ref, u_ref, g_ref, up_ref, gp_ref, w_ref, du_ref, dg_ref, dw_ref, hp_ref, dyp_ref):
        i = pl.program_id(1)
        uv = u_ref[...]
        sg = jax.nn.sigmoid(g_ref[...])
        hp_ref[pl.ds(0, HALO), :] = jnp.where(i > 0, up_ref[...] * jax.nn.sigmoid(gp_ref[...]), 0.0)
        hp_ref[pl.ds(HALO, TC), :] = uv * sg
        dyv = dy_ref[...]
        dyp_ref[pl.ds(0, TC), :] = dyv
        dyp_ref[pl.ds(TC, HALO), :] = jnp.where(i < nt - 1, dyn_ref[...], 0.0)

        @pl.when(i == 0)
        def _():
            dw_ref[...] = jnp.zeros((CONV_ROWS, CONV_CH), F32)

        dh = jnp.zeros((TC, CONV_CH), F32)
        for j in range(CONV_KERNEL):
            d = CONV_KERNEL - 1 - j
            dh = dh + dyp_ref[pl.ds(d, TC), :] * w_ref[pl.ds(j, 1), :]
            dw_ref[pl.ds(j, 1), :] += jnp.sum(dyv * hp_ref[pl.ds(HALO - (CONV_KERNEL - 1) + j, TC), :], axis=0, keepdims=True)
        du_ref[...] = (dh * sg).astype(BF16)
        dg_ref[...] = (dh * uv * sg * (1.0 - sg)).astype(BF16)

    prev = lambda i: jnp.maximum(i * hpt - 1, 0)
    nxt = lambda i: jnp.minimum((i + 1) * hpt, T // HALO - 1)
    return pl.pallas_call(
        body, name="conv_bwd", grid=(nch, nt),
        in_specs=[pl.BlockSpec((TC, CONV_CH), lambda j, i: (i, j)),
                  pl.BlockSpec((HALO, CONV_CH), lambda j, i: (nxt(i), j)),
                  pl.BlockSpec((TC, CONV_CH), lambda j, i: (i, ub + j)),
                  pl.BlockSpec((TC, CONV_CH), lambda j, i: (i, gb + j)),
                  pl.BlockSpec((HALO, CONV_CH), lambda j, i: (prev(i), ub + j)),
                  pl.BlockSpec((HALO, CONV_CH), lambda j, i: (prev(i), gb + j)),
                  pl.BlockSpec((CONV_ROWS, CONV_CH), lambda j, i: (0, j))],
        out_specs=[pl.BlockSpec((TC, CONV_CH), lambda j, i: (i, j)),
                   pl.BlockSpec((TC, CONV_CH), lambda j, i: (i, j)),
                   pl.BlockSpec((CONV_ROWS, CONV_CH), lambda j, i: (0, j))],
        out_shape=[jax.ShapeDtypeStruct((T, CONV_WIDTH), BF16), jax.ShapeDtypeStruct((T, CONV_WIDTH), BF16),
                   jax.ShapeDtypeStruct((CONV_ROWS, CONV_WIDTH), F32)],
        scratch_shapes=[pltpu.VMEM((TC + HALO, CONV_CH), F32), pltpu.VMEM((TC + HALO, CONV_CH), F32)],
        compiler_params=_params(("parallel", "arbitrary")),
    )(dy, dy, z, z, z, z, cw)


def _attn_bwd(z, da, biasm, sinks_l, qg, kg):
    T = z.shape[0]
    nb = T // BLOCK
    kb, vb = K0 // KV_WIDTH, V0 // KV_WIDTH

    def body(sk_ref, q_ref, kc_ref, kp_ref, vc_ref, vp_ref, da_ref, b_ref, qg_ref, kg_ref,
             dq_ref, dkv_ref, db_ref, sm_ref, ck_ref, cv_ref, pk_ref, pv_ref, nk_ref, nv_ref):
        n = pl.program_id(0)
        lane = lax.broadcasted_iota(jnp.int32, (1, LANES), 1)

        @pl.when(n == 0)
        def _():
            db_ref[...] = jnp.zeros(db_ref.shape, F32)
            sm_ref[...] = jnp.zeros(sm_ref.shape, F32)
            ck_ref[...] = jnp.zeros(ck_ref.shape, F32)
            cv_ref[...] = jnp.zeros(cv_ref.shape, F32)

        pk_ref[...] = jnp.zeros(pk_ref.shape, F32)
        pv_ref[...] = jnp.zeros(pv_ref.shape, F32)

        @pl.when(n < nb)
        def _():
            ok = _band_ok(n)
            kcat = jnp.concatenate([kp_ref[...], kc_ref[...]], axis=0)
            vcat = jnp.concatenate([vp_ref[...], vc_ref[...]], axis=0).astype(BF16)
            qv = q_ref[...]
            dav = da_ref[...]
            dqg = jnp.zeros((1, HEAD_DIM), F32)
            dsk = jnp.zeros((1, LANES), F32)
            for kvh in range(N_KV_HEADS):
                cs = slice(kvh * HEAD_DIM, (kvh + 1) * HEAD_DIM)
                kn, _, _ = _head_norm(kcat[:, cs], kg_ref[...])
                kn = kn.astype(BF16)
                vh = vcat[:, cs]
                dkn = jnp.zeros((2 * BLOCK, HEAD_DIM), F32)
                dvv = jnp.zeros((2 * BLOCK, HEAD_DIM), F32)
                for g in range(GQA_GROUP):
                    h = kvh * GQA_GROUP + g
                    hs = slice(h * HEAD_DIM, (h + 1) * HEAD_DIM)
                    qn, qhat, rq = _head_norm(qv[:, hs], qg_ref[...])
                    qnb = qn.astype(BF16)
                    s = lax.dot_general(qnb, kn, NT, preferred_element_type=F32) * SCALE + b_ref[h]
                    s = jnp.where(ok, s, NEG)
                    p, psink = _softmax_sink(s, sk_ref[h])
                    dob = dav[:, hs].astype(BF16)
                    dp = lax.dot_general(dob, vh, NT, preferred_element_type=F32)
                    delta = jnp.sum(p * dp, axis=-1, keepdims=True)
                    ds = p * (dp - delta)
                    dsk = dsk + jnp.where(lane == h, -jnp.sum(psink * delta), 0.0)
                    db_ref[h] += ds
                    dsb = ds.astype(BF16)
                    dqn = jnp.dot(dsb, kn, preferred_element_type=F32) * SCALE
                    dkn = dkn + lax.dot_general(dsb, qnb, TN, preferred_element_type=F32) * SCALE
                    dvv = dvv + lax.dot_general(p.astype(BF16), dob, TN, preferred_element_type=F32)
                    dqg = dqg + jnp.sum(dqn * qhat, axis=0, keepdims=True)
                    wq = dqn * qg_ref[...]
                    dq_ref[:, hs] = (rq * (wq - qhat * jnp.mean(wq * qhat, axis=-1, keepdims=True))).astype(BF16)
                pk_ref[:, cs] = dkn[:BLOCK]
                pv_ref[:, cs] = dvv[:BLOCK]
                nk_ref[:, cs] = dkn[BLOCK:]
                nv_ref[:, cs] = dvv[BLOCK:]
            sm_ref[pl.ds(0, 1), pl.ds(0, HEAD_DIM)] += dqg
            sm_ref[pl.ds(2, 1), :] += dsk

        @pl.when(n >= 1)
        def _():
            dkt = ck_ref[...] + pk_ref[...]
            kpv = kp_ref[...]
            dkg = jnp.zeros((1, HEAD_DIM), F32)
            for kvh in range(N_KV_HEADS):
                cs = slice(kvh * HEAD_DIM, (kvh + 1) * HEAD_DIM)
                _, khat, rk = _head_norm(kpv[:, cs], kg_ref[...])
                dk = dkt[:, cs]
                dkg = dkg + jnp.sum(dk * khat, axis=0, keepdims=True)
                wk = dk * kg_ref[...]
                dkv_ref[:, cs] = (rk * (wk - khat * jnp.mean(wk * khat, axis=-1, keepdims=True))).astype(BF16)
            dkv_ref[:, KV_WIDTH:] = (cv_ref[...] + pv_ref[...]).astype(BF16)
            sm_ref[pl.ds(1, 1), pl.ds(0, HEAD_DIM)] += dkg

        ck_ref[...] = nk_ref[...]
        cv_ref[...] = nv_ref[...]

    cur = lambda n: jnp.minimum(n, nb - 1)
    prev = lambda n: jnp.maximum(n - 1, 0)
    carry = pltpu.VMEM((BLOCK, KV_WIDTH), F32)
    return pl.pallas_call(
        body, name="attn_bwd", grid=(nb + 1,),
        in_specs=[pl.BlockSpec(memory_space=pltpu.SMEM),
                  pl.BlockSpec((BLOCK, ATTN_WIDTH), lambda n: (cur(n), 0)),
                  pl.BlockSpec((BLOCK, KV_WIDTH), lambda n: (cur(n), kb)),
                  pl.BlockSpec((BLOCK, KV_WIDTH), lambda n: (prev(n), kb)),
                  pl.BlockSpec((BLOCK, KV_WIDTH), lambda n: (cur(n), vb)),
                  pl.BlockSpec((BLOCK, KV_WIDTH), lambda n: (prev(n), vb)),
                  pl.BlockSpec((BLOCK, ATTN_WIDTH), lambda n: (cur(n), 0)),
                  pl.BlockSpec((N_HEADS, BLOCK, 2 * BLOCK), lambda n: (0, 0, 0)),
                  pl.BlockSpec((1, HEAD_DIM), lambda n: (0, 0)),
                  pl.BlockSpec((1, HEAD_DIM), lambda n: (0, 0))],
        out_specs=[pl.BlockSpec((BLOCK, ATTN_WIDTH), lambda n: (cur(n), 0)),
                   pl.BlockSpec((BLOCK, 2 * KV_WIDTH), lambda n: (prev(n), 0)),
                   pl.BlockSpec((N_HEADS, BLOCK, 2 * BLOCK), lambda n: (0, 0, 0)),
                   pl.BlockSpec((8, LANES), lambda n: (0, 0))],
        out_shape=[jax.ShapeDtypeStruct((T, ATTN_WIDTH), BF16), jax.ShapeDtypeStruct((T, 2 * KV_WIDTH), BF16),
                   jax.ShapeDtypeStruct((N_HEADS, BLOCK, 2 * BLOCK), F32), jax.ShapeDtypeStruct((8, LANES), F32)],
        scratch_shapes=[carry] * 6,
        compiler_params=_params(("arbitrary",)),
    )(sinks_l, z, z, z, z, z, da, biasm, qg, kg)


def _bucket_reduce(dbias, onehot_t):
    def body(d_ref, oh_ref, o_ref):
        d = d_ref[...]
        hi = d.astype(BF16)
        r1 = d - hi.astype(F32)
        mid = r1.astype(BF16)
        lo = (r1 - mid.astype(F32)).astype(BF16)
        oh = oh_ref[...]
        acc = lax.dot_general(lo, oh, NT, preferred_element_type=F32)
        acc = acc + lax.dot_general(mid, oh, NT, preferred_element_type=F32)
        o_ref[...] = acc + lax.dot_general(hi, oh, NT, preferred_element_type=F32)

    return pl.pallas_call(
        body, name="bucket_reduce",
        out_shape=jax.ShapeDtypeStruct((N_HEADS, LANES), F32),
        compiler_params=_params(),
    )(dbias, onehot_t)


def _adamw(w, g, m, v, name):
    R, C = w.shape
    TR = _tile(R, 512)

    def body(w_ref, g_ref, m_ref, v_ref, d_ref, nm_ref, nv_ref):
        gv = g_ref[...]
        mn = ADAM_B1 * m_ref[...] + (1.0 - ADAM_B1) * gv
        vn = ADAM_B2 * v_ref[...] + (1.0 - ADAM_B2) * jnp.square(gv)
        m_hat = mn / (1.0 - ADAM_B1 ** ADAM_STEP)
        v_hat = vn / (1.0 - ADAM_B2 ** ADAM_STEP)
        d_ref[...] = -ADAM_LR * (m_hat / (jnp.sqrt(v_hat) + ADAM_EPS) + ADAM_WD * w_ref[...])
        nm_ref[...] = mn
        nv_ref[...] = vn

    spec = pl.BlockSpec((TR, C), lambda i: (i, 0))
    shp = jax.ShapeDtypeStruct((R, C), F32)
    return pl.pallas_call(
        body, name=name, grid=(R // TR,),
        in_specs=[spec] * 4, out_specs=[spec] * 3, out_shape=[shp] * 3,
        compiler_params=_params(("parallel",)),
    )(w, g, m, v)


def _place():
    return lax.axis_index("x"), lax.axis_index("y"), lax.axis_index("c")


def _other_chips(x, y):
    return [(1 - x, y), (x, 1 - y), (1 - x, 1 - y)]


def _remote(src, dst, send_sem, recv_sem, dev):
    return pltpu.make_async_remote_copy(src_ref=src, dst_ref=dst, send_sem=send_sem, recv_sem=recv_sem,
                                        device_id=dev, device_id_type=MESH)


def _gather_shards(bufs):
    nbuf = len(bufs)

    def body(*refs):
        ins, outs = refs[:nbuf], refs[nbuf:2 * nbuf]
        send_sems, recv_sems, loc_sems = refs[2 * nbuf:]
        x, y, c = _place()
        me = 2 * x + y
        sib = (x, y, 1 - c)
        chips = _other_chips(x, y)
        started = []
        local = []
        for b in range(nbuf):
            hh = bufs[b].shape[0] // 2
            cp = pltpu.make_async_copy(ins[b], outs[b].at[me], loc_sems.at[b])
            cp.start()
            local.append(cp)
            for j, (cx, cy) in enumerate(chips):
                k = 6 * b + j
                cp = _remote(ins[b].at[pl.ds(c * hh, hh), :], outs[b].at[me, pl.ds(c * hh, hh), :],
                             send_sems.at[k], recv_sems.at[k], (cx, cy, c))
                cp.start()
                started.append(cp)
        for b in range(nbuf):
            hh = bufs[b].shape[0] // 2
            for j, (cx, cy) in enumerate(chips):
                rows = outs[b].at[2 * cx + cy, pl.ds(c * hh, hh), :]
                _remote(rows, rows, send_sems.at[6 * b + j], recv_sems.at[6 * b + j], sib).wait_recv()
                k = 6 * b + 3 + j
                cp = _remote(rows, rows, send_sems.at[k], recv_sems.at[k], sib)
                cp.start()
                started.append(cp)
        for b in range(nbuf):
            hh = bufs[b].shape[0] // 2
            for j, (cx, cy) in enumerate(chips):
                rows = outs[b].at[2 * cx + cy, pl.ds((1 - c) * hh, hh), :]
                k = 6 * b + 3 + j
                _remote(rows, rows, send_sems.at[k], recv_sems.at[k], sib).wait_recv()
        for cp in started:
            cp.wait_send()
        for cp in local:
            cp.wait()

    return pl.pallas_call(
        body, name="gather_weights",
        in_specs=[ANY] * nbuf, out_specs=[ANY] * nbuf,
        out_shape=[jax.ShapeDtypeStruct((N_CHIPS,) + b.shape, b.dtype) for b in bufs],
        scratch_shapes=[pltpu.SemaphoreType.DMA((6 * nbuf,)), pltpu.SemaphoreType.DMA((6 * nbuf,)),
                        pltpu.SemaphoreType.DMA((nbuf,))],
        compiler_params=pltpu.CompilerParams(has_side_effects=True),
    )(*bufs)


def _swap_halves(bufs):
    nbuf = len(bufs)

    def body(*refs):
        ins, outs = refs[:nbuf], refs[nbuf:2 * nbuf]
        send_sems, recv_sems = refs[2 * nbuf:]
        x, y, c = _place()
        sib = (x, y, 1 - c)
        cps = []
        for b in range(nbuf):
            hh = bufs[b].shape[1] // 2
            cp = _remote(ins[b].at[:, pl.ds((1 - c) * hh, hh), :], outs[b], send_sems.at[b], recv_sems.at[b], sib)
            cp.start()
            cps.append(cp)
        for cp in cps:
            cp.wait()

    return pl.pallas_call(
        body, name="grad_swap_halves",
        in_specs=[ANY] * nbuf, out_specs=[ANY] * nbuf,
        out_shape=[jax.ShapeDtypeStruct((N_CHIPS, b.shape[1] // 2, b.shape[2]), b.dtype) for b in bufs],
        scratch_shapes=[pltpu.SemaphoreType.DMA((nbuf,)), pltpu.SemaphoreType.DMA((nbuf,))],
        compiler_params=pltpu.CompilerParams(has_side_effects=True),
    )(*bufs)


def _chip_sum(g, got, sel, out_dtype, name):
    _, R, C = g.shape
    hh = R // 2
    TR = _tile(hh, 512)
    nslot = sel[1].shape[0]

    def body(off_ref, sh_ref, g_ref, r_ref, o_ref):
        o_ref[...] = (g_ref[...] + r_ref[...]).astype(out_dtype)

    return pl.pallas_call(
        body, name=name,
        grid_spec=pltpu.PrefetchScalarGridSpec(
            num_scalar_prefetch=2, grid=(nslot, hh // TR),
            in_specs=[pl.BlockSpec((None, TR, C), lambda s, i, off, sh: (sh[s], off[0] + i, 0)),
                      pl.BlockSpec((None, TR, C), lambda s, i, off, sh: (sh[s], i, 0))],
            out_specs=pl.BlockSpec((None, TR, C), lambda s, i, off, sh: (s, i, 0))),
        out_shape=jax.ShapeDtypeStruct((nslot, hh, C), out_dtype),
        compiler_params=_params(("parallel", "parallel")),
    )(sel[0], sel[1], g, got)


def _exchange_chips(bufs):
    nbuf = len(bufs)

    def body(*refs):
        ins, outs = refs[:nbuf], refs[nbuf:2 * nbuf]
        send_sems, recv_sems = refs[2 * nbuf:]
        x, y, c = _place()
        cps = []
        for b in range(nbuf):
            for j, (cx, cy) in enumerate(_other_chips(x, y)):
                k = 3 * b + j
                cp = _remote(ins[b].at[j], outs[b].at[j], send_sems.at[k], recv_sems.at[k], (cx, cy, c))
                cp.start()
                cps.append(cp)
        for cp in cps:
            cp.wait()

    return pl.pallas_call(
        body, name="grad_exchange_chips",
        in_specs=[ANY] * nbuf, out_specs=[ANY] * nbuf,
        out_shape=[jax.ShapeDtypeStruct(b.shape, b.dtype) for b in bufs],
        scratch_shapes=[pltpu.SemaphoreType.DMA((3 * nbuf,)), pltpu.SemaphoreType.DMA((3 * nbuf,))],
        compiler_params=pltpu.CompilerParams(has_side_effects=True),
    )(*bufs)


def _shard_sum(own, got, name):
    _, hh, C = own.shape
    TR = _tile(hh, 512)

    def body(o_ref, r_ref, out_ref):
        acc = o_ref[...]
        for j in range(N_CHIPS - 1):
            acc = acc + r_ref[j].astype(F32)
        out_ref[...] = acc

    return pl.pallas_call(
        body, name=name, grid=(hh // TR,),
        in_specs=[pl.BlockSpec((None, TR, C), lambda i: (0, i, 0)),
                  pl.BlockSpec((N_CHIPS - 1, TR, C), lambda i: (0, i, 0))],
        out_specs=pl.BlockSpec((TR, C), lambda i: (i, 0)),
        out_shape=jax.ShapeDtypeStruct((hh, C), F32),
        compiler_params=_params(("parallel",)),
    )(own, got)


def _join_halves(bufs):
    nbuf = len(bufs)

    def body(*refs):
        ins, outs = refs[:nbuf], refs[nbuf:2 * nbuf]
        send_sems, recv_sems, loc_sems = refs[2 * nbuf:]
        x, y, c = _place()
        sib = (x, y, 1 - c)
        cps = []
        for b in range(nbuf):
            hh = bufs[b].shape[0]
            rows = outs[b].at[pl.ds(c * hh, hh), :]
            loc = pltpu.make_async_copy(ins[b], rows, loc_sems.at[b])
            loc.start()
            cp = _remote(ins[b], rows, send_sems.at[b], recv_sems.at[b], sib)
            cp.start()
            cps.append((loc, cp))
        for b, (loc, cp) in enumerate(cps):
            hh = bufs[b].shape[0]
            theirs = outs[b].at[pl.ds((1 - c) * hh, hh), :]
            _remote(ins[b], theirs, send_sems.at[b], recv_sems.at[b], sib).wait_recv()
            cp.wait_send()
            loc.wait()

    return pl.pallas_call(
        body, name="grad_join_halves",
        in_specs=[ANY] * nbuf, out_specs=[ANY] * nbuf,
        out_shape=[jax.ShapeDtypeStruct((2 * b.shape[0], b.shape[1]), b.dtype) for b in bufs],
        scratch_shapes=[pltpu.SemaphoreType.DMA((nbuf,)), pltpu.SemaphoreType.DMA((nbuf,)),
                        pltpu.SemaphoreType.DMA((nbuf,))],
        compiler_params=pltpu.CompilerParams(has_side_effects=True),
    )(*bufs)


def _sum_devices(part):
    R = part.shape[0]

    def body(p_ref, o_ref, all_ref, send_sems, recv_sems):
        x, y, c = _place()
        me = 4 * x + 2 * y + c
        all_ref[me] = p_ref[...]
        cps = []
        for k in range(1, N_DEV):
            px, py, pc = x ^ (k >> 2), y ^ ((k >> 1) & 1), c ^ (k & 1)
            cp = _remote(p_ref, all_ref.at[me], send_sems.at[k - 1], recv_sems.at[k - 1], (px, py, pc))
            cp.start()
            cps.append(cp)
        for k in range(1, N_DEV):
            peer = me ^ k
            _remote(p_ref, all_ref.at[peer], send_sems.at[k - 1], recv_sems.at[k - 1], (x, y, c)).wait_recv()
        for cp in cps:
            cp.wait_send()
        acc = all_ref[0]
        for d in range(1, N_DEV):
            acc = acc + all_ref[d]
        o_ref[...] = acc

    return pl.pallas_call(
        body, name="sum_small_grads",
        in_specs=[pl.BlockSpec(memory_space=pltpu.VMEM)],
        out_specs=pl.BlockSpec(memory_space=pltpu.VMEM),
        out_shape=jax.ShapeDtypeStruct((R, LANES), F32),
        scratch_shapes=[pltpu.VMEM((N_DEV, R, LANES), F32),
                        pltpu.SemaphoreType.DMA((N_DEV - 1,)), pltpu.SemaphoreType.DMA((N_DEV - 1,))],
        compiler_params=pltpu.CompilerParams(has_side_effects=True, vmem_limit_bytes=VMEM_LIMIT),
    )(part)


def _pack(parts):
    flat = jnp.concatenate([p.reshape(-1).astype(F32) for p in parts])
    n = flat.shape[0]
    rows = -(-n // LANES)
    rows = -(-rows // 8) * 8
    return jnp.pad(flat, (0, rows * LANES - n)).reshape(rows, LANES)


def _unpack(packed, shapes):
    flat = packed.reshape(-1)
    out, off = [], 0
    for s in shapes:
        n = int(np.prod(s))
        out.append(flat[off:off + n].reshape(s))
        off += n
    return out


def kernel(x, rel_bias, norm_mix_g, w_in, q_norm_g, k_norm_g, sinks, conv_w, conv_b, conv_ln_g, conv_ln_b, attn_out_g, conv_out_g, w_out, norm_mlp_g, w_mlp_up, w_mlp_down, loss_target, m_rel_bias, m_norm_mix_g, m_w_in, m_q_norm_g, m_k_norm_g, m_sinks, m_conv_w, m_conv_b, m_conv_ln_g, m_conv_ln_b, m_attn_out_g, m_conv_out_g, m_w_out, m_norm_mlp_g, m_w_mlp_up, m_w_mlp_down, v_rel_bias, v_norm_mix_g, v_w_in, v_q_norm_g, v_k_norm_g, v_sinks, v_conv_w, v_conv_b, v_conv_ln_g, v_conv_ln_b, v_attn_out_g, v_conv_out_g, v_w_out, v_norm_mlp_g, v_w_mlp_up, v_w_mlp_down):
    T = x.shape[1]
    L = DEPTH
    xi, yi, ci = _place()
    shard = 2 * xi + yi
    in_sh = IN_WIDTH // N_CHIPS
    out_sh = MIX_WIDTH // N_CHIPS
    ff_sh = D_FF // N_CHIPS
    cv_sh = CONV_WIDTH // N_CHIPS

    cw_pad = jnp.pad(conv_w, ((0, 0), (0, CONV_ROWS - CONV_KERNEL), (0, 0))).reshape(L * CONV_ROWS, cv_sh)
    g_in, g_out, g_up, g_down, g_cw = _gather_shards([
        w_in.astype(BF16).reshape(L * D_MODEL, in_sh),
        w_out.astype(BF16).reshape(L * out_sh, D_MODEL),
        w_mlp_up.astype(BF16).reshape(L * D_MODEL, ff_sh),
        w_mlp_down.astype(BF16).reshape(L * ff_sh, D_MODEL),
        cw_pad])
    W_in = g_in.reshape(N_CHIPS, L, D_MODEL, in_sh).transpose(1, 2, 0, 3).reshape(L, D_MODEL, IN_WIDTH)
    W_out = g_out.reshape(N_CHIPS, L, out_sh, D_MODEL).transpose(1, 0, 2, 3).reshape(L, MIX_WIDTH, D_MODEL)
    W_up = g_up.reshape(N_CHIPS, L, D_MODEL, ff_sh).transpose(1, 2, 0, 3).reshape(L, D_MODEL, D_FF)
    W_down = g_down.reshape(N_CHIPS, L, ff_sh, D_MODEL).transpose(1, 0, 2, 3).reshape(L, D_FF, D_MODEL)
    CW = g_cw.reshape(N_CHIPS, L, CONV_ROWS, cv_sh).transpose(1, 2, 0, 3).reshape(L, CONV_ROWS, CONV_WIDTH)

    bucket, in_window = _band()
    bk = jnp.asarray(np.where(in_window, bucket, -1))[None]
    biasm = jnp.full((N_HEADS, BLOCK, 2 * BLOCK), NEG, F32)
    for b in range(NUM_BUCKETS):
        biasm = jnp.where(bk == b, rel_bias[b][:, None, None], biasm)
    onehot_t = np.zeros((LANES, BLOCK * 2 * BLOCK), np.float32)
    onehot_t[bucket.reshape(-1), np.arange(BLOCK * 2 * BLOCK)] = 1.0
    onehot_t = jnp.asarray(onehot_t, dtype=BF16)

    row = lambda a, l: a[l][None, :]

    xs = x.reshape(T, D_MODEL)
    saved = []
    for l in range(L):
        h, z = _norm_matmul(xs, row(norm_mix_g, l), W_in, l, F32, "mix_in_proj")
        a = _attn_fwd(z, biasm, sinks[l], row(q_norm_g, l), row(k_norm_g, l))
        yc = _conv_fwd(z, CW[l], row(conv_b, l))
        mix = _mix_norm(a, yc, row(conv_ln_g, l), row(conv_ln_b, l), row(attn_out_g, l), row(conv_out_g, l))
        x1 = _matmul_res(mix, W_out, l, xs, False, "mix_out_proj")
        h2, up = _norm_matmul(x1, row(norm_mlp_g, l), W_up, l, BF16, "mlp_up_proj")
        x2 = _matmul_res(up, W_down, l, x1, True, "mlp_down_proj")
        saved.append((xs, h, z, a, yc, mix, x1, h2, up))
        xs = x2

    loss_parts, g = _loss_grad(xs, loss_target.reshape(T, D_MODEL))

    B_in = B_out = B_up = B_down = None
    small = [None] * L
    dbias_sum = None
    for l in reversed(range(L)):
        x0, h, z, a, yc, mix, x1, h2, up = saved[l]
        d_up = _dact(g, W_down, l, up)
        B_down = _matmul_tn(up, g, True, B_down, (N_CHIPS, L, ff_sh, D_MODEL), (None, None, ff_sh, D_MODEL),
                            lambda i, j: (i, l, 0, 0), ff_sh, D_MODEL, "grad_w_mlp_down")
        B_up = _matmul_tn(h2, d_up, False, B_up, (N_CHIPS, L, D_MODEL, ff_sh), (None, None, D_MODEL, ff_sh),
                          lambda i, j: (j, l, 0, 0), D_MODEL, ff_sh, "grad_w_mlp_up")
        g1, d_gmlp = _matmul_nt_normbwd(d_up, W_up, l, x1, row(norm_mlp_g, l), g, "mlp_in_bwd")
        d_a, d_y, sm_mix = _mix_bwd(g1, W_out, l, a, yc, row(conv_ln_g, l), row(conv_ln_b, l),
                                    row(attn_out_g, l), row(conv_out_g, l))
        B_out = _matmul_tn(mix, g1, False, B_out, (N_CHIPS, L, out_sh, D_MODEL), (N_CHIPS, None, out_sh, D_MODEL),
                           lambda i, j: (0, l, 0, 0), MIX_WIDTH, D_MODEL, "grad_w_out")
        d_u, d_gate, d_cw = _conv_bwd(d_y, z, CW[l])
        d_q, d_kv, dbias, sm_attn = _attn_bwd(z, d_a, biasm, sinks[l], row(q_norm_g, l), row(k_norm_g, l))
        dbias_sum = dbias if dbias_sum is None else dbias_sum + dbias
        d_z = jnp.concatenate([d_q, d_kv, d_u, d_gate], axis=1)
        B_in = _matmul_tn(h, d_z, False, B_in, (L, D_MODEL, IN_WIDTH), (None, D_MODEL, IN_WIDTH),
                          lambda i, j: (l, 0, 0), D_MODEL, IN_WIDTH, "grad_w_in")
        g, d_gmix = _matmul_nt_normbwd(d_z, W_in, l, x0, row(norm_mix_g, l), g1, "mix_in_bwd")
        small[l] = (d_gmix[0], sm_attn[0, :HEAD_DIM], sm_attn[1, :HEAD_DIM], sm_attn[2, :N_HEADS],
                    d_cw[:CONV_KERNEL], sm_mix[4], sm_mix[2], sm_mix[3], sm_mix[0], sm_mix[1], d_gmlp[0])
    grad_x = g.reshape(1, T, D_MODEL)

    d_rel = _bucket_reduce(dbias_sum.reshape(N_HEADS, BLOCK * 2 * BLOCK), onehot_t)[:, :NUM_BUCKETS].T
    stack = lambda k: jnp.stack([small[l][k] for l in range(L)])
    small_shapes = [(), (NUM_BUCKETS, N_HEADS), (L, D_MODEL), (L, HEAD_DIM), (L, HEAD_DIM), (L, N_HEADS),
                    (L, CONV_KERNEL, CONV_WIDTH), (L, CONV_WIDTH), (L, CONV_WIDTH), (L, CONV_WIDTH),
                    (L, CONV_WIDTH), (L, CONV_WIDTH), (L, D_MODEL)]
    part = _pack([jnp.sum(loss_parts[:, 0, 0]), d_rel] + [stack(k) for k in range(11)])
    tot = _unpack(_sum_devices(part), small_shapes)
    loss = tot[0]
    (g_rel, g_nmix, g_qn, g_kn, g_sk, g_cw_full, g_cb, g_lng, g_lnb, g_aog, g_cog, g_nmlp) = tot[1:]
    g_cw_sh = lax.dynamic_slice_in_dim(g_cw_full, shard * cv_sh, cv_sh, axis=2)

    small_w = [rel_bias, norm_mix_g, q_norm_g, k_norm_g, sinks, conv_w, conv_b, conv_ln_g, conv_ln_b,
               attn_out_g, conv_out_g, norm_mlp_g]
    small_m = [m_rel_bias, m_norm_mix_g, m_q_norm_g, m_k_norm_g, m_sinks, m_conv_w, m_conv_b, m_conv_ln_g,
               m_conv_ln_b, m_attn_out_g, m_conv_out_g, m_norm_mlp_g]
    small_v = [v_rel_bias, v_norm_mix_g, v_q_norm_g, v_k_norm_g, v_sinks, v_conv_w, v_conv_b, v_conv_ln_g,
               v_conv_ln_b, v_attn_out_g, v_conv_out_g, v_norm_mlp_g]
    small_g = [g_rel, g_nmix, g_qn, g_kn, g_sk, g_cw_sh, g_cb, g_lng, g_lnb, g_aog, g_cog, g_nmlp]
    shapes = [w.shape for w in small_w]
    sd, sm_, sv_ = _adamw(_pack(small_w), _pack(small_g), _pack(small_m), _pack(small_v), "adamw_small")
    small_d, small_nm, small_nv = _unpack(sd, shapes), _unpack(sm_, shapes), _unpack(sv_, shapes)

    G_in = B_in.reshape(L, D_MODEL, N_CHIPS, in_sh).transpose(2, 0, 1, 3).reshape(N_CHIPS, L * D_MODEL, in_sh)
    G = [G_in, B_out.reshape(N_CHIPS, L * out_sh, D_MODEL), B_up.reshape(N_CHIPS, L * D_MODEL, ff_sh),
         B_down.reshape(N_CHIPS, L * ff_sh, D_MODEL)]
    names = ["w_in", "w_out", "w_mlp_up", "w_mlp_down"]
    got = _swap_halves(G)
    own_sel = shard.astype(jnp.int32)[None]
    send_sel = jnp.stack([shard ^ 2, shard ^ 1, shard ^ 3]).astype(jnp.int32)
    owns, sends = [], []
    for b in range(4):
        hh = G[b].shape[1] // 2
        off = (ci * (hh // _tile(hh, 512))).astype(jnp.int32)[None]
        owns.append(_chip_sum(G[b], got[b], (off, own_sel), F32, "chip_sum_own_" + names[b]))
        sends.append(_chip_sum(G[b], got[b], (off, send_sel), BF16, "chip_sum_send_" + names[b]))
    arrived = _exchange_chips(sends)
    halves = [_shard_sum(owns[b], arrived[b], "shard_sum_" + names[b]) for b in range(4)]
    grads = _join_halves(halves)

    big_w = [w_in, w_out, w_mlp_up, w_mlp_down]
    big_m = [m_w_in, m_w_out, m_w_mlp_up, m_w_mlp_down]
    big_v = [v_w_in, v_w_out, v_w_mlp_up, v_w_mlp_down]
    big_g, big_d, big_nm, big_nv = [], [], [], []
    for b in range(4):
        shp = big_w[b].shape
        flat = lambda t: t.reshape(shp[0] * shp[1], shp[2])
        d, nm, nv = _adamw(flat(big_w[b]), grads[b], flat(big_m[b]), flat(big_v[b]), "adamw_" + names[b])
        big_g.append(grads[b].reshape(shp))
        big_d.append(d.reshape(shp))
        big_nm.append(nm.reshape(shp))
        big_nv.append(nv.reshape(shp))

    def ordered(sm, bg):
        return [sm[0], sm[1], bg[0], sm[2], sm[3], sm[4], sm[5], sm[6], sm[7], sm[8], sm[9], sm[10], bg[1], sm[11],
                bg[2], bg[3]]

    return (loss, grad_x, *ordered(small_g, big_g), *ordered(small_d, big_d), *ordered(small_nm, big_nm),
            *ordered(small_nv, big_nv))
```

```python
import math

import numpy as np
import jax
import jax.numpy as jnp
from jax import lax
from jax.experimental import pallas as pl
from jax.experimental.pallas import tpu as pltpu

F32 = jnp.float32
BF16 = jnp.bfloat16

D_MODEL = 1024
DEPTH = 4
HEAD_DIM = 64
N_HEADS = 8
N_KV_HEADS = 2
GQA_GROUP = N_HEADS // N_KV_HEADS
ATTN_WIDTH = N_HEADS * HEAD_DIM
KV_WIDTH = N_KV_HEADS * HEAD_DIM
CONV_WIDTH = D_MODEL - ATTN_WIDTH
MIX_WIDTH = ATTN_WIDTH + CONV_WIDTH
IN_WIDTH = ATTN_WIDTH + 2 * KV_WIDTH + 2 * CONV_WIDTH
BLOCK = 128
CONV_KERNEL = 31
CONV_ROWS = 32
HALO = 32
CONV_CH = 256
NUM_BUCKETS = 32
MAX_DISTANCE = 128
D_FF = 4 * D_MODEL
EPS = 1e-6
NEG = -1e30
SCALE = 1.0 / math.sqrt(HEAD_DIM)

ADAM_LR = 0.001
ADAM_B1 = 0.9
ADAM_B2 = 0.999
ADAM_EPS = 1e-08
ADAM_WD = 0.01
ADAM_STEP = 10

N_CHIPS = 4
N_DEV = 8
LANES = 128
VMEM_LIMIT = 52 * 1024 * 1024

Q0, K0, V0, U0, G0 = 0, ATTN_WIDTH, ATTN_WIDTH + KV_WIDTH, ATTN_WIDTH + 2 * KV_WIDTH, ATTN_WIDTH + 2 * KV_WIDTH + CONV_WIDTH

NT = (((1,), (1,)), ((), ()))
TN = (((0,), (0,)), ((), ()))
MESH = pl.DeviceIdType.MESH
ANY = pl.BlockSpec(memory_space=pl.ANY)


def _params(sem=None):
    return pltpu.CompilerParams(dimension_semantics=sem, vmem_limit_bytes=VMEM_LIMIT)


def _chunk(n):
    for c in range(1024, 0, -LANES):
        if n % c == 0:
            return c
    raise ValueError(n)


def _tile(t, want):
    return min(t, want)


def _t5_bucket(n):
    n = np.asarray(n)
    max_exact = NUM_BUCKETS // 2
    large = max_exact + (np.log(np.maximum(n, 1) / max_exact) / np.log(MAX_DISTANCE / max_exact)
                         * (NUM_BUCKETS - max_exact)).astype(np.int32)
    large = np.minimum(large, NUM_BUCKETS - 1)
    return np.where(n < max_exact, n, large).astype(np.int32)


def _band_buckets():
    qi = np.arange(BLOCK)[:, None]
    j = np.arange(BLOCK)[None, :]
    return _t5_bucket(np.where(j <= qi, qi - j, qi + BLOCK - j))


def _norm_matmul(x, g, w_all, l, out_dtype, name):
    T, D = x.shape
    N = w_all.shape[2]
    TM = _tile(T, 512)
    CH = _chunk(N)

    def body(x_ref, g_ref, w_ref, h_ref, z_ref):
        xv = x_ref[...]
        r = lax.rsqrt(jnp.mean(xv * xv, axis=-1, keepdims=True) + EPS)
        h = (xv * r * g_ref[...]).astype(BF16)
        h_ref[...] = h
        for c0 in range(0, N, CH):
            z_ref[:, c0:c0 + CH] = jnp.dot(h, w_ref[:, c0:c0 + CH], preferred_element_type=F32).astype(z_ref.dtype)

    return pl.pallas_call(
        body, name=name, grid=(T // TM,),
        in_specs=[pl.BlockSpec((TM, D), lambda i: (i, 0)),
                  pl.BlockSpec((1, D), lambda i: (0, 0)),
                  pl.BlockSpec((None, D, N), lambda i: (l, 0, 0))],
        out_specs=[pl.BlockSpec((TM, D), lambda i: (i, 0)),
                   pl.BlockSpec((TM, N), lambda i: (i, 0))],
        out_shape=[jax.ShapeDtypeStruct((T, D), BF16), jax.ShapeDtypeStruct((T, N), out_dtype)],
        compiler_params=_params(("parallel",)),
    )(x, g, w_all)


def _matmul_res(a, w_all, l, res, relu2, name):
    T, K = a.shape
    N = w_all.shape[2]
    TM = _tile(T, 512)
    CH = _chunk(K)

    def body(a_ref, w_ref, res_ref, o_ref):
        acc = res_ref[...]
        for k0 in range(0, K, CH):
            av = a_ref[:, k0:k0 + CH]
            if relu2:
                av = jnp.square(jnp.maximum(av.astype(F32), 0.0)).astype(BF16)
            acc = acc + jnp.dot(av, w_ref[k0:k0 + CH, :], preferred_element_type=F32)
        o_ref[...] = acc

    return pl.pallas_call(
        body, name=name, grid=(T // TM,),
        in_specs=[pl.BlockSpec((TM, K), lambda i: (i, 0)),
                  pl.BlockSpec((None, K, N), lambda i: (l, 0, 0)),
                  pl.BlockSpec((TM, N), lambda i: (i, 0))],
        out_specs=pl.BlockSpec((TM, N), lambda i: (i, 0)),
        out_shape=jax.ShapeDtypeStruct((T, N), F32),
        compiler_params=_params(("parallel",)),
    )(a, w_all, res)


def _head_norm(t, g):
    r = lax.rsqrt(jnp.mean(t * t, axis=-1, keepdims=True) + EPS)
    that = t * r
    return that * g, that, r


def _softmax_sink(s, sink):
    m = jnp.maximum(jnp.max(s, axis=-1, keepdims=True), sink)
    p = jnp.exp(s - m)
    es = jnp.exp(sink - m)
    den = jnp.sum(p, axis=-1, keepdims=True) + es
    return p / den, es / den


GROUP_ROWS = GQA_GROUP * BLOCK


def _own_block():
    row = lax.broadcasted_iota(jnp.int32, (GROUP_ROWS, BLOCK), 0)
    col = lax.broadcasted_iota(jnp.int32, (GROUP_ROWS, BLOCK), 1)
    return (row & (BLOCK - 1)) >= col


def _stack_heads(ref, kvh):
    return jnp.concatenate([ref[:, (kvh * GQA_GROUP + g) * HEAD_DIM:(kvh * GQA_GROUP + g + 1) * HEAD_DIM]
                            for g in range(GQA_GROUP)], axis=0)


def _band_probs(n, own, s_own, s_prev, bias, sink):
    s = jnp.where(own, s_own, s_prev) * SCALE + bias
    s = jnp.where(jnp.logical_or(own, n > 0), s, NEG)
    return _softmax_sink(s, sink)


def _attn_fwd(z, biasc, sink_rows, qg, kg):
    T = z.shape[0]
    nb = T // BLOCK
    kb, vb = K0 // KV_WIDTH, V0 // KV_WIDTH

    def body(q_ref, kc_ref, kp_ref, vc_ref, vp_ref, b_ref, sk_ref, qg_ref, kg_ref, a_ref):
        n = pl.program_id(0)
        own = _own_block()
        heads = range(N_KV_HEADS)
        cols = [slice(kvh * HEAD_DIM, (kvh + 1) * HEAD_DIM) for kvh in heads]
        kcn = [_head_norm(kc_ref[:, cs], kg_ref[...])[0].astype(BF16) for cs in cols]
        kpn = [_head_norm(kp_ref[:, cs], kg_ref[...])[0].astype(BF16) for cs in cols]
        qnb = [_head_norm(_stack_heads(q_ref, kvh), qg_ref[...])[0].astype(BF16) for kvh in heads]
        s_own = [lax.dot_general(qnb[k], kcn[k], NT, preferred_element_type=F32) for k in heads]
        s_prev = [lax.dot_general(qnb[k], kpn[k], NT, preferred_element_type=F32) for k in heads]
        p = [_band_probs(n, own, s_own[k], s_prev[k], b_ref[k], sk_ref[k])[0] for k in heads]
        p_own = [jnp.where(own, p[k], 0.0).astype(BF16) for k in heads]
        p_prev = [jnp.where(own, 0.0, p[k]).astype(BF16) for k in heads]
        o_own = [jnp.dot(p_own[k], vc_ref[:, cols[k]].astype(BF16), preferred_element_type=F32) for k in heads]
        o_prev = [jnp.dot(p_prev[k], vp_ref[:, cols[k]].astype(BF16), preferred_element_type=F32) for k in heads]
        for kvh in heads:
            o = o_own[kvh] + o_prev[kvh]
            for g in range(GQA_GROUP):
                h = kvh * GQA_GROUP + g
                a_ref[:, h * HEAD_DIM:(h + 1) * HEAD_DIM] = o[g * BLOCK:(g + 1) * BLOCK]

    cur = lambda n: n
    prev = lambda n: jnp.maximum(n - 1, 0)
    return pl.pallas_call(
        body, name="attn_fwd", grid=(nb,),
        in_specs=[pl.BlockSpec((BLOCK, ATTN_WIDTH), lambda n: (n, 0)),
                  pl.BlockSpec((BLOCK, KV_WIDTH), lambda n: (cur(n), kb)),
                  pl.BlockSpec((BLOCK, KV_WIDTH), lambda n: (prev(n), kb)),
                  pl.BlockSpec((BLOCK, KV_WIDTH), lambda n: (cur(n), vb)),
                  pl.BlockSpec((BLOCK, KV_WIDTH), lambda n: (prev(n), vb)),
                  pl.BlockSpec((N_KV_HEADS, GROUP_ROWS, BLOCK), lambda n: (0, 0, 0)),
                  pl.BlockSpec((N_KV_HEADS, GROUP_ROWS, 1), lambda n: (0, 0, 0)),
                  pl.BlockSpec((1, HEAD_DIM), lambda n: (0, 0)),
                  pl.BlockSpec((1, HEAD_DIM), lambda n: (0, 0))],
        out_specs=pl.BlockSpec((BLOCK, ATTN_WIDTH), lambda n: (n, 0)),
        out_shape=jax.ShapeDtypeStruct((T, ATTN_WIDTH), F32),
        compiler_params=_params(("parallel",)),
    )(z, z, z, z, z, biasc, sink_rows, qg, kg)


SHIFTS = 8
CONV_RC = 64


def _shifted_copies(src_ref, dst_ref, total):
    for b in range(SHIFTS):
        rows = (total - b) // SHIFTS * SHIFTS
        for r0 in range(0, rows, CONV_RC):
            nr = min(CONV_RC, rows - r0)
            dst_ref[b, pl.ds(r0, nr), :] = src_ref[pl.ds(r0 + b, nr), :]


def _tap(ref, r0, o):
    return ref[o % SHIFTS, pl.ds(r0 + (o // SHIFTS) * SHIFTS, CONV_RC), :]


def _conv_fwd(z, cw, cb):
    T = z.shape[0]
    TC = _tile(T, 512)
    ub, gb = U0 // CONV_CH, G0 // CONV_CH
    hpt = TC // HALO
    lead = HALO - (CONV_KERNEL - 1)

    def body(u_ref, g_ref, up_ref, gp_ref, w_ref, b_ref, y_ref, hp_ref, hs_ref):
        i = pl.program_id(0)
        hp_ref[pl.ds(0, HALO), :] = jnp.where(i > 0, up_ref[...] * jax.nn.sigmoid(gp_ref[...]), 0.0)
        hp_ref[pl.ds(HALO, TC), :] = u_ref[...] * jax.nn.sigmoid(g_ref[...])
        _shifted_copies(hp_ref, hs_ref, TC + HALO)
        for r0 in range(0, TC, CONV_RC):
            acc = jnp.zeros((CONV_RC, CONV_CH), F32) + b_ref[...]
            for j in range(CONV_KERNEL):
                acc = acc + _tap(hs_ref, r0, lead + j) * w_ref[pl.ds(j, 1), :]
            y_ref[pl.ds(r0, CONV_RC), :] = acc

    prev = lambda i: jnp.maximum(i * hpt - 1, 0)
    return pl.pallas_call(
        body, name="conv_fwd", grid=(T // TC, CONV_WIDTH // CONV_CH),
        in_specs=[pl.BlockSpec((TC, CONV_CH), lambda i, j: (i, ub + j)),
                  pl.BlockSpec((TC, CONV_CH), lambda i, j: (i, gb + j)),
                  pl.BlockSpec((HALO, CONV_CH), lambda i, j: (prev(i), ub + j)),
                  pl.BlockSpec((HALO, CONV_CH), lambda i, j: (prev(i), gb + j)),
                  pl.BlockSpec((CONV_ROWS, CONV_CH), lambda i, j: (0, j)),
                  pl.BlockSpec((1, CONV_CH), lambda i, j: (0, j))],
        out_specs=pl.BlockSpec((TC, CONV_CH), lambda i, j: (i, j)),
        out_shape=jax.ShapeDtypeStruct((T, CONV_WIDTH), F32),
        scratch_shapes=[pltpu.VMEM((TC + HALO, CONV_CH), F32), pltpu.VMEM((SHIFTS, TC + HALO, CONV_CH), F32)],
        compiler_params=_params(("parallel", "parallel")),
    )(z, z, z, z, cw, cb)


def _ln_silu(y, ln_g, ln_b):
    mu = jnp.mean(y, axis=-1, keepdims=True)
    yc = y - mu
    var = jnp.mean(yc * yc, axis=-1, keepdims=True)
    rstd = lax.rsqrt(var + EPS)
    yhat = yc * rstd
    yn = yhat * ln_g + ln_b
    sg = jax.nn.sigmoid(yn)
    return yn * sg, yn, sg, yhat, rstd


def _mix_norm(a, y, ln_g, ln_b, ag, cg):
    T = a.shape[0]
    TM = _tile(T, 512)

    def body(a_ref, y_ref, lg_ref, lb_ref, ag_ref, cg_ref, o_ref):
        av = a_ref[...]
        ra = lax.rsqrt(jnp.mean(av * av, axis=-1, keepdims=True) + EPS)
        o_ref[:, :ATTN_WIDTH] = (av * ra * ag_ref[...]).astype(BF16)
        c, _, _, _, _ = _ln_silu(y_ref[...], lg_ref[...], lb_ref[...])
        rc = lax.rsqrt(jnp.mean(c * c, axis=-1, keepdims=True) + EPS)
        o_ref[:, ATTN_WIDTH:] = (c * rc * cg_ref[...]).astype(BF16)

    vec = pl.BlockSpec((1, CONV_WIDTH), lambda i: (0, 0))
    return pl.pallas_call(
        body, name="mix_norm", grid=(T // TM,),
        in_specs=[pl.BlockSpec((TM, ATTN_WIDTH), lambda i: (i, 0)),
                  pl.BlockSpec((TM, CONV_WIDTH), lambda i: (i, 0)), vec, vec, vec, vec],
        out_specs=pl.BlockSpec((TM, MIX_WIDTH), lambda i: (i, 0)),
        out_shape=jax.ShapeDtypeStruct((T, MIX_WIDTH), BF16),
        compiler_params=_params(("parallel",)),
    )(a, y, ln_g, ln_b, ag, cg)


def _loss_grad(y, tgt):
    T, D = y.shape
    TM = _tile(T, 512)
    nt = T // TM

    def body(y_ref, t_ref, part_ref, dy_ref):
        diff = y_ref[...] - t_ref[...]
        dy_ref[...] = diff / D
        tok = jnp.mean(diff * diff, axis=-1, keepdims=True)
        part_ref[...] = jnp.zeros((1, LANES), F32) + 0.5 * jnp.sum(tok)

    return pl.pallas_call(
        body, name="loss_grad", grid=(nt,),
        in_specs=[pl.BlockSpec((TM, D), lambda i: (i, 0)), pl.BlockSpec((TM, D), lambda i: (i, 0))],
        out_specs=[pl.BlockSpec((None, 1, LANES), lambda i: (i, 0, 0)), pl.BlockSpec((TM, D), lambda i: (i, 0))],
        out_shape=[jax.ShapeDtypeStruct((nt, 1, LANES), F32), jax.ShapeDtypeStruct((T, D), F32)],
        compiler_params=_params(("parallel",)),
    )(y, tgt)


def _dact(g, w_all, l, up):
    T, N = g.shape
    K = w_all.shape[1]
    TM = _tile(T, 512)
    CH = _chunk(K)

    def body(g_ref, w_ref, up_ref, o_ref):
        gv = g_ref[...].astype(BF16)
        for k0 in range(0, K, CH):
            da = lax.dot_general(gv, w_ref[k0:k0 + CH, :], NT, preferred_element_type=F32)
            upv = up_ref[:, k0:k0 + CH].astype(F32)
            o_ref[:, k0:k0 + CH] = (da * (2.0 * jnp.maximum(upv, 0.0))).astype(BF16)

    return pl.pallas_call(
        body, name="mlp_dact", grid=(T // TM,),
        in_specs=[pl.BlockSpec((TM, N), lambda i: (i, 0)),
                  pl.BlockSpec((None, K, N), lambda i: (l, 0, 0)),
                  pl.BlockSpec((TM, K), lambda i: (i, 0))],
        out_specs=pl.BlockSpec((TM, K), lambda i: (i, 0)),
        out_shape=jax.ShapeDtypeStruct((T, K), BF16),
        compiler_params=_params(("parallel",)),
    )(g, w_all, up)


def _matmul_tn(a, b, relu2, buf, buf_shape, out_block, out_index, tm, tn, name):
    T, M = a.shape
    N = b.shape[1]
    TK = _tile(T, 1024)
    nk = T // TK

    def body(*refs):
        a_ref, b_ref = refs[0], refs[1]
        o_ref = refs[-1]
        k = pl.program_id(2)
        av = a_ref[...]
        if relu2:
            av = jnp.square(jnp.maximum(av.astype(F32), 0.0)).astype(BF16)
        c = lax.dot_general(av, b_ref[...].astype(BF16), TN, preferred_element_type=F32).reshape(o_ref.shape)

        @pl.when(k == 0)
        def _():
            o_ref[...] = c

        @pl.when(k > 0)
        def _():
            o_ref[...] += c

    in_specs = [pl.BlockSpec((TK, tm), lambda i, j, k: (k, i)), pl.BlockSpec((TK, tn), lambda i, j, k: (k, j))]
    args = [a, b]
    aliases = {}
    if buf is not None:
        in_specs.append(ANY)
        args.append(buf)
        aliases = {2: 0}
    return pl.pallas_call(
        body, name=name, grid=(M // tm, N // tn, nk),
        in_specs=in_specs,
        out_specs=pl.BlockSpec(out_block, lambda i, j, k: out_index(i, j)),
        out_shape=jax.ShapeDtypeStruct(buf_shape, F32),
        input_output_aliases=aliases,
        compiler_params=_params(("parallel", "parallel", "arbitrary")),
    )(*args)


def _matmul_nt_normbwd(dz, w_all, l, x, gvec, gres, name):
    T, K = dz.shape
    D = x.shape[1]
    TM = _tile(T, 512)
    CH = _chunk(K)

    def body(dz_ref, w_ref, x_ref, gv_ref, gr_ref, o_ref, dg_ref):
        i = pl.program_id(0)
        dh = jnp.zeros((TM, D), F32)
        for k0 in range(0, K, CH):
            dh = dh + lax.dot_general(dz_ref[:, k0:k0 + CH], w_ref[:, k0:k0 + CH], NT, preferred_element_type=F32)
        xv = x_ref[...]
        r = lax.rsqrt(jnp.mean(xv * xv, axis=-1, keepdims=True) + EPS)
        xhat = xv * r
        dg = jnp.sum(dh * xhat, axis=0, keepdims=True)

        @pl.when(i == 0)
        def _():
            dg_ref[...] = dg

        @pl.when(i > 0)
        def _():
            dg_ref[...] += dg

        wv = dh * gv_ref[...]
        o_ref[...] = gr_ref[...] + r * (wv - xhat * jnp.mean(wv * xhat, axis=-1, keepdims=True))

    return pl.pallas_call(
        body, name=name, grid=(T // TM,),
        in_specs=[pl.BlockSpec((TM, K), lambda i: (i, 0)),
                  pl.BlockSpec((None, D, K), lambda i: (l, 0, 0)),
                  pl.BlockSpec((TM, D), lambda i: (i, 0)),
                  pl.BlockSpec((1, D), lambda i: (0, 0)),
                  pl.BlockSpec((TM, D), lambda i: (i, 0))],
        out_specs=[pl.BlockSpec((TM, D), lambda i: (i, 0)), pl.BlockSpec((1, D), lambda i: (0, 0))],
        out_shape=[jax.ShapeDtypeStruct((T, D), F32), jax.ShapeDtypeStruct((1, D), F32)],
        compiler_params=_params(("arbitrary",)),
    )(dz, w_all, x, gvec, gres)


def _mix_bwd(g1, w_all, l, a, y, ln_g, ln_b, ag, cg):
    T, D = g1.shape
    TM = _tile(T, 512)

    def body(g_ref, w_ref, a_ref, y_ref, lg_ref, lb_ref, ag_ref, cg_ref, da_ref, dy_ref, sm_ref):
        i = pl.program_id(0)
        dmix = lax.dot_general(g_ref[...].astype(BF16), w_ref[...], NT, preferred_element_type=F32)
        dma, dmc = dmix[:, :ATTN_WIDTH], dmix[:, ATTN_WIDTH:]
        av = a_ref[...]
        ra = lax.rsqrt(jnp.mean(av * av, axis=-1, keepdims=True) + EPS)
        ahat = av * ra
        d_ag = jnp.sum(dma * ahat, axis=0, keepdims=True)
        wa = dma * ag_ref[...]
        da_ref[...] = ra * (wa - ahat * jnp.mean(wa * ahat, axis=-1, keepdims=True))

        c, yn, sg, yhat, rstd = _ln_silu(y_ref[...], lg_ref[...], lb_ref[...])
        rc = lax.rsqrt(jnp.mean(c * c, axis=-1, keepdims=True) + EPS)
        chat = c * rc
        d_cg = jnp.sum(dmc * chat, axis=0, keepdims=True)
        wc = dmc * cg_ref[...]
        dc = rc * (wc - chat * jnp.mean(wc * chat, axis=-1, keepdims=True))
        dyn = dc * (sg * (1.0 + yn * (1.0 - sg)))
        d_lg = jnp.sum(dyn * yhat, axis=0, keepdims=True)
        d_lb = jnp.sum(dyn, axis=0, keepdims=True)
        dyh = dyn * lg_ref[...]
        dy = rstd * (dyh - jnp.mean(dyh, axis=-1, keepdims=True) - yhat * jnp.mean(dyh * yhat, axis=-1, keepdims=True))
        dy_ref[...] = dy
        d_cb = jnp.sum(dy, axis=0, keepdims=True)
        sums = jnp.concatenate([d_ag, d_cg, d_lg, d_lb, d_cb, jnp.zeros((3, CONV_WIDTH), F32)], axis=0)

        @pl.when(i == 0)
        def _():
            sm_ref[...] = sums

        @pl.when(i > 0)
        def _():
            sm_ref[...] += sums

    vec = pl.BlockSpec((1, CONV_WIDTH), lambda i: (0, 0))
    return pl.pallas_call(
        body, name="mix_bwd", grid=(T // TM,),
        in_specs=[pl.BlockSpec((TM, D), lambda i: (i, 0)),
                  pl.BlockSpec((None, MIX_WIDTH, D), lambda i: (l, 0, 0)),
                  pl.BlockSpec((TM, ATTN_WIDTH), lambda i: (i, 0)),
                  pl.BlockSpec((TM, CONV_WIDTH), lambda i: (i, 0)), vec, vec, vec, vec],
        out_specs=[pl.BlockSpec((TM, ATTN_WIDTH), lambda i: (i, 0)),
                   pl.BlockSpec((TM, CONV_WIDTH), lambda i: (i, 0)),
                   pl.BlockSpec((8, CONV_WIDTH), lambda i: (0, 0))],
        out_shape=[jax.ShapeDtypeStruct((T, ATTN_WIDTH), F32), jax.ShapeDtypeStruct((T, CONV_WIDTH), F32),
                   jax.ShapeDtypeStruct((8, CONV_WIDTH), F32)],
        compiler_params=_params(("arbitrary",)),
    )(g1, w_all, a, y, ln_g, ln_b, ag, cg)


def _conv_bwd(dy, z, cw):
    T = z.shape[0]
    TC = _tile(T, 512)
    nt = T // TC
    ub, gb = U0 // CONV_CH, G0 // CONV_CH
    nch = CONV_WIDTH // CONV_CH
    hpt = TC // HALO

    lead = HALO - (CONV_KERNEL - 1)

    def body(dy_ref, dyn_ref, u_ref, g_ref, up_ref, gp_ref, w_ref, du_ref, dg_ref, dw_ref,
             hp_ref, hs_ref, dyp_ref, dys_ref):
        i = pl.program_id(1)
        hp_ref[pl.ds(0, HALO), :] = jnp.where(i > 0, up_ref[...] * jax.nn.sigmoid(gp_ref[...]), 0.0)
        hp_ref[pl.ds(HALO, TC), :] = u_ref[...] * jax.nn.sigmoid(g_ref[...])
        _shifted_copies(hp_ref, hs_ref, TC + HALO)
        dyp_ref[pl.ds(0, TC), :] = dy_ref[...]
        dyp_ref[pl.ds(TC, HALO), :] = jnp.where(i < nt - 1, dyn_ref[...], 0.0)
        _shifted_copies(dyp_ref, dys_ref, TC + HALO)

        @pl.when(i == 0)
        def _():
            dw_ref[...] = jnp.zeros((CONV_ROWS, CONV_CH), F32)

        for r0 in range(0, TC, CONV_RC):
            rows = pl.ds(r0, CONV_RC)
            dh = jnp.zeros((CONV_RC, CONV_CH), F32)
            for j in range(CONV_KERNEL):
                dh = dh + _tap(dys_ref, r0, CONV_KERNEL - 1 - j) * w_ref[pl.ds(j, 1), :]
            uv = u_ref[rows, :]
            sg = jax.nn.sigmoid(g_ref[rows, :])
            du_ref[rows, :] = (dh * sg).astype(BF16)
            dg_ref[rows, :] = (dh * uv * sg * (1.0 - sg)).astype(BF16)
        for j in range(CONV_KERNEL):
            acc = jnp.zeros((SHIFTS, CONV_CH), F32)
            for r0 in range(0, TC, CONV_RC):
                prod = dy_ref[pl.ds(r0, CONV_RC), :] * _tap(hs_ref, r0, lead + j)
                acc = acc + jnp.sum(prod.reshape(CONV_RC // SHIFTS, SHIFTS, CONV_CH), axis=0)
            dw_ref[pl.ds(j, 1), :] += jnp.sum(acc, axis=0, keepdims=True)

    prev = lambda i: jnp.maximum(i * hpt - 1, 0)
    nxt = lambda i: jnp.minimum((i + 1) * hpt, T // HALO - 1)
    return pl.pallas_call(
        body, name="conv_bwd", grid=(nch, nt),
        in_specs=[pl.BlockSpec((TC, CONV_CH), lambda j, i: (i, j)),
                  pl.BlockSpec((HALO, CONV_CH), lambda j, i: (nxt(i), j)),
                  pl.BlockSpec((TC, CONV_CH), lambda j, i: (i, ub + j)),
                  pl.BlockSpec((TC, CONV_CH), lambda j, i: (i, gb + j)),
                  pl.BlockSpec((HALO, CONV_CH), lambda j, i: (prev(i), ub + j)),
                  pl.BlockSpec((HALO, CONV_CH), lambda j, i: (prev(i), gb + j)),
                  pl.BlockSpec((CONV_ROWS, CONV_CH), lambda j, i: (0, j))],
        out_specs=[pl.BlockSpec((TC, CONV_CH), lambda j, i: (i, j)),
                   pl.BlockSpec((TC, CONV_CH), lambda j, i: (i, j)),
                   pl.BlockSpec((CONV_ROWS, CONV_CH), lambda j, i: (0, j))],
        out_shape=[jax.ShapeDtypeStruct((T, CONV_WIDTH), BF16), jax.ShapeDtypeStruct((T, CONV_WIDTH), BF16),
                   jax.ShapeDtypeStruct((CONV_ROWS, CONV_WIDTH), F32)],
        scratch_shapes=[pltpu.VMEM((TC + HALO, CONV_CH), F32), pltpu.VMEM((SHIFTS, TC + HALO, CONV_CH), F32),
                        pltpu.VMEM((TC + HALO, CONV_CH), F32), pltpu.VMEM((SHIFTS, TC + HALO, CONV_CH), F32)],
        compiler_params=_params(("parallel", "arbitrary")),
    )(dy, dy, z, z, z, z, cw)


def _attn_bwd(z, da, biasc, sink_rows, qg, kg):
    T = z.shape[0]
    nb = T // BLOCK
    kb, vb = K0 // KV_WIDTH, V0 // KV_WIDTH

    def body(q_ref, kc_ref, kp_ref, vc_ref, vp_ref, da_ref, b_ref, sk_ref, qg_ref, kg_ref,
             dq_ref, dkv_ref, db_ref, sm_ref, ck_ref, cv_ref, pk_ref, pv_ref, nk_ref, nv_ref):
        n = pl.program_id(0)
        lane = lax.broadcasted_iota(jnp.int32, (1, LANES), 1)

        @pl.when(n == 0)
        def _():
            db_ref[...] = jnp.zeros(db_ref.shape, F32)
            sm_ref[...] = jnp.zeros(sm_ref.shape, F32)
            ck_ref[...] = jnp.zeros(ck_ref.shape, F32)
            cv_ref[...] = jnp.zeros(cv_ref.shape, F32)

        pk_ref[...] = jnp.zeros(pk_ref.shape, F32)
        pv_ref[...] = jnp.zeros(pv_ref.shape, F32)

        @pl.when(n < nb)
        def _():
            own = _own_block()
            heads = range(N_KV_HEADS)
            cols = [slice(kvh * HEAD_DIM, (kvh + 1) * HEAD_DIM) for kvh in heads]
            dot_nt = lambda a, b: lax.dot_general(a, b, NT, preferred_element_type=F32)
            dot_tn = lambda a, b: lax.dot_general(a, b, TN, preferred_element_type=F32)
            kcn = [_head_norm(kc_ref[:, cs], kg_ref[...])[0].astype(BF16) for cs in cols]
            kpn = [_head_norm(kp_ref[:, cs], kg_ref[...])[0].astype(BF16) for cs in cols]
            qnorm = [_head_norm(_stack_heads(q_ref, k), qg_ref[...]) for k in heads]
            qnb = [qnorm[k][0].astype(BF16) for k in heads]
            dob = [_stack_heads(da_ref, k).astype(BF16) for k in heads]
            s_own = [dot_nt(qnb[k], kcn[k]) for k in heads]
            s_prev = [dot_nt(qnb[k], kpn[k]) for k in heads]
            dp_own = [dot_nt(dob[k], vc_ref[:, cols[k]].astype(BF16)) for k in heads]
            dp_prev = [dot_nt(dob[k], vp_ref[:, cols[k]].astype(BF16)) for k in heads]
            probs = [_band_probs(n, own, s_own[k], s_prev[k], b_ref[k], sk_ref[k]) for k in heads]
            ds_own, ds_prev, p_own, p_prev = [], [], [], []
            dsk = jnp.zeros((1, LANES), F32)
            for k in heads:
                p, psink = probs[k]
                dp = jnp.where(own, dp_own[k], dp_prev[k])
                delta = jnp.sum(p * dp, axis=-1, keepdims=True)
                ds = p * (dp - delta)
                db_ref[k] += ds
                dsink = psink * delta
                for g in range(GQA_GROUP):
                    dsk = dsk + jnp.where(lane == k * GQA_GROUP + g, -jnp.sum(dsink[g * BLOCK:(g + 1) * BLOCK]), 0.0)
                ds_own.append(jnp.where(own, ds, 0.0).astype(BF16))
                ds_prev.append(jnp.where(own, 0.0, ds).astype(BF16))
                p_own.append(jnp.where(own, p, 0.0).astype(BF16))
                p_prev.append(jnp.where(own, 0.0, p).astype(BF16))
            dqn_own = [jnp.dot(ds_own[k], kcn[k], preferred_element_type=F32) for k in heads]
            dqn_prev = [jnp.dot(ds_prev[k], kpn[k], preferred_element_type=F32) for k in heads]
            for k in heads:
                nk_ref[:, cols[k]] = dot_tn(ds_own[k], qnb[k]) * SCALE
                pk_ref[:, cols[k]] = dot_tn(ds_prev[k], qnb[k]) * SCALE
                nv_ref[:, cols[k]] = dot_tn(p_own[k], dob[k])
                pv_ref[:, cols[k]] = dot_tn(p_prev[k], dob[k])
            dqg = jnp.zeros((1, HEAD_DIM), F32)
            for k in heads:
                _, qhat, rq = qnorm[k]
                dqn = (dqn_own[k] + dqn_prev[k]) * SCALE
                dqg = dqg + jnp.sum(dqn * qhat, axis=0, keepdims=True)
                wq = dqn * qg_ref[...]
                dq = (rq * (wq - qhat * jnp.mean(wq * qhat, axis=-1, keepdims=True))).astype(BF16)
                for g in range(GQA_GROUP):
                    h = k * GQA_GROUP + g
                    dq_ref[:, h * HEAD_DIM:(h + 1) * HEAD_DIM] = dq[g * BLOCK:(g + 1) * BLOCK]
            sm_ref[pl.ds(0, 1), pl.ds(0, HEAD_DIM)] += dqg
            sm_ref[pl.ds(2, 1), :] += dsk

        @pl.when(n >= 1)
        def _():
            dkt = ck_ref[...] + pk_ref[...]
            kpv = kp_ref[...]
            dkg = jnp.zeros((1, HEAD_DIM), F32)
            for kvh in range(N_KV_HEADS):
                cs = slice(kvh * HEAD_DIM, (kvh + 1) * HEAD_DIM)
                _, khat, rk = _head_norm(kpv[:, cs], kg_ref[...])
                dk = dkt[:, cs]
                dkg = dkg + jnp.sum(dk * khat, axis=0, keepdims=True)
                wk = dk * kg_ref[...]
                dkv_ref[:, cs] = (rk * (wk - khat * jnp.mean(wk * khat, axis=-1, keepdims=True))).astype(BF16)
            dkv_ref[:, KV_WIDTH:] = (cv_ref[...] + pv_ref[...]).astype(BF16)
            sm_ref[pl.ds(1, 1), pl.ds(0, HEAD_DIM)] += dkg

        ck_ref[...] = nk_ref[...]
        cv_ref[...] = nv_ref[...]

    cur = lambda n: jnp.minimum(n, nb - 1)
    prev = lambda n: jnp.maximum(n - 1, 0)
    carry = pltpu.VMEM((BLOCK, KV_WIDTH), F32)
    return pl.pallas_call(
        body, name="attn_bwd", grid=(nb + 1,),
        in_specs=[pl.BlockSpec((BLOCK, ATTN_WIDTH), lambda n: (cur(n), 0)),
                  pl.BlockSpec((BLOCK, KV_WIDTH), lambda n: (cur(n), kb)),
                  pl.BlockSpec((BLOCK, KV_WIDTH), lambda n: (prev(n), kb)),
                  pl.BlockSpec((BLOCK, KV_WIDTH), lambda n: (cur(n), vb)),
                  pl.BlockSpec((BLOCK, KV_WIDTH), lambda n: (prev(n), vb)),
                  pl.BlockSpec((BLOCK, ATTN_WIDTH), lambda n: (cur(n), 0)),
                  pl.BlockSpec((N_KV_HEADS, GROUP_ROWS, BLOCK), lambda n: (0, 0, 0)),
                  pl.BlockSpec((N_KV_HEADS, GROUP_ROWS, 1), lambda n: (0, 0, 0)),
                  pl.BlockSpec((1, HEAD_DIM), lambda n: (0, 0)),
                  pl.BlockSpec((1, HEAD_DIM), lambda n: (0, 0))],
        out_specs=[pl.BlockSpec((BLOCK, ATTN_WIDTH), lambda n: (cur(n), 0)),
                   pl.BlockSpec((BLOCK, 2 * KV_WIDTH), lambda n: (prev(n), 0)),
                   pl.BlockSpec((N_KV_HEADS, GROUP_ROWS, BLOCK), lambda n: (0, 0, 0)),
                   pl.BlockSpec((8, LANES), lambda n: (0, 0))],
        out_shape=[jax.ShapeDtypeStruct((T, ATTN_WIDTH), BF16), jax.ShapeDtypeStruct((T, 2 * KV_WIDTH), BF16),
                   jax.ShapeDtypeStruct((N_KV_HEADS, GROUP_ROWS, BLOCK), F32), jax.ShapeDtypeStruct((8, LANES), F32)],
        scratch_shapes=[carry] * 6,
        compiler_params=_params(("arbitrary",)),
    )(z, z, z, z, z, da, biasc, sink_rows, qg, kg)


def _bucket_reduce(dbias, onehot_t):
    def body(d_ref, oh_ref, o_ref):
        d = d_ref[...]
        hi = d.astype(BF16)
        r1 = d - hi.astype(F32)
        mid = r1.astype(BF16)
        lo = (r1 - mid.astype(F32)).astype(BF16)
        oh = oh_ref[...]
        acc = lax.dot_general(lo, oh, NT, preferred_element_type=F32)
        acc = acc + lax.dot_general(mid, oh, NT, preferred_element_type=F32)
        o_ref[...] = acc + lax.dot_general(hi, oh, NT, preferred_element_type=F32)

    return pl.pallas_call(
        body, name="bucket_reduce",
        out_shape=jax.ShapeDtypeStruct((N_HEADS, LANES), F32),
        compiler_params=_params(),
    )(dbias, onehot_t)


def _adamw(w, g, m, v, name):
    R, C = w.shape
    TR = _tile(R, 512)

    def body(w_ref, g_ref, m_ref, v_ref, d_ref, nm_ref, nv_ref):
        gv = g_ref[...]
        mn = ADAM_B1 * m_ref[...] + (1.0 - ADAM_B1) * gv
        vn = ADAM_B2 * v_ref[...] + (1.0 - ADAM_B2) * jnp.square(gv)
        m_hat = mn / (1.0 - ADAM_B1 ** ADAM_STEP)
        v_hat = vn / (1.0 - ADAM_B2 ** ADAM_STEP)
        d_ref[...] = -ADAM_LR * (m_hat / (jnp.sqrt(v_hat) + ADAM_EPS) + ADAM_WD * w_ref[...])
        nm_ref[...] = mn
        nv_ref[...] = vn

    spec = pl.BlockSpec((TR, C), lambda i: (i, 0))
    shp = jax.ShapeDtypeStruct((R, C), F32)
    return pl.pallas_call(
        body, name=name, grid=(R // TR,),
        in_specs=[spec] * 4, out_specs=[spec] * 3, out_shape=[shp] * 3,
        compiler_params=_params(("parallel",)),
    )(w, g, m, v)


def _place():
    return lax.axis_index("x"), lax.axis_index("y"), lax.axis_index("c")


def _other_chips(x, y):
    return [(1 - x, y), (x, 1 - y), (1 - x, 1 - y)]


def _remote(src, dst, send_sem, recv_sem, dev):
    return pltpu.make_async_remote_copy(src_ref=src, dst_ref=dst, send_sem=send_sem, recv_sem=recv_sem,
                                        device_id=dev, device_id_type=MESH)


def _gather_shards(bufs):
    nbuf = len(bufs)

    def body(*refs):
        ins, outs = refs[:nbuf], refs[nbuf:2 * nbuf]
        send_sems, recv_sems = refs[2 * nbuf:]
        x, y, c = _place()
        me = 2 * x + y
        sib = (x, y, 1 - c)
        chips = _other_chips(x, y)
        started = []
        for b in range(nbuf):
            hh = bufs[b].shape[0] // 2
            for j, (cx, cy) in enumerate(chips):
                k = 6 * b + j
                cp = _remote(ins[b].at[pl.ds(c * hh, hh), :], outs[b].at[me, pl.ds(c * hh, hh), :],
                             send_sems.at[k], recv_sems.at[k], (cx, cy, c))
                cp.start()
                started.append(cp)
        for b in range(nbuf):
            hh = bufs[b].shape[0] // 2
            for j, (cx, cy) in enumerate(chips):
                rows = outs[b].at[2 * cx + cy, pl.ds(c * hh, hh), :]
                _remote(rows, rows, send_sems.at[6 * b + j], recv_sems.at[6 * b + j], sib).wait_recv()
                k = 6 * b + 3 + j
                cp = _remote(rows, rows, send_sems.at[k], recv_sems.at[k], sib)
                cp.start()
                started.append(cp)
        for b in range(nbuf):
            hh = bufs[b].shape[0] // 2
            for j, (cx, cy) in enumerate(chips):
                rows = outs[b].at[2 * cx + cy, pl.ds((1 - c) * hh, hh), :]
                k = 6 * b + 3 + j
                _remote(rows, rows, send_sems.at[k], recv_sems.at[k], sib).wait_recv()
        for cp in started:
            cp.wait_send()

    return pl.pallas_call(
        body, name="gather_weights",
        in_specs=[ANY] * nbuf, out_specs=[ANY] * nbuf,
        out_shape=[jax.ShapeDtypeStruct((N_CHIPS,) + b.shape, b.dtype) for b in bufs],
        scratch_shapes=[pltpu.SemaphoreType.DMA((6 * nbuf,)), pltpu.SemaphoreType.DMA((6 * nbuf,))],
        compiler_params=pltpu.CompilerParams(has_side_effects=True),
    )(*bufs)


def _swap_halves(bufs):
    nbuf = len(bufs)

    def body(*refs):
        ins, outs = refs[:nbuf], refs[nbuf:2 * nbuf]
        send_sems, recv_sems = refs[2 * nbuf:]
        x, y, c = _place()
        sib = (x, y, 1 - c)
        cps = []
        for b in range(nbuf):
            hh = bufs[b].shape[1] // 2
            cp = _remote(ins[b].at[:, pl.ds((1 - c) * hh, hh), :], outs[b], send_sems.at[b], recv_sems.at[b], sib)
            cp.start()
            cps.append(cp)
        for cp in cps:
            cp.wait()

    return pl.pallas_call(
        body, name="grad_swap_halves",
        in_specs=[ANY] * nbuf, out_specs=[ANY] * nbuf,
        out_shape=[jax.ShapeDtypeStruct((N_CHIPS, b.shape[1] // 2, b.shape[2]), b.dtype) for b in bufs],
        scratch_shapes=[pltpu.SemaphoreType.DMA((nbuf,)), pltpu.SemaphoreType.DMA((nbuf,))],
        compiler_params=pltpu.CompilerParams(has_side_effects=True),
    )(*bufs)


def _chip_sum(g, got, sel, out_dtype, name):
    _, R, C = g.shape
    hh = R // 2
    TR = _tile(hh, 512)
    nslot = sel[1].shape[0]

    def body(off_ref, sh_ref, g_ref, r_ref, o_ref):
        o_ref[...] = (g_ref[...] + r_ref[...]).astype(out_dtype)

    return pl.pallas_call(
        body, name=name,
        grid_spec=pltpu.PrefetchScalarGridSpec(
            num_scalar_prefetch=2, grid=(nslot, hh // TR),
            in_specs=[pl.BlockSpec((None, TR, C), lambda s, i, off, sh: (sh[s], off[0] + i, 0)),
                      pl.BlockSpec((None, TR, C), lambda s, i, off, sh: (sh[s], i, 0))],
            out_specs=pl.BlockSpec((None, TR, C), lambda s, i, off, sh: (s, i, 0))),
        out_shape=jax.ShapeDtypeStruct((nslot, hh, C), out_dtype),
        compiler_params=_params(("parallel", "parallel")),
    )(sel[0], sel[1], g, got)


def _exchange_chips(bufs):
    nbuf = len(bufs)

    def body(*refs):
        ins, outs = refs[:nbuf], refs[nbuf:2 * nbuf]
        send_sems, recv_sems = refs[2 * nbuf:]
        x, y, c = _place()
        cps = []
        for b in range(nbuf):
            for j, (cx, cy) in enumerate(_other_chips(x, y)):
                k = 3 * b + j
                cp = _remote(ins[b].at[j], outs[b].at[j], send_sems.at[k], recv_sems.at[k], (cx, cy, c))
                cp.start()
                cps.append(cp)
        for cp in cps:
            cp.wait()

    return pl.pallas_call(
        body, name="grad_exchange_chips",
        in_specs=[ANY] * nbuf, out_specs=[ANY] * nbuf,
        out_shape=[jax.ShapeDtypeStruct(b.shape, b.dtype) for b in bufs],
        scratch_shapes=[pltpu.SemaphoreType.DMA((3 * nbuf,)), pltpu.SemaphoreType.DMA((3 * nbuf,))],
        compiler_params=pltpu.CompilerParams(has_side_effects=True),
    )(*bufs)


def _shard_sum(own, got, off, name):
    _, hh, C = own.shape
    TR = _tile(hh, 512)

    def body(off_ref, o_ref, r_ref, out_ref):
        acc = o_ref[...]
        for j in range(N_CHIPS - 1):
            acc = acc + r_ref[j].astype(F32)
        out_ref[...] = acc

    return pl.pallas_call(
        body, name=name,
        grid_spec=pltpu.PrefetchScalarGridSpec(
            num_scalar_prefetch=1, grid=(hh // TR,),
            in_specs=[pl.BlockSpec((None, TR, C), lambda i, off: (0, i, 0)),
                      pl.BlockSpec((N_CHIPS - 1, TR, C), lambda i, off: (0, i, 0))],
            out_specs=pl.BlockSpec((TR, C), lambda i, off: (off[0] + i, 0))),
        out_shape=jax.ShapeDtypeStruct((2 * hh, C), F32),
        compiler_params=_params(("parallel",)),
    )(off, own, got)


def _join_halves(bufs):
    nbuf = len(bufs)

    def body(*refs):
        outs = refs[nbuf:2 * nbuf]
        send_sems, recv_sems = refs[2 * nbuf:]
        x, y, c = _place()
        sib = (x, y, 1 - c)
        cps = []
        for b in range(nbuf):
            hh = bufs[b].shape[0] // 2
            rows = outs[b].at[pl.ds(c * hh, hh), :]
            cp = _remote(rows, rows, send_sems.at[b], recv_sems.at[b], sib)
            cp.start()
            cps.append(cp)
        for b, cp in enumerate(cps):
            hh = bufs[b].shape[0] // 2
            theirs = outs[b].at[pl.ds((1 - c) * hh, hh), :]
            _remote(theirs, theirs, send_sems.at[b], recv_sems.at[b], sib).wait_recv()
            cp.wait_send()

    return pl.pallas_call(
        body, name="grad_join_halves",
        in_specs=[ANY] * nbuf, out_specs=[ANY] * nbuf,
        out_shape=[jax.ShapeDtypeStruct(b.shape, b.dtype) for b in bufs],
        input_output_aliases={b: b for b in range(nbuf)},
        scratch_shapes=[pltpu.SemaphoreType.DMA((nbuf,)), pltpu.SemaphoreType.DMA((nbuf,))],
        compiler_params=pltpu.CompilerParams(has_side_effects=True),
    )(*bufs)


def _sum_devices(part):
    R = part.shape[0]

    def body(p_ref, o_ref, all_ref, send_sems, recv_sems):
        x, y, c = _place()
        me = 4 * x + 2 * y + c
        all_ref[me] = p_ref[...]
        cps = []
        for k in range(1, N_DEV):
            px, py, pc = x ^ (k >> 2), y ^ ((k >> 1) & 1), c ^ (k & 1)
            cp = _remote(p_ref, all_ref.at[me], send_sems.at[k - 1], recv_sems.at[k - 1], (px, py, pc))
            cp.start()
            cps.append(cp)
        for k in range(1, N_DEV):
            peer = me ^ k
            _remote(p_ref, all_ref.at[peer], send_sems.at[k - 1], recv_sems.at[k - 1], (x, y, c)).wait_recv()
        for cp in cps:
            cp.wait_send()
        acc = all_ref[0]
        for d in range(1, N_DEV):
            acc = acc + all_ref[d]
        o_ref[...] = acc

    return pl.pallas_call(
        body, name="sum_small_grads",
        in_specs=[pl.BlockSpec(memory_space=pltpu.VMEM)],
        out_specs=pl.BlockSpec(memory_space=pltpu.VMEM),
        out_shape=jax.ShapeDtypeStruct((R, LANES), F32),
        scratch_shapes=[pltpu.VMEM((N_DEV, R, LANES), F32),
                        pltpu.SemaphoreType.DMA((N_DEV - 1,)), pltpu.SemaphoreType.DMA((N_DEV - 1,))],
        compiler_params=pltpu.CompilerParams(has_side_effects=True, vmem_limit_bytes=VMEM_LIMIT),
    )(part)


def _pack(parts):
    flat = jnp.concatenate([p.reshape(-1).astype(F32) for p in parts])
    n = flat.shape[0]
    rows = -(-n // LANES)
    rows = -(-rows // 8) * 8
    return jnp.pad(flat, (0, rows * LANES - n)).reshape(rows, LANES)


def _unpack(packed, shapes):
    flat = packed.reshape(-1)
    out, off = [], 0
    for s in shapes:
        n = int(np.prod(s))
        out.append(flat[off:off + n].reshape(s))
        off += n
    return out


def kernel(x, rel_bias, norm_mix_g, w_in, q_norm_g, k_norm_g, sinks, conv_w, conv_b, conv_ln_g, conv_ln_b, attn_out_g, conv_out_g, w_out, norm_mlp_g, w_mlp_up, w_mlp_down, loss_target, m_rel_bias, m_norm_mix_g, m_w_in, m_q_norm_g, m_k_norm_g, m_sinks, m_conv_w, m_conv_b, m_conv_ln_g, m_conv_ln_b, m_attn_out_g, m_conv_out_g, m_w_out, m_norm_mlp_g, m_w_mlp_up, m_w_mlp_down, v_rel_bias, v_norm_mix_g, v_w_in, v_q_norm_g, v_k_norm_g, v_sinks, v_conv_w, v_conv_b, v_conv_ln_g, v_conv_ln_b, v_attn_out_g, v_conv_out_g, v_w_out, v_norm_mlp_g, v_w_mlp_up, v_w_mlp_down):
    T = x.shape[1]
    L = DEPTH
    xi, yi, ci = _place()
    shard = 2 * xi + yi
    in_sh = IN_WIDTH // N_CHIPS
    out_sh = MIX_WIDTH // N_CHIPS
    ff_sh = D_FF // N_CHIPS
    cv_sh = CONV_WIDTH // N_CHIPS

    cw_pad = jnp.pad(conv_w, ((0, 0), (0, CONV_ROWS - CONV_KERNEL), (0, 0))).reshape(L * CONV_ROWS, cv_sh)
    mine = [w_in.astype(BF16).reshape(L * D_MODEL, in_sh),
            w_out.astype(BF16).reshape(L * out_sh, D_MODEL),
            w_mlp_up.astype(BF16).reshape(L * D_MODEL, ff_sh),
            w_mlp_down.astype(BF16).reshape(L * ff_sh, D_MODEL),
            cw_pad]
    g_in, g_out, g_up, g_down, g_cw = [lax.dynamic_update_slice(got, own[None], (shard, 0, 0))
                                       for got, own in zip(_gather_shards(mine), mine)]
    W_in = g_in.reshape(N_CHIPS, L, D_MODEL, in_sh).transpose(1, 2, 0, 3).reshape(L, D_MODEL, IN_WIDTH)
    W_out = g_out.reshape(N_CHIPS, L, out_sh, D_MODEL).transpose(1, 0, 2, 3).reshape(L, MIX_WIDTH, D_MODEL)
    W_up = g_up.reshape(N_CHIPS, L, D_MODEL, ff_sh).transpose(1, 2, 0, 3).reshape(L, D_MODEL, D_FF)
    W_down = g_down.reshape(N_CHIPS, L, ff_sh, D_MODEL).transpose(1, 0, 2, 3).reshape(L, D_FF, D_MODEL)
    CW = g_cw.reshape(N_CHIPS, L, CONV_ROWS, cv_sh).transpose(1, 2, 0, 3).reshape(L, CONV_ROWS, CONV_WIDTH)

    bucket = _band_buckets()
    bk = jnp.asarray(bucket)[None]
    biasc = jnp.zeros((N_HEADS, BLOCK, BLOCK), F32)
    for b in range(NUM_BUCKETS):
        biasc = jnp.where(bk == b, rel_bias[b][:, None, None], biasc)
    biasc = biasc.reshape(N_KV_HEADS, GROUP_ROWS, BLOCK)
    onehot_t = np.zeros((LANES, BLOCK * BLOCK), np.float32)
    onehot_t[bucket.reshape(-1), np.arange(BLOCK * BLOCK)] = 1.0
    onehot_t = jnp.asarray(onehot_t, dtype=BF16)
    sink_rows = lambda l: jnp.repeat(sinks[l], BLOCK).reshape(N_KV_HEADS, GROUP_ROWS, 1)

    row = lambda a, l: a[l][None, :]

    xs = x.reshape(T, D_MODEL)
    saved = []
    for l in range(L):
        h, z = _norm_matmul(xs, row(norm_mix_g, l), W_in, l, F32, "mix_in_proj")
        a = _attn_fwd(z, biasc, sink_rows(l), row(q_norm_g, l), row(k_norm_g, l))
        yc = _conv_fwd(z, CW[l], row(conv_b, l))
        mix = _mix_norm(a, yc, row(conv_ln_g, l), row(conv_ln_b, l), row(attn_out_g, l), row(conv_out_g, l))
        x1 = _matmul_res(mix, W_out, l, xs, False, "mix_out_proj")
        h2, up = _norm_matmul(x1, row(norm_mlp_g, l), W_up, l, BF16, "mlp_up_proj")
        x2 = _matmul_res(up, W_down, l, x1, True, "mlp_down_proj")
        saved.append((xs, h, z, a, yc, mix, x1, h2, up))
        xs = x2

    loss_parts, g = _loss_grad(xs, loss_target.reshape(T, D_MODEL))

    B_in = B_out = B_up = B_down = None
    small = [None] * L
    dbias_sum = None
    for l in reversed(range(L)):
        x0, h, z, a, yc, mix, x1, h2, up = saved[l]
        d_up = _dact(g, W_down, l, up)
        B_down = _matmul_tn(up, g, True, B_down, (N_CHIPS, L, ff_sh, D_MODEL), (None, None, ff_sh, D_MODEL),
                            lambda i, j: (i, l, 0, 0), ff_sh, D_MODEL, "grad_w_mlp_down")
        B_up = _matmul_tn(h2, d_up, False, B_up, (N_CHIPS, L, D_MODEL, ff_sh), (None, None, D_MODEL, ff_sh),
                          lambda i, j: (j, l, 0, 0), D_MODEL, ff_sh, "grad_w_mlp_up")
        g1, d_gmlp = _matmul_nt_normbwd(d_up, W_up, l, x1, row(norm_mlp_g, l), g, "mlp_in_bwd")
        d_a, d_y, sm_mix = _mix_bwd(g1, W_out, l, a, yc, row(conv_ln_g, l), row(conv_ln_b, l),
                                    row(attn_out_g, l), row(conv_out_g, l))
        B_out = _matmul_tn(mix, g1, False, B_out, (N_CHIPS, L, out_sh, D_MODEL), (N_CHIPS, None, out_sh, D_MODEL),
                           lambda i, j: (0, l, 0, 0), MIX_WIDTH, D_MODEL, "grad_w_out")
        d_u, d_gate, d_cw = _conv_bwd(d_y, z, CW[l])
        d_q, d_kv, dbias, sm_attn = _attn_bwd(z, d_a, biasc, sink_rows(l), row(q_norm_g, l), row(k_norm_g, l))
        dbias_sum = dbias if dbias_sum is None else dbias_sum + dbias
        d_z = jnp.concatenate([d_q, d_kv, d_u, d_gate], axis=1)
        B_in = _matmul_tn(h, d_z, False, B_in, (L, D_MODEL, IN_WIDTH), (None, D_MODEL, IN_WIDTH),
                          lambda i, j: (l, 0, 0), D_MODEL, IN_WIDTH, "grad_w_in")
        g, d_gmix = _matmul_nt_normbwd(d_z, W_in, l, x0, row(norm_mix_g, l), g1, "mix_in_bwd")
        small[l] = (d_gmix[0], sm_attn[0, :HEAD_DIM], sm_attn[1, :HEAD_DIM], sm_attn[2, :N_HEADS],
                    d_cw[:CONV_KERNEL], sm_mix[4], sm_mix[2], sm_mix[3], sm_mix[0], sm_mix[1], d_gmlp[0])
    grad_x = g.reshape(1, T, D_MODEL)

    d_rel = _bucket_reduce(dbias_sum.reshape(N_HEADS, BLOCK * BLOCK), onehot_t)[:, :NUM_BUCKETS].T
    stack = lambda k: jnp.stack([small[l][k] for l in range(L)])
    small_shapes = [(), (NUM_BUCKETS, N_HEADS), (L, D_MODEL), (L, HEAD_DIM), (L, HEAD_DIM), (L, N_HEADS),
                    (L, CONV_KERNEL, CONV_WIDTH), (L, CONV_WIDTH), (L, CONV_WIDTH), (L, CONV_WIDTH),
                    (L, CONV_WIDTH), (L, CONV_WIDTH), (L, D_MODEL)]
    part = _pack([jnp.sum(loss_parts[:, 0, 0]), d_rel] + [stack(k) for k in range(11)])
    tot = _unpack(_sum_devices(part), small_shapes)
    loss = tot[0]
    (g_rel, g_nmix, g_qn, g_kn, g_sk, g_cw_full, g_cb, g_lng, g_lnb, g_aog, g_cog, g_nmlp) = tot[1:]
    g_cw_sh = lax.dynamic_slice_in_dim(g_cw_full, shard * cv_sh, cv_sh, axis=2)

    small_w = [rel_bias, norm_mix_g, q_norm_g, k_norm_g, sinks, conv_w, conv_b, conv_ln_g, conv_ln_b,
               attn_out_g, conv_out_g, norm_mlp_g]
    small_m = [m_rel_bias, m_norm_mix_g, m_q_norm_g, m_k_norm_g, m_sinks, m_conv_w, m_conv_b, m_conv_ln_g,
               m_conv_ln_b, m_attn_out_g, m_conv_out_g, m_norm_mlp_g]
    small_v = [v_rel_bias, v_norm_mix_g, v_q_norm_g, v_k_norm_g, v_sinks, v_conv_w, v_conv_b, v_conv_ln_g,
               v_conv_ln_b, v_attn_out_g, v_conv_out_g, v_norm_mlp_g]
    small_g = [g_rel, g_nmix, g_qn, g_kn, g_sk, g_cw_sh, g_cb, g_lng, g_lnb, g_aog, g_cog, g_nmlp]
    shapes = [w.shape for w in small_w]
    sd, sm_, sv_ = _adamw(_pack(small_w), _pack(small_g), _pack(small_m), _pack(small_v), "adamw_small")
    small_d, small_nm, small_nv = _unpack(sd, shapes), _unpack(sm_, shapes), _unpack(sv_, shapes)

    G_in = B_in.reshape(L, D_MODEL, N_CHIPS, in_sh).transpose(2, 0, 1, 3).reshape(N_CHIPS, L * D_MODEL, in_sh)
    G = [G_in, B_out.reshape(N_CHIPS, L * out_sh, D_MODEL), B_up.reshape(N_CHIPS, L * D_MODEL, ff_sh),
         B_down.reshape(N_CHIPS, L * ff_sh, D_MODEL)]
    names = ["w_in", "w_out", "w_mlp_up", "w_mlp_down"]
    got = _swap_halves(G)
    own_sel = shard.astype(jnp.int32)[None]
    send_sel = jnp.stack([shard ^ 2, shard ^ 1, shard ^ 3]).astype(jnp.int32)
    owns, sends, offs = [], [], []
    for b in range(4):
        hh = G[b].shape[1] // 2
        off = (ci * (hh // _tile(hh, 512))).astype(jnp.int32)[None]
        offs.append(off)
        owns.append(_chip_sum(G[b], got[b], (off, own_sel), F32, "chip_sum_own_" + names[b]))
        sends.append(_chip_sum(G[b], got[b], (off, send_sel), BF16, "chip_sum_send_" + names[b]))
    arrived = _exchange_chips(sends)
    halves = [_shard_sum(owns[b], arrived[b], offs[b], "shard_sum_" + names[b]) for b in range(4)]
    grads = _join_halves(halves)

    big_w = [w_in, w_out, w_mlp_up, w_mlp_down]
    big_m = [m_w_in, m_w_out, m_w_mlp_up, m_w_mlp_down]
    big_v = [v_w_in, v_w_out, v_w_mlp_up, v_w_mlp_down]
    big_g, big_d, big_nm, big_nv = [], [], [], []
    for b in range(4):
        shp = big_w[b].shape
        flat = lambda t: t.reshape(shp[0] * shp[1], shp[2])
        d, nm, nv = _adamw(flat(big_w[b]), grads[b], flat(big_m[b]), flat(big_v[b]), "adamw_" + names[b])
        big_g.append(grads[b].reshape(shp))
        big_d.append(d.reshape(shp))
        big_nm.append(nm.reshape(shp))
        big_nv.append(nv.reshape(shp))

    def ordered(sm, bg):
        return [sm[0], sm[1], bg[0], sm[2], sm[3], sm[4], sm[5], sm[6], sm[7], sm[8], sm[9], sm[10], bg[1], sm[11],
                bg[2], bg[3]]

    return (loss, grad_x, *ordered(small_g, big_g), *ordered(small_d, big_d), *ordered(small_nm, big_nm),
            *ordered(small_nv, big_nv))
```

```python
import math

import numpy as np
import jax
import jax.numpy as jnp
from jax import lax
from jax.experimental import pallas as pl
from jax.experimental.pallas import tpu as pltpu

F32 = jnp.float32
BF16 = jnp.bfloat16

D_MODEL = 1024
DEPTH = 4
HEAD_DIM = 64
N_HEADS = 8
N_KV_HEADS = 2
GQA_GROUP = N_HEADS // N_KV_HEADS
ATTN_WIDTH = N_HEADS * HEAD_DIM
KV_WIDTH = N_KV_HEADS * HEAD_DIM
CONV_WIDTH = D_MODEL - ATTN_WIDTH
MIX_WIDTH = ATTN_WIDTH + CONV_WIDTH
IN_WIDTH = ATTN_WIDTH + 2 * KV_WIDTH + 2 * CONV_WIDTH
BLOCK = 128
CONV_KERNEL = 31
CONV_ROWS = 32
HALO = 32
CONV_CH = 256
NUM_BUCKETS = 32
MAX_DISTANCE = 128
D_FF = 4 * D_MODEL
EPS = 1e-6
NEG = -1e30
SCALE = 1.0 / math.sqrt(HEAD_DIM)

ADAM_LR = 0.001
ADAM_B1 = 0.9
ADAM_B2 = 0.999
ADAM_EPS = 1e-08
ADAM_WD = 0.01
ADAM_STEP = 10

N_CHIPS = 4
N_DEV = 8
LANES = 128
VMEM_LIMIT = 52 * 1024 * 1024

Q0, K0, V0, U0, G0 = 0, ATTN_WIDTH, ATTN_WIDTH + KV_WIDTH, ATTN_WIDTH + 2 * KV_WIDTH, ATTN_WIDTH + 2 * KV_WIDTH + CONV_WIDTH

NT = (((1,), (1,)), ((), ()))
TN = (((0,), (0,)), ((), ()))
MESH = pl.DeviceIdType.MESH
ANY = pl.BlockSpec(memory_space=pl.ANY)


def _params(sem=None):
    return pltpu.CompilerParams(dimension_semantics=sem, vmem_limit_bytes=VMEM_LIMIT)


def _chunk(n):
    for c in range(1024, 0, -LANES):
        if n % c == 0:
            return c
    raise ValueError(n)


def _tile(t, want):
    return min(t, want)


def _t5_bucket(n):
    n = np.asarray(n)
    max_exact = NUM_BUCKETS // 2
    large = max_exact + (np.log(np.maximum(n, 1) / max_exact) / np.log(MAX_DISTANCE / max_exact)
                         * (NUM_BUCKETS - max_exact)).astype(np.int32)
    large = np.minimum(large, NUM_BUCKETS - 1)
    return np.where(n < max_exact, n, large).astype(np.int32)


def _band_buckets():
    qi = np.arange(BLOCK)[:, None]
    j = np.arange(BLOCK)[None, :]
    return _t5_bucket(np.where(j <= qi, qi - j, qi + BLOCK - j))


def _norm_matmul(x, g, w_all, l, out_dtype, name):
    T, D = x.shape
    N = w_all.shape[2]
    TM = _tile(T, 512)
    CH = _chunk(N)

    def body(x_ref, g_ref, w_ref, h_ref, z_ref):
        xv = x_ref[...]
        r = lax.rsqrt(jnp.mean(xv * xv, axis=-1, keepdims=True) + EPS)
        h = (xv * r * g_ref[...]).astype(BF16)
        h_ref[...] = h
        for c0 in range(0, N, CH):
            z_ref[:, c0:c0 + CH] = jnp.dot(h, w_ref[:, c0:c0 + CH], preferred_element_type=F32).astype(z_ref.dtype)

    return pl.pallas_call(
        body, name=name, grid=(T // TM,),
        in_specs=[pl.BlockSpec((TM, D), lambda i: (i, 0)),
                  pl.BlockSpec((1, D), lambda i: (0, 0)),
                  pl.BlockSpec((None, D, N), lambda i: (l, 0, 0))],
        out_specs=[pl.BlockSpec((TM, D), lambda i: (i, 0)),
                   pl.BlockSpec((TM, N), lambda i: (i, 0))],
        out_shape=[jax.ShapeDtypeStruct((T, D), BF16), jax.ShapeDtypeStruct((T, N), out_dtype)],
        compiler_params=_params(("parallel",)),
    )(x, g, w_all)


def _matmul_res(a, w_all, l, res, relu2, name):
    T, K = a.shape
    N = w_all.shape[2]
    TM = _tile(T, 512)
    CH = _chunk(K)

    def body(a_ref, w_ref, res_ref, o_ref):
        acc = res_ref[...]
        for k0 in range(0, K, CH):
            av = a_ref[:, k0:k0 + CH]
            if relu2:
                av = jnp.square(jnp.maximum(av.astype(F32), 0.0)).astype(BF16)
            acc = acc + jnp.dot(av, w_ref[k0:k0 + CH, :], preferred_element_type=F32)
        o_ref[...] = acc

    return pl.pallas_call(
        body, name=name, grid=(T // TM,),
        in_specs=[pl.BlockSpec((TM, K), lambda i: (i, 0)),
                  pl.BlockSpec((None, K, N), lambda i: (l, 0, 0)),
                  pl.BlockSpec((TM, N), lambda i: (i, 0))],
        out_specs=pl.BlockSpec((TM, N), lambda i: (i, 0)),
        out_shape=jax.ShapeDtypeStruct((T, N), F32),
        compiler_params=_params(("parallel",)),
    )(a, w_all, res)


def _head_norm(t, g):
    r = lax.rsqrt(jnp.mean(t * t, axis=-1, keepdims=True) + EPS)
    that = t * r
    return that * g, that, r


def _softmax_sink(s, sink):
    m = jnp.maximum(jnp.max(s, axis=-1, keepdims=True), sink)
    p = jnp.exp(s - m)
    es = jnp.exp(sink - m)
    den = jnp.sum(p, axis=-1, keepdims=True) + es
    return p / den, es / den


GROUP_ROWS = GQA_GROUP * BLOCK


def _own_block():
    row = lax.broadcasted_iota(jnp.int32, (GROUP_ROWS, BLOCK), 0)
    col = lax.broadcasted_iota(jnp.int32, (GROUP_ROWS, BLOCK), 1)
    return (row & (BLOCK - 1)) >= col


ATTN_QB = 4
KV_COLS = [slice(k * HEAD_DIM, (k + 1) * HEAD_DIM) for k in range(N_KV_HEADS)]


def _blk(i):
    return pl.ds(i * BLOCK, BLOCK)


def _stack_heads(ref, i, kvh):
    return jnp.concatenate([ref[_blk(i), (kvh * GQA_GROUP + g) * HEAD_DIM:(kvh * GQA_GROUP + g + 1) * HEAD_DIM]
                            for g in range(GQA_GROUP)], axis=0)


def _unstack_heads(ref, i, kvh, val):
    for g in range(GQA_GROUP):
        h = kvh * GQA_GROUP + g
        ref[_blk(i), h * HEAD_DIM:(h + 1) * HEAD_DIM] = val[g * BLOCK:(g + 1) * BLOCK]


def _band_probs(first, own, s_own, s_prev, bias, sink):
    s = jnp.where(own, s_own, s_prev) * SCALE + bias
    if first is not None:
        s = jnp.where(jnp.logical_or(own, jnp.logical_not(first)), s, NEG)
    return _softmax_sink(s, sink)


def _dot_nt(a, b):
    return lax.dot_general(a, b, NT, preferred_element_type=F32)


def _dot_tn(a, b):
    return lax.dot_general(a, b, TN, preferred_element_type=F32)


def _attn_fwd(z, biasc, sink_rows, qg, kg):
    T = z.shape[0]
    nb = T // BLOCK
    qb = min(ATTN_QB, nb)
    TQ = qb * BLOCK
    kb, vb = K0 // KV_WIDTH, V0 // KV_WIDTH
    groups = [(i, k) for i in range(qb) for k in range(N_KV_HEADS)]

    def body(q_ref, kc_ref, kp_ref, vc_ref, vp_ref, b_ref, sk_ref, qg_ref, kg_ref, a_ref):
        n = pl.program_id(0)
        own = _own_block()
        kn, vv = {}, {}
        for k in range(N_KV_HEADS):
            kn[-1, k] = _head_norm(kp_ref[:, KV_COLS[k]], kg_ref[...])[0].astype(BF16)
            vv[-1, k] = vp_ref[:, KV_COLS[k]].astype(BF16)
        for i, k in groups:
            kn[i, k] = _head_norm(kc_ref[_blk(i), KV_COLS[k]], kg_ref[...])[0].astype(BF16)
            vv[i, k] = vc_ref[_blk(i), KV_COLS[k]].astype(BF16)
        qnb = {g: _head_norm(_stack_heads(q_ref, *g), qg_ref[...])[0].astype(BF16) for g in groups}
        s_own = {(i, k): _dot_nt(qnb[i, k], kn[i, k]) for i, k in groups}
        s_prev = {(i, k): _dot_nt(qnb[i, k], kn[i - 1, k]) for i, k in groups}
        p = {(i, k): _band_probs(n == 0 if i == 0 else None, own, s_own[i, k], s_prev[i, k], b_ref[k], sk_ref[k])[0]
             for i, k in groups}
        p_own = {g: jnp.where(own, p[g], 0.0).astype(BF16) for g in groups}
        p_prev = {g: jnp.where(own, 0.0, p[g]).astype(BF16) for g in groups}
        o_own = {(i, k): jnp.dot(p_own[i, k], vv[i, k], preferred_element_type=F32) for i, k in groups}
        o_prev = {(i, k): jnp.dot(p_prev[i, k], vv[i - 1, k], preferred_element_type=F32) for i, k in groups}
        for i, k in groups:
            _unstack_heads(a_ref, i, k, o_own[i, k] + o_prev[i, k])

    prev = lambda n: jnp.maximum(n * qb - 1, 0)
    return pl.pallas_call(
        body, name="attn_fwd", grid=(nb // qb,),
        in_specs=[pl.BlockSpec((TQ, ATTN_WIDTH), lambda n: (n, 0)),
                  pl.BlockSpec((TQ, KV_WIDTH), lambda n: (n, kb)),
                  pl.BlockSpec((BLOCK, KV_WIDTH), lambda n: (prev(n), kb)),
                  pl.BlockSpec((TQ, KV_WIDTH), lambda n: (n, vb)),
                  pl.BlockSpec((BLOCK, KV_WIDTH), lambda n: (prev(n), vb)),
                  pl.BlockSpec((N_KV_HEADS, GROUP_ROWS, BLOCK), lambda n: (0, 0, 0)),
                  pl.BlockSpec((N_KV_HEADS, GROUP_ROWS, 1), lambda n: (0, 0, 0)),
                  pl.BlockSpec((1, HEAD_DIM), lambda n: (0, 0)),
                  pl.BlockSpec((1, HEAD_DIM), lambda n: (0, 0))],
        out_specs=pl.BlockSpec((TQ, ATTN_WIDTH), lambda n: (n, 0)),
        out_shape=jax.ShapeDtypeStruct((T, ATTN_WIDTH), F32),
        compiler_params=_params(("parallel",)),
    )(z, z, z, z, z, biasc, sink_rows, qg, kg)


SHIFTS = 8
CONV_RC = 64


def _shifted_copies(src_ref, dst_ref, total):
    for b in range(SHIFTS):
        rows = (total - b) // SHIFTS * SHIFTS
        for r0 in range(0, rows, CONV_RC):
            nr = min(CONV_RC, rows - r0)
            dst_ref[b, pl.ds(r0, nr), :] = src_ref[pl.ds(r0 + b, nr), :]


def _tap(ref, r0, o):
    return ref[o % SHIFTS, pl.ds(r0 + (o // SHIFTS) * SHIFTS, CONV_RC), :]


def _conv_fwd(z, cw, cb):
    T = z.shape[0]
    TC = _tile(T, 512)
    ub, gb = U0 // CONV_CH, G0 // CONV_CH
    hpt = TC // HALO
    lead = HALO - (CONV_KERNEL - 1)

    def body(u_ref, g_ref, up_ref, gp_ref, w_ref, b_ref, y_ref, hp_ref, hs_ref):
        i = pl.program_id(0)
        hp_ref[pl.ds(0, HALO), :] = jnp.where(i > 0, up_ref[...] * jax.nn.sigmoid(gp_ref[...]), 0.0)
        hp_ref[pl.ds(HALO, TC), :] = u_ref[...] * jax.nn.sigmoid(g_ref[...])
        _shifted_copies(hp_ref, hs_ref, TC + HALO)
        for r0 in range(0, TC, CONV_RC):
            acc = jnp.zeros((CONV_RC, CONV_CH), F32) + b_ref[...]
            for j in range(CONV_KERNEL):
                acc = acc + _tap(hs_ref, r0, lead + j) * w_ref[pl.ds(j, 1), :]
            y_ref[pl.ds(r0, CONV_RC), :] = acc

    prev = lambda i: jnp.maximum(i * hpt - 1, 0)
    return pl.pallas_call(
        body, name="conv_fwd", grid=(T // TC, CONV_WIDTH // CONV_CH),
        in_specs=[pl.BlockSpec((TC, CONV_CH), lambda i, j: (i, ub + j)),
                  pl.BlockSpec((TC, CONV_CH), lambda i, j: (i, gb + j)),
                  pl.BlockSpec((HALO, CONV_CH), lambda i, j: (prev(i), ub + j)),
                  pl.BlockSpec((HALO, CONV_CH), lambda i, j: (prev(i), gb + j)),
                  pl.BlockSpec((CONV_ROWS, CONV_CH), lambda i, j: (0, j)),
                  pl.BlockSpec((1, CONV_CH), lambda i, j: (0, j))],
        out_specs=pl.BlockSpec((TC, CONV_CH), lambda i, j: (i, j)),
        out_shape=jax.ShapeDtypeStruct((T, CONV_WIDTH), F32),
        scratch_shapes=[pltpu.VMEM((TC + HALO, CONV_CH), F32), pltpu.VMEM((SHIFTS, TC + HALO, CONV_CH), F32)],
        compiler_params=_params(("parallel", "parallel")),
    )(z, z, z, z, cw, cb)


def _ln_silu(y, ln_g, ln_b):
    mu = jnp.mean(y, axis=-1, keepdims=True)
    yc = y - mu
    var = jnp.mean(yc * yc, axis=-1, keepdims=True)
    rstd = lax.rsqrt(var + EPS)
    yhat = yc * rstd
    yn = yhat * ln_g + ln_b
    sg = jax.nn.sigmoid(yn)
    return yn * sg, yn, sg, yhat, rstd


def _mix_norm(a, y, ln_g, ln_b, ag, cg):
    T = a.shape[0]
    TM = _tile(T, 512)

    def body(a_ref, y_ref, lg_ref, lb_ref, ag_ref, cg_ref, o_ref):
        av = a_ref[...]
        ra = lax.rsqrt(jnp.mean(av * av, axis=-1, keepdims=True) + EPS)
        o_ref[:, :ATTN_WIDTH] = (av * ra * ag_ref[...]).astype(BF16)
        c, _, _, _, _ = _ln_silu(y_ref[...], lg_ref[...], lb_ref[...])
        rc = lax.rsqrt(jnp.mean(c * c, axis=-1, keepdims=True) + EPS)
        o_ref[:, ATTN_WIDTH:] = (c * rc * cg_ref[...]).astype(BF16)

    vec = pl.BlockSpec((1, CONV_WIDTH), lambda i: (0, 0))
    return pl.pallas_call(
        body, name="mix_norm", grid=(T // TM,),
        in_specs=[pl.BlockSpec((TM, ATTN_WIDTH), lambda i: (i, 0)),
                  pl.BlockSpec((TM, CONV_WIDTH), lambda i: (i, 0)), vec, vec, vec, vec],
        out_specs=pl.BlockSpec((TM, MIX_WIDTH), lambda i: (i, 0)),
        out_shape=jax.ShapeDtypeStruct((T, MIX_WIDTH), BF16),
        compiler_params=_params(("parallel",)),
    )(a, y, ln_g, ln_b, ag, cg)


def _loss_grad(y, tgt):
    T, D = y.shape
    TM = _tile(T, 512)
    nt = T // TM

    def body(y_ref, t_ref, part_ref, dy_ref):
        diff = y_ref[...] - t_ref[...]
        dy_ref[...] = diff / D
        tok = jnp.mean(diff * diff, axis=-1, keepdims=True)
        part_ref[...] = jnp.zeros((1, LANES), F32) + 0.5 * jnp.sum(tok)

    return pl.pallas_call(
        body, name="loss_grad", grid=(nt,),
        in_specs=[pl.BlockSpec((TM, D), lambda i: (i, 0)), pl.BlockSpec((TM, D), lambda i: (i, 0))],
        out_specs=[pl.BlockSpec((None, 1, LANES), lambda i: (i, 0, 0)), pl.BlockSpec((TM, D), lambda i: (i, 0))],
        out_shape=[jax.ShapeDtypeStruct((nt, 1, LANES), F32), jax.ShapeDtypeStruct((T, D), F32)],
        compiler_params=_params(("parallel",)),
    )(y, tgt)


def _dact(g, w_all, l, up):
    T, N = g.shape
    K = w_all.shape[1]
    TM = _tile(T, 512)
    CH = _chunk(K)

    def body(g_ref, w_ref, up_ref, o_ref):
        gv = g_ref[...].astype(BF16)
        for k0 in range(0, K, CH):
            da = lax.dot_general(gv, w_ref[k0:k0 + CH, :], NT, preferred_element_type=F32)
            upv = up_ref[:, k0:k0 + CH].astype(F32)
            o_ref[:, k0:k0 + CH] = (da * (2.0 * jnp.maximum(upv, 0.0))).astype(BF16)

    return pl.pallas_call(
        body, name="mlp_dact", grid=(T // TM,),
        in_specs=[pl.BlockSpec((TM, N), lambda i: (i, 0)),
                  pl.BlockSpec((None, K, N), lambda i: (l, 0, 0)),
                  pl.BlockSpec((TM, K), lambda i: (i, 0))],
        out_specs=pl.BlockSpec((TM, K), lambda i: (i, 0)),
        out_shape=jax.ShapeDtypeStruct((T, K), BF16),
        compiler_params=_params(("parallel",)),
    )(g, w_all, up)


def _matmul_tn(a, b, relu2, buf, buf_shape, out_block, out_index, tm, tn, name):
    T, M = a.shape
    N = b.shape[1]
    TK = _tile(T, 1024)
    nk = T // TK

    def body(*refs):
        a_ref, b_ref = refs[0], refs[1]
        o_ref = refs[-1]
        k = pl.program_id(2)
        av = a_ref[...]
        if relu2:
            av = jnp.square(jnp.maximum(av.astype(F32), 0.0)).astype(BF16)
        c = lax.dot_general(av, b_ref[...].astype(BF16), TN, preferred_element_type=F32).reshape(o_ref.shape)

        @pl.when(k == 0)
        def _():
            o_ref[...] = c

        @pl.when(k > 0)
        def _():
            o_ref[...] += c

    in_specs = [pl.BlockSpec((TK, tm), lambda i, j, k: (k, i)), pl.BlockSpec((TK, tn), lambda i, j, k: (k, j))]
    args = [a, b]
    aliases = {}
    if buf is not None:
        in_specs.append(ANY)
        args.append(buf)
        aliases = {2: 0}
    return pl.pallas_call(
        body, name=name, grid=(M // tm, N // tn, nk),
        in_specs=in_specs,
        out_specs=pl.BlockSpec(out_block, lambda i, j, k: out_index(i, j)),
        out_shape=jax.ShapeDtypeStruct(buf_shape, F32),
        input_output_aliases=aliases,
        compiler_params=_params(("parallel", "parallel", "arbitrary")),
    )(*args)


def _matmul_nt_normbwd(dz, w_all, l, x, gvec, gres, name):
    T, K = dz.shape
    D = x.shape[1]
    TM = _tile(T, 512)
    CH = _chunk(K)

    def body(dz_ref, w_ref, x_ref, gv_ref, gr_ref, o_ref, dg_ref):
        i = pl.program_id(0)
        dh = jnp.zeros((TM, D), F32)
        for k0 in range(0, K, CH):
            dh = dh + lax.dot_general(dz_ref[:, k0:k0 + CH], w_ref[:, k0:k0 + CH], NT, preferred_element_type=F32)
        xv = x_ref[...]
        r = lax.rsqrt(jnp.mean(xv * xv, axis=-1, keepdims=True) + EPS)
        xhat = xv * r
        dg = jnp.sum(dh * xhat, axis=0, keepdims=True)

        @pl.when(i == 0)
        def _():
            dg_ref[...] = dg

        @pl.when(i > 0)
        def _():
            dg_ref[...] += dg

        wv = dh * gv_ref[...]
        o_ref[...] = gr_ref[...] + r * (wv - xhat * jnp.mean(wv * xhat, axis=-1, keepdims=True))

    return pl.pallas_call(
        body, name=name, grid=(T // TM,),
        in_specs=[pl.BlockSpec((TM, K), lambda i: (i, 0)),
                  pl.BlockSpec((None, D, K), lambda i: (l, 0, 0)),
                  pl.BlockSpec((TM, D), lambda i: (i, 0)),
                  pl.BlockSpec((1, D), lambda i: (0, 0)),
                  pl.BlockSpec((TM, D), lambda i: (i, 0))],
        out_specs=[pl.BlockSpec((TM, D), lambda i: (i, 0)), pl.BlockSpec((1, D), lambda i: (0, 0))],
        out_shape=[jax.ShapeDtypeStruct((T, D), F32), jax.ShapeDtypeStruct((1, D), F32)],
        compiler_params=_params(("arbitrary",)),
    )(dz, w_all, x, gvec, gres)


def _mix_bwd(g1, w_all, l, a, y, ln_g, ln_b, ag, cg):
    T, D = g1.shape
    TM = _tile(T, 512)

    def body(g_ref, w_ref, a_ref, y_ref, lg_ref, lb_ref, ag_ref, cg_ref, da_ref, dy_ref, sm_ref):
        i = pl.program_id(0)
        dmix = lax.dot_general(g_ref[...].astype(BF16), w_ref[...], NT, preferred_element_type=F32)
        dma, dmc = dmix[:, :ATTN_WIDTH], dmix[:, ATTN_WIDTH:]
        av = a_ref[...]
        ra = lax.rsqrt(jnp.mean(av * av, axis=-1, keepdims=True) + EPS)
        ahat = av * ra
        d_ag = jnp.sum(dma * ahat, axis=0, keepdims=True)
        wa = dma * ag_ref[...]
        da_ref[...] = ra * (wa - ahat * jnp.mean(wa * ahat, axis=-1, keepdims=True))

        c, yn, sg, yhat, rstd = _ln_silu(y_ref[...], lg_ref[...], lb_ref[...])
        rc = lax.rsqrt(jnp.mean(c * c, axis=-1, keepdims=True) + EPS)
        chat = c * rc
        d_cg = jnp.sum(dmc * chat, axis=0, keepdims=True)
        wc = dmc * cg_ref[...]
        dc = rc * (wc - chat * jnp.mean(wc * chat, axis=-1, keepdims=True))
        dyn = dc * (sg * (1.0 + yn * (1.0 - sg)))
        d_lg = jnp.sum(dyn * yhat, axis=0, keepdims=True)
        d_lb = jnp.sum(dyn, axis=0, keepdims=True)
        dyh = dyn * lg_ref[...]
        dy = rstd * (dyh - jnp.mean(dyh, axis=-1, keepdims=True) - yhat * jnp.mean(dyh * yhat, axis=-1, keepdims=True))
        dy_ref[...] = dy
        d_cb = jnp.sum(dy, axis=0, keepdims=True)
        sums = jnp.concatenate([d_ag, d_cg, d_lg, d_lb, d_cb, jnp.zeros((3, CONV_WIDTH), F32)], axis=0)

        @pl.when(i == 0)
        def _():
            sm_ref[...] = sums

        @pl.when(i > 0)
        def _():
            sm_ref[...] += sums

    vec = pl.BlockSpec((1, CONV_WIDTH), lambda i: (0, 0))
    return pl.pallas_call(
        body, name="mix_bwd", grid=(T // TM,),
        in_specs=[pl.BlockSpec((TM, D), lambda i: (i, 0)),
                  pl.BlockSpec((None, MIX_WIDTH, D), lambda i: (l, 0, 0)),
                  pl.BlockSpec((TM, ATTN_WIDTH), lambda i: (i, 0)),
                  pl.BlockSpec((TM, CONV_WIDTH), lambda i: (i, 0)), vec, vec, vec, vec],
        out_specs=[pl.BlockSpec((TM, ATTN_WIDTH), lambda i: (i, 0)),
                   pl.BlockSpec((TM, CONV_WIDTH), lambda i: (i, 0)),
                   pl.BlockSpec((8, CONV_WIDTH), lambda i: (0, 0))],
        out_shape=[jax.ShapeDtypeStruct((T, ATTN_WIDTH), F32), jax.ShapeDtypeStruct((T, CONV_WIDTH), F32),
                   jax.ShapeDtypeStruct((8, CONV_WIDTH), F32)],
        compiler_params=_params(("arbitrary",)),
    )(g1, w_all, a, y, ln_g, ln_b, ag, cg)


def _conv_bwd(dy, z, cw):
    T = z.shape[0]
    TC = _tile(T, 512)
    nt = T // TC
    ub, gb = U0 // CONV_CH, G0 // CONV_CH
    nch = CONV_WIDTH // CONV_CH
    hpt = TC // HALO

    lead = HALO - (CONV_KERNEL - 1)

    def body(dy_ref, dyn_ref, u_ref, g_ref, up_ref, gp_ref, w_ref, du_ref, dg_ref, dw_ref,
             hp_ref, hs_ref, dyp_ref, dys_ref):
        i = pl.program_id(1)
        hp_ref[pl.ds(0, HALO), :] = jnp.where(i > 0, up_ref[...] * jax.nn.sigmoid(gp_ref[...]), 0.0)
        hp_ref[pl.ds(HALO, TC), :] = u_ref[...] * jax.nn.sigmoid(g_ref[...])
        _shifted_copies(hp_ref, hs_ref, TC + HALO)
        dyp_ref[pl.ds(0, TC), :] = dy_ref[...]
        dyp_ref[pl.ds(TC, HALO), :] = jnp.where(i < nt - 1, dyn_ref[...], 0.0)
        _shifted_copies(dyp_ref, dys_ref, TC + HALO)

        @pl.when(i == 0)
        def _():
            dw_ref[...] = jnp.zeros((CONV_ROWS, CONV_CH), F32)

        for r0 in range(0, TC, CONV_RC):
            rows = pl.ds(r0, CONV_RC)
            dh = jnp.zeros((CONV_RC, CONV_CH), F32)
            for j in range(CONV_KERNEL):
                dh = dh + _tap(dys_ref, r0, CONV_KERNEL - 1 - j) * w_ref[pl.ds(j, 1), :]
            uv = u_ref[rows, :]
            sg = jax.nn.sigmoid(g_ref[rows, :])
            du_ref[rows, :] = (dh * sg).astype(BF16)
            dg_ref[rows, :] = (dh * uv * sg * (1.0 - sg)).astype(BF16)
        for j in range(CONV_KERNEL):
            acc = jnp.zeros((SHIFTS, CONV_CH), F32)
            for r0 in range(0, TC, CONV_RC):
                prod = dy_ref[pl.ds(r0, CONV_RC), :] * _tap(hs_ref, r0, lead + j)
                acc = acc + jnp.sum(prod.reshape(CONV_RC // SHIFTS, SHIFTS, CONV_CH), axis=0)
            dw_ref[pl.ds(j, 1), :] += jnp.sum(acc, axis=0, keepdims=True)

    prev = lambda i: jnp.maximum(i * hpt - 1, 0)
    nxt = lambda i: jnp.minimum((i + 1) * hpt, T // HALO - 1)
    return pl.pallas_call(
        body, name="conv_bwd", grid=(nch, nt),
        in_specs=[pl.BlockSpec((TC, CONV_CH), lambda j, i: (i, j)),
                  pl.BlockSpec((HALO, CONV_CH), lambda j, i: (nxt(i), j)),
                  pl.BlockSpec((TC, CONV_CH), lambda j, i: (i, ub + j)),
                  pl.BlockSpec((TC, CONV_CH), lambda j, i: (i, gb + j)),
                  pl.BlockSpec((HALO, CONV_CH), lambda j, i: (prev(i), ub + j)),
                  pl.BlockSpec((HALO, CONV_CH), lambda j, i: (prev(i), gb + j)),
                  pl.BlockSpec((CONV_ROWS, CONV_CH), lambda j, i: (0, j))],
        out_specs=[pl.BlockSpec((TC, CONV_CH), lambda j, i: (i, j)),
                   pl.BlockSpec((TC, CONV_CH), lambda j, i: (i, j)),
                   pl.BlockSpec((CONV_ROWS, CONV_CH), lambda j, i: (0, j))],
        out_shape=[jax.ShapeDtypeStruct((T, CONV_WIDTH), BF16), jax.ShapeDtypeStruct((T, CONV_WIDTH), BF16),
                   jax.ShapeDtypeStruct((CONV_ROWS, CONV_WIDTH), F32)],
        scratch_shapes=[pltpu.VMEM((TC + HALO, CONV_CH), F32), pltpu.VMEM((SHIFTS, TC + HALO, CONV_CH), F32),
                        pltpu.VMEM((TC + HALO, CONV_CH), F32), pltpu.VMEM((SHIFTS, TC + HALO, CONV_CH), F32)],
        compiler_params=_params(("parallel", "arbitrary")),
    )(dy, dy, z, z, z, z, cw)


def _norm_bwd(d, that, r, g):
    w = d * g
    return r * (w - that * jnp.mean(w * that, axis=-1, keepdims=True)), jnp.sum(d * that, axis=0, keepdims=True)


def _attn_bwd(z, da, biasc, sink_rows, qg, kg):
    T = z.shape[0]
    nb = T // BLOCK
    qb = min(ATTN_QB, nb)
    TQ = qb * BLOCK
    ns = nb // qb
    kb, vb = K0 // KV_WIDTH, V0 // KV_WIDTH
    groups = [(i, k) for i in range(qb) for k in range(N_KV_HEADS)]

    def body(q_ref, kc_ref, kp_ref, vc_ref, vp_ref, da_ref, b_ref, sk_ref, qg_ref, kg_ref,
             dq_ref, dkv_ref, db_ref, sm_ref, ck_ref, cv_ref, pk_ref, pv_ref, nk_ref, nv_ref):
        n = pl.program_id(0)
        lane = lax.broadcasted_iota(jnp.int32, (1, LANES), 1)

        @pl.when(n == 0)
        def _():
            db_ref[...] = jnp.zeros(db_ref.shape, F32)
            sm_ref[...] = jnp.zeros(sm_ref.shape, F32)
            ck_ref[...] = jnp.zeros(ck_ref.shape, F32)
            cv_ref[...] = jnp.zeros(cv_ref.shape, F32)

        pk_ref[...] = jnp.zeros(pk_ref.shape, F32)
        pv_ref[...] = jnp.zeros(pv_ref.shape, F32)

        @pl.when(n < ns)
        def _():
            own = _own_block()
            knorm, kn, vv = {}, {}, {}
            for k in range(N_KV_HEADS):
                kn[-1, k] = _head_norm(kp_ref[:, KV_COLS[k]], kg_ref[...])[0].astype(BF16)
                vv[-1, k] = vp_ref[:, KV_COLS[k]].astype(BF16)
            for i, k in groups:
                knorm[i, k] = _head_norm(kc_ref[_blk(i), KV_COLS[k]], kg_ref[...])
                kn[i, k] = knorm[i, k][0].astype(BF16)
                vv[i, k] = vc_ref[_blk(i), KV_COLS[k]].astype(BF16)
            qnorm = {g: _head_norm(_stack_heads(q_ref, *g), qg_ref[...]) for g in groups}
            qnb = {g: qnorm[g][0].astype(BF16) for g in groups}
            dob = {g: _stack_heads(da_ref, *g).astype(BF16) for g in groups}
            s_own = {(i, k): _dot_nt(qnb[i, k], kn[i, k]) for i, k in groups}
            s_prev = {(i, k): _dot_nt(qnb[i, k], kn[i - 1, k]) for i, k in groups}
            dp_own = {(i, k): _dot_nt(dob[i, k], vv[i, k]) for i, k in groups}
            dp_prev = {(i, k): _dot_nt(dob[i, k], vv[i - 1, k]) for i, k in groups}
            probs = {(i, k): _band_probs(n == 0 if i == 0 else None, own, s_own[i, k], s_prev[i, k], b_ref[k], sk_ref[k])
                     for i, k in groups}
            ds_own, ds_prev, p_own, p_prev = {}, {}, {}, {}
            dsk = jnp.zeros((1, LANES), F32)
            dbias = [jnp.zeros((GROUP_ROWS, BLOCK), F32) for _ in range(N_KV_HEADS)]
            for i, k in groups:
                p, psink = probs[i, k]
                dp = jnp.where(own, dp_own[i, k], dp_prev[i, k])
                delta = jnp.sum(p * dp, axis=-1, keepdims=True)
                ds = p * (dp - delta)
                dbias[k] = dbias[k] + ds
                dsink = psink * delta
                for g in range(GQA_GROUP):
                    dsk = dsk + jnp.where(lane == k * GQA_GROUP + g, -jnp.sum(dsink[g * BLOCK:(g + 1) * BLOCK]), 0.0)
                ds_own[i, k] = jnp.where(own, ds, 0.0).astype(BF16)
                ds_prev[i, k] = jnp.where(own, 0.0, ds).astype(BF16)
                p_own[i, k] = jnp.where(own, p, 0.0).astype(BF16)
                p_prev[i, k] = jnp.where(own, 0.0, p).astype(BF16)
            for k in range(N_KV_HEADS):
                db_ref[k] += dbias[k]
            dqn_own = {(i, k): jnp.dot(ds_own[i, k], kn[i, k], preferred_element_type=F32) for i, k in groups}
            dqn_prev = {(i, k): jnp.dot(ds_prev[i, k], kn[i - 1, k], preferred_element_type=F32) for i, k in groups}
            dk_own = {g: _dot_tn(ds_own[g], qnb[g]) * SCALE for g in groups}
            dk_prev = {g: _dot_tn(ds_prev[g], qnb[g]) * SCALE for g in groups}
            dv_own = {g: _dot_tn(p_own[g], dob[g]) for g in groups}
            dv_prev = {g: _dot_tn(p_prev[g], dob[g]) for g in groups}
            dqg = jnp.zeros((1, HEAD_DIM), F32)
            dkg = jnp.zeros((1, HEAD_DIM), F32)
            for i, k in groups:
                _, qhat, rq = qnorm[i, k]
                dq, dg = _norm_bwd((dqn_own[i, k] + dqn_prev[i, k]) * SCALE, qhat, rq, qg_ref[...])
                dqg = dqg + dg
                _unstack_heads(dq_ref, i, k, dq.astype(BF16))
                if i == 0:
                    pk_ref[:, KV_COLS[k]] = dk_prev[i, k]
                    pv_ref[:, KV_COLS[k]] = dv_prev[i, k]
                if i == qb - 1:
                    nk_ref[:, KV_COLS[k]] = dk_own[i, k]
                    nv_ref[:, KV_COLS[k]] = dv_own[i, k]
                else:
                    _, khat, rk = knorm[i, k]
                    dk, dg = _norm_bwd(dk_own[i, k] + dk_prev[i + 1, k], khat, rk, kg_ref[...])
                    dkg = dkg + dg
                    dkv_ref[_blk(i + 1), KV_COLS[k]] = dk.astype(BF16)
                    dkv_ref[_blk(i + 1), pl.ds(KV_WIDTH + k * HEAD_DIM, HEAD_DIM)] = (dv_own[i, k] + dv_prev[i + 1, k]).astype(BF16)
            sm_ref[pl.ds(0, 1), pl.ds(0, HEAD_DIM)] += dqg
            sm_ref[pl.ds(1, 1), pl.ds(0, HEAD_DIM)] += dkg
            sm_ref[pl.ds(2, 1), :] += dsk

        @pl.when(n >= 1)
        def _():
            dkg = jnp.zeros((1, HEAD_DIM), F32)
            for k in range(N_KV_HEADS):
                _, khat, rk = _head_norm(kp_ref[:, KV_COLS[k]], kg_ref[...])
                dk, dg = _norm_bwd(ck_ref[:, KV_COLS[k]] + pk_ref[:, KV_COLS[k]], khat, rk, kg_ref[...])
                dkg = dkg + dg
                dkv_ref[_blk(0), KV_COLS[k]] = dk.astype(BF16)
            dkv_ref[_blk(0), pl.ds(KV_WIDTH, KV_WIDTH)] = (cv_ref[...] + pv_ref[...]).astype(BF16)
            sm_ref[pl.ds(1, 1), pl.ds(0, HEAD_DIM)] += dkg

        ck_ref[...] = nk_ref[...]
        cv_ref[...] = nv_ref[...]

    cur = lambda n: jnp.minimum(n, ns - 1)
    prev = lambda n: jnp.maximum(n * qb - 1, 0)
    carry = pltpu.VMEM((BLOCK, KV_WIDTH), F32)
    return pl.pallas_call(
        body, name="attn_bwd", grid=(ns + 1,),
        in_specs=[pl.BlockSpec((TQ, ATTN_WIDTH), lambda n: (cur(n), 0)),
                  pl.BlockSpec((TQ, KV_WIDTH), lambda n: (cur(n), kb)),
                  pl.BlockSpec((BLOCK, KV_WIDTH), lambda n: (prev(n), kb)),
                  pl.BlockSpec((TQ, KV_WIDTH), lambda n: (cur(n), vb)),
                  pl.BlockSpec((BLOCK, KV_WIDTH), lambda n: (prev(n), vb)),
                  pl.BlockSpec((TQ, ATTN_WIDTH), lambda n: (cur(n), 0)),
                  pl.BlockSpec((N_KV_HEADS, GROUP_ROWS, BLOCK), lambda n: (0, 0, 0)),
                  pl.BlockSpec((N_KV_HEADS, GROUP_ROWS, 1), lambda n: (0, 0, 0)),
                  pl.BlockSpec((1, HEAD_DIM), lambda n: (0, 0)),
                  pl.BlockSpec((1, HEAD_DIM), lambda n: (0, 0))],
        out_specs=[pl.BlockSpec((TQ, ATTN_WIDTH), lambda n: (cur(n), 0)),
                   pl.BlockSpec((TQ, 2 * KV_WIDTH), lambda n: (n, 0)),
                   pl.BlockSpec((N_KV_HEADS, GROUP_ROWS, BLOCK), lambda n: (0, 0, 0)),
                   pl.BlockSpec((8, LANES), lambda n: (0, 0))],
        out_shape=[jax.ShapeDtypeStruct((T, ATTN_WIDTH), BF16), jax.ShapeDtypeStruct(((ns + 1) * TQ, 2 * KV_WIDTH), BF16),
                   jax.ShapeDtypeStruct((N_KV_HEADS, GROUP_ROWS, BLOCK), F32), jax.ShapeDtypeStruct((8, LANES), F32)],
        scratch_shapes=[carry] * 6,
        compiler_params=_params(("arbitrary",)),
    )(z, z, z, z, z, da, biasc, sink_rows, qg, kg)


def _bucket_reduce(dbias, onehot_t):
    def body(d_ref, oh_ref, o_ref):
        d = d_ref[...]
        hi = d.astype(BF16)
        r1 = d - hi.astype(F32)
        mid = r1.astype(BF16)
        lo = (r1 - mid.astype(F32)).astype(BF16)
        oh = oh_ref[...]
        acc = lax.dot_general(lo, oh, NT, preferred_element_type=F32)
        acc = acc + lax.dot_general(mid, oh, NT, preferred_element_type=F32)
        o_ref[...] = acc + lax.dot_general(hi, oh, NT, preferred_element_type=F32)

    return pl.pallas_call(
        body, name="bucket_reduce",
        out_shape=jax.ShapeDtypeStruct((N_HEADS, LANES), F32),
        compiler_params=_params(),
    )(dbias, onehot_t)


def _adamw(w, g, m, v, name):
    R, C = w.shape
    TR = _tile(R, 512)

    def body(w_ref, g_ref, m_ref, v_ref, d_ref, nm_ref, nv_ref):
        gv = g_ref[...]
        mn = ADAM_B1 * m_ref[...] + (1.0 - ADAM_B1) * gv
        vn = ADAM_B2 * v_ref[...] + (1.0 - ADAM_B2) * jnp.square(gv)
        m_hat = mn / (1.0 - ADAM_B1 ** ADAM_STEP)
        v_hat = vn / (1.0 - ADAM_B2 ** ADAM_STEP)
        d_ref[...] = -ADAM_LR * (m_hat / (jnp.sqrt(v_hat) + ADAM_EPS) + ADAM_WD * w_ref[...])
        nm_ref[...] = mn
        nv_ref[...] = vn

    spec = pl.BlockSpec((TR, C), lambda i: (i, 0))
    shp = jax.ShapeDtypeStruct((R, C), F32)
    return pl.pallas_call(
        body, name=name, grid=(R // TR,),
        in_specs=[spec] * 4, out_specs=[spec] * 3, out_shape=[shp] * 3,
        compiler_params=_params(("parallel",)),
    )(w, g, m, v)


def _place():
    return lax.axis_index("x"), lax.axis_index("y"), lax.axis_index("c")


def _other_chips(x, y):
    return [(1 - x, y), (x, 1 - y), (1 - x, 1 - y)]


def _remote(src, dst, send_sem, recv_sem, dev):
    return pltpu.make_async_remote_copy(src_ref=src, dst_ref=dst, send_sem=send_sem, recv_sem=recv_sem,
                                        device_id=dev, device_id_type=MESH)


def _gather_shards(bufs):
    nbuf = len(bufs)

    def body(*refs):
        ins, outs = refs[:nbuf], refs[nbuf:2 * nbuf]
        send_sems, recv_sems = refs[2 * nbuf:]
        x, y, c = _place()
        me = 2 * x + y
        sib = (x, y, 1 - c)
        chips = _other_chips(x, y)
        started = []
        for b in range(nbuf):
            hh = bufs[b].shape[0] // 2
            for j, (cx, cy) in enumerate(chips):
                k = 6 * b + j
                cp = _remote(ins[b].at[pl.ds(c * hh, hh), :], outs[b].at[me, pl.ds(c * hh, hh), :],
                             send_sems.at[k], recv_sems.at[k], (cx, cy, c))
                cp.start()
                started.append(cp)
        for b in range(nbuf):
            hh = bufs[b].shape[0] // 2
            for j, (cx, cy) in enumerate(chips):
                rows = outs[b].at[2 * cx + cy, pl.ds(c * hh, hh), :]
                _remote(rows, rows, send_sems.at[6 * b + j], recv_sems.at[6 * b + j], sib).wait_recv()
                k = 6 * b + 3 + j
                cp = _remote(rows, rows, send_sems.at[k], recv_sems.at[k], sib)
                cp.start()
                started.append(cp)
        for b in range(nbuf):
            hh = bufs[b].shape[0] // 2
            for j, (cx, cy) in enumerate(chips):
                rows = outs[b].at[2 * cx + cy, pl.ds((1 - c) * hh, hh), :]
                k = 6 * b + 3 + j
                _remote(rows, rows, send_sems.at[k], recv_sems.at[k], sib).wait_recv()
        for cp in started:
            cp.wait_send()

    return pl.pallas_call(
        body, name="gather_weights",
        in_specs=[ANY] * nbuf, out_specs=[ANY] * nbuf,
        out_shape=[jax.ShapeDtypeStruct((N_CHIPS,) + b.shape, b.dtype) for b in bufs],
        scratch_shapes=[pltpu.SemaphoreType.DMA((6 * nbuf,)), pltpu.SemaphoreType.DMA((6 * nbuf,))],
        compiler_params=pltpu.CompilerParams(has_side_effects=True),
    )(*bufs)


def _swap_halves(bufs):
    nbuf = len(bufs)

    def body(*refs):
        ins, outs = refs[:nbuf], refs[nbuf:2 * nbuf]
        send_sems, recv_sems = refs[2 * nbuf:]
        x, y, c = _place()
        sib = (x, y, 1 - c)
        cps = []
        for b in range(nbuf):
            hh = bufs[b].shape[1] // 2
            cp = _remote(ins[b].at[:, pl.ds((1 - c) * hh, hh), :], outs[b], send_sems.at[b], recv_sems.at[b], sib)
            cp.start()
            cps.append(cp)
        for cp in cps:
            cp.wait()

    return pl.pallas_call(
        body, name="grad_swap_halves",
        in_specs=[ANY] * nbuf, out_specs=[ANY] * nbuf,
        out_shape=[jax.ShapeDtypeStruct((N_CHIPS, b.shape[1] // 2, b.shape[2]), b.dtype) for b in bufs],
        scratch_shapes=[pltpu.SemaphoreType.DMA((nbuf,)), pltpu.SemaphoreType.DMA((nbuf,))],
        compiler_params=pltpu.CompilerParams(has_side_effects=True),
    )(*bufs)


def _chip_sum(g, got, sel, out_dtype, name):
    _, R, C = g.shape
    hh = R // 2
    TR = _tile(hh, 512)
    nslot = sel[1].shape[0]

    def body(off_ref, sh_ref, g_ref, r_ref, o_ref):
        o_ref[...] = (g_ref[...] + r_ref[...]).astype(out_dtype)

    return pl.pallas_call(
        body, name=name,
        grid_spec=pltpu.PrefetchScalarGridSpec(
            num_scalar_prefetch=2, grid=(nslot, hh // TR),
            in_specs=[pl.BlockSpec((None, TR, C), lambda s, i, off, sh: (sh[s], off[0] + i, 0)),
                      pl.BlockSpec((None, TR, C), lambda s, i, off, sh: (sh[s], i, 0))],
            out_specs=pl.BlockSpec((None, TR, C), lambda s, i, off, sh: (s, i, 0))),
        out_shape=jax.ShapeDtypeStruct((nslot, hh, C), out_dtype),
        compiler_params=_params(("parallel", "parallel")),
    )(sel[0], sel[1], g, got)


def _exchange_chips(bufs):
    nbuf = len(bufs)

    def body(*refs):
        ins, outs = refs[:nbuf], refs[nbuf:2 * nbuf]
        send_sems, recv_sems = refs[2 * nbuf:]
        x, y, c = _place()
        cps = []
        for b in range(nbuf):
            for j, (cx, cy) in enumerate(_other_chips(x, y)):
                k = 3 * b + j
                cp = _remote(ins[b].at[j], outs[b].at[j], send_sems.at[k], recv_sems.at[k], (cx, cy, c))
                cp.start()
                cps.append(cp)
        for cp in cps:
            cp.wait()

    return pl.pallas_call(
        body, name="grad_exchange_chips",
        in_specs=[ANY] * nbuf, out_specs=[ANY] * nbuf,
        out_shape=[jax.ShapeDtypeStruct(b.shape, b.dtype) for b in bufs],
        scratch_shapes=[pltpu.SemaphoreType.DMA((3 * nbuf,)), pltpu.SemaphoreType.DMA((3 * nbuf,))],
        compiler_params=pltpu.CompilerParams(has_side_effects=True),
    )(*bufs)


def _shard_sum(own, got, off, name):
    _, hh, C = own.shape
    TR = _tile(hh, 512)

    def body(off_ref, o_ref, r_ref, out_ref):
        acc = o_ref[...]
        for j in range(N_CHIPS - 1):
            acc = acc + r_ref[j].astype(F32)
        out_ref[...] = acc

    return pl.pallas_call(
        body, name=name,
        grid_spec=pltpu.PrefetchScalarGridSpec(
            num_scalar_prefetch=1, grid=(hh // TR,),
            in_specs=[pl.BlockSpec((None, TR, C), lambda i, off: (0, i, 0)),
                      pl.BlockSpec((N_CHIPS - 1, TR, C), lambda i, off: (0, i, 0))],
            out_specs=pl.BlockSpec((TR, C), lambda i, off: (off[0] + i, 0))),
        out_shape=jax.ShapeDtypeStruct((2 * hh, C), F32),
        compiler_params=_params(("parallel",)),
    )(off, own, got)


def _join_halves(bufs):
    nbuf = len(bufs)

    def body(*refs):
        outs = refs[nbuf:2 * nbuf]
        send_sems, recv_sems = refs[2 * nbuf:]
        x, y, c = _place()
        sib = (x, y, 1 - c)
        cps = []
        for b in range(nbuf):
            hh = bufs[b].shape[0] // 2
            rows = outs[b].at[pl.ds(c * hh, hh), :]
            cp = _remote(rows, rows, send_sems.at[b], recv_sems.at[b], sib)
            cp.start()
            cps.append(cp)
        for b, cp in enumerate(cps):
            hh = bufs[b].shape[0] // 2
            theirs = outs[b].at[pl.ds((1 - c) * hh, hh), :]
            _remote(theirs, theirs, send_sems.at[b], recv_sems.at[b], sib).wait_recv()
            cp.wait_send()

    return pl.pallas_call(
        body, name="grad_join_halves",
        in_specs=[ANY] * nbuf, out_specs=[ANY] * nbuf,
        out_shape=[jax.ShapeDtypeStruct(b.shape, b.dtype) for b in bufs],
        input_output_aliases={b: b for b in range(nbuf)},
        scratch_shapes=[pltpu.SemaphoreType.DMA((nbuf,)), pltpu.SemaphoreType.DMA((nbuf,))],
        compiler_params=pltpu.CompilerParams(has_side_effects=True),
    )(*bufs)


def _sum_devices(part):
    R = part.shape[0]

    def body(p_ref, o_ref, all_ref, send_sems, recv_sems):
        x, y, c = _place()
        me = 4 * x + 2 * y + c
        all_ref[me] = p_ref[...]
        cps = []
        for k in range(1, N_DEV):
            px, py, pc = x ^ (k >> 2), y ^ ((k >> 1) & 1), c ^ (k & 1)
            cp = _remote(p_ref, all_ref.at[me], send_sems.at[k - 1], recv_sems.at[k - 1], (px, py, pc))
            cp.start()
            cps.append(cp)
        for k in range(1, N_DEV):
            peer = me ^ k
            _remote(p_ref, all_ref.at[peer], send_sems.at[k - 1], recv_sems.at[k - 1], (x, y, c)).wait_recv()
        for cp in cps:
            cp.wait_send()
        acc = all_ref[0]
        for d in range(1, N_DEV):
            acc = acc + all_ref[d]
        o_ref[...] = acc

    return pl.pallas_call(
        body, name="sum_small_grads",
        in_specs=[pl.BlockSpec(memory_space=pltpu.VMEM)],
        out_specs=pl.BlockSpec(memory_space=pltpu.VMEM),
        out_shape=jax.ShapeDtypeStruct((R, LANES), F32),
        scratch_shapes=[pltpu.VMEM((N_DEV, R, LANES), F32),
                        pltpu.SemaphoreType.DMA((N_DEV - 1,)), pltpu.SemaphoreType.DMA((N_DEV - 1,))],
        compiler_params=pltpu.CompilerParams(has_side_effects=True, vmem_limit_bytes=VMEM_LIMIT),
    )(part)


def _pack(parts):
    flat = jnp.concatenate([p.reshape(-1).astype(F32) for p in parts])
    n = flat.shape[0]
    rows = -(-n // LANES)
    rows = -(-rows // 8) * 8
    return jnp.pad(flat, (0, rows * LANES - n)).reshape(rows, LANES)


def _unpack(packed, shapes):
    flat = packed.reshape(-1)
    out, off = [], 0
    for s in shapes:
        n = int(np.prod(s))
        out.append(flat[off:off + n].reshape(s))
        off += n
    return out


def kernel(x, rel_bias, norm_mix_g, w_in, q_norm_g, k_norm_g, sinks, conv_w, conv_b, conv_ln_g, conv_ln_b, attn_out_g, conv_out_g, w_out, norm_mlp_g, w_mlp_up, w_mlp_down, loss_target, m_rel_bias, m_norm_mix_g, m_w_in, m_q_norm_g, m_k_norm_g, m_sinks, m_conv_w, m_conv_b, m_conv_ln_g, m_conv_ln_b, m_attn_out_g, m_conv_out_g, m_w_out, m_norm_mlp_g, m_w_mlp_up, m_w_mlp_down, v_rel_bias, v_norm_mix_g, v_w_in, v_q_norm_g, v_k_norm_g, v_sinks, v_conv_w, v_conv_b, v_conv_ln_g, v_conv_ln_b, v_attn_out_g, v_conv_out_g, v_w_out, v_norm_mlp_g, v_w_mlp_up, v_w_mlp_down):
    T = x.shape[1]
    L = DEPTH
    xi, yi, ci = _place()
    shard = 2 * xi + yi
    in_sh = IN_WIDTH // N_CHIPS
    out_sh = MIX_WIDTH // N_CHIPS
    ff_sh = D_FF // N_CHIPS
    cv_sh = CONV_WIDTH // N_CHIPS

    cw_pad = jnp.pad(conv_w, ((0, 0), (0, CONV_ROWS - CONV_KERNEL), (0, 0))).reshape(L * CONV_ROWS, cv_sh)
    mine = [w_in.astype(BF16).reshape(L * D_MODEL, in_sh),
            w_out.astype(BF16).reshape(L * out_sh, D_MODEL),
            w_mlp_up.astype(BF16).reshape(L * D_MODEL, ff_sh),
            w_mlp_down.astype(BF16).reshape(L * ff_sh, D_MODEL),
            cw_pad]
    g_in, g_out, g_up, g_down, g_cw = [lax.dynamic_update_slice(got, own[None], (shard, 0, 0))
                                       for got, own in zip(_gather_shards(mine), mine)]
    W_in = g_in.reshape(N_CHIPS, L, D_MODEL, in_sh).transpose(1, 2, 0, 3).reshape(L, D_MODEL, IN_WIDTH)
    W_out = g_out.reshape(N_CHIPS, L, out_sh, D_MODEL).transpose(1, 0, 2, 3).reshape(L, MIX_WIDTH, D_MODEL)
    W_up = g_up.reshape(N_CHIPS, L, D_MODEL, ff_sh).transpose(1, 2, 0, 3).reshape(L, D_MODEL, D_FF)
    W_down = g_down.reshape(N_CHIPS, L, ff_sh, D_MODEL).transpose(1, 0, 2, 3).reshape(L, D_FF, D_MODEL)
    CW = g_cw.reshape(N_CHIPS, L, CONV_ROWS, cv_sh).transpose(1, 2, 0, 3).reshape(L, CONV_ROWS, CONV_WIDTH)

    bucket = _band_buckets()
    bk = jnp.asarray(bucket)[None]
    biasc = jnp.zeros((N_HEADS, BLOCK, BLOCK), F32)
    for b in range(NUM_BUCKETS):
        biasc = jnp.where(bk == b, rel_bias[b][:, None, None], biasc)
    biasc = biasc.reshape(N_KV_HEADS, GROUP_ROWS, BLOCK)
    onehot_t = np.zeros((LANES, BLOCK * BLOCK), np.float32)
    onehot_t[bucket.reshape(-1), np.arange(BLOCK * BLOCK)] = 1.0
    onehot_t = jnp.asarray(onehot_t, dtype=BF16)
    sink_rows = lambda l: jnp.repeat(sinks[l], BLOCK).reshape(N_KV_HEADS, GROUP_ROWS, 1)

    row = lambda a, l: a[l][None, :]

    xs = x.reshape(T, D_MODEL)
    saved = []
    for l in range(L):
        h, z = _norm_matmul(xs, row(norm_mix_g, l), W_in, l, F32, "mix_in_proj")
        a = _attn_fwd(z, biasc, sink_rows(l), row(q_norm_g, l), row(k_norm_g, l))
        yc = _conv_fwd(z, CW[l], row(conv_b, l))
        mix = _mix_norm(a, yc, row(conv_ln_g, l), row(conv_ln_b, l), row(attn_out_g, l), row(conv_out_g, l))
        x1 = _matmul_res(mix, W_out, l, xs, False, "mix_out_proj")
        h2, up = _norm_matmul(x1, row(norm_mlp_g, l), W_up, l, BF16, "mlp_up_proj")
        x2 = _matmul_res(up, W_down, l, x1, True, "mlp_down_proj")
        saved.append((xs, h, z, a, yc, mix, x1, h2, up))
        xs = x2

    loss_parts, g = _loss_grad(xs, loss_target.reshape(T, D_MODEL))

    B_in = B_out = B_up = B_down = None
    small = [None] * L
    dbias_sum = None
    for l in reversed(range(L)):
        x0, h, z, a, yc, mix, x1, h2, up = saved[l]
        d_up = _dact(g, W_down, l, up)
        B_down = _matmul_tn(up, g, True, B_down, (N_CHIPS, L, ff_sh, D_MODEL), (None, None, ff_sh, D_MODEL),
                            lambda i, j: (i, l, 0, 0), ff_sh, D_MODEL, "grad_w_mlp_down")
        B_up = _matmul_tn(h2, d_up, False, B_up, (N_CHIPS, L, D_MODEL, ff_sh), (None, None, D_MODEL, ff_sh),
                          lambda i, j: (j, l, 0, 0), D_MODEL, ff_sh, "grad_w_mlp_up")
        g1, d_gmlp = _matmul_nt_normbwd(d_up, W_up, l, x1, row(norm_mlp_g, l), g, "mlp_in_bwd")
        d_a, d_y, sm_mix = _mix_bwd(g1, W_out, l, a, yc, row(conv_ln_g, l), row(conv_ln_b, l),
                                    row(attn_out_g, l), row(conv_out_g, l))
        B_out = _matmul_tn(mix, g1, False, B_out, (N_CHIPS, L, out_sh, D_MODEL), (N_CHIPS, None, out_sh, D_MODEL),
                           lambda i, j: (0, l, 0, 0), MIX_WIDTH, D_MODEL, "grad_w_out")
        d_u, d_gate, d_cw = _conv_bwd(d_y, z, CW[l])
        d_q, d_kv, dbias, sm_attn = _attn_bwd(z, d_a, biasc, sink_rows(l), row(q_norm_g, l), row(k_norm_g, l))
        dbias_sum = dbias if dbias_sum is None else dbias_sum + dbias
        d_z = jnp.concatenate([d_q, d_kv[BLOCK:BLOCK + T], d_u, d_gate], axis=1)
        B_in = _matmul_tn(h, d_z, False, B_in, (L, D_MODEL, IN_WIDTH), (None, D_MODEL, IN_WIDTH),
                          lambda i, j: (l, 0, 0), D_MODEL, IN_WIDTH, "grad_w_in")
        g, d_gmix = _matmul_nt_normbwd(d_z, W_in, l, x0, row(norm_mix_g, l), g1, "mix_in_bwd")
        small[l] = (d_gmix[0], sm_attn[0, :HEAD_DIM], sm_attn[1, :HEAD_DIM], sm_attn[2, :N_HEADS],
                    d_cw[:CONV_KERNEL], sm_mix[4], sm_mix[2], sm_mix[3], sm_mix[0], sm_mix[1], d_gmlp[0])
    grad_x = g.reshape(1, T, D_MODEL)

    d_rel = _bucket_reduce(dbias_sum.reshape(N_HEADS, BLOCK * BLOCK), onehot_t)[:, :NUM_BUCKETS].T
    stack = lambda k: jnp.stack([small[l][k] for l in range(L)])
    small_shapes = [(), (NUM_BUCKETS, N_HEADS), (L, D_MODEL), (L, HEAD_DIM), (L, HEAD_DIM), (L, N_HEADS),
                    (L, CONV_KERNEL, CONV_WIDTH), (L, CONV_WIDTH), (L, CONV_WIDTH), (L, CONV_WIDTH),
                    (L, CONV_WIDTH), (L, CONV_WIDTH), (L, D_MODEL)]
    part = _pack([jnp.sum(loss_parts[:, 0, 0]), d_rel] + [stack(k) for k in range(11)])
    tot = _unpack(_sum_devices(part), small_shapes)
    loss = tot[0]
    (g_rel, g_nmix, g_qn, g_kn, g_sk, g_cw_full, g_cb, g_lng, g_lnb, g_aog, g_cog, g_nmlp) = tot[1:]
    g_cw_sh = lax.dynamic_slice_in_dim(g_cw_full, shard * cv_sh, cv_sh, axis=2)

    small_w = [rel_bias, norm_mix_g, q_norm_g, k_norm_g, sinks, conv_w, conv_b, conv_ln_g, conv_ln_b,
               attn_out_g, conv_out_g, norm_mlp_g]
    small_m = [m_rel_bias, m_norm_mix_g, m_q_norm_g, m_k_norm_g, m_sinks, m_conv_w, m_conv_b, m_conv_ln_g,
               m_conv_ln_b, m_attn_out_g, m_conv_out_g, m_norm_mlp_g]
    small_v = [v_rel_bias, v_norm_mix_g, v_q_norm_g, v_k_norm_g, v_sinks, v_conv_w, v_conv_b, v_conv_ln_g,
               v_conv_ln_b, v_attn_out_g, v_conv_out_g, v_norm_mlp_g]
    small_g = [g_rel, g_nmix, g_qn, g_kn, g_sk, g_cw_sh, g_cb, g_lng, g_lnb, g_aog, g_cog, g_nmlp]
    shapes = [w.shape for w in small_w]
    sd, sm_, sv_ = _adamw(_pack(small_w), _pack(small_g), _pack(small_m), _pack(small_v), "adamw_small")
    small_d, small_nm, small_nv = _unpack(sd, shapes), _unpack(sm_, shapes), _unpack(sv_, shapes)

    G_in = B_in.reshape(L, D_MODEL, N_CHIPS, in_sh).transpose(2, 0, 1, 3).reshape(N_CHIPS, L * D_MODEL, in_sh)
    G = [G_in, B_out.reshape(N_CHIPS, L * out_sh, D_MODEL), B_up.reshape(N_CHIPS, L * D_MODEL, ff_sh),
         B_down.reshape(N_CHIPS, L * ff_sh, D_MODEL)]
    names = ["w_in", "w_out", "w_mlp_up", "w_mlp_down"]
    got = _swap_halves(G)
    own_sel = shard.astype(jnp.int32)[None]
    send_sel = jnp.stack([shard ^ 2, shard ^ 1, shard ^ 3]).astype(jnp.int32)
    owns, sends, offs = [], [], []
    for b in range(4):
        hh = G[b].shape[1] // 2
        off = (ci * (hh // _tile(hh, 512))).astype(jnp.int32)[None]
        offs.append(off)
        owns.append(_chip_sum(G[b], got[b], (off, own_sel), F32, "chip_sum_own_" + names[b]))
        sends.append(_chip_sum(G[b], got[b], (off, send_sel), BF16, "chip_sum_send_" + names[b]))
    arrived = _exchange_chips(sends)
    halves = [_shard_sum(owns[b], arrived[b], offs[b], "shard_sum_" + names[b]) for b in range(4)]
    grads = _join_halves(halves)

    big_w = [w_in, w_out, w_mlp_up, w_mlp_down]
    big_m = [m_w_in, m_w_out, m_w_mlp_up, m_w_mlp_down]
    big_v = [v_w_in, v_w_out, v_w_mlp_up, v_w_mlp_down]
    big_g, big_d, big_nm, big_nv = [], [], [], []
    for b in range(4):
        shp = big_w[b].shape
        flat = lambda t: t.reshape(shp[0] * shp[1], shp[2])
        d, nm, nv = _adamw(flat(big_w[b]), grads[b], flat(big_m[b]), flat(big_v[b]), "adamw_" + names[b])
        big_g.append(grads[b].reshape(shp))
        big_d.append(d.reshape(shp))
        big_nm.append(nm.reshape(shp))
        big_nv.append(nv.reshape(shp))

    def ordered(sm, bg):
        return [sm[0], sm[1], bg[0], sm[2], sm[3], sm[4], sm[5], sm[6], sm[7], sm[8], sm[9], sm[10], bg[1], sm[11],
                bg[2], bg[3]]

    return (loss, grad_x, *ordered(small_g, big_g), *ordered(small_d, big_d), *ordered(small_nm, big_nm),
            *ordered(small_nv, big_nv))
```

```python
import math

import numpy as np
import jax
import jax.numpy as jnp
from jax import lax
from jax.experimental import pallas as pl
from jax.experimental.pallas import tpu as pltpu

F32 = jnp.float32
BF16 = jnp.bfloat16

D_MODEL = 1024
DEPTH = 4
HEAD_DIM = 64
N_HEADS = 8
N_KV_HEADS = 2
GQA_GROUP = N_HEADS // N_KV_HEADS
ATTN_WIDTH = N_HEADS * HEAD_DIM
KV_WIDTH = N_KV_HEADS * HEAD_DIM
CONV_WIDTH = D_MODEL - ATTN_WIDTH
MIX_WIDTH = ATTN_WIDTH + CONV_WIDTH
IN_WIDTH = ATTN_WIDTH + 2 * KV_WIDTH + 2 * CONV_WIDTH
BLOCK = 128
CONV_KERNEL = 31
CONV_ROWS = 32
HALO = 32
CONV_CH = 256
NUM_BUCKETS = 32
MAX_DISTANCE = 128
D_FF = 4 * D_MODEL
EPS = 1e-6
NEG = -1e30
SCALE = 1.0 / math.sqrt(HEAD_DIM)

ADAM_LR = 0.001
ADAM_B1 = 0.9
ADAM_B2 = 0.999
ADAM_EPS = 1e-08
ADAM_WD = 0.01
ADAM_STEP = 10

N_CHIPS = 4
N_DEV = 8
LANES = 128
VMEM_LIMIT = 52 * 1024 * 1024

Q0, K0, V0, U0, G0 = 0, ATTN_WIDTH, ATTN_WIDTH + KV_WIDTH, ATTN_WIDTH + 2 * KV_WIDTH, ATTN_WIDTH + 2 * KV_WIDTH + CONV_WIDTH

NT = (((1,), (1,)), ((), ()))
TN = (((0,), (0,)), ((), ()))
MESH = pl.DeviceIdType.MESH
ANY = pl.BlockSpec(memory_space=pl.ANY)


def _params(sem=None):
    return pltpu.CompilerParams(dimension_semantics=sem, vmem_limit_bytes=VMEM_LIMIT)


def _chunk(n):
    for c in range(1024, 0, -LANES):
        if n % c == 0:
            return c
    raise ValueError(n)


def _tile(t, want):
    return min(t, want)


def _after_spec(after):
    return [] if after is None else [pl.BlockSpec((8, LANES), lambda *_: (0, 0))]


def _after_arg(after):
    return [] if after is None else [after]


def _t5_bucket(n):
    n = np.asarray(n)
    max_exact = NUM_BUCKETS // 2
    large = max_exact + (np.log(np.maximum(n, 1) / max_exact) / np.log(MAX_DISTANCE / max_exact)
                         * (NUM_BUCKETS - max_exact)).astype(np.int32)
    large = np.minimum(large, NUM_BUCKETS - 1)
    return np.where(n < max_exact, n, large).astype(np.int32)


def _band_buckets():
    qi = np.arange(BLOCK)[:, None]
    j = np.arange(BLOCK)[None, :]
    return _t5_bucket(np.where(j <= qi, qi - j, qi + BLOCK - j))


def _norm_matmul(x, g, w_all, l, out_dtype, name, after=None):
    T, D = x.shape
    N = w_all.shape[2]
    TM = _tile(T, 512)
    CH = _chunk(N)

    def body(x_ref, g_ref, w_ref, *rest):
        h_ref, z_ref = rest[-2:]
        xv = x_ref[...]
        r = lax.rsqrt(jnp.mean(xv * xv, axis=-1, keepdims=True) + EPS)
        h = (xv * r * g_ref[...]).astype(BF16)
        h_ref[...] = h
        for c0 in range(0, N, CH):
            z_ref[:, c0:c0 + CH] = jnp.dot(h, w_ref[:, c0:c0 + CH], preferred_element_type=F32).astype(z_ref.dtype)

    return pl.pallas_call(
        body, name=name, grid=(T // TM,),
        in_specs=[pl.BlockSpec((TM, D), lambda i: (i, 0)),
                  pl.BlockSpec((1, D), lambda i: (0, 0)),
                  pl.BlockSpec((None, D, N), lambda i: (l, 0, 0))] + _after_spec(after),
        out_specs=[pl.BlockSpec((TM, D), lambda i: (i, 0)),
                   pl.BlockSpec((TM, N), lambda i: (i, 0))],
        out_shape=[jax.ShapeDtypeStruct((T, D), BF16), jax.ShapeDtypeStruct((T, N), out_dtype)],
        compiler_params=_params(("parallel",)),
    )(x, g, w_all, *_after_arg(after))


def _matmul_res(a, w_all, l, res, relu2, name):
    T, K = a.shape
    N = w_all.shape[2]
    TM = _tile(T, 512)
    CH = _chunk(K)

    def body(a_ref, w_ref, res_ref, o_ref):
        acc = res_ref[...]
        for k0 in range(0, K, CH):
            av = a_ref[:, k0:k0 + CH]
            if relu2:
                av = jnp.square(jnp.maximum(av.astype(F32), 0.0)).astype(BF16)
            acc = acc + jnp.dot(av, w_ref[k0:k0 + CH, :], preferred_element_type=F32)
        o_ref[...] = acc

    return pl.pallas_call(
        body, name=name, grid=(T // TM,),
        in_specs=[pl.BlockSpec((TM, K), lambda i: (i, 0)),
                  pl.BlockSpec((None, K, N), lambda i: (l, 0, 0)),
                  pl.BlockSpec((TM, N), lambda i: (i, 0))],
        out_specs=pl.BlockSpec((TM, N), lambda i: (i, 0)),
        out_shape=jax.ShapeDtypeStruct((T, N), F32),
        compiler_params=_params(("parallel",)),
    )(a, w_all, res)


def _head_norm(t, g):
    r = lax.rsqrt(jnp.mean(t * t, axis=-1, keepdims=True) + EPS)
    that = t * r
    return that * g, that, r


def _softmax_sink(s, sink):
    m = jnp.maximum(jnp.max(s, axis=-1, keepdims=True), sink)
    p = jnp.exp(s - m)
    es = jnp.exp(sink - m)
    den = jnp.sum(p, axis=-1, keepdims=True) + es
    return p / den, es / den


GROUP_ROWS = GQA_GROUP * BLOCK


def _own_block():
    row = lax.broadcasted_iota(jnp.int32, (GROUP_ROWS, BLOCK), 0)
    col = lax.broadcasted_iota(jnp.int32, (GROUP_ROWS, BLOCK), 1)
    return (row & (BLOCK - 1)) >= col


ATTN_QB = 4
KV_COLS = [slice(k * HEAD_DIM, (k + 1) * HEAD_DIM) for k in range(N_KV_HEADS)]


def _blk(i):
    return pl.ds(i * BLOCK, BLOCK)


def _stack_heads(ref, i, kvh):
    return jnp.concatenate([ref[_blk(i), (kvh * GQA_GROUP + g) * HEAD_DIM:(kvh * GQA_GROUP + g + 1) * HEAD_DIM]
                            for g in range(GQA_GROUP)], axis=0)


def _unstack_heads(ref, i, kvh, val):
    for g in range(GQA_GROUP):
        h = kvh * GQA_GROUP + g
        ref[_blk(i), h * HEAD_DIM:(h + 1) * HEAD_DIM] = val[g * BLOCK:(g + 1) * BLOCK]


def _band_probs(first, own, s_own, s_prev, bias, sink):
    s = jnp.where(own, s_own, s_prev) * SCALE + bias
    if first is not None:
        s = jnp.where(jnp.logical_or(own, jnp.logical_not(first)), s, NEG)
    return _softmax_sink(s, sink)


def _dot_nt(a, b):
    return lax.dot_general(a, b, NT, preferred_element_type=F32)


def _dot_tn(a, b):
    return lax.dot_general(a, b, TN, preferred_element_type=F32)


def _attn_fwd(z, biasc, sink_rows, qg, kg):
    T = z.shape[0]
    nb = T // BLOCK
    qb = min(ATTN_QB, nb)
    TQ = qb * BLOCK
    kb, vb = K0 // KV_WIDTH, V0 // KV_WIDTH
    groups = [(i, k) for i in range(qb) for k in range(N_KV_HEADS)]

    def body(q_ref, kc_ref, kp_ref, vc_ref, vp_ref, b_ref, sk_ref, qg_ref, kg_ref, a_ref):
        n = pl.program_id(0)
        own = _own_block()
        kn, vv = {}, {}
        for k in range(N_KV_HEADS):
            kn[-1, k] = _head_norm(kp_ref[:, KV_COLS[k]], kg_ref[...])[0].astype(BF16)
            vv[-1, k] = vp_ref[:, KV_COLS[k]].astype(BF16)
        for i, k in groups:
            kn[i, k] = _head_norm(kc_ref[_blk(i), KV_COLS[k]], kg_ref[...])[0].astype(BF16)
            vv[i, k] = vc_ref[_blk(i), KV_COLS[k]].astype(BF16)
        qnb = {g: _head_norm(_stack_heads(q_ref, *g), qg_ref[...])[0].astype(BF16) for g in groups}
        s_own = {(i, k): _dot_nt(qnb[i, k], kn[i, k]) for i, k in groups}
        s_prev = {(i, k): _dot_nt(qnb[i, k], kn[i - 1, k]) for i, k in groups}
        p = {(i, k): _band_probs(n == 0 if i == 0 else None, own, s_own[i, k], s_prev[i, k], b_ref[k], sk_ref[k])[0]
             for i, k in groups}
        p_own = {g: jnp.where(own, p[g], 0.0).astype(BF16) for g in groups}
        p_prev = {g: jnp.where(own, 0.0, p[g]).astype(BF16) for g in groups}
        o_own = {(i, k): jnp.dot(p_own[i, k], vv[i, k], preferred_element_type=F32) for i, k in groups}
        o_prev = {(i, k): jnp.dot(p_prev[i, k], vv[i - 1, k], preferred_element_type=F32) for i, k in groups}
        for i, k in groups:
            _unstack_heads(a_ref, i, k, o_own[i, k] + o_prev[i, k])

    prev = lambda n: jnp.maximum(n * qb - 1, 0)
    return pl.pallas_call(
        body, name="attn_fwd", grid=(nb // qb,),
        in_specs=[pl.BlockSpec((TQ, ATTN_WIDTH), lambda n: (n, 0)),
                  pl.BlockSpec((TQ, KV_WIDTH), lambda n: (n, kb)),
                  pl.BlockSpec((BLOCK, KV_WIDTH), lambda n: (prev(n), kb)),
                  pl.BlockSpec((TQ, KV_WIDTH), lambda n: (n, vb)),
                  pl.BlockSpec((BLOCK, KV_WIDTH), lambda n: (prev(n), vb)),
                  pl.BlockSpec((N_KV_HEADS, GROUP_ROWS, BLOCK), lambda n: (0, 0, 0)),
                  pl.BlockSpec((N_KV_HEADS, GROUP_ROWS, 1), lambda n: (0, 0, 0)),
                  pl.BlockSpec((1, HEAD_DIM), lambda n: (0, 0)),
                  pl.BlockSpec((1, HEAD_DIM), lambda n: (0, 0))],
        out_specs=pl.BlockSpec((TQ, ATTN_WIDTH), lambda n: (n, 0)),
        out_shape=jax.ShapeDtypeStruct((T, ATTN_WIDTH), F32),
        compiler_params=_params(("parallel",)),
    )(z, z, z, z, z, biasc, sink_rows, qg, kg)


SHIFTS = 8
CONV_RC = 64


def _shifted_copies(src_ref, dst_ref, total):
    for b in range(SHIFTS):
        rows = (total - b) // SHIFTS * SHIFTS
        for r0 in range(0, rows, CONV_RC):
            nr = min(CONV_RC, rows - r0)
            dst_ref[b, pl.ds(r0, nr), :] = src_ref[pl.ds(r0 + b, nr), :]


def _tap(ref, r0, o):
    return ref[o % SHIFTS, pl.ds(r0 + (o // SHIFTS) * SHIFTS, CONV_RC), :]


def _conv_fwd(z, cw, cb):
    T = z.shape[0]
    TC = _tile(T, 512)
    ub, gb = U0 // CONV_CH, G0 // CONV_CH
    hpt = TC // HALO
    lead = HALO - (CONV_KERNEL - 1)

    def body(u_ref, g_ref, up_ref, gp_ref, w_ref, b_ref, y_ref, hp_ref, hs_ref):
        i = pl.program_id(0)
        hp_ref[pl.ds(0, HALO), :] = jnp.where(i > 0, up_ref[...] * jax.nn.sigmoid(gp_ref[...]), 0.0)
        hp_ref[pl.ds(HALO, TC), :] = u_ref[...] * jax.nn.sigmoid(g_ref[...])
        _shifted_copies(hp_ref, hs_ref, TC + HALO)
        for r0 in range(0, TC, CONV_RC):
            acc = jnp.zeros((CONV_RC, CONV_CH), F32) + b_ref[...]
            for j in range(CONV_KERNEL):
                acc = acc + _tap(hs_ref, r0, lead + j) * w_ref[pl.ds(j, 1), :]
            y_ref[pl.ds(r0, CONV_RC), :] = acc

    prev = lambda i: jnp.maximum(i * hpt - 1, 0)
    return pl.pallas_call(
        body, name="conv_fwd", grid=(T // TC, CONV_WIDTH // CONV_CH),
        in_specs=[pl.BlockSpec((TC, CONV_CH), lambda i, j: (i, ub + j)),
                  pl.BlockSpec((TC, CONV_CH), lambda i, j: (i, gb + j)),
                  pl.BlockSpec((HALO, CONV_CH), lambda i, j: (prev(i), ub + j)),
                  pl.BlockSpec((HALO, CONV_CH), lambda i, j: (prev(i), gb + j)),
                  pl.BlockSpec((CONV_ROWS, CONV_CH), lambda i, j: (0, j)),
                  pl.BlockSpec((1, CONV_CH), lambda i, j: (0, j))],
        out_specs=pl.BlockSpec((TC, CONV_CH), lambda i, j: (i, j)),
        out_shape=jax.ShapeDtypeStruct((T, CONV_WIDTH), F32),
        scratch_shapes=[pltpu.VMEM((TC + HALO, CONV_CH), F32), pltpu.VMEM((SHIFTS, TC + HALO, CONV_CH), F32)],
        compiler_params=_params(("parallel", "parallel")),
    )(z, z, z, z, cw, cb)


def _ln_silu(y, ln_g, ln_b):
    mu = jnp.mean(y, axis=-1, keepdims=True)
    yc = y - mu
    var = jnp.mean(yc * yc, axis=-1, keepdims=True)
    rstd = lax.rsqrt(var + EPS)
    yhat = yc * rstd
    yn = yhat * ln_g + ln_b
    sg = jax.nn.sigmoid(yn)
    return yn * sg, yn, sg, yhat, rstd


def _mix_norm(a, y, ln_g, ln_b, ag, cg):
    T = a.shape[0]
    TM = _tile(T, 512)

    def body(a_ref, y_ref, lg_ref, lb_ref, ag_ref, cg_ref, o_ref):
        av = a_ref[...]
        ra = lax.rsqrt(jnp.mean(av * av, axis=-1, keepdims=True) + EPS)
        o_ref[:, :ATTN_WIDTH] = (av * ra * ag_ref[...]).astype(BF16)
        c, _, _, _, _ = _ln_silu(y_ref[...], lg_ref[...], lb_ref[...])
        rc = lax.rsqrt(jnp.mean(c * c, axis=-1, keepdims=True) + EPS)
        o_ref[:, ATTN_WIDTH:] = (c * rc * cg_ref[...]).astype(BF16)

    vec = pl.BlockSpec((1, CONV_WIDTH), lambda i: (0, 0))
    return pl.pallas_call(
        body, name="mix_norm", grid=(T // TM,),
        in_specs=[pl.BlockSpec((TM, ATTN_WIDTH), lambda i: (i, 0)),
                  pl.BlockSpec((TM, CONV_WIDTH), lambda i: (i, 0)), vec, vec, vec, vec],
        out_specs=pl.BlockSpec((TM, MIX_WIDTH), lambda i: (i, 0)),
        out_shape=jax.ShapeDtypeStruct((T, MIX_WIDTH), BF16),
        compiler_params=_params(("parallel",)),
    )(a, y, ln_g, ln_b, ag, cg)


def _loss_grad(y, tgt):
    T, D = y.shape
    TM = _tile(T, 512)
    nt = T // TM

    def body(y_ref, t_ref, part_ref, dy_ref):
        diff = y_ref[...] - t_ref[...]
        dy_ref[...] = diff / D
        tok = jnp.mean(diff * diff, axis=-1, keepdims=True)
        part_ref[...] = jnp.zeros((1, LANES), F32) + 0.5 * jnp.sum(tok)

    return pl.pallas_call(
        body, name="loss_grad", grid=(nt,),
        in_specs=[pl.BlockSpec((TM, D), lambda i: (i, 0)), pl.BlockSpec((TM, D), lambda i: (i, 0))],
        out_specs=[pl.BlockSpec((None, 1, LANES), lambda i: (i, 0, 0)), pl.BlockSpec((TM, D), lambda i: (i, 0))],
        out_shape=[jax.ShapeDtypeStruct((nt, 1, LANES), F32), jax.ShapeDtypeStruct((T, D), F32)],
        compiler_params=_params(("parallel",)),
    )(y, tgt)


def _dact(g, w_all, l, up, after=None):
    T, N = g.shape
    K = w_all.shape[1]
    TM = _tile(T, 512)
    CH = _chunk(K)

    def body(g_ref, w_ref, up_ref, *rest):
        o_ref = rest[-1]
        gv = g_ref[...].astype(BF16)
        for k0 in range(0, K, CH):
            da = lax.dot_general(gv, w_ref[k0:k0 + CH, :], NT, preferred_element_type=F32)
            upv = up_ref[:, k0:k0 + CH].astype(F32)
            o_ref[:, k0:k0 + CH] = (da * (2.0 * jnp.maximum(upv, 0.0))).astype(BF16)

    return pl.pallas_call(
        body, name="mlp_dact", grid=(T // TM,),
        in_specs=[pl.BlockSpec((TM, N), lambda i: (i, 0)),
                  pl.BlockSpec((None, K, N), lambda i: (l, 0, 0)),
                  pl.BlockSpec((TM, K), lambda i: (i, 0))] + _after_spec(after),
        out_specs=pl.BlockSpec((TM, K), lambda i: (i, 0)),
        out_shape=jax.ShapeDtypeStruct((T, K), BF16),
        compiler_params=_params(("parallel",)),
    )(g, w_all, up, *_after_arg(after))


def _matmul_tn(a, b, relu2, buf, buf_shape, out_block, out_index, tm, tn, name):
    T, M = a.shape
    N = b.shape[1]
    TK = _tile(T, 1024)
    nk = T // TK

    def body(*refs):
        a_ref, b_ref = refs[0], refs[1]
        o_ref = refs[-1]
        k = pl.program_id(2)
        av = a_ref[...]
        if relu2:
            av = jnp.square(jnp.maximum(av.astype(F32), 0.0)).astype(BF16)
        c = lax.dot_general(av, b_ref[...].astype(BF16), TN, preferred_element_type=F32).reshape(o_ref.shape)

        @pl.when(k == 0)
        def _():
            o_ref[...] = c

        @pl.when(k > 0)
        def _():
            o_ref[...] += c

    in_specs = [pl.BlockSpec((TK, tm), lambda i, j, k: (k, i)), pl.BlockSpec((TK, tn), lambda i, j, k: (k, j))]
    args = [a, b]
    aliases = {}
    if buf is not None:
        in_specs.append(ANY)
        args.append(buf)
        aliases = {2: 0}
    return pl.pallas_call(
        body, name=name, grid=(M // tm, N // tn, nk),
        in_specs=in_specs,
        out_specs=pl.BlockSpec(out_block, lambda i, j, k: out_index(i, j)),
        out_shape=jax.ShapeDtypeStruct(buf_shape, F32),
        input_output_aliases=aliases,
        compiler_params=_params(("parallel", "parallel", "arbitrary")),
    )(*args)


def _matmul_nt_normbwd(dz, w_all, l, x, gvec, gres, name):
    T, K = dz.shape
    D = x.shape[1]
    TM = _tile(T, 512)
    CH = _chunk(K)

    def body(dz_ref, w_ref, x_ref, gv_ref, gr_ref, o_ref, dg_ref):
        i = pl.program_id(0)
        dh = jnp.zeros((TM, D), F32)
        for k0 in range(0, K, CH):
            dh = dh + lax.dot_general(dz_ref[:, k0:k0 + CH], w_ref[:, k0:k0 + CH], NT, preferred_element_type=F32)
        xv = x_ref[...]
        r = lax.rsqrt(jnp.mean(xv * xv, axis=-1, keepdims=True) + EPS)
        xhat = xv * r
        dg = jnp.sum(dh * xhat, axis=0, keepdims=True)

        @pl.when(i == 0)
        def _():
            dg_ref[...] = dg

        @pl.when(i > 0)
        def _():
            dg_ref[...] += dg

        wv = dh * gv_ref[...]
        o_ref[...] = gr_ref[...] + r * (wv - xhat * jnp.mean(wv * xhat, axis=-1, keepdims=True))

    return pl.pallas_call(
        body, name=name, grid=(T // TM,),
        in_specs=[pl.BlockSpec((TM, K), lambda i: (i, 0)),
                  pl.BlockSpec((None, D, K), lambda i: (l, 0, 0)),
                  pl.BlockSpec((TM, D), lambda i: (i, 0)),
                  pl.BlockSpec((1, D), lambda i: (0, 0)),
                  pl.BlockSpec((TM, D), lambda i: (i, 0))],
        out_specs=[pl.BlockSpec((TM, D), lambda i: (i, 0)), pl.BlockSpec((1, D), lambda i: (0, 0))],
        out_shape=[jax.ShapeDtypeStruct((T, D), F32), jax.ShapeDtypeStruct((1, D), F32)],
        compiler_params=_params(("arbitrary",)),
    )(dz, w_all, x, gvec, gres)


def _mix_bwd(g1, w_all, l, a, y, ln_g, ln_b, ag, cg):
    T, D = g1.shape
    TM = _tile(T, 512)

    def body(g_ref, w_ref, a_ref, y_ref, lg_ref, lb_ref, ag_ref, cg_ref, da_ref, dy_ref, sm_ref):
        i = pl.program_id(0)
        dmix = lax.dot_general(g_ref[...].astype(BF16), w_ref[...], NT, preferred_element_type=F32)
        dma, dmc = dmix[:, :ATTN_WIDTH], dmix[:, ATTN_WIDTH:]
        av = a_ref[...]
        ra = lax.rsqrt(jnp.mean(av * av, axis=-1, keepdims=True) + EPS)
        ahat = av * ra
        d_ag = jnp.sum(dma * ahat, axis=0, keepdims=True)
        wa = dma * ag_ref[...]
        da_ref[...] = ra * (wa - ahat * jnp.mean(wa * ahat, axis=-1, keepdims=True))

        c, yn, sg, yhat, rstd = _ln_silu(y_ref[...], lg_ref[...], lb_ref[...])
        rc = lax.rsqrt(jnp.mean(c * c, axis=-1, keepdims=True) + EPS)
        chat = c * rc
        d_cg = jnp.sum(dmc * chat, axis=0, keepdims=True)
        wc = dmc * cg_ref[...]
        dc = rc * (wc - chat * jnp.mean(wc * chat, axis=-1, keepdims=True))
        dyn = dc * (sg * (1.0 + yn * (1.0 - sg)))
        d_lg = jnp.sum(dyn * yhat, axis=0, keepdims=True)
        d_lb = jnp.sum(dyn, axis=0, keepdims=True)
        dyh = dyn * lg_ref[...]
        dy = rstd * (dyh - jnp.mean(dyh, axis=-1, keepdims=True) - yhat * jnp.mean(dyh * yhat, axis=-1, keepdims=True))
        dy_ref[...] = dy
        d_cb = jnp.sum(dy, axis=0, keepdims=True)
        sums = jnp.concatenate([d_ag, d_cg, d_lg, d_lb, d_cb, jnp.zeros((3, CONV_WIDTH), F32)], axis=0)

        @pl.when(i == 0)
        def _():
            sm_ref[...] = sums

        @pl.when(i > 0)
        def _():
            sm_ref[...] += sums

    vec = pl.BlockSpec((1, CONV_WIDTH), lambda i: (0, 0))
    return pl.pallas_call(
        body, name="mix_bwd", grid=(T // TM,),
        in_specs=[pl.BlockSpec((TM, D), lambda i: (i, 0)),
                  pl.BlockSpec((None, MIX_WIDTH, D), lambda i: (l, 0, 0)),
                  pl.BlockSpec((TM, ATTN_WIDTH), lambda i: (i, 0)),
                  pl.BlockSpec((TM, CONV_WIDTH), lambda i: (i, 0)), vec, vec, vec, vec],
        out_specs=[pl.BlockSpec((TM, ATTN_WIDTH), lambda i: (i, 0)),
                   pl.BlockSpec((TM, CONV_WIDTH), lambda i: (i, 0)),
                   pl.BlockSpec((8, CONV_WIDTH), lambda i: (0, 0))],
        out_shape=[jax.ShapeDtypeStruct((T, ATTN_WIDTH), F32), jax.ShapeDtypeStruct((T, CONV_WIDTH), F32),
                   jax.ShapeDtypeStruct((8, CONV_WIDTH), F32)],
        compiler_params=_params(("arbitrary",)),
    )(g1, w_all, a, y, ln_g, ln_b, ag, cg)


def _conv_bwd(dy, z, cw):
    T = z.shape[0]
    TC = _tile(T, 512)
    nt = T // TC
    ub, gb = U0 // CONV_CH, G0 // CONV_CH
    nch = CONV_WIDTH // CONV_CH
    hpt = TC // HALO

    lead = HALO - (CONV_KERNEL - 1)

    def body(dy_ref, dyn_ref, u_ref, g_ref, up_ref, gp_ref, w_ref, du_ref, dg_ref, dw_ref,
             hp_ref, hs_ref, dyp_ref, dys_ref):
        i = pl.program_id(1)
        hp_ref[pl.ds(0, HALO), :] = jnp.where(i > 0, up_ref[...] * jax.nn.sigmoid(gp_ref[...]), 0.0)
        hp_ref[pl.ds(HALO, TC), :] = u_ref[...] * jax.nn.sigmoid(g_ref[...])
        _shifted_copies(hp_ref, hs_ref, TC + HALO)
        dyp_ref[pl.ds(0, TC), :] = dy_ref[...]
        dyp_ref[pl.ds(TC, HALO), :] = jnp.where(i < nt - 1, dyn_ref[...], 0.0)
        _shifted_copies(dyp_ref, dys_ref, TC + HALO)

        @pl.when(i == 0)
        def _():
            dw_ref[...] = jnp.zeros((CONV_ROWS, CONV_CH), F32)

        for r0 in range(0, TC, CONV_RC):
            rows = pl.ds(r0, CONV_RC)
            dh = jnp.zeros((CONV_RC, CONV_CH), F32)
            for j in range(CONV_KERNEL):
                dh = dh + _tap(dys_ref, r0, CONV_KERNEL - 1 - j) * w_ref[pl.ds(j, 1), :]
            uv = u_ref[rows, :]
            sg = jax.nn.sigmoid(g_ref[rows, :])
            du_ref[rows, :] = (dh * sg).astype(BF16)
            dg_ref[rows, :] = (dh * uv * sg * (1.0 - sg)).astype(BF16)
        for j in range(CONV_KERNEL):
            acc = jnp.zeros((SHIFTS, CONV_CH), F32)
            for r0 in range(0, TC, CONV_RC):
                prod = dy_ref[pl.ds(r0, CONV_RC), :] * _tap(hs_ref, r0, lead + j)
                acc = acc + jnp.sum(prod.reshape(CONV_RC // SHIFTS, SHIFTS, CONV_CH), axis=0)
            dw_ref[pl.ds(j, 1), :] += jnp.sum(acc, axis=0, keepdims=True)

    prev = lambda i: jnp.maximum(i * hpt - 1, 0)
    nxt = lambda i: jnp.minimum((i + 1) * hpt, T // HALO - 1)
    return pl.pallas_call(
        body, name="conv_bwd", grid=(nch, nt),
        in_specs=[pl.BlockSpec((TC, CONV_CH), lambda j, i: (i, j)),
                  pl.BlockSpec((HALO, CONV_CH), lambda j, i: (nxt(i), j)),
                  pl.BlockSpec((TC, CONV_CH), lambda j, i: (i, ub + j)),
                  pl.BlockSpec((TC, CONV_CH), lambda j, i: (i, gb + j)),
                  pl.BlockSpec((HALO, CONV_CH), lambda j, i: (prev(i), ub + j)),
                  pl.BlockSpec((HALO, CONV_CH), lambda j, i: (prev(i), gb + j)),
                  pl.BlockSpec((CONV_ROWS, CONV_CH), lambda j, i: (0, j))],
        out_specs=[pl.BlockSpec((TC, CONV_CH), lambda j, i: (i, j)),
                   pl.BlockSpec((TC, CONV_CH), lambda j, i: (i, j)),
                   pl.BlockSpec((CONV_ROWS, CONV_CH), lambda j, i: (0, j))],
        out_shape=[jax.ShapeDtypeStruct((T, CONV_WIDTH), BF16), jax.ShapeDtypeStruct((T, CONV_WIDTH), BF16),
                   jax.ShapeDtypeStruct((CONV_ROWS, CONV_WIDTH), F32)],
        scratch_shapes=[pltpu.VMEM((TC + HALO, CONV_CH), F32), pltpu.VMEM((SHIFTS, TC + HALO, CONV_CH), F32),
                        pltpu.VMEM((TC + HALO, CONV_CH), F32), pltpu.VMEM((SHIFTS, TC + HALO, CONV_CH), F32)],
        compiler_params=_params(("parallel", "arbitrary")),
    )(dy, dy, z, z, z, z, cw)


def _norm_bwd(d, that, r, g):
    w = d * g
    return r * (w - that * jnp.mean(w * that, axis=-1, keepdims=True)), jnp.sum(d * that, axis=0, keepdims=True)


def _attn_bwd(z, da, biasc, sink_rows, qg, kg):
    T = z.shape[0]
    nb = T // BLOCK
    qb = min(ATTN_QB, nb)
    TQ = qb * BLOCK
    ns = nb // qb
    kb, vb = K0 // KV_WIDTH, V0 // KV_WIDTH
    groups = [(i, k) for i in range(qb) for k in range(N_KV_HEADS)]

    def body(q_ref, kc_ref, kp_ref, vc_ref, vp_ref, da_ref, b_ref, sk_ref, qg_ref, kg_ref,
             dq_ref, dkv_ref, db_ref, sm_ref, ck_ref, cv_ref, pk_ref, pv_ref, nk_ref, nv_ref):
        n = pl.program_id(0)
        lane = lax.broadcasted_iota(jnp.int32, (1, LANES), 1)

        @pl.when(n == 0)
        def _():
            db_ref[...] = jnp.zeros(db_ref.shape, F32)
            sm_ref[...] = jnp.zeros(sm_ref.shape, F32)
            ck_ref[...] = jnp.zeros(ck_ref.shape, F32)
            cv_ref[...] = jnp.zeros(cv_ref.shape, F32)

        pk_ref[...] = jnp.zeros(pk_ref.shape, F32)
        pv_ref[...] = jnp.zeros(pv_ref.shape, F32)

        @pl.when(n < ns)
        def _():
            own = _own_block()
            knorm, kn, vv = {}, {}, {}
            for k in range(N_KV_HEADS):
                kn[-1, k] = _head_norm(kp_ref[:, KV_COLS[k]], kg_ref[...])[0].astype(BF16)
                vv[-1, k] = vp_ref[:, KV_COLS[k]].astype(BF16)
            for i, k in groups:
                knorm[i, k] = _head_norm(kc_ref[_blk(i), KV_COLS[k]], kg_ref[...])
                kn[i, k] = knorm[i, k][0].astype(BF16)
                vv[i, k] = vc_ref[_blk(i), KV_COLS[k]].astype(BF16)
            qnorm = {g: _head_norm(_stack_heads(q_ref, *g), qg_ref[...]) for g in groups}
            qnb = {g: qnorm[g][0].astype(BF16) for g in groups}
            dob = {g: _stack_heads(da_ref, *g).astype(BF16) for g in groups}
            s_own = {(i, k): _dot_nt(qnb[i, k], kn[i, k]) for i, k in groups}
            s_prev = {(i, k): _dot_nt(qnb[i, k], kn[i - 1, k]) for i, k in groups}
            dp_own = {(i, k): _dot_nt(dob[i, k], vv[i, k]) for i, k in groups}
            dp_prev = {(i, k): _dot_nt(dob[i, k], vv[i - 1, k]) for i, k in groups}
            probs = {(i, k): _band_probs(n == 0 if i == 0 else None, own, s_own[i, k], s_prev[i, k], b_ref[k], sk_ref[k])
                     for i, k in groups}
            ds_own, ds_prev, p_own, p_prev = {}, {}, {}, {}
            dsk = jnp.zeros((1, LANES), F32)
            dbias = [jnp.zeros((GROUP_ROWS, BLOCK), F32) for _ in range(N_KV_HEADS)]
            for i, k in groups:
                p, psink = probs[i, k]
                dp = jnp.where(own, dp_own[i, k], dp_prev[i, k])
                delta = jnp.sum(p * dp, axis=-1, keepdims=True)
                ds = p * (dp - delta)
                dbias[k] = dbias[k] + ds
                dsink = psink * delta
                for g in range(GQA_GROUP):
                    dsk = dsk + jnp.where(lane == k * GQA_GROUP + g, -jnp.sum(dsink[g * BLOCK:(g + 1) * BLOCK]), 0.0)
                ds_own[i, k] = jnp.where(own, ds, 0.0).astype(BF16)
                ds_prev[i, k] = jnp.where(own, 0.0, ds).astype(BF16)
                p_own[i, k] = jnp.where(own, p, 0.0).astype(BF16)
                p_prev[i, k] = jnp.where(own, 0.0, p).astype(BF16)
            for k in range(N_KV_HEADS):
                db_ref[k] += dbias[k]
            dqn_own = {(i, k): jnp.dot(ds_own[i, k], kn[i, k], preferred_element_type=F32) for i, k in groups}
            dqn_prev = {(i, k): jnp.dot(ds_prev[i, k], kn[i - 1, k], preferred_element_type=F32) for i, k in groups}
            dk_own = {g: _dot_tn(ds_own[g], qnb[g]) * SCALE for g in groups}
            dk_prev = {g: _dot_tn(ds_prev[g], qnb[g]) * SCALE for g in groups}
            dv_own = {g: _dot_tn(p_own[g], dob[g]) for g in groups}
            dv_prev = {g: _dot_tn(p_prev[g], dob[g]) for g in groups}
            dqg = jnp.zeros((1, HEAD_DIM), F32)
            dkg = jnp.zeros((1, HEAD_DIM), F32)
            for i, k in groups:
                _, qhat, rq = qnorm[i, k]
                dq, dg = _norm_bwd((dqn_own[i, k] + dqn_prev[i, k]) * SCALE, qhat, rq, qg_ref[...])
                dqg = dqg + dg
                _unstack_heads(dq_ref, i, k, dq.astype(BF16))
                if i == 0:
                    pk_ref[:, KV_COLS[k]] = dk_prev[i, k]
                    pv_ref[:, KV_COLS[k]] = dv_prev[i, k]
                if i == qb - 1:
                    nk_ref[:, KV_COLS[k]] = dk_own[i, k]
                    nv_ref[:, KV_COLS[k]] = dv_own[i, k]
                else:
                    _, khat, rk = knorm[i, k]
                    dk, dg = _norm_bwd(dk_own[i, k] + dk_prev[i + 1, k], khat, rk, kg_ref[...])
                    dkg = dkg + dg
                    dkv_ref[_blk(i + 1), KV_COLS[k]] = dk.astype(BF16)
                    dkv_ref[_blk(i + 1), pl.ds(KV_WIDTH + k * HEAD_DIM, HEAD_DIM)] = (dv_own[i, k] + dv_prev[i + 1, k]).astype(BF16)
            sm_ref[pl.ds(0, 1), pl.ds(0, HEAD_DIM)] += dqg
            sm_ref[pl.ds(1, 1), pl.ds(0, HEAD_DIM)] += dkg
            sm_ref[pl.ds(2, 1), :] += dsk

        @pl.when(n >= 1)
        def _():
            dkg = jnp.zeros((1, HEAD_DIM), F32)
            for k in range(N_KV_HEADS):
                _, khat, rk = _head_norm(kp_ref[:, KV_COLS[k]], kg_ref[...])
                dk, dg = _norm_bwd(ck_ref[:, KV_COLS[k]] + pk_ref[:, KV_COLS[k]], khat, rk, kg_ref[...])
                dkg = dkg + dg
                dkv_ref[_blk(0), KV_COLS[k]] = dk.astype(BF16)
            dkv_ref[_blk(0), pl.ds(KV_WIDTH, KV_WIDTH)] = (cv_ref[...] + pv_ref[...]).astype(BF16)
            sm_ref[pl.ds(1, 1), pl.ds(0, HEAD_DIM)] += dkg

        ck_ref[...] = nk_ref[...]
        cv_ref[...] = nv_ref[...]

    cur = lambda n: jnp.minimum(n, ns - 1)
    prev = lambda n: jnp.maximum(n * qb - 1, 0)
    carry = pltpu.VMEM((BLOCK, KV_WIDTH), F32)
    return pl.pallas_call(
        body, name="attn_bwd", grid=(ns + 1,),
        in_specs=[pl.BlockSpec((TQ, ATTN_WIDTH), lambda n: (cur(n), 0)),
                  pl.BlockSpec((TQ, KV_WIDTH), lambda n: (cur(n), kb)),
                  pl.BlockSpec((BLOCK, KV_WIDTH), lambda n: (prev(n), kb)),
                  pl.BlockSpec((TQ, KV_WIDTH), lambda n: (cur(n), vb)),
                  pl.BlockSpec((BLOCK, KV_WIDTH), lambda n: (prev(n), vb)),
                  pl.BlockSpec((TQ, ATTN_WIDTH), lambda n: (cur(n), 0)),
                  pl.BlockSpec((N_KV_HEADS, GROUP_ROWS, BLOCK), lambda n: (0, 0, 0)),
                  pl.BlockSpec((N_KV_HEADS, GROUP_ROWS, 1), lambda n: (0, 0, 0)),
                  pl.BlockSpec((1, HEAD_DIM), lambda n: (0, 0)),
                  pl.BlockSpec((1, HEAD_DIM), lambda n: (0, 0))],
        out_specs=[pl.BlockSpec((TQ, ATTN_WIDTH), lambda n: (cur(n), 0)),
                   pl.BlockSpec((TQ, 2 * KV_WIDTH), lambda n: (n, 0)),
                   pl.BlockSpec((N_KV_HEADS, GROUP_ROWS, BLOCK), lambda n: (0, 0, 0)),
                   pl.BlockSpec((8, LANES), lambda n: (0, 0))],
        out_shape=[jax.ShapeDtypeStruct((T, ATTN_WIDTH), BF16), jax.ShapeDtypeStruct(((ns + 1) * TQ, 2 * KV_WIDTH), BF16),
                   jax.ShapeDtypeStruct((N_KV_HEADS, GROUP_ROWS, BLOCK), F32), jax.ShapeDtypeStruct((8, LANES), F32)],
        scratch_shapes=[carry] * 6,
        compiler_params=_params(("arbitrary",)),
    )(z, z, z, z, z, da, biasc, sink_rows, qg, kg)


def _bucket_reduce(dbias, onehot_t):
    def body(d_ref, oh_ref, o_ref):
        d = d_ref[...]
        hi = d.astype(BF16)
        r1 = d - hi.astype(F32)
        mid = r1.astype(BF16)
        lo = (r1 - mid.astype(F32)).astype(BF16)
        oh = oh_ref[...]
        acc = lax.dot_general(lo, oh, NT, preferred_element_type=F32)
        acc = acc + lax.dot_general(mid, oh, NT, preferred_element_type=F32)
        o_ref[...] = acc + lax.dot_general(hi, oh, NT, preferred_element_type=F32)

    return pl.pallas_call(
        body, name="bucket_reduce",
        out_shape=jax.ShapeDtypeStruct((N_HEADS, LANES), F32),
        compiler_params=_params(),
    )(dbias, onehot_t)


def _adamw(w, g, m, v, name):
    R, C = w.shape
    TR = _tile(R, 512)

    def body(w_ref, g_ref, m_ref, v_ref, d_ref, nm_ref, nv_ref):
        gv = g_ref[...]
        mn = ADAM_B1 * m_ref[...] + (1.0 - ADAM_B1) * gv
        vn = ADAM_B2 * v_ref[...] + (1.0 - ADAM_B2) * jnp.square(gv)
        m_hat = mn / (1.0 - ADAM_B1 ** ADAM_STEP)
        v_hat = vn / (1.0 - ADAM_B2 ** ADAM_STEP)
        d_ref[...] = -ADAM_LR * (m_hat / (jnp.sqrt(v_hat) + ADAM_EPS) + ADAM_WD * w_ref[...])
        nm_ref[...] = mn
        nv_ref[...] = vn

    spec = pl.BlockSpec((TR, C), lambda i: (i, 0))
    shp = jax.ShapeDtypeStruct((R, C), F32)
    return pl.pallas_call(
        body, name=name, grid=(R // TR,),
        in_specs=[spec] * 4, out_specs=[spec] * 3, out_shape=[shp] * 3,
        compiler_params=_params(("parallel",)),
    )(w, g, m, v)


def _place():
    return lax.axis_index("x"), lax.axis_index("y"), lax.axis_index("c")


def _other_chips(x, y):
    return [(1 - x, y), (x, 1 - y), (1 - x, 1 - y)]


def _remote(src, dst, send_sem, recv_sem, dev):
    return pltpu.make_async_remote_copy(src_ref=src, dst_ref=dst, send_sem=send_sem, recv_sem=recv_sem,
                                        device_id=dev, device_id_type=MESH)


def _gather_shards(bufs):
    nbuf = len(bufs)

    def body(*refs):
        ins, outs = refs[:nbuf], refs[nbuf:2 * nbuf]
        send_sems, recv_sems = refs[2 * nbuf:]
        x, y, c = _place()
        me = 2 * x + y
        sib = (x, y, 1 - c)
        chips = _other_chips(x, y)
        started = []
        for b in range(nbuf):
            hh = bufs[b].shape[0] // 2
            for j, (cx, cy) in enumerate(chips):
                k = 6 * b + j
                cp = _remote(ins[b].at[pl.ds(c * hh, hh), :], outs[b].at[me, pl.ds(c * hh, hh), :],
                             send_sems.at[k], recv_sems.at[k], (cx, cy, c))
                cp.start()
                started.append(cp)
        for b in range(nbuf):
            hh = bufs[b].shape[0] // 2
            for j, (cx, cy) in enumerate(chips):
                rows = outs[b].at[2 * cx + cy, pl.ds(c * hh, hh), :]
                _remote(rows, rows, send_sems.at[6 * b + j], recv_sems.at[6 * b + j], sib).wait_recv()
                k = 6 * b + 3 + j
                cp = _remote(rows, rows, send_sems.at[k], recv_sems.at[k], sib)
                cp.start()
                started.append(cp)
        for b in range(nbuf):
            hh = bufs[b].shape[0] // 2
            for j, (cx, cy) in enumerate(chips):
                rows = outs[b].at[2 * cx + cy, pl.ds((1 - c) * hh, hh), :]
                k = 6 * b + 3 + j
                _remote(rows, rows, send_sems.at[k], recv_sems.at[k], sib).wait_recv()
        for cp in started:
            cp.wait_send()

    return pl.pallas_call(
        body, name="gather_weights",
        in_specs=[ANY] * nbuf, out_specs=[ANY] * nbuf,
        out_shape=[jax.ShapeDtypeStruct((N_CHIPS,) + b.shape, b.dtype) for b in bufs],
        scratch_shapes=[pltpu.SemaphoreType.DMA((6 * nbuf,)), pltpu.SemaphoreType.DMA((6 * nbuf,))],
        compiler_params=pltpu.CompilerParams(has_side_effects=True),
    )(*bufs)


HBM = pl.BlockSpec(memory_space=pltpu.HBM)
SEM = pl.BlockSpec(memory_space=pltpu.SEMAPHORE)
DATAFLOW = pltpu.SideEffectType.DATAFLOW_SIDE_EFFECTING


def _gather_plan(shapes):
    def plan(srcs, lands):
        x, y, c = _place()
        out = []
        for b, shp in enumerate(shapes):
            hh = shp[0] // 2
            for cx, cy in _other_chips(x, y):
                out.append((srcs[b].at[pl.ds(c * hh, hh), :], lands[b].at[2 * x + y, pl.ds(c * hh, hh), :], (cx, cy, c)))
        return out
    return plan


def _exchange_plan(shapes):
    def plan(srcs, lands):
        x, y, c = _place()
        return [(srcs[b].at[j], lands[b].at[j], (cx, cy, c))
                for b in range(len(shapes)) for j, (cx, cy) in enumerate(_other_chips(x, y))]
    return plan


def _start_copies(name, srcs, land_shapes, plan):
    n = len(srcs)
    ncopy = 3 * n

    def body(*refs):
        ins, lands = refs[:n], refs[n:2 * n]
        send_sems, recv_sems, token = refs[2 * n], refs[2 * n + 1], refs[-1]
        for k, (src, dst, dev) in enumerate(plan(ins, lands)):
            _remote(src, dst, send_sems.at[k], recv_sems.at[k], dev).start()
        token[...] = jnp.zeros_like(token)

    hbm = lambda a: pltpu.with_memory_space_constraint(a, pltpu.HBM)
    lands = [lax.empty(s, a.dtype) for s, a in zip(land_shapes, srcs)]
    outs = pl.pallas_call(
        body, name=name,
        out_shape=(pltpu.SemaphoreType.DMA((ncopy,)), pltpu.SemaphoreType.DMA((ncopy,)),
                   *[pltpu.HBM(a.shape, a.dtype) for a in srcs], *[pltpu.HBM(a.shape, a.dtype) for a in lands],
                   jax.ShapeDtypeStruct((8, LANES), F32)),
        in_specs=[HBM] * (2 * n),
        out_specs=(SEM, SEM, *([HBM] * (2 * n)), pl.BlockSpec(memory_space=pltpu.VMEM)),
        input_output_aliases={i: 2 + i for i in range(2 * n)},
        compiler_params=pltpu.CompilerParams(has_side_effects=DATAFLOW),
    )(*[hbm(a) for a in srcs], *[hbm(a) for a in lands])
    return outs[0], outs[1], list(outs[2:2 + n]), list(outs[2 + n:2 + 2 * n]), outs[-1]


def _wait_copies(name, send_sems, recv_sems, srcs, lands, plan, after):
    n = len(srcs)

    def body(*refs):
        ins, lnds = refs[:n], refs[n:2 * n]
        ssem, rsem = refs[2 * n], refs[2 * n + 1]
        for k, (src, dst, dev) in enumerate(plan(ins, lnds)):
            cp = _remote(src, dst, ssem.at[k], rsem.at[k], dev)
            cp.wait_send()
            cp.wait_recv()

    outs = pl.pallas_call(
        body, name=name,
        out_shape=(*[pltpu.HBM(a.shape, a.dtype) for a in srcs], *[pltpu.HBM(a.shape, a.dtype) for a in lands]),
        in_specs=[HBM] * (2 * n) + [SEM, SEM, ANY],
        out_specs=tuple([HBM] * (2 * n)),
        input_output_aliases={i: i for i in range(2 * n)},
        compiler_params=pltpu.CompilerParams(has_side_effects=DATAFLOW),
    )(*srcs, *lands, send_sems, recv_sems, after)
    return list(outs[n:])


def _forward_halves(bufs):
    nbuf = len(bufs)

    def body(*refs):
        outs = refs[nbuf:2 * nbuf]
        send_sems, recv_sems = refs[2 * nbuf:]
        x, y, c = _place()
        sib = (x, y, 1 - c)
        cps = []
        for b in range(nbuf):
            hh = bufs[b].shape[1] // 2
            for j, (cx, cy) in enumerate(_other_chips(x, y)):
                rows = outs[b].at[2 * cx + cy, pl.ds(c * hh, hh), :]
                cp = _remote(rows, rows, send_sems.at[3 * b + j], recv_sems.at[3 * b + j], sib)
                cp.start()
                cps.append(cp)
        for b in range(nbuf):
            hh = bufs[b].shape[1] // 2
            for j, (cx, cy) in enumerate(_other_chips(x, y)):
                rows = outs[b].at[2 * cx + cy, pl.ds((1 - c) * hh, hh), :]
                _remote(rows, rows, send_sems.at[3 * b + j], recv_sems.at[3 * b + j], sib).wait_recv()
        for cp in cps:
            cp.wait_send()

    return pl.pallas_call(
        body, name="gather_forward_halves",
        in_specs=[ANY] * nbuf, out_specs=[ANY] * nbuf,
        out_shape=[jax.ShapeDtypeStruct(b.shape, b.dtype) for b in bufs],
        input_output_aliases={b: b for b in range(nbuf)},
        scratch_shapes=[pltpu.SemaphoreType.DMA((3 * nbuf,)), pltpu.SemaphoreType.DMA((3 * nbuf,))],
        compiler_params=pltpu.CompilerParams(has_side_effects=True),
    )(*bufs)


def _swap_halves(bufs):
    nbuf = len(bufs)

    def body(*refs):
        ins, outs = refs[:nbuf], refs[nbuf:2 * nbuf]
        send_sems, recv_sems = refs[2 * nbuf:]
        x, y, c = _place()
        sib = (x, y, 1 - c)
        cps = []
        for b in range(nbuf):
            hh = bufs[b].shape[1] // 2
            cp = _remote(ins[b].at[:, pl.ds((1 - c) * hh, hh), :], outs[b], send_sems.at[b], recv_sems.at[b], sib)
            cp.start()
            cps.append(cp)
        for cp in cps:
            cp.wait()

    return pl.pallas_call(
        body, name="grad_swap_halves",
        in_specs=[ANY] * nbuf, out_specs=[ANY] * nbuf,
        out_shape=[jax.ShapeDtypeStruct((N_CHIPS, b.shape[1] // 2, b.shape[2]), b.dtype) for b in bufs],
        scratch_shapes=[pltpu.SemaphoreType.DMA((nbuf,)), pltpu.SemaphoreType.DMA((nbuf,))],
        compiler_params=pltpu.CompilerParams(has_side_effects=True),
    )(*bufs)


def _chip_sum(g, got, sel, out_dtype, name):
    _, R, C = g.shape
    hh = R // 2
    TR = _tile(hh, 512)
    nslot = sel[1].shape[0]

    def body(off_ref, sh_ref, g_ref, r_ref, o_ref):
        o_ref[...] = (g_ref[...] + r_ref[...]).astype(out_dtype)

    return pl.pallas_call(
        body, name=name,
        grid_spec=pltpu.PrefetchScalarGridSpec(
            num_scalar_prefetch=2, grid=(nslot, hh // TR),
            in_specs=[pl.BlockSpec((None, TR, C), lambda s, i, off, sh: (sh[s], off[0] + i, 0)),
                      pl.BlockSpec((None, TR, C), lambda s, i, off, sh: (sh[s], i, 0))],
            out_specs=pl.BlockSpec((None, TR, C), lambda s, i, off, sh: (s, i, 0))),
        out_shape=jax.ShapeDtypeStruct((nslot, hh, C), out_dtype),
        compiler_params=_params(("parallel", "parallel")),
    )(sel[0], sel[1], g, got)


def _exchange_chips(bufs):
    nbuf = len(bufs)

    def body(*refs):
        ins, outs = refs[:nbuf], refs[nbuf:2 * nbuf]
        send_sems, recv_sems = refs[2 * nbuf:]
        x, y, c = _place()
        cps = []
        for b in range(nbuf):
            for j, (cx, cy) in enumerate(_other_chips(x, y)):
                k = 3 * b + j
                cp = _remote(ins[b].at[j], outs[b].at[j], send_sems.at[k], recv_sems.at[k], (cx, cy, c))
                cp.start()
                cps.append(cp)
        for cp in cps:
            cp.wait()

    return pl.pallas_call(
        body, name="grad_exchange_chips",
        in_specs=[ANY] * nbuf, out_specs=[ANY] * nbuf,
        out_shape=[jax.ShapeDtypeStruct(b.shape, b.dtype) for b in bufs],
        scratch_shapes=[pltpu.SemaphoreType.DMA((3 * nbuf,)), pltpu.SemaphoreType.DMA((3 * nbuf,))],
        compiler_params=pltpu.CompilerParams(has_side_effects=True),
    )(*bufs)


def _shard_sum(own, got, off, tr, full, rows, name):
    _, hh, C = own.shape

    def body(off_ref, o_ref, r_ref, *rest):
        acc = o_ref[...]
        for j in range(N_CHIPS - 1):
            acc = acc + r_ref[j].astype(F32)
        rest[-1][...] = acc

    in_specs = [pl.BlockSpec((None, tr, C), lambda i, off: (0, i, 0)),
                pl.BlockSpec((N_CHIPS - 1, tr, C), lambda i, off: (0, i, 0))]
    args = [off, own, got]
    aliases = {}
    if full is not None:
        in_specs.append(ANY)
        args.append(full)
        aliases = {3: 0}
    return pl.pallas_call(
        body, name=name,
        grid_spec=pltpu.PrefetchScalarGridSpec(
            num_scalar_prefetch=1, grid=(hh // tr,), in_specs=in_specs,
            out_specs=pl.BlockSpec((tr, C), lambda i, off: (off[0] + i, 0))),
        out_shape=jax.ShapeDtypeStruct((rows, C), F32),
        input_output_aliases=aliases,
        compiler_params=_params(("parallel",)),
    )(*args)


def _join_halves(bufs, spans):
    nbuf = len(bufs)
    ncopy = nbuf * len(spans)

    def body(*refs):
        outs = refs[nbuf:2 * nbuf]
        send_sems, recv_sems = refs[2 * nbuf:]
        x, y, c = _place()
        sib = (x, y, 1 - c)
        cps = []
        for b in range(nbuf):
            for s, (r0, nr) in enumerate(spans[b]):
                k = b * len(spans[b]) + s
                rows = outs[b].at[pl.ds(r0 + c * (nr // 2), nr // 2), :]
                cp = _remote(rows, rows, send_sems.at[k], recv_sems.at[k], sib)
                cp.start()
                cps.append(cp)
        for b in range(nbuf):
            for s, (r0, nr) in enumerate(spans[b]):
                k = b * len(spans[b]) + s
                theirs = outs[b].at[pl.ds(r0 + (1 - c) * (nr // 2), nr // 2), :]
                _remote(theirs, theirs, send_sems.at[k], recv_sems.at[k], sib).wait_recv()
        for cp in cps:
            cp.wait_send()

    return pl.pallas_call(
        body, name="grad_join_halves",
        in_specs=[ANY] * nbuf, out_specs=[ANY] * nbuf,
        out_shape=[jax.ShapeDtypeStruct(b.shape, b.dtype) for b in bufs],
        input_output_aliases={b: b for b in range(nbuf)},
        scratch_shapes=[pltpu.SemaphoreType.DMA((ncopy,)), pltpu.SemaphoreType.DMA((ncopy,))],
        compiler_params=pltpu.CompilerParams(has_side_effects=True),
    )(*bufs)


def _sum_devices(part):
    R = part.shape[0]

    def body(p_ref, o_ref, all_ref, send_sems, recv_sems):
        x, y, c = _place()
        me = 4 * x + 2 * y + c
        all_ref[me] = p_ref[...]
        cps = []
        for k in range(1, N_DEV):
            px, py, pc = x ^ (k >> 2), y ^ ((k >> 1) & 1), c ^ (k & 1)
            cp = _remote(p_ref, all_ref.at[me], send_sems.at[k - 1], recv_sems.at[k - 1], (px, py, pc))
            cp.start()
            cps.append(cp)
        for k in range(1, N_DEV):
            peer = me ^ k
            _remote(p_ref, all_ref.at[peer], send_sems.at[k - 1], recv_sems.at[k - 1], (x, y, c)).wait_recv()
        for cp in cps:
            cp.wait_send()
        acc = all_ref[0]
        for d in range(1, N_DEV):
            acc = acc + all_ref[d]
        o_ref[...] = acc

    return pl.pallas_call(
        body, name="sum_small_grads",
        in_specs=[pl.BlockSpec(memory_space=pltpu.VMEM)],
        out_specs=pl.BlockSpec(memory_space=pltpu.VMEM),
        out_shape=jax.ShapeDtypeStruct((R, LANES), F32),
        scratch_shapes=[pltpu.VMEM((N_DEV, R, LANES), F32),
                        pltpu.SemaphoreType.DMA((N_DEV - 1,)), pltpu.SemaphoreType.DMA((N_DEV - 1,))],
        compiler_params=pltpu.CompilerParams(has_side_effects=True, vmem_limit_bytes=VMEM_LIMIT),
    )(part)


def _pack(parts):
    flat = jnp.concatenate([p.reshape(-1).astype(F32) for p in parts])
    n = flat.shape[0]
    rows = -(-n // LANES)
    rows = -(-rows // 8) * 8
    return jnp.pad(flat, (0, rows * LANES - n)).reshape(rows, LANES)


def _unpack(packed, shapes):
    flat = packed.reshape(-1)
    out, off = [], 0
    for s in shapes:
        n = int(np.prod(s))
        out.append(flat[off:off + n].reshape(s))
        off += n
    return out


def kernel(x, rel_bias, norm_mix_g, w_in, q_norm_g, k_norm_g, sinks, conv_w, conv_b, conv_ln_g, conv_ln_b, attn_out_g, conv_out_g, w_out, norm_mlp_g, w_mlp_up, w_mlp_down, loss_target, m_rel_bias, m_norm_mix_g, m_w_in, m_q_norm_g, m_k_norm_g, m_sinks, m_conv_w, m_conv_b, m_conv_ln_g, m_conv_ln_b, m_attn_out_g, m_conv_out_g, m_w_out, m_norm_mlp_g, m_w_mlp_up, m_w_mlp_down, v_rel_bias, v_norm_mix_g, v_w_in, v_q_norm_g, v_k_norm_g, v_sinks, v_conv_w, v_conv_b, v_conv_ln_g, v_conv_ln_b, v_attn_out_g, v_conv_out_g, v_w_out, v_norm_mlp_g, v_w_mlp_up, v_w_mlp_down):
    T = x.shape[1]
    L = DEPTH
    xi, yi, ci = _place()
    shard = 2 * xi + yi
    in_sh = IN_WIDTH // N_CHIPS
    out_sh = MIX_WIDTH // N_CHIPS
    ff_sh = D_FF // N_CHIPS
    cv_sh = CONV_WIDTH // N_CHIPS

    def my_shards(lo, hi):
        n = hi - lo
        cw_pad = jnp.pad(conv_w[lo:hi], ((0, 0), (0, CONV_ROWS - CONV_KERNEL), (0, 0)))
        return [w_in[lo:hi].astype(BF16).reshape(n * D_MODEL, in_sh),
                w_out[lo:hi].astype(BF16).reshape(n * out_sh, D_MODEL),
                w_mlp_up[lo:hi].astype(BF16).reshape(n * D_MODEL, ff_sh),
                w_mlp_down[lo:hi].astype(BF16).reshape(n * ff_sh, D_MODEL),
                cw_pad.reshape(n * CONV_ROWS, cv_sh)]

    def whole_weights(gathered, mine, n):
        g_in, g_out, g_up, g_down, g_cw = [lax.dynamic_update_slice(got, own[None], (shard, 0, 0))
                                           for got, own in zip(gathered, mine)]
        return (g_in.reshape(N_CHIPS, n, D_MODEL, in_sh).transpose(1, 2, 0, 3).reshape(n, D_MODEL, IN_WIDTH),
                g_out.reshape(N_CHIPS, n, out_sh, D_MODEL).transpose(1, 0, 2, 3).reshape(n, MIX_WIDTH, D_MODEL),
                g_up.reshape(N_CHIPS, n, D_MODEL, ff_sh).transpose(1, 2, 0, 3).reshape(n, D_MODEL, D_FF),
                g_down.reshape(N_CHIPS, n, ff_sh, D_MODEL).transpose(1, 0, 2, 3).reshape(n, D_FF, D_MODEL),
                g_cw.reshape(N_CHIPS, n, CONV_ROWS, cv_sh).transpose(1, 2, 0, 3).reshape(n, CONV_ROWS, CONV_WIDTH))

    mine0, mine1 = my_shards(0, 1), my_shards(1, L)
    got0 = _gather_shards(mine0)
    weights = {0: whole_weights(got0, mine0, 1)}
    mine1[4], _ = lax.optimization_barrier((mine1[4], got0[4]))
    gather_plan = _gather_plan([m.shape for m in mine1])
    gather_sems = _start_copies("gather_rest_start", mine1, [(N_CHIPS,) + m.shape for m in mine1], gather_plan)

    bucket = _band_buckets()
    bk = jnp.asarray(bucket)[None]
    biasc = jnp.zeros((N_HEADS, BLOCK, BLOCK), F32)
    for b in range(NUM_BUCKETS):
        biasc = jnp.where(bk == b, rel_bias[b][:, None, None], biasc)
    biasc = biasc.reshape(N_KV_HEADS, GROUP_ROWS, BLOCK)
    onehot_t = np.zeros((LANES, BLOCK * BLOCK), np.float32)
    onehot_t[bucket.reshape(-1), np.arange(BLOCK * BLOCK)] = 1.0
    onehot_t = jnp.asarray(onehot_t, dtype=BF16)
    sink_rows = lambda l: jnp.repeat(sinks[l], BLOCK).reshape(N_KV_HEADS, GROUP_ROWS, 1)

    row = lambda a, l: a[l][None, :]

    xs = x.reshape(T, D_MODEL)
    saved = []
    for l in range(L):
        if l == 1:
            send_sems, recv_sems, srcs, lands, _ = gather_sems
            lands = _wait_copies("gather_rest_wait", send_sems, recv_sems, srcs, lands, gather_plan, xs)
            weights[1] = whole_weights(_forward_halves(lands), mine1, L - 1)
        (W_in, W_out, W_up, W_down, CW), wl = weights[min(l, 1)], max(l - 1, 0)
        h, z = _norm_matmul(xs, row(norm_mix_g, l), W_in, wl, F32, "mix_in_proj", gather_sems[4] if l == 0 else None)
        a = _attn_fwd(z, biasc, sink_rows(l), row(q_norm_g, l), row(k_norm_g, l))
        yc = _conv_fwd(z, CW[wl], row(conv_b, l))
        mix = _mix_norm(a, yc, row(conv_ln_g, l), row(conv_ln_b, l), row(attn_out_g, l), row(conv_out_g, l))
        x1 = _matmul_res(mix, W_out, wl, xs, False, "mix_out_proj")
        h2, up = _norm_matmul(x1, row(norm_mlp_g, l), W_up, wl, BF16, "mlp_up_proj")
        x2 = _matmul_res(up, W_down, wl, x1, True, "mlp_down_proj")
        saved.append((xs, h, z, a, yc, mix, x1, h2, up))
        xs = x2

    loss_parts, g = _loss_grad(xs, loss_target.reshape(T, D_MODEL))

    names = ["w_in", "w_out", "w_mlp_up", "w_mlp_down"]
    shard_rows = [D_MODEL, out_sh, D_MODEL, ff_sh]
    own_sel = shard.astype(jnp.int32)[None]
    send_sel = jnp.stack([shard ^ 2, shard ^ 1, shard ^ 3]).astype(jnp.int32)

    def chip_sums(bufs, n, tag):
        b_in, b_out, b_up, b_down = bufs
        G = [b_in.reshape(n, D_MODEL, N_CHIPS, in_sh).transpose(2, 0, 1, 3).reshape(N_CHIPS, n * D_MODEL, in_sh),
             b_out.reshape(N_CHIPS, n * out_sh, D_MODEL), b_up.reshape(N_CHIPS, n * D_MODEL, ff_sh),
             b_down.reshape(N_CHIPS, n * ff_sh, D_MODEL)]
        got = _swap_halves(G)
        owns, sends = [], []
        for b in range(4):
            hh = G[b].shape[1] // 2
            off = (ci * (hh // _tile(hh, 512))).astype(jnp.int32)[None]
            owns.append(_chip_sum(G[b], got[b], (off, own_sel), F32, "chip_sum_own_" + names[b] + tag))
            sends.append(_chip_sum(G[b], got[b], (off, send_sel), BF16, "chip_sum_send_" + names[b] + tag))
        return owns, sends

    stacked = {0: [None] * 4, 1: [None] * 4}
    small = [None] * L
    dbias_sum = None
    exchange = None
    for l in reversed(range(L)):
        x0, h, z, a, yc, mix, x1, h2, up = saved[l]
        (W_in, W_out, W_up, W_down, CW), wl = weights[min(l, 1)], max(l - 1, 0)
        grp, n = min(l, 1), (L - 1 if l else 1)
        b_in, b_out, b_up, b_down = stacked[grp]
        token = None
        if l == 0:
            owns1, sends1 = chip_sums(stacked[1], L - 1, "_rest")
            exchange_plan = _exchange_plan([s.shape for s in sends1])
            exchange = _start_copies("grad_exchange_rest_start", sends1, [s.shape for s in sends1], exchange_plan)
            token = exchange[4]
        d_up = _dact(g, W_down, wl, up, token)
        b_down = _matmul_tn(up, g, True, b_down, (N_CHIPS, n, ff_sh, D_MODEL), (None, None, ff_sh, D_MODEL),
                            lambda i, j: (i, wl, 0, 0), ff_sh, D_MODEL, "grad_w_mlp_down")
        b_up = _matmul_tn(h2, d_up, False, b_up, (N_CHIPS, n, D_MODEL, ff_sh), (None, None, D_MODEL, ff_sh),
                          lambda i, j: (j, wl, 0, 0), D_MODEL, ff_sh, "grad_w_mlp_up")
        g1, d_gmlp = _matmul_nt_normbwd(d_up, W_up, wl, x1, row(norm_mlp_g, l), g, "mlp_in_bwd")
        d_a, d_y, sm_mix = _mix_bwd(g1, W_out, wl, a, yc, row(conv_ln_g, l), row(conv_ln_b, l),
                                    row(attn_out_g, l), row(conv_out_g, l))
        b_out = _matmul_tn(mix, g1, False, b_out, (N_CHIPS, n, out_sh, D_MODEL), (N_CHIPS, None, out_sh, D_MODEL),
                           lambda i, j: (0, wl, 0, 0), MIX_WIDTH, D_MODEL, "grad_w_out")
        d_u, d_gate, d_cw = _conv_bwd(d_y, z, CW[wl])
        d_q, d_kv, dbias, sm_attn = _attn_bwd(z, d_a, biasc, sink_rows(l), row(q_norm_g, l), row(k_norm_g, l))
        dbias_sum = dbias if dbias_sum is None else dbias_sum + dbias
        d_z = jnp.concatenate([d_q, d_kv[BLOCK:BLOCK + T], d_u, d_gate], axis=1)
        b_in = _matmul_tn(h, d_z, False, b_in, (n, D_MODEL, IN_WIDTH), (None, D_MODEL, IN_WIDTH),
                          lambda i, j: (wl, 0, 0), D_MODEL, IN_WIDTH, "grad_w_in")
        g, d_gmix = _matmul_nt_normbwd(d_z, W_in, wl, x0, row(norm_mix_g, l), g1, "mix_in_bwd")
        stacked[grp] = [b_in, b_out, b_up, b_down]
        small[l] = (d_gmix[0], sm_attn[0, :HEAD_DIM], sm_attn[1, :HEAD_DIM], sm_attn[2, :N_HEADS],
                    d_cw[:CONV_KERNEL], sm_mix[4], sm_mix[2], sm_mix[3], sm_mix[0], sm_mix[1], d_gmlp[0])
    grad_x = g.reshape(1, T, D_MODEL)

    d_rel = _bucket_reduce(dbias_sum.reshape(N_HEADS, BLOCK * BLOCK), onehot_t)[:, :NUM_BUCKETS].T
    stack = lambda k: jnp.stack([small[l][k] for l in range(L)])
    small_shapes = [(), (NUM_BUCKETS, N_HEADS), (L, D_MODEL), (L, HEAD_DIM), (L, HEAD_DIM), (L, N_HEADS),
                    (L, CONV_KERNEL, CONV_WIDTH), (L, CONV_WIDTH), (L, CONV_WIDTH), (L, CONV_WIDTH),
                    (L, CONV_WIDTH), (L, CONV_WIDTH), (L, D_MODEL)]
    part = _pack([jnp.sum(loss_parts[:, 0, 0]), d_rel] + [stack(k) for k in range(11)])
    tot = _unpack(_sum_devices(part), small_shapes)
    loss = tot[0]
    (g_rel, g_nmix, g_qn, g_kn, g_sk, g_cw_full, g_cb, g_lng, g_lnb, g_aog, g_cog, g_nmlp) = tot[1:]
    g_cw_sh = lax.dynamic_slice_in_dim(g_cw_full, shard * cv_sh, cv_sh, axis=2)

    small_w = [rel_bias, norm_mix_g, q_norm_g, k_norm_g, sinks, conv_w, conv_b, conv_ln_g, conv_ln_b,
               attn_out_g, conv_out_g, norm_mlp_g]
    small_m = [m_rel_bias, m_norm_mix_g, m_q_norm_g, m_k_norm_g, m_sinks, m_conv_w, m_conv_b, m_conv_ln_g,
               m_conv_ln_b, m_attn_out_g, m_conv_out_g, m_norm_mlp_g]
    small_v = [v_rel_bias, v_norm_mix_g, v_q_norm_g, v_k_norm_g, v_sinks, v_conv_w, v_conv_b, v_conv_ln_g,
               v_conv_ln_b, v_attn_out_g, v_conv_out_g, v_norm_mlp_g]
    small_g = [g_rel, g_nmix, g_qn, g_kn, g_sk, g_cw_sh, g_cb, g_lng, g_lnb, g_aog, g_cog, g_nmlp]
    shapes = [w.shape for w in small_w]
    sd, sm_, sv_ = _adamw(_pack(small_w), _pack(small_g), _pack(small_m), _pack(small_v), "adamw_small")
    small_d, small_nm, small_nv = _unpack(sd, shapes), _unpack(sm_, shapes), _unpack(sv_, shapes)

    send_sems, recv_sems, srcs, lands, _ = exchange
    arrived1 = _wait_copies("grad_exchange_rest_wait", send_sems, recv_sems, srcs, lands, exchange_plan, g)
    owns0, sends0 = chip_sums(stacked[0], 1, "_first")
    arrived0 = _exchange_chips(sends0)
    grads, spans = [], []
    for b in range(4):
        R = shard_rows[b]
        full = None
        spans.append([(0, R), (R, (L - 1) * R)])
        for (r0, nr), own, arrived, tag in zip(spans[b], (owns0[b], owns1[b]), (arrived0[b], arrived1[b]), ("_first", "_rest")):
            tr = min(512, math.gcd(R, nr // 2))
            off = ((r0 + ci * (nr // 2)) // tr).astype(jnp.int32)[None]
            full = _shard_sum(own, arrived, off, tr, full, L * R, "shard_sum_" + names[b] + tag)
        grads.append(full)
    grads = _join_halves(grads, spans)

    big_w = [w_in, w_out, w_mlp_up, w_mlp_down]
    big_m = [m_w_in, m_w_out, m_w_mlp_up, m_w_mlp_down]
    big_v = [v_w_in, v_w_out, v_w_mlp_up, v_w_mlp_down]
    big_g, big_d, big_nm, big_nv = [], [], [], []
    for b in range(4):
        shp = big_w[b].shape
        flat = lambda t: t.reshape(shp[0] * shp[1], shp[2])
        d, nm, nv = _adamw(flat(big_w[b]), grads[b], flat(big_m[b]), flat(big_v[b]), "adamw_" + names[b])
        big_g.append(grads[b].reshape(shp))
        big_d.append(d.reshape(shp))
        big_nm.append(nm.reshape(shp))
        big_nv.append(nv.reshape(shp))

    def ordered(sm, bg):
        return [sm[0], sm[1], bg[0], sm[2], sm[3], sm[4], sm[5], sm[6], sm[7], sm[8], sm[9], sm[10], bg[1], sm[11],
                bg[2], bg[3]]

    return (loss, grad_x, *ordered(small_g, big_g), *ordered(small_d, big_d), *ordered(small_nm, big_nm),
            *ordered(small_nv, big_nv))
```

```python
import math

import numpy as np
import jax
import jax.numpy as jnp
from jax import lax
from jax.experimental import pallas as pl
from jax.experimental.pallas import tpu as pltpu

F32 = jnp.float32
BF16 = jnp.bfloat16

D_MODEL = 1024
DEPTH = 4
HEAD_DIM = 64
N_HEADS = 8
N_KV_HEADS = 2
GQA_GROUP = N_HEADS // N_KV_HEADS
ATTN_WIDTH = N_HEADS * HEAD_DIM
KV_WIDTH = N_KV_HEADS * HEAD_DIM
CONV_WIDTH = D_MODEL - ATTN_WIDTH
MIX_WIDTH = ATTN_WIDTH + CONV_WIDTH
IN_WIDTH = ATTN_WIDTH + 2 * KV_WIDTH + 2 * CONV_WIDTH
BLOCK = 128
CONV_KERNEL = 31
CONV_ROWS = 32
HALO = 32
CONV_CH = 256
NUM_BUCKETS = 32
MAX_DISTANCE = 128
D_FF = 4 * D_MODEL
EPS = 1e-6
NEG = -1e30
SCALE = 1.0 / math.sqrt(HEAD_DIM)

ADAM_LR = 0.001
ADAM_B1 = 0.9
ADAM_B2 = 0.999
ADAM_EPS = 1e-08
ADAM_WD = 0.01
ADAM_STEP = 10

N_CHIPS = 4
N_DEV = 8
LANES = 128
VMEM_LIMIT = 52 * 1024 * 1024
K_CHUNK = 4096

Q0, K0, V0, U0, G0 = 0, ATTN_WIDTH, ATTN_WIDTH + KV_WIDTH, ATTN_WIDTH + 2 * KV_WIDTH, ATTN_WIDTH + 2 * KV_WIDTH + CONV_WIDTH

NT = (((1,), (1,)), ((), ()))
TN = (((0,), (0,)), ((), ()))
MESH = pl.DeviceIdType.MESH
ANY = pl.BlockSpec(memory_space=pl.ANY)


def _params(sem=None):
    return pltpu.CompilerParams(dimension_semantics=sem, vmem_limit_bytes=VMEM_LIMIT)


def _chunk(n, cap=1024):
    for c in range(cap, 0, -LANES):
        if n % c == 0:
            return c
    raise ValueError(n)


def _tile(t, want):
    return min(t, want)


def _after_spec(after):
    return [] if after is None else [pl.BlockSpec((8, LANES), lambda *_: (0, 0))]


def _after_arg(after):
    return [] if after is None else [after]


def _t5_bucket(n):
    n = np.asarray(n)
    max_exact = NUM_BUCKETS // 2
    large = max_exact + (np.log(np.maximum(n, 1) / max_exact) / np.log(MAX_DISTANCE / max_exact)
                         * (NUM_BUCKETS - max_exact)).astype(np.int32)
    large = np.minimum(large, NUM_BUCKETS - 1)
    return np.where(n < max_exact, n, large).astype(np.int32)


def _band_buckets():
    qi = np.arange(BLOCK)[:, None]
    j = np.arange(BLOCK)[None, :]
    return _t5_bucket(np.where(j <= qi, qi - j, qi + BLOCK - j))


def _norm_matmul(x, g, w_all, l, out_dtype, name, after=None):
    T, D = x.shape
    N = w_all.shape[2]
    TM = _tile(T, 512)
    CH = _chunk(N)

    def body(x_ref, g_ref, w_ref, *rest):
        h_ref, z_ref = rest[-2:]
        xv = x_ref[...]
        r = lax.rsqrt(jnp.mean(xv * xv, axis=-1, keepdims=True) + EPS)
        h = (xv * r * g_ref[...]).astype(BF16)
        h_ref[...] = h
        for c0 in range(0, N, CH):
            z_ref[:, c0:c0 + CH] = jnp.dot(h, w_ref[:, c0:c0 + CH], preferred_element_type=F32).astype(z_ref.dtype)

    return pl.pallas_call(
        body, name=name, grid=(T // TM,),
        in_specs=[pl.BlockSpec((TM, D), lambda i: (i, 0)),
                  pl.BlockSpec((1, D), lambda i: (0, 0)),
                  pl.BlockSpec((None, D, N), lambda i: (l, 0, 0))] + _after_spec(after),
        out_specs=[pl.BlockSpec((TM, D), lambda i: (i, 0)),
                   pl.BlockSpec((TM, N), lambda i: (i, 0))],
        out_shape=[jax.ShapeDtypeStruct((T, D), BF16), jax.ShapeDtypeStruct((T, N), out_dtype)],
        compiler_params=_params(("parallel",)),
    )(x, g, w_all, *_after_arg(after))


def _matmul_res(a, w_all, l, res, relu2, name):
    T, K = a.shape
    N = w_all.shape[2]
    TM = _tile(T, 512)
    CH = _chunk(K, K_CHUNK)

    def body(a_ref, w_ref, res_ref, o_ref):
        acc = res_ref[...]
        for k0 in range(0, K, CH):
            av = a_ref[:, k0:k0 + CH]
            if relu2:
                av = jnp.square(jnp.maximum(av.astype(F32), 0.0)).astype(BF16)
            acc = acc + jnp.dot(av, w_ref[k0:k0 + CH, :], preferred_element_type=F32)
        o_ref[...] = acc

    return pl.pallas_call(
        body, name=name, grid=(T // TM,),
        in_specs=[pl.BlockSpec((TM, K), lambda i: (i, 0)),
                  pl.BlockSpec((None, K, N), lambda i: (l, 0, 0)),
                  pl.BlockSpec((TM, N), lambda i: (i, 0))],
        out_specs=pl.BlockSpec((TM, N), lambda i: (i, 0)),
        out_shape=jax.ShapeDtypeStruct((T, N), F32),
        compiler_params=_params(("parallel",)),
    )(a, w_all, res)


def _head_norm(t, g):
    r = lax.rsqrt(jnp.mean(t * t, axis=-1, keepdims=True) + EPS)
    that = t * r
    return that * g, that, r


def _softmax_sink(s, sink):
    m = jnp.maximum(jnp.max(s, axis=-1, keepdims=True), sink)
    p = jnp.exp(s - m)
    es = jnp.exp(sink - m)
    den = jnp.sum(p, axis=-1, keepdims=True) + es
    return p / den, es / den


GROUP_ROWS = GQA_GROUP * BLOCK


def _own_block():
    row = lax.broadcasted_iota(jnp.int32, (GROUP_ROWS, BLOCK), 0)
    col = lax.broadcasted_iota(jnp.int32, (GROUP_ROWS, BLOCK), 1)
    return (row & (BLOCK - 1)) >= col


ATTN_QB = 4
KV_COLS = [slice(k * HEAD_DIM, (k + 1) * HEAD_DIM) for k in range(N_KV_HEADS)]


def _blk(i):
    return pl.ds(i * BLOCK, BLOCK)


def _stack_heads(ref, i, kvh):
    return jnp.concatenate([ref[_blk(i), (kvh * GQA_GROUP + g) * HEAD_DIM:(kvh * GQA_GROUP + g + 1) * HEAD_DIM]
                            for g in range(GQA_GROUP)], axis=0)


def _unstack_heads(ref, i, kvh, val):
    for g in range(GQA_GROUP):
        h = kvh * GQA_GROUP + g
        ref[_blk(i), h * HEAD_DIM:(h + 1) * HEAD_DIM] = val[g * BLOCK:(g + 1) * BLOCK]


def _band_probs(first, own, s_own, s_prev, bias, sink):
    s = jnp.where(own, s_own, s_prev) * SCALE + bias
    if first is not None:
        s = jnp.where(jnp.logical_or(own, jnp.logical_not(first)), s, NEG)
    return _softmax_sink(s, sink)


def _dot_nt(a, b):
    return lax.dot_general(a, b, NT, preferred_element_type=F32)


def _dot_tn(a, b):
    return lax.dot_general(a, b, TN, preferred_element_type=F32)


def _attn_fwd(z, biasc, sink_rows, qg, kg):
    T = z.shape[0]
    nb = T // BLOCK
    qb = min(ATTN_QB, nb)
    TQ = qb * BLOCK
    kb, vb = K0 // KV_WIDTH, V0 // KV_WIDTH
    groups = [(i, k) for i in range(qb) for k in range(N_KV_HEADS)]

    def body(q_ref, kc_ref, kp_ref, vc_ref, vp_ref, b_ref, sk_ref, qg_ref, kg_ref, a_ref):
        n = pl.program_id(0)
        own = _own_block()
        kn, vv = {}, {}
        for k in range(N_KV_HEADS):
            kn[-1, k] = _head_norm(kp_ref[:, KV_COLS[k]], kg_ref[...])[0].astype(BF16)
            vv[-1, k] = vp_ref[:, KV_COLS[k]].astype(BF16)
        for i, k in groups:
            kn[i, k] = _head_norm(kc_ref[_blk(i), KV_COLS[k]], kg_ref[...])[0].astype(BF16)
            vv[i, k] = vc_ref[_blk(i), KV_COLS[k]].astype(BF16)
        qnb = {g: _head_norm(_stack_heads(q_ref, *g), qg_ref[...])[0].astype(BF16) for g in groups}
        s_own = {(i, k): _dot_nt(qnb[i, k], kn[i, k]) for i, k in groups}
        s_prev = {(i, k): _dot_nt(qnb[i, k], kn[i - 1, k]) for i, k in groups}
        p = {(i, k): _band_probs(n == 0 if i == 0 else None, own, s_own[i, k], s_prev[i, k], b_ref[k], sk_ref[k])[0]
             for i, k in groups}
        p_own = {g: jnp.where(own, p[g], 0.0).astype(BF16) for g in groups}
        p_prev = {g: jnp.where(own, 0.0, p[g]).astype(BF16) for g in groups}
        o_own = {(i, k): jnp.dot(p_own[i, k], vv[i, k], preferred_element_type=F32) for i, k in groups}
        o_prev = {(i, k): jnp.dot(p_prev[i, k], vv[i - 1, k], preferred_element_type=F32) for i, k in groups}
        for i, k in groups:
            _unstack_heads(a_ref, i, k, o_own[i, k] + o_prev[i, k])

    prev = lambda n: jnp.maximum(n * qb - 1, 0)
    return pl.pallas_call(
        body, name="attn_fwd", grid=(nb // qb,),
        in_specs=[pl.BlockSpec((TQ, ATTN_WIDTH), lambda n: (n, 0)),
                  pl.BlockSpec((TQ, KV_WIDTH), lambda n: (n, kb)),
                  pl.BlockSpec((BLOCK, KV_WIDTH), lambda n: (prev(n), kb)),
                  pl.BlockSpec((TQ, KV_WIDTH), lambda n: (n, vb)),
                  pl.BlockSpec((BLOCK, KV_WIDTH), lambda n: (prev(n), vb)),
                  pl.BlockSpec((N_KV_HEADS, GROUP_ROWS, BLOCK), lambda n: (0, 0, 0)),
                  pl.BlockSpec((N_KV_HEADS, GROUP_ROWS, 1), lambda n: (0, 0, 0)),
                  pl.BlockSpec((1, HEAD_DIM), lambda n: (0, 0)),
                  pl.BlockSpec((1, HEAD_DIM), lambda n: (0, 0))],
        out_specs=pl.BlockSpec((TQ, ATTN_WIDTH), lambda n: (n, 0)),
        out_shape=jax.ShapeDtypeStruct((T, ATTN_WIDTH), F32),
        compiler_params=_params(("parallel",)),
    )(z, z, z, z, z, biasc, sink_rows, qg, kg)


SHIFTS = 8
CONV_RC = 64


def _shifted_copies(src_ref, dst_ref, total):
    for b in range(SHIFTS):
        rows = (total - b) // SHIFTS * SHIFTS
        for r0 in range(0, rows, CONV_RC):
            nr = min(CONV_RC, rows - r0)
            dst_ref[b, pl.ds(r0, nr), :] = src_ref[pl.ds(r0 + b, nr), :]


def _tap(ref, r0, o):
    return ref[o % SHIFTS, pl.ds(r0 + (o // SHIFTS) * SHIFTS, CONV_RC), :]


def _conv_fwd(z, cw, cb):
    T = z.shape[0]
    TC = _tile(T, 512)
    ub, gb = U0 // CONV_CH, G0 // CONV_CH
    hpt = TC // HALO
    lead = HALO - (CONV_KERNEL - 1)

    def body(u_ref, g_ref, up_ref, gp_ref, w_ref, b_ref, y_ref, hp_ref, hs_ref):
        i = pl.program_id(0)
        hp_ref[pl.ds(0, HALO), :] = jnp.where(i > 0, up_ref[...] * jax.nn.sigmoid(gp_ref[...]), 0.0)
        hp_ref[pl.ds(HALO, TC), :] = u_ref[...] * jax.nn.sigmoid(g_ref[...])
        _shifted_copies(hp_ref, hs_ref, TC + HALO)
        for r0 in range(0, TC, CONV_RC):
            acc = jnp.zeros((CONV_RC, CONV_CH), F32) + b_ref[...]
            for j in range(CONV_KERNEL):
                acc = acc + _tap(hs_ref, r0, lead + j) * w_ref[pl.ds(j, 1), :]
            y_ref[pl.ds(r0, CONV_RC), :] = acc

    prev = lambda i: jnp.maximum(i * hpt - 1, 0)
    return pl.pallas_call(
        body, name="conv_fwd", grid=(T // TC, CONV_WIDTH // CONV_CH),
        in_specs=[pl.BlockSpec((TC, CONV_CH), lambda i, j: (i, ub + j)),
                  pl.BlockSpec((TC, CONV_CH), lambda i, j: (i, gb + j)),
                  pl.BlockSpec((HALO, CONV_CH), lambda i, j: (prev(i), ub + j)),
                  pl.BlockSpec((HALO, CONV_CH), lambda i, j: (prev(i), gb + j)),
                  pl.BlockSpec((CONV_ROWS, CONV_CH), lambda i, j: (0, j)),
                  pl.BlockSpec((1, CONV_CH), lambda i, j: (0, j))],
        out_specs=pl.BlockSpec((TC, CONV_CH), lambda i, j: (i, j)),
        out_shape=jax.ShapeDtypeStruct((T, CONV_WIDTH), F32),
        scratch_shapes=[pltpu.VMEM((TC + HALO, CONV_CH), F32), pltpu.VMEM((SHIFTS, TC + HALO, CONV_CH), F32)],
        compiler_params=_params(("parallel", "parallel")),
    )(z, z, z, z, cw, cb)


def _ln_silu(y, ln_g, ln_b):
    mu = jnp.mean(y, axis=-1, keepdims=True)
    yc = y - mu
    var = jnp.mean(yc * yc, axis=-1, keepdims=True)
    rstd = lax.rsqrt(var + EPS)
    yhat = yc * rstd
    yn = yhat * ln_g + ln_b
    sg = jax.nn.sigmoid(yn)
    return yn * sg, yn, sg, yhat, rstd


def _mix_norm(a, y, ln_g, ln_b, ag, cg):
    T = a.shape[0]
    TM = _tile(T, 512)

    def body(a_ref, y_ref, lg_ref, lb_ref, ag_ref, cg_ref, o_ref):
        av = a_ref[...]
        ra = lax.rsqrt(jnp.mean(av * av, axis=-1, keepdims=True) + EPS)
        o_ref[:, :ATTN_WIDTH] = (av * ra * ag_ref[...]).astype(BF16)
        c, _, _, _, _ = _ln_silu(y_ref[...], lg_ref[...], lb_ref[...])
        rc = lax.rsqrt(jnp.mean(c * c, axis=-1, keepdims=True) + EPS)
        o_ref[:, ATTN_WIDTH:] = (c * rc * cg_ref[...]).astype(BF16)

    vec = pl.BlockSpec((1, CONV_WIDTH), lambda i: (0, 0))
    return pl.pallas_call(
        body, name="mix_norm", grid=(T // TM,),
        in_specs=[pl.BlockSpec((TM, ATTN_WIDTH), lambda i: (i, 0)),
                  pl.BlockSpec((TM, CONV_WIDTH), lambda i: (i, 0)), vec, vec, vec, vec],
        out_specs=pl.BlockSpec((TM, MIX_WIDTH), lambda i: (i, 0)),
        out_shape=jax.ShapeDtypeStruct((T, MIX_WIDTH), BF16),
        compiler_params=_params(("parallel",)),
    )(a, y, ln_g, ln_b, ag, cg)


def _loss_grad(y, tgt):
    T, D = y.shape
    TM = _tile(T, 512)
    nt = T // TM

    def body(y_ref, t_ref, part_ref, dy_ref):
        diff = y_ref[...] - t_ref[...]
        dy_ref[...] = diff / D
        tok = jnp.mean(diff * diff, axis=-1, keepdims=True)
        part_ref[...] = jnp.zeros((1, LANES), F32) + 0.5 * jnp.sum(tok)

    return pl.pallas_call(
        body, name="loss_grad", grid=(nt,),
        in_specs=[pl.BlockSpec((TM, D), lambda i: (i, 0)), pl.BlockSpec((TM, D), lambda i: (i, 0))],
        out_specs=[pl.BlockSpec((None, 1, LANES), lambda i: (i, 0, 0)), pl.BlockSpec((TM, D), lambda i: (i, 0))],
        out_shape=[jax.ShapeDtypeStruct((nt, 1, LANES), F32), jax.ShapeDtypeStruct((T, D), F32)],
        compiler_params=_params(("parallel",)),
    )(y, tgt)


def _dact(g, w_all, l, up, after=None):
    T, N = g.shape
    K = w_all.shape[1]
    TM = _tile(T, 512)
    CH = _chunk(K)

    def body(g_ref, w_ref, up_ref, *rest):
        o_ref = rest[-1]
        gv = g_ref[...].astype(BF16)
        for k0 in range(0, K, CH):
            da = lax.dot_general(gv, w_ref[k0:k0 + CH, :], NT, preferred_element_type=F32)
            upv = up_ref[:, k0:k0 + CH].astype(F32)
            o_ref[:, k0:k0 + CH] = (da * (2.0 * jnp.maximum(upv, 0.0))).astype(BF16)

    return pl.pallas_call(
        body, name="mlp_dact", grid=(T // TM,),
        in_specs=[pl.BlockSpec((TM, N), lambda i: (i, 0)),
                  pl.BlockSpec((None, K, N), lambda i: (l, 0, 0)),
                  pl.BlockSpec((TM, K), lambda i: (i, 0))] + _after_spec(after),
        out_specs=pl.BlockSpec((TM, K), lambda i: (i, 0)),
        out_shape=jax.ShapeDtypeStruct((T, K), BF16),
        compiler_params=_params(("parallel",)),
    )(g, w_all, up, *_after_arg(after))


def _matmul_tn(a, b, relu2, buf, buf_shape, out_block, out_index, tm, tn, name):
    T, M = a.shape
    N = b.shape[1]
    TK = _tile(T, 1024)
    nk = T // TK

    def body(*refs):
        a_ref, b_ref = refs[0], refs[1]
        o_ref = refs[-1]
        k = pl.program_id(2)
        av = a_ref[...]
        if relu2:
            av = jnp.square(jnp.maximum(av.astype(F32), 0.0)).astype(BF16)
        c = lax.dot_general(av, b_ref[...].astype(BF16), TN, preferred_element_type=F32).reshape(o_ref.shape)

        @pl.when(k == 0)
        def _():
            o_ref[...] = c

        @pl.when(k > 0)
        def _():
            o_ref[...] += c

    in_specs = [pl.BlockSpec((TK, tm), lambda i, j, k: (k, i)), pl.BlockSpec((TK, tn), lambda i, j, k: (k, j))]
    args = [a, b]
    aliases = {}
    if buf is not None:
        in_specs.append(ANY)
        args.append(buf)
        aliases = {2: 0}
    return pl.pallas_call(
        body, name=name, grid=(M // tm, N // tn, nk),
        in_specs=in_specs,
        out_specs=pl.BlockSpec(out_block, lambda i, j, k: out_index(i, j)),
        out_shape=jax.ShapeDtypeStruct(buf_shape, F32),
        input_output_aliases=aliases,
        compiler_params=_params(("parallel", "parallel", "arbitrary")),
    )(*args)


def _matmul_nt_normbwd(dz, w_all, l, x, gvec, gres, name, after=None):
    T, K = dz.shape
    D = x.shape[1]
    TM = _tile(T, 512)
    CH = _chunk(K, K_CHUNK)

    def body(dz_ref, w_ref, x_ref, gv_ref, gr_ref, *rest):
        o_ref, dg_ref = rest[-2:]
        i = pl.program_id(0)
        dh = jnp.zeros((TM, D), F32)
        for k0 in range(0, K, CH):
            dh = dh + lax.dot_general(dz_ref[:, k0:k0 + CH], w_ref[:, k0:k0 + CH], NT, preferred_element_type=F32)
        xv = x_ref[...]
        r = lax.rsqrt(jnp.mean(xv * xv, axis=-1, keepdims=True) + EPS)
        xhat = xv * r
        dg = jnp.sum(dh * xhat, axis=0, keepdims=True)

        @pl.when(i == 0)
        def _():
            dg_ref[...] = dg

        @pl.when(i > 0)
        def _():
            dg_ref[...] += dg

        wv = dh * gv_ref[...]
        o_ref[...] = gr_ref[...] + r * (wv - xhat * jnp.mean(wv * xhat, axis=-1, keepdims=True))

    return pl.pallas_call(
        body, name=name, grid=(T // TM,),
        in_specs=[pl.BlockSpec((TM, K), lambda i: (i, 0)),
                  pl.BlockSpec((None, D, K), lambda i: (l, 0, 0)),
                  pl.BlockSpec((TM, D), lambda i: (i, 0)),
                  pl.BlockSpec((1, D), lambda i: (0, 0)),
                  pl.BlockSpec((TM, D), lambda i: (i, 0))] + _after_spec(after),
        out_specs=[pl.BlockSpec((TM, D), lambda i: (i, 0)), pl.BlockSpec((1, D), lambda i: (0, 0))],
        out_shape=[jax.ShapeDtypeStruct((T, D), F32), jax.ShapeDtypeStruct((1, D), F32)],
        compiler_params=_params(("arbitrary",)),
    )(dz, w_all, x, gvec, gres, *_after_arg(after))


def _mix_bwd(g1, w_all, l, a, y, ln_g, ln_b, ag, cg):
    T, D = g1.shape
    TM = _tile(T, 512)

    def body(g_ref, w_ref, a_ref, y_ref, lg_ref, lb_ref, ag_ref, cg_ref, da_ref, dy_ref, sm_ref):
        i = pl.program_id(0)
        dmix = lax.dot_general(g_ref[...].astype(BF16), w_ref[...], NT, preferred_element_type=F32)
        dma, dmc = dmix[:, :ATTN_WIDTH], dmix[:, ATTN_WIDTH:]
        av = a_ref[...]
        ra = lax.rsqrt(jnp.mean(av * av, axis=-1, keepdims=True) + EPS)
        ahat = av * ra
        d_ag = jnp.sum(dma * ahat, axis=0, keepdims=True)
        wa = dma * ag_ref[...]
        da_ref[...] = ra * (wa - ahat * jnp.mean(wa * ahat, axis=-1, keepdims=True))

        c, yn, sg, yhat, rstd = _ln_silu(y_ref[...], lg_ref[...], lb_ref[...])
        rc = lax.rsqrt(jnp.mean(c * c, axis=-1, keepdims=True) + EPS)
        chat = c * rc
        d_cg = jnp.sum(dmc * chat, axis=0, keepdims=True)
        wc = dmc * cg_ref[...]
        dc = rc * (wc - chat * jnp.mean(wc * chat, axis=-1, keepdims=True))
        dyn = dc * (sg * (1.0 + yn * (1.0 - sg)))
        d_lg = jnp.sum(dyn * yhat, axis=0, keepdims=True)
        d_lb = jnp.sum(dyn, axis=0, keepdims=True)
        dyh = dyn * lg_ref[...]
        dy = rstd * (dyh - jnp.mean(dyh, axis=-1, keepdims=True) - yhat * jnp.mean(dyh * yhat, axis=-1, keepdims=True))
        dy_ref[...] = dy
        d_cb = jnp.sum(dy, axis=0, keepdims=True)
        sums = jnp.concatenate([d_ag, d_cg, d_lg, d_lb, d_cb, jnp.zeros((3, CONV_WIDTH), F32)], axis=0)

        @pl.when(i == 0)
        def _():
            sm_ref[...] = sums

        @pl.when(i > 0)
        def _():
            sm_ref[...] += sums

    vec = pl.BlockSpec((1, CONV_WIDTH), lambda i: (0, 0))
    return pl.pallas_call(
        body, name="mix_bwd", grid=(T // TM,),
        in_specs=[pl.BlockSpec((TM, D), lambda i: (i, 0)),
                  pl.BlockSpec((None, MIX_WIDTH, D), lambda i: (l, 0, 0)),
                  pl.BlockSpec((TM, ATTN_WIDTH), lambda i: (i, 0)),
                  pl.BlockSpec((TM, CONV_WIDTH), lambda i: (i, 0)), vec, vec, vec, vec],
        out_specs=[pl.BlockSpec((TM, ATTN_WIDTH), lambda i: (i, 0)),
                   pl.BlockSpec((TM, CONV_WIDTH), lambda i: (i, 0)),
                   pl.BlockSpec((8, CONV_WIDTH), lambda i: (0, 0))],
        out_shape=[jax.ShapeDtypeStruct((T, ATTN_WIDTH), F32), jax.ShapeDtypeStruct((T, CONV_WIDTH), F32),
                   jax.ShapeDtypeStruct((8, CONV_WIDTH), F32)],
        compiler_params=_params(("arbitrary",)),
    )(g1, w_all, a, y, ln_g, ln_b, ag, cg)


def _conv_bwd(dy, z, cw):
    T = z.shape[0]
    TC = _tile(T, 512)
    nt = T // TC
    ub, gb = U0 // CONV_CH, G0 // CONV_CH
    nch = CONV_WIDTH // CONV_CH
    hpt = TC // HALO

    lead = HALO - (CONV_KERNEL - 1)

    def body(dy_ref, dyn_ref, u_ref, g_ref, up_ref, gp_ref, w_ref, du_ref, dg_ref, dw_ref,
             hp_ref, hs_ref, dyp_ref, dys_ref):
        i = pl.program_id(1)
        hp_ref[pl.ds(0, HALO), :] = jnp.where(i > 0, up_ref[...] * jax.nn.sigmoid(gp_ref[...]), 0.0)
        hp_ref[pl.ds(HALO, TC), :] = u_ref[...] * jax.nn.sigmoid(g_ref[...])
        _shifted_copies(hp_ref, hs_ref, TC + HALO)
        dyp_ref[pl.ds(0, TC), :] = dy_ref[...]
        dyp_ref[pl.ds(TC, HALO), :] = jnp.where(i < nt - 1, dyn_ref[...], 0.0)
        _shifted_copies(dyp_ref, dys_ref, TC + HALO)

        @pl.when(i == 0)
        def _():
            dw_ref[...] = jnp.zeros((CONV_ROWS, CONV_CH), F32)

        for r0 in range(0, TC, CONV_RC):
            rows = pl.ds(r0, CONV_RC)
            dh = jnp.zeros((CONV_RC, CONV_CH), F32)
            for j in range(CONV_KERNEL):
                dh = dh + _tap(dys_ref, r0, CONV_KERNEL - 1 - j) * w_ref[pl.ds(j, 1), :]
            uv = u_ref[rows, :]
            sg = jax.nn.sigmoid(g_ref[rows, :])
            du_ref[rows, :] = (dh * sg).astype(BF16)
            dg_ref[rows, :] = (dh * uv * sg * (1.0 - sg)).astype(BF16)
        for j in range(CONV_KERNEL):
            acc = jnp.zeros((SHIFTS, CONV_CH), F32)
            for r0 in range(0, TC, CONV_RC):
                prod = dy_ref[pl.ds(r0, CONV_RC), :] * _tap(hs_ref, r0, lead + j)
                acc = acc + jnp.sum(prod.reshape(CONV_RC // SHIFTS, SHIFTS, CONV_CH), axis=0)
            dw_ref[pl.ds(j, 1), :] += jnp.sum(acc, axis=0, keepdims=True)

    prev = lambda i: jnp.maximum(i * hpt - 1, 0)
    nxt = lambda i: jnp.minimum((i + 1) * hpt, T // HALO - 1)
    return pl.pallas_call(
        body, name="conv_bwd", grid=(nch, nt),
        in_specs=[pl.BlockSpec((TC, CONV_CH), lambda j, i: (i, j)),
                  pl.BlockSpec((HALO, CONV_CH), lambda j, i: (nxt(i), j)),
                  pl.BlockSpec((TC, CONV_CH), lambda j, i: (i, ub + j)),
                  pl.BlockSpec((TC, CONV_CH), lambda j, i: (i, gb + j)),
                  pl.BlockSpec((HALO, CONV_CH), lambda j, i: (prev(i), ub + j)),
                  pl.BlockSpec((HALO, CONV_CH), lambda j, i: (prev(i), gb + j)),
                  pl.BlockSpec((CONV_ROWS, CONV_CH), lambda j, i: (0, j))],
        out_specs=[pl.BlockSpec((TC, CONV_CH), lambda j, i: (i, j)),
                   pl.BlockSpec((TC, CONV_CH), lambda j, i: (i, j)),
                   pl.BlockSpec((CONV_ROWS, CONV_CH), lambda j, i: (0, j))],
        out_shape=[jax.ShapeDtypeStruct((T, CONV_WIDTH), BF16), jax.ShapeDtypeStruct((T, CONV_WIDTH), BF16),
                   jax.ShapeDtypeStruct((CONV_ROWS, CONV_WIDTH), F32)],
        scratch_shapes=[pltpu.VMEM((TC + HALO, CONV_CH), F32), pltpu.VMEM((SHIFTS, TC + HALO, CONV_CH), F32),
                        pltpu.VMEM((TC + HALO, CONV_CH), F32), pltpu.VMEM((SHIFTS, TC + HALO, CONV_CH), F32)],
        compiler_params=_params(("parallel", "arbitrary")),
    )(dy, dy, z, z, z, z, cw)


def _norm_bwd(d, that, r, g):
    w = d * g
    return r * (w - that * jnp.mean(w * that, axis=-1, keepdims=True)), jnp.sum(d * that, axis=0, keepdims=True)


def _attn_bwd(z, da, biasc, sink_rows, qg, kg):
    T = z.shape[0]
    nb = T // BLOCK
    qb = min(ATTN_QB, nb)
    TQ = qb * BLOCK
    ns = nb // qb
    kb, vb = K0 // KV_WIDTH, V0 // KV_WIDTH
    groups = [(i, k) for i in range(qb) for k in range(N_KV_HEADS)]

    def body(q_ref, kc_ref, kp_ref, vc_ref, vp_ref, da_ref, b_ref, sk_ref, qg_ref, kg_ref,
             dq_ref, dkv_ref, db_ref, sm_ref, ck_ref, cv_ref, pk_ref, pv_ref, nk_ref, nv_ref):
        n = pl.program_id(0)
        lane = lax.broadcasted_iota(jnp.int32, (1, LANES), 1)

        @pl.when(n == 0)
        def _():
            db_ref[...] = jnp.zeros(db_ref.shape, F32)
            sm_ref[...] = jnp.zeros(sm_ref.shape, F32)
            ck_ref[...] = jnp.zeros(ck_ref.shape, F32)
            cv_ref[...] = jnp.zeros(cv_ref.shape, F32)

        pk_ref[...] = jnp.zeros(pk_ref.shape, F32)
        pv_ref[...] = jnp.zeros(pv_ref.shape, F32)

        @pl.when(n < ns)
        def _():
            own = _own_block()
            knorm, kn, vv = {}, {}, {}
            for k in range(N_KV_HEADS):
                kn[-1, k] = _head_norm(kp_ref[:, KV_COLS[k]], kg_ref[...])[0].astype(BF16)
                vv[-1, k] = vp_ref[:, KV_COLS[k]].astype(BF16)
            for i, k in groups:
                knorm[i, k] = _head_norm(kc_ref[_blk(i), KV_COLS[k]], kg_ref[...])
                kn[i, k] = knorm[i, k][0].astype(BF16)
                vv[i, k] = vc_ref[_blk(i), KV_COLS[k]].astype(BF16)
            qnorm = {g: _head_norm(_stack_heads(q_ref, *g), qg_ref[...]) for g in groups}
            qnb = {g: qnorm[g][0].astype(BF16) for g in groups}
            dob = {g: _stack_heads(da_ref, *g).astype(BF16) for g in groups}
            s_own = {(i, k): _dot_nt(qnb[i, k], kn[i, k]) for i, k in groups}
            s_prev = {(i, k): _dot_nt(qnb[i, k], kn[i - 1, k]) for i, k in groups}
            dp_own = {(i, k): _dot_nt(dob[i, k], vv[i, k]) for i, k in groups}
            dp_prev = {(i, k): _dot_nt(dob[i, k], vv[i - 1, k]) for i, k in groups}
            probs = {(i, k): _band_probs(n == 0 if i == 0 else None, own, s_own[i, k], s_prev[i, k], b_ref[k], sk_ref[k])
                     for i, k in groups}
            ds_own, ds_prev, p_own, p_prev = {}, {}, {}, {}
            dsk = jnp.zeros((1, LANES), F32)
            dbias = [jnp.zeros((GROUP_ROWS, BLOCK), F32) for _ in range(N_KV_HEADS)]
            for i, k in groups:
                p, psink = probs[i, k]
                dp = jnp.where(own, dp_own[i, k], dp_prev[i, k])
                delta = jnp.sum(p * dp, axis=-1, keepdims=True)
                ds = p * (dp - delta)
                dbias[k] = dbias[k] + ds
                dsink = psink * delta
                for g in range(GQA_GROUP):
                    dsk = dsk + jnp.where(lane == k * GQA_GROUP + g, -jnp.sum(dsink[g * BLOCK:(g + 1) * BLOCK]), 0.0)
                ds_own[i, k] = jnp.where(own, ds, 0.0).astype(BF16)
                ds_prev[i, k] = jnp.where(own, 0.0, ds).astype(BF16)
                p_own[i, k] = jnp.where(own, p, 0.0).astype(BF16)
                p_prev[i, k] = jnp.where(own, 0.0, p).astype(BF16)
            for k in range(N_KV_HEADS):
                db_ref[k] += dbias[k]
            dqn_own = {(i, k): jnp.dot(ds_own[i, k], kn[i, k], preferred_element_type=F32) for i, k in groups}
            dqn_prev = {(i, k): jnp.dot(ds_prev[i, k], kn[i - 1, k], preferred_element_type=F32) for i, k in groups}
            dk_own = {g: _dot_tn(ds_own[g], qnb[g]) * SCALE for g in groups}
            dk_prev = {g: _dot_tn(ds_prev[g], qnb[g]) * SCALE for g in groups}
            dv_own = {g: _dot_tn(p_own[g], dob[g]) for g in groups}
            dv_prev = {g: _dot_tn(p_prev[g], dob[g]) for g in groups}
            dqg = jnp.zeros((1, HEAD_DIM), F32)
            dkg = jnp.zeros((1, HEAD_DIM), F32)
            for i, k in groups:
                _, qhat, rq = qnorm[i, k]
                dq, dg = _norm_bwd((dqn_own[i, k] + dqn_prev[i, k]) * SCALE, qhat, rq, qg_ref[...])
                dqg = dqg + dg
                _unstack_heads(dq_ref, i, k, dq.astype(BF16))
                if i == 0:
                    pk_ref[:, KV_COLS[k]] = dk_prev[i, k]
                    pv_ref[:, KV_COLS[k]] = dv_prev[i, k]
                if i == qb - 1:
                    nk_ref[:, KV_COLS[k]] = dk_own[i, k]
                    nv_ref[:, KV_COLS[k]] = dv_own[i, k]
                else:
                    _, khat, rk = knorm[i, k]
                    dk, dg = _norm_bwd(dk_own[i, k] + dk_prev[i + 1, k], khat, rk, kg_ref[...])
                    dkg = dkg + dg
                    dkv_ref[_blk(i + 1), KV_COLS[k]] = dk.astype(BF16)
                    dkv_ref[_blk(i + 1), pl.ds(KV_WIDTH + k * HEAD_DIM, HEAD_DIM)] = (dv_own[i, k] + dv_prev[i + 1, k]).astype(BF16)
            sm_ref[pl.ds(0, 1), pl.ds(0, HEAD_DIM)] += dqg
            sm_ref[pl.ds(1, 1), pl.ds(0, HEAD_DIM)] += dkg
            sm_ref[pl.ds(2, 1), :] += dsk

        @pl.when(n >= 1)
        def _():
            dkg = jnp.zeros((1, HEAD_DIM), F32)
            for k in range(N_KV_HEADS):
                _, khat, rk = _head_norm(kp_ref[:, KV_COLS[k]], kg_ref[...])
                dk, dg = _norm_bwd(ck_ref[:, KV_COLS[k]] + pk_ref[:, KV_COLS[k]], khat, rk, kg_ref[...])
                dkg = dkg + dg
                dkv_ref[_blk(0), KV_COLS[k]] = dk.astype(BF16)
            dkv_ref[_blk(0), pl.ds(KV_WIDTH, KV_WIDTH)] = (cv_ref[...] + pv_ref[...]).astype(BF16)
            sm_ref[pl.ds(1, 1), pl.ds(0, HEAD_DIM)] += dkg

        ck_ref[...] = nk_ref[...]
        cv_ref[...] = nv_ref[...]

    cur = lambda n: jnp.minimum(n, ns - 1)
    prev = lambda n: jnp.maximum(n * qb - 1, 0)
    carry = pltpu.VMEM((BLOCK, KV_WIDTH), F32)
    return pl.pallas_call(
        body, name="attn_bwd", grid=(ns + 1,),
        in_specs=[pl.BlockSpec((TQ, ATTN_WIDTH), lambda n: (cur(n), 0)),
                  pl.BlockSpec((TQ, KV_WIDTH), lambda n: (cur(n), kb)),
                  pl.BlockSpec((BLOCK, KV_WIDTH), lambda n: (prev(n), kb)),
                  pl.BlockSpec((TQ, KV_WIDTH), lambda n: (cur(n), vb)),
                  pl.BlockSpec((BLOCK, KV_WIDTH), lambda n: (prev(n), vb)),
                  pl.BlockSpec((TQ, ATTN_WIDTH), lambda n: (cur(n), 0)),
                  pl.BlockSpec((N_KV_HEADS, GROUP_ROWS, BLOCK), lambda n: (0, 0, 0)),
                  pl.BlockSpec((N_KV_HEADS, GROUP_ROWS, 1), lambda n: (0, 0, 0)),
                  pl.BlockSpec((1, HEAD_DIM), lambda n: (0, 0)),
                  pl.BlockSpec((1, HEAD_DIM), lambda n: (0, 0))],
        out_specs=[pl.BlockSpec((TQ, ATTN_WIDTH), lambda n: (cur(n), 0)),
                   pl.BlockSpec((TQ, 2 * KV_WIDTH), lambda n: (n, 0)),
                   pl.BlockSpec((N_KV_HEADS, GROUP_ROWS, BLOCK), lambda n: (0, 0, 0)),
                   pl.BlockSpec((8, LANES), lambda n: (0, 0))],
        out_shape=[jax.ShapeDtypeStruct((T, ATTN_WIDTH), BF16), jax.ShapeDtypeStruct(((ns + 1) * TQ, 2 * KV_WIDTH), BF16),
                   jax.ShapeDtypeStruct((N_KV_HEADS, GROUP_ROWS, BLOCK), F32), jax.ShapeDtypeStruct((8, LANES), F32)],
        scratch_shapes=[carry] * 6,
        compiler_params=_params(("arbitrary",)),
    )(z, z, z, z, z, da, biasc, sink_rows, qg, kg)


def _bucket_reduce(dbias, onehot_t):
    def body(d_ref, oh_ref, o_ref):
        d = d_ref[...]
        hi = d.astype(BF16)
        r1 = d - hi.astype(F32)
        mid = r1.astype(BF16)
        lo = (r1 - mid.astype(F32)).astype(BF16)
        oh = oh_ref[...]
        acc = lax.dot_general(lo, oh, NT, preferred_element_type=F32)
        acc = acc + lax.dot_general(mid, oh, NT, preferred_element_type=F32)
        o_ref[...] = acc + lax.dot_general(hi, oh, NT, preferred_element_type=F32)

    return pl.pallas_call(
        body, name="bucket_reduce",
        out_shape=jax.ShapeDtypeStruct((N_HEADS, LANES), F32),
        compiler_params=_params(),
    )(dbias, onehot_t)


def _adamw(w, g, m, v, name):
    R, C = w.shape
    TR = _tile(R, 512)

    def body(w_ref, g_ref, m_ref, v_ref, d_ref, nm_ref, nv_ref):
        gv = g_ref[...]
        mn = ADAM_B1 * m_ref[...] + (1.0 - ADAM_B1) * gv
        vn = ADAM_B2 * v_ref[...] + (1.0 - ADAM_B2) * jnp.square(gv)
        m_hat = mn / (1.0 - ADAM_B1 ** ADAM_STEP)
        v_hat = vn / (1.0 - ADAM_B2 ** ADAM_STEP)
        d_ref[...] = -ADAM_LR * (m_hat / (jnp.sqrt(v_hat) + ADAM_EPS) + ADAM_WD * w_ref[...])
        nm_ref[...] = mn
        nv_ref[...] = vn

    spec = pl.BlockSpec((TR, C), lambda i: (i, 0))
    shp = jax.ShapeDtypeStruct((R, C), F32)
    return pl.pallas_call(
        body, name=name, grid=(R // TR,),
        in_specs=[spec] * 4, out_specs=[spec] * 3, out_shape=[shp] * 3,
        compiler_params=_params(("parallel",)),
    )(w, g, m, v)


def _place():
    return lax.axis_index("x"), lax.axis_index("y"), lax.axis_index("c")


def _other_chips(x, y):
    return [(1 - x, y), (x, 1 - y), (1 - x, 1 - y)]


def _remote(src, dst, send_sem, recv_sem, dev):
    return pltpu.make_async_remote_copy(src_ref=src, dst_ref=dst, send_sem=send_sem, recv_sem=recv_sem,
                                        device_id=dev, device_id_type=MESH)


def _gather_shards(bufs):
    nbuf = len(bufs)

    def body(*refs):
        ins, outs = refs[:nbuf], refs[nbuf:2 * nbuf]
        send_sems, recv_sems = refs[2 * nbuf:]
        x, y, c = _place()
        me = 2 * x + y
        sib = (x, y, 1 - c)
        chips = _other_chips(x, y)
        started = []
        for b in range(nbuf):
            hh = bufs[b].shape[0] // 2
            for j, (cx, cy) in enumerate(chips):
                k = 6 * b + j
                cp = _remote(ins[b].at[pl.ds(c * hh, hh), :], outs[b].at[me, pl.ds(c * hh, hh), :],
                             send_sems.at[k], recv_sems.at[k], (cx, cy, c))
                cp.start()
                started.append(cp)
        for b in range(nbuf):
            hh = bufs[b].shape[0] // 2
            for j, (cx, cy) in enumerate(chips):
                rows = outs[b].at[2 * cx + cy, pl.ds(c * hh, hh), :]
                _remote(rows, rows, send_sems.at[6 * b + j], recv_sems.at[6 * b + j], sib).wait_recv()
                k = 6 * b + 3 + j
                cp = _remote(rows, rows, send_sems.at[k], recv_sems.at[k], sib)
                cp.start()
                started.append(cp)
        for b in range(nbuf):
            hh = bufs[b].shape[0] // 2
            for j, (cx, cy) in enumerate(chips):
                rows = outs[b].at[2 * cx + cy, pl.ds((1 - c) * hh, hh), :]
                k = 6 * b + 3 + j
                _remote(rows, rows, send_sems.at[k], recv_sems.at[k], sib).wait_recv()
        for cp in started:
            cp.wait_send()

    return pl.pallas_call(
        body, name="gather_weights",
        in_specs=[ANY] * nbuf, out_specs=[ANY] * nbuf,
        out_shape=[jax.ShapeDtypeStruct((N_CHIPS,) + b.shape, b.dtype) for b in bufs],
        scratch_shapes=[pltpu.SemaphoreType.DMA((6 * nbuf,)), pltpu.SemaphoreType.DMA((6 * nbuf,))],
        compiler_params=pltpu.CompilerParams(has_side_effects=True),
    )(*bufs)


HBM = pl.BlockSpec(memory_space=pltpu.HBM)
SEM = pl.BlockSpec(memory_space=pltpu.SEMAPHORE)
DATAFLOW = pltpu.SideEffectType.DATAFLOW_SIDE_EFFECTING


def _gather_plan(shapes):
    def plan(srcs, lands):
        x, y, c = _place()
        out = []
        for b, shp in enumerate(shapes):
            hh = shp[0] // 2
            for cx, cy in _other_chips(x, y):
                out.append((srcs[b].at[pl.ds(c * hh, hh), :], lands[b].at[2 * x + y, pl.ds(c * hh, hh), :], (cx, cy, c)))
        return out
    return plan


def _exchange_plan(shapes):
    def plan(srcs, lands):
        x, y, c = _place()
        return [(srcs[b].at[j], lands[b].at[j], (cx, cy, c))
                for b in range(len(shapes)) for j, (cx, cy) in enumerate(_other_chips(x, y))]
    return plan


def _start_copies(name, srcs, land_shapes, plan):
    n = len(srcs)
    ncopy = 3 * n

    def body(*refs):
        ins, lands = refs[:n], refs[n:2 * n]
        send_sems, recv_sems, token = refs[2 * n], refs[2 * n + 1], refs[-1]
        for k, (src, dst, dev) in enumerate(plan(ins, lands)):
            _remote(src, dst, send_sems.at[k], recv_sems.at[k], dev).start()
        token[...] = jnp.zeros_like(token)

    hbm = lambda a: pltpu.with_memory_space_constraint(a, pltpu.HBM)
    lands = [lax.empty(s, a.dtype) for s, a in zip(land_shapes, srcs)]
    outs = pl.pallas_call(
        body, name=name,
        out_shape=(pltpu.SemaphoreType.DMA((ncopy,)), pltpu.SemaphoreType.DMA((ncopy,)),
                   *[pltpu.HBM(a.shape, a.dtype) for a in srcs], *[pltpu.HBM(a.shape, a.dtype) for a in lands],
                   jax.ShapeDtypeStruct((8, LANES), F32)),
        in_specs=[HBM] * (2 * n),
        out_specs=(SEM, SEM, *([HBM] * (2 * n)), pl.BlockSpec(memory_space=pltpu.VMEM)),
        input_output_aliases={i: 2 + i for i in range(2 * n)},
        compiler_params=pltpu.CompilerParams(has_side_effects=DATAFLOW),
    )(*[hbm(a) for a in srcs], *[hbm(a) for a in lands])
    return outs[0], outs[1], list(outs[2:2 + n]), list(outs[2 + n:2 + 2 * n]), outs[-1]


def _wait_copies(name, send_sems, recv_sems, srcs, lands, plan, after):
    n = len(srcs)

    def body(*refs):
        ins, lnds = refs[:n], refs[n:2 * n]
        ssem, rsem = refs[2 * n], refs[2 * n + 1]
        for k, (src, dst, dev) in enumerate(plan(ins, lnds)):
            cp = _remote(src, dst, ssem.at[k], rsem.at[k], dev)
            cp.wait_send()
            cp.wait_recv()

    outs = pl.pallas_call(
        body, name=name,
        out_shape=(*[pltpu.HBM(a.shape, a.dtype) for a in srcs], *[pltpu.HBM(a.shape, a.dtype) for a in lands]),
        in_specs=[HBM] * (2 * n) + [SEM, SEM, ANY],
        out_specs=tuple([HBM] * (2 * n)),
        input_output_aliases={i: i for i in range(2 * n)},
        compiler_params=pltpu.CompilerParams(has_side_effects=DATAFLOW),
    )(*srcs, *lands, send_sems, recv_sems, after)
    return list(outs[n:])


def _forward_halves(bufs):
    nbuf = len(bufs)

    def body(*refs):
        outs = refs[nbuf:2 * nbuf]
        send_sems, recv_sems = refs[2 * nbuf:]
        x, y, c = _place()
        sib = (x, y, 1 - c)
        cps = []
        for b in range(nbuf):
            hh = bufs[b].shape[1] // 2
            for j, (cx, cy) in enumerate(_other_chips(x, y)):
                rows = outs[b].at[2 * cx + cy, pl.ds(c * hh, hh), :]
                cp = _remote(rows, rows, send_sems.at[3 * b + j], recv_sems.at[3 * b + j], sib)
                cp.start()
                cps.append(cp)
        for b in range(nbuf):
            hh = bufs[b].shape[1] // 2
            for j, (cx, cy) in enumerate(_other_chips(x, y)):
                rows = outs[b].at[2 * cx + cy, pl.ds((1 - c) * hh, hh), :]
                _remote(rows, rows, send_sems.at[3 * b + j], recv_sems.at[3 * b + j], sib).wait_recv()
        for cp in cps:
            cp.wait_send()

    return pl.pallas_call(
        body, name="gather_forward_halves",
        in_specs=[ANY] * nbuf, out_specs=[ANY] * nbuf,
        out_shape=[jax.ShapeDtypeStruct(b.shape, b.dtype) for b in bufs],
        input_output_aliases={b: b for b in range(nbuf)},
        scratch_shapes=[pltpu.SemaphoreType.DMA((3 * nbuf,)), pltpu.SemaphoreType.DMA((3 * nbuf,))],
        compiler_params=pltpu.CompilerParams(has_side_effects=True),
    )(*bufs)


def _swap_halves(bufs):
    nbuf = len(bufs)

    def body(*refs):
        ins, outs = refs[:nbuf], refs[nbuf:2 * nbuf]
        send_sems, recv_sems = refs[2 * nbuf:]
        x, y, c = _place()
        sib = (x, y, 1 - c)
        cps = []
        for b in range(nbuf):
            hh = bufs[b].shape[1] // 2
            cp = _remote(ins[b].at[:, pl.ds((1 - c) * hh, hh), :], outs[b], send_sems.at[b], recv_sems.at[b], sib)
            cp.start()
            cps.append(cp)
        for cp in cps:
            cp.wait()

    return pl.pallas_call(
        body, name="grad_swap_halves",
        in_specs=[ANY] * nbuf, out_specs=[ANY] * nbuf,
        out_shape=[jax.ShapeDtypeStruct((N_CHIPS, b.shape[1] // 2, b.shape[2]), b.dtype) for b in bufs],
        scratch_shapes=[pltpu.SemaphoreType.DMA((nbuf,)), pltpu.SemaphoreType.DMA((nbuf,))],
        compiler_params=pltpu.CompilerParams(has_side_effects=True),
    )(*bufs)


def _chip_sum(g, got, sel, out_dtype, name):
    _, R, C = g.shape
    hh = R // 2
    TR = _tile(hh, 512)
    nslot = sel[1].shape[0]

    def body(off_ref, sh_ref, g_ref, r_ref, o_ref):
        o_ref[...] = (g_ref[...] + r_ref[...]).astype(out_dtype)

    return pl.pallas_call(
        body, name=name,
        grid_spec=pltpu.PrefetchScalarGridSpec(
            num_scalar_prefetch=2, grid=(nslot, hh // TR),
            in_specs=[pl.BlockSpec((None, TR, C), lambda s, i, off, sh: (sh[s], off[0] + i, 0)),
                      pl.BlockSpec((None, TR, C), lambda s, i, off, sh: (sh[s], i, 0))],
            out_specs=pl.BlockSpec((None, TR, C), lambda s, i, off, sh: (s, i, 0))),
        out_shape=jax.ShapeDtypeStruct((nslot, hh, C), out_dtype),
        compiler_params=_params(("parallel", "parallel")),
    )(sel[0], sel[1], g, got)


def _exchange_chips(bufs):
    nbuf = len(bufs)

    def body(*refs):
        ins, outs = refs[:nbuf], refs[nbuf:2 * nbuf]
        send_sems, recv_sems = refs[2 * nbuf:]
        x, y, c = _place()
        cps = []
        for b in range(nbuf):
            for j, (cx, cy) in enumerate(_other_chips(x, y)):
                k = 3 * b + j
                cp = _remote(ins[b].at[j], outs[b].at[j], send_sems.at[k], recv_sems.at[k], (cx, cy, c))
                cp.start()
                cps.append(cp)
        for cp in cps:
            cp.wait()

    return pl.pallas_call(
        body, name="grad_exchange_chips",
        in_specs=[ANY] * nbuf, out_specs=[ANY] * nbuf,
        out_shape=[jax.ShapeDtypeStruct(b.shape, b.dtype) for b in bufs],
        scratch_shapes=[pltpu.SemaphoreType.DMA((3 * nbuf,)), pltpu.SemaphoreType.DMA((3 * nbuf,))],
        compiler_params=pltpu.CompilerParams(has_side_effects=True),
    )(*bufs)


def _shard_sum(own, got, off, tr, full, rows, name):
    _, hh, C = own.shape

    def body(off_ref, o_ref, r_ref, *rest):
        acc = o_ref[...]
        for j in range(N_CHIPS - 1):
            acc = acc + r_ref[j].astype(F32)
        rest[-1][...] = acc

    in_specs = [pl.BlockSpec((None, tr, C), lambda i, off: (0, i, 0)),
                pl.BlockSpec((N_CHIPS - 1, tr, C), lambda i, off: (0, i, 0))]
    args = [off, own, got]
    aliases = {}
    if full is not None:
        in_specs.append(ANY)
        args.append(full)
        aliases = {3: 0}
    return pl.pallas_call(
        body, name=name,
        grid_spec=pltpu.PrefetchScalarGridSpec(
            num_scalar_prefetch=1, grid=(hh // tr,), in_specs=in_specs,
            out_specs=pl.BlockSpec((tr, C), lambda i, off: (off[0] + i, 0))),
        out_shape=jax.ShapeDtypeStruct((rows, C), F32),
        input_output_aliases=aliases,
        compiler_params=_params(("parallel",)),
    )(*args)


def _join_halves(bufs, spans):
    nbuf = len(bufs)
    ncopy = nbuf * len(spans)

    def body(*refs):
        outs = refs[nbuf:2 * nbuf]
        send_sems, recv_sems = refs[2 * nbuf:]
        x, y, c = _place()
        sib = (x, y, 1 - c)
        cps = []
        for b in range(nbuf):
            for s, (r0, nr) in enumerate(spans[b]):
                k = b * len(spans[b]) + s
                rows = outs[b].at[pl.ds(r0 + c * (nr // 2), nr // 2), :]
                cp = _remote(rows, rows, send_sems.at[k], recv_sems.at[k], sib)
                cp.start()
                cps.append(cp)
        for b in range(nbuf):
            for s, (r0, nr) in enumerate(spans[b]):
                k = b * len(spans[b]) + s
                theirs = outs[b].at[pl.ds(r0 + (1 - c) * (nr // 2), nr // 2), :]
                _remote(theirs, theirs, send_sems.at[k], recv_sems.at[k], sib).wait_recv()
        for cp in cps:
            cp.wait_send()

    return pl.pallas_call(
        body, name="grad_join_halves",
        in_specs=[ANY] * nbuf, out_specs=[ANY] * nbuf,
        out_shape=[jax.ShapeDtypeStruct(b.shape, b.dtype) for b in bufs],
        input_output_aliases={b: b for b in range(nbuf)},
        scratch_shapes=[pltpu.SemaphoreType.DMA((ncopy,)), pltpu.SemaphoreType.DMA((ncopy,))],
        compiler_params=pltpu.CompilerParams(has_side_effects=True),
    )(*bufs)


def _sum_devices(part):
    R = part.shape[0]

    def body(p_ref, o_ref, all_ref, send_sems, recv_sems):
        x, y, c = _place()
        me = 4 * x + 2 * y + c
        all_ref[me] = p_ref[...]
        cps = []
        for k in range(1, N_DEV):
            px, py, pc = x ^ (k >> 2), y ^ ((k >> 1) & 1), c ^ (k & 1)
            cp = _remote(p_ref, all_ref.at[me], send_sems.at[k - 1], recv_sems.at[k - 1], (px, py, pc))
            cp.start()
            cps.append(cp)
        for k in range(1, N_DEV):
            peer = me ^ k
            _remote(p_ref, all_ref.at[peer], send_sems.at[k - 1], recv_sems.at[k - 1], (x, y, c)).wait_recv()
        for cp in cps:
            cp.wait_send()
        acc = all_ref[0]
        for d in range(1, N_DEV):
            acc = acc + all_ref[d]
        o_ref[...] = acc

    return pl.pallas_call(
        body, name="sum_small_grads",
        in_specs=[pl.BlockSpec(memory_space=pltpu.VMEM)],
        out_specs=pl.BlockSpec(memory_space=pltpu.VMEM),
        out_shape=jax.ShapeDtypeStruct((R, LANES), F32),
        scratch_shapes=[pltpu.VMEM((N_DEV, R, LANES), F32),
                        pltpu.SemaphoreType.DMA((N_DEV - 1,)), pltpu.SemaphoreType.DMA((N_DEV - 1,))],
        compiler_params=pltpu.CompilerParams(has_side_effects=True, vmem_limit_bytes=VMEM_LIMIT),
    )(part)


def _pack(parts):
    flat = jnp.concatenate([p.reshape(-1).astype(F32) for p in parts])
    n = flat.shape[0]
    rows = -(-n // LANES)
    rows = -(-rows // 8) * 8
    return jnp.pad(flat, (0, rows * LANES - n)).reshape(rows, LANES)


def _unpack(packed, shapes):
    flat = packed.reshape(-1)
    out, off = [], 0
    for s in shapes:
        n = int(np.prod(s))
        out.append(flat[off:off + n].reshape(s))
        off += n
    return out


def kernel(x, rel_bias, norm_mix_g, w_in, q_norm_g, k_norm_g, sinks, conv_w, conv_b, conv_ln_g, conv_ln_b, attn_out_g, conv_out_g, w_out, norm_mlp_g, w_mlp_up, w_mlp_down, loss_target, m_rel_bias, m_norm_mix_g, m_w_in, m_q_norm_g, m_k_norm_g, m_sinks, m_conv_w, m_conv_b, m_conv_ln_g, m_conv_ln_b, m_attn_out_g, m_conv_out_g, m_w_out, m_norm_mlp_g, m_w_mlp_up, m_w_mlp_down, v_rel_bias, v_norm_mix_g, v_w_in, v_q_norm_g, v_k_norm_g, v_sinks, v_conv_w, v_conv_b, v_conv_ln_g, v_conv_ln_b, v_attn_out_g, v_conv_out_g, v_w_out, v_norm_mlp_g, v_w_mlp_up, v_w_mlp_down):
    T = x.shape[1]
    L = DEPTH
    xi, yi, ci = _place()
    shard = 2 * xi + yi
    in_sh = IN_WIDTH // N_CHIPS
    out_sh = MIX_WIDTH // N_CHIPS
    ff_sh = D_FF // N_CHIPS
    cv_sh = CONV_WIDTH // N_CHIPS

    MIXING, MLP = ("w_in", "w_out", "conv_w"), ("w_mlp_up", "w_mlp_down")

    def my_shard(name, lo, hi):
        n = hi - lo
        if name == "w_in":
            return w_in[lo:hi].astype(BF16).reshape(n * D_MODEL, in_sh)
        if name == "w_out":
            return w_out[lo:hi].astype(BF16).reshape(n * out_sh, D_MODEL)
        if name == "w_mlp_up":
            return w_mlp_up[lo:hi].astype(BF16).reshape(n * D_MODEL, ff_sh)
        if name == "w_mlp_down":
            return w_mlp_down[lo:hi].astype(BF16).reshape(n * ff_sh, D_MODEL)
        cw_pad = jnp.pad(conv_w[lo:hi], ((0, 0), (0, CONV_ROWS - CONV_KERNEL), (0, 0)))
        return cw_pad.reshape(n * CONV_ROWS, cv_sh)

    def whole_weight(name, gathered, own, n):
        g = lax.dynamic_update_slice(gathered, own[None], (shard, 0, 0))
        if name == "w_in":
            return g.reshape(N_CHIPS, n, D_MODEL, in_sh).transpose(1, 2, 0, 3).reshape(n, D_MODEL, IN_WIDTH)
        if name == "w_out":
            return g.reshape(N_CHIPS, n, out_sh, D_MODEL).transpose(1, 0, 2, 3).reshape(n, MIX_WIDTH, D_MODEL)
        if name == "w_mlp_up":
            return g.reshape(N_CHIPS, n, D_MODEL, ff_sh).transpose(1, 2, 0, 3).reshape(n, D_MODEL, D_FF)
        if name == "w_mlp_down":
            return g.reshape(N_CHIPS, n, ff_sh, D_MODEL).transpose(1, 0, 2, 3).reshape(n, D_FF, D_MODEL)
        return g.reshape(N_CHIPS, n, CONV_ROWS, cv_sh).transpose(1, 2, 0, 3).reshape(n, CONV_ROWS, CONV_WIDTH)

    weight_of = {}

    def provide(names, lo, hi, gathered, mine):
        for name, g, own in zip(names, gathered, mine):
            whole = whole_weight(name, g, own, hi - lo)
            for l in range(lo, hi):
                weight_of[name, l] = (whole, l - lo)

    def gather_behind(tag, names, lo, hi, first):
        mine = [my_shard(name, lo, hi) for name in names]
        mine[0], _ = lax.optimization_barrier((mine[0], first))
        plan = _gather_plan([m.shape for m in mine])
        send_sems, recv_sems, srcs, lands, token = _start_copies(
            "gather_" + tag + "_start", mine, [(N_CHIPS,) + m.shape for m in mine], plan)

        def finish(after):
            got = _wait_copies("gather_" + tag + "_wait", send_sems, recv_sems, srcs, lands, plan, after)
            provide(names, lo, hi, _forward_halves(got), mine)
        return token, finish

    mine0 = [my_shard(name, 0, 1) for name in MIXING]
    got0 = _gather_shards(mine0)
    provide(MIXING, 0, 1, got0, mine0)
    token_mlp0, finish_mlp0 = gather_behind("mlp0", MLP, 0, 1, got0[0])
    token_rest, finish_rest = gather_behind("rest", MIXING + MLP, 1, L, token_mlp0)

    bucket = _band_buckets()
    bk = jnp.asarray(bucket)[None]
    biasc = jnp.zeros((N_HEADS, BLOCK, BLOCK), F32)
    for b in range(NUM_BUCKETS):
        biasc = jnp.where(bk == b, rel_bias[b][:, None, None], biasc)
    biasc = biasc.reshape(N_KV_HEADS, GROUP_ROWS, BLOCK)
    onehot_t = np.zeros((LANES, BLOCK * BLOCK), np.float32)
    onehot_t[bucket.reshape(-1), np.arange(BLOCK * BLOCK)] = 1.0
    onehot_t = jnp.asarray(onehot_t, dtype=BF16)
    sink_rows = lambda l: jnp.repeat(sinks[l], BLOCK).reshape(N_KV_HEADS, GROUP_ROWS, 1)

    row = lambda a, l: a[l][None, :]

    xs = x.reshape(T, D_MODEL)
    saved = []
    for l in range(L):
        if l == 1:
            finish_rest(xs)
        h, z = _norm_matmul(xs, row(norm_mix_g, l), *weight_of["w_in", l], F32, "mix_in_proj", token_rest if l == 0 else None)
        a = _attn_fwd(z, biasc, sink_rows(l), row(q_norm_g, l), row(k_norm_g, l))
        cw, cl = weight_of["conv_w", l]
        yc = _conv_fwd(z, cw[cl], row(conv_b, l))
        mix = _mix_norm(a, yc, row(conv_ln_g, l), row(conv_ln_b, l), row(attn_out_g, l), row(conv_out_g, l))
        x1 = _matmul_res(mix, *weight_of["w_out", l], xs, False, "mix_out_proj")
        if l == 0:
            finish_mlp0(x1)
        h2, up = _norm_matmul(x1, row(norm_mlp_g, l), *weight_of["w_mlp_up", l], BF16, "mlp_up_proj")
        x2 = _matmul_res(up, *weight_of["w_mlp_down", l], x1, True, "mlp_down_proj")
        saved.append((xs, h, z, a, yc, mix, x1, h2, up))
        xs = x2

    loss_parts, g = _loss_grad(xs, loss_target.reshape(T, D_MODEL))

    names = ["w_in", "w_out", "w_mlp_up", "w_mlp_down"]
    shard_rows = {"w_in": D_MODEL, "w_out": out_sh, "w_mlp_up": D_MODEL, "w_mlp_down": ff_sh}
    own_sel = shard.astype(jnp.int32)[None]
    send_sel = jnp.stack([shard ^ 2, shard ^ 1, shard ^ 3]).astype(jnp.int32)

    def by_shard(name, buf, n):
        if name == "w_in":
            return buf.reshape(n, D_MODEL, N_CHIPS, in_sh).transpose(2, 0, 1, 3).reshape(N_CHIPS, n * D_MODEL, in_sh)
        return buf.reshape(N_CHIPS, n * shard_rows[name], buf.shape[-1])

    def chip_sums(tag, group, n):
        order = list(group)
        G = [by_shard(name, group[name], n) for name in order]
        got = _swap_halves(G)
        owns, sends = {}, {}
        for name, g_all, g_got in zip(order, G, got):
            hh = g_all.shape[1] // 2
            off = (ci * (hh // _tile(hh, 512))).astype(jnp.int32)[None]
            owns[name] = _chip_sum(g_all, g_got, (off, own_sel), F32, "chip_sum_own_" + name + tag)
            sends[name] = _chip_sum(g_all, g_got, (off, send_sel), BF16, "chip_sum_send_" + name + tag)
        return owns, sends

    def exchange_behind(tag, group, n):
        owns, sends = chip_sums(tag, group, n)
        order = list(sends)
        bufs = [sends[name] for name in order]
        plan = _exchange_plan([b.shape for b in bufs])
        send_sems, recv_sems, srcs, lands, token = _start_copies(
            "grad_exchange" + tag + "_start", bufs, [b.shape for b in bufs], plan)

        def finish(after):
            got = _wait_copies("grad_exchange" + tag + "_wait", send_sems, recv_sems, srcs, lands, plan, after)
            return {name: (owns[name], arrived) for name, arrived in zip(order, got)}
        return token, finish

    rest = dict.fromkeys(names)
    first = dict.fromkeys(names)
    small = [None] * L
    dbias_sum = None
    token = None
    for l in reversed(range(L)):
        x0, h, z, a, yc, mix, x1, h2, up = saved[l]
        stack, n, sl = (first, 1, 0) if l == 0 else (rest, L - 1, l - 1)
        if l == 0:
            token, finish_rest_grads = exchange_behind("_rest", rest, L - 1)
        d_up = _dact(g, *weight_of["w_mlp_down", l], up, token)
        stack["w_mlp_down"] = _matmul_tn(up, g, True, stack["w_mlp_down"], (N_CHIPS, n, ff_sh, D_MODEL),
                                         (None, None, ff_sh, D_MODEL), lambda i, j: (i, sl, 0, 0), ff_sh, D_MODEL,
                                         "grad_w_mlp_down")
        stack["w_mlp_up"] = _matmul_tn(h2, d_up, False, stack["w_mlp_up"], (N_CHIPS, n, D_MODEL, ff_sh),
                                       (None, None, D_MODEL, ff_sh), lambda i, j: (j, sl, 0, 0), D_MODEL, ff_sh,
                                       "grad_w_mlp_up")
        if l == 0:
            token, finish_mlp0_grads = exchange_behind("_mlp0", {k: first[k] for k in ("w_mlp_up", "w_mlp_down")}, 1)
        g1, d_gmlp = _matmul_nt_normbwd(d_up, *weight_of["w_mlp_up", l], x1, row(norm_mlp_g, l), g, "mlp_in_bwd",
                                        token if l == 0 else None)
        d_a, d_y, sm_mix = _mix_bwd(g1, *weight_of["w_out", l], a, yc, row(conv_ln_g, l), row(conv_ln_b, l),
                                    row(attn_out_g, l), row(conv_out_g, l))
        stack["w_out"] = _matmul_tn(mix, g1, False, stack["w_out"], (N_CHIPS, n, out_sh, D_MODEL),
                                    (N_CHIPS, None, out_sh, D_MODEL), lambda i, j: (0, sl, 0, 0), MIX_WIDTH, D_MODEL,
                                    "grad_w_out")
        cw, cl = weight_of["conv_w", l]
        d_u, d_gate, d_cw = _conv_bwd(d_y, z, cw[cl])
        d_q, d_kv, dbias, sm_attn = _attn_bwd(z, d_a, biasc, sink_rows(l), row(q_norm_g, l), row(k_norm_g, l))
        dbias_sum = dbias if dbias_sum is None else dbias_sum + dbias
        d_z = jnp.concatenate([d_q, d_kv[BLOCK:BLOCK + T], d_u, d_gate], axis=1)
        stack["w_in"] = _matmul_tn(h, d_z, False, stack["w_in"], (n, D_MODEL, IN_WIDTH), (None, D_MODEL, IN_WIDTH),
                                   lambda i, j: (sl, 0, 0), D_MODEL, IN_WIDTH, "grad_w_in")
        g, d_gmix = _matmul_nt_normbwd(d_z, *weight_of["w_in", l], x0, row(norm_mix_g, l), g1, "mix_in_bwd")
        small[l] = (d_gmix[0], sm_attn[0, :HEAD_DIM], sm_attn[1, :HEAD_DIM], sm_attn[2, :N_HEADS],
                    d_cw[:CONV_KERNEL], sm_mix[4], sm_mix[2], sm_mix[3], sm_mix[0], sm_mix[1], d_gmlp[0])
    grad_x = g.reshape(1, T, D_MODEL)

    d_rel = _bucket_reduce(dbias_sum.reshape(N_HEADS, BLOCK * BLOCK), onehot_t)[:, :NUM_BUCKETS].T
    stack = lambda k: jnp.stack([small[l][k] for l in range(L)])
    small_shapes = [(), (NUM_BUCKETS, N_HEADS), (L, D_MODEL), (L, HEAD_DIM), (L, HEAD_DIM), (L, N_HEADS),
                    (L, CONV_KERNEL, CONV_WIDTH), (L, CONV_WIDTH), (L, CONV_WIDTH), (L, CONV_WIDTH),
                    (L, CONV_WIDTH), (L, CONV_WIDTH), (L, D_MODEL)]
    part = _pack([jnp.sum(loss_parts[:, 0, 0]), d_rel] + [stack(k) for k in range(11)])
    tot = _unpack(_sum_devices(part), small_shapes)
    loss = tot[0]
    (g_rel, g_nmix, g_qn, g_kn, g_sk, g_cw_full, g_cb, g_lng, g_lnb, g_aog, g_cog, g_nmlp) = tot[1:]
    g_cw_sh = lax.dynamic_slice_in_dim(g_cw_full, shard * cv_sh, cv_sh, axis=2)

    small_w = [rel_bias, norm_mix_g, q_norm_g, k_norm_g, sinks, conv_w, conv_b, conv_ln_g, conv_ln_b,
               attn_out_g, conv_out_g, norm_mlp_g]
    small_m = [m_rel_bias, m_norm_mix_g, m_q_norm_g, m_k_norm_g, m_sinks, m_conv_w, m_conv_b, m_conv_ln_g,
               m_conv_ln_b, m_attn_out_g, m_conv_out_g, m_norm_mlp_g]
    small_v = [v_rel_bias, v_norm_mix_g, v_q_norm_g, v_k_norm_g, v_sinks, v_conv_w, v_conv_b, v_conv_ln_g,
               v_conv_ln_b, v_attn_out_g, v_conv_out_g, v_norm_mlp_g]
    small_g = [g_rel, g_nmix, g_qn, g_kn, g_sk, g_cw_sh, g_cb, g_lng, g_lnb, g_aog, g_cog, g_nmlp]
    shapes = [w.shape for w in small_w]
    sd, sm_, sv_ = _adamw(_pack(small_w), _pack(small_g), _pack(small_m), _pack(small_v), "adamw_small")
    small_d, small_nm, small_nv = _unpack(sd, shapes), _unpack(sm_, shapes), _unpack(sv_, shapes)

    owns_mix0, sends_mix0 = chip_sums("_mix0", {k: first[k] for k in ("w_in", "w_out")}, 1)
    arrived_mix0 = _exchange_chips([sends_mix0[k] for k in ("w_in", "w_out")])
    parts0 = {"w_in": (owns_mix0["w_in"], arrived_mix0[0]), "w_out": (owns_mix0["w_out"], arrived_mix0[1]),
              **finish_mlp0_grads(g)}
    parts1 = finish_rest_grads(g)
    grads, spans = [], []
    for name in names:
        R = shard_rows[name]
        full = None
        spans.append([(0, R), (R, (L - 1) * R)])
        for (r0, nr), (own, arrived), tag in zip(spans[-1], (parts0[name], parts1[name]), ("_first", "_rest")):
            tr = min(512, math.gcd(R, nr // 2))
            off = ((r0 + ci * (nr // 2)) // tr).astype(jnp.int32)[None]
            full = _shard_sum(own, arrived, off, tr, full, L * R, "shard_sum_" + name + tag)
        grads.append(full)
    grads = _join_halves(grads, spans)

    big_w = [w_in, w_out, w_mlp_up, w_mlp_down]
    big_m = [m_w_in, m_w_out, m_w_mlp_up, m_w_mlp_down]
    big_v = [v_w_in, v_w_out, v_w_mlp_up, v_w_mlp_down]
    big_g, big_d, big_nm, big_nv = [], [], [], []
    for b in range(4):
        shp = big_w[b].shape
        flat = lambda t: t.reshape(shp[0] * shp[1], shp[2])
        d, nm, nv = _adamw(flat(big_w[b]), grads[b], flat(big_m[b]), flat(big_v[b]), "adamw_" + names[b])
        big_g.append(grads[b].reshape(shp))
        big_d.append(d.reshape(shp))
        big_nm.append(nm.reshape(shp))
        big_nv.append(nv.reshape(shp))

    def ordered(sm, bg):
        return [sm[0], sm[1], bg[0], sm[2], sm[3], sm[4], sm[5], sm[6], sm[7], sm[8], sm[9], sm[10], bg[1], sm[11],
                bg[2], bg[3]]

    return (loss, grad_x, *ordered(small_g, big_g), *ordered(small_d, big_d), *ordered(small_nm, big_nm),
            *ordered(small_nv, big_nv))
```

```python
import math

import numpy as np
import jax
import jax.numpy as jnp
from jax import lax
from jax.experimental import pallas as pl
from jax.experimental.pallas import tpu as pltpu

F32 = jnp.float32
BF16 = jnp.bfloat16

D_MODEL = 1024
DEPTH = 4
HEAD_DIM = 64
N_HEADS = 8
N_KV_HEADS = 2
GQA_GROUP = N_HEADS // N_KV_HEADS
ATTN_WIDTH = N_HEADS * HEAD_DIM
KV_WIDTH = N_KV_HEADS * HEAD_DIM
CONV_WIDTH = D_MODEL - ATTN_WIDTH
MIX_WIDTH = ATTN_WIDTH + CONV_WIDTH
IN_WIDTH = ATTN_WIDTH + 2 * KV_WIDTH + 2 * CONV_WIDTH
BLOCK = 128
CONV_KERNEL = 31
CONV_ROWS = 32
HALO = 32
CONV_CH = 256
NUM_BUCKETS = 32
MAX_DISTANCE = 128
D_FF = 4 * D_MODEL
EPS = 1e-6
NEG = -1e30
SCALE = 1.0 / math.sqrt(HEAD_DIM)

ADAM_LR = 0.001
ADAM_B1 = 0.9
ADAM_B2 = 0.999
ADAM_EPS = 1e-08
ADAM_WD = 0.01
ADAM_STEP = 10

N_CHIPS = 4
N_DEV = 8
LANES = 128
VMEM_LIMIT = 52 * 1024 * 1024
K_CHUNK = 4096

Q0, K0, V0, U0, G0 = 0, ATTN_WIDTH, ATTN_WIDTH + KV_WIDTH, ATTN_WIDTH + 2 * KV_WIDTH, ATTN_WIDTH + 2 * KV_WIDTH + CONV_WIDTH

NT = (((1,), (1,)), ((), ()))
TN = (((0,), (0,)), ((), ()))
MESH = pl.DeviceIdType.MESH
ANY = pl.BlockSpec(memory_space=pl.ANY)


def _params(sem=None):
    return pltpu.CompilerParams(dimension_semantics=sem, vmem_limit_bytes=VMEM_LIMIT)


def _chunk(n, cap=1024):
    for c in range(cap, 0, -LANES):
        if n % c == 0:
            return c
    raise ValueError(n)


def _tile(t, want):
    return min(t, want)


def _after_spec(after):
    return [] if after is None else [pl.BlockSpec((8, LANES), lambda *_: (0, 0))]


def _after_arg(after):
    return [] if after is None else [after]


def _t5_bucket(n):
    n = np.asarray(n)
    max_exact = NUM_BUCKETS // 2
    large = max_exact + (np.log(np.maximum(n, 1) / max_exact) / np.log(MAX_DISTANCE / max_exact)
                         * (NUM_BUCKETS - max_exact)).astype(np.int32)
    large = np.minimum(large, NUM_BUCKETS - 1)
    return np.where(n < max_exact, n, large).astype(np.int32)


def _band_buckets():
    qi = np.arange(BLOCK)[:, None]
    j = np.arange(BLOCK)[None, :]
    return _t5_bucket(np.where(j <= qi, qi - j, qi + BLOCK - j))


def _norm_matmul(x, g, w_all, l, out_dtype, name, after=None):
    T, D = x.shape
    N = w_all.shape[2]
    TM = _tile(T, 512)
    CH = _chunk(N)

    def body(x_ref, g_ref, w_ref, *rest):
        h_ref, z_ref = rest[-2:]
        xv = x_ref[...]
        r = lax.rsqrt(jnp.mean(xv * xv, axis=-1, keepdims=True) + EPS)
        h = (xv * r * g_ref[...]).astype(BF16)
        h_ref[...] = h
        for c0 in range(0, N, CH):
            z_ref[:, c0:c0 + CH] = jnp.dot(h, w_ref[:, c0:c0 + CH], preferred_element_type=F32).astype(z_ref.dtype)

    return pl.pallas_call(
        body, name=name, grid=(T // TM,),
        in_specs=[pl.BlockSpec((TM, D), lambda i: (i, 0)),
                  pl.BlockSpec((1, D), lambda i: (0, 0)),
                  pl.BlockSpec((None, D, N), lambda i: (l, 0, 0))] + _after_spec(after),
        out_specs=[pl.BlockSpec((TM, D), lambda i: (i, 0)),
                   pl.BlockSpec((TM, N), lambda i: (i, 0))],
        out_shape=[jax.ShapeDtypeStruct((T, D), BF16), jax.ShapeDtypeStruct((T, N), out_dtype)],
        compiler_params=_params(("parallel",)),
    )(x, g, w_all, *_after_arg(after))


def _matmul_res(a, w_all, l, res, relu2, name):
    T, K = a.shape
    N = w_all.shape[2]
    TM = _tile(T, 512)
    CH = _chunk(K, K_CHUNK)

    def body(a_ref, w_ref, res_ref, o_ref):
        acc = res_ref[...]
        for k0 in range(0, K, CH):
            av = a_ref[:, k0:k0 + CH]
            if relu2:
                av = jnp.square(jnp.maximum(av.astype(F32), 0.0)).astype(BF16)
            acc = acc + jnp.dot(av, w_ref[k0:k0 + CH, :], preferred_element_type=F32)
        o_ref[...] = acc

    return pl.pallas_call(
        body, name=name, grid=(T // TM,),
        in_specs=[pl.BlockSpec((TM, K), lambda i: (i, 0)),
                  pl.BlockSpec((None, K, N), lambda i: (l, 0, 0)),
                  pl.BlockSpec((TM, N), lambda i: (i, 0))],
        out_specs=pl.BlockSpec((TM, N), lambda i: (i, 0)),
        out_shape=jax.ShapeDtypeStruct((T, N), F32),
        compiler_params=_params(("parallel",)),
    )(a, w_all, res)


def _head_norm(t, g):
    r = lax.rsqrt(jnp.mean(t * t, axis=-1, keepdims=True) + EPS)
    that = t * r
    return that * g, that, r


def _softmax_sink(s, sink):
    m = jnp.maximum(jnp.max(s, axis=-1, keepdims=True), sink)
    p = jnp.exp(s - m)
    es = jnp.exp(sink - m)
    den = jnp.sum(p, axis=-1, keepdims=True) + es
    return p / den, es / den


GROUP_ROWS = GQA_GROUP * BLOCK


def _own_block():
    row = lax.broadcasted_iota(jnp.int32, (GROUP_ROWS, BLOCK), 0)
    col = lax.broadcasted_iota(jnp.int32, (GROUP_ROWS, BLOCK), 1)
    return (row & (BLOCK - 1)) >= col


ATTN_QB = 4
KV_COLS = [slice(k * HEAD_DIM, (k + 1) * HEAD_DIM) for k in range(N_KV_HEADS)]


def _blk(i):
    return pl.ds(i * BLOCK, BLOCK)


def _stack_heads(ref, i, kvh):
    return jnp.concatenate([ref[_blk(i), (kvh * GQA_GROUP + g) * HEAD_DIM:(kvh * GQA_GROUP + g + 1) * HEAD_DIM]
                            for g in range(GQA_GROUP)], axis=0)


def _unstack_heads(ref, i, kvh, val):
    for g in range(GQA_GROUP):
        h = kvh * GQA_GROUP + g
        ref[_blk(i), h * HEAD_DIM:(h + 1) * HEAD_DIM] = val[g * BLOCK:(g + 1) * BLOCK]


def _band_probs(first, own, s_own, s_prev, bias, sink):
    s = jnp.where(own, s_own, s_prev) * SCALE + bias
    if first is not None:
        s = jnp.where(jnp.logical_or(own, jnp.logical_not(first)), s, NEG)
    return _softmax_sink(s, sink)


def _dot_nt(a, b):
    return lax.dot_general(a, b, NT, preferred_element_type=F32)


def _dot_tn(a, b):
    return lax.dot_general(a, b, TN, preferred_element_type=F32)


def _attn_fwd(z, biasc, sink_rows, qg, kg):
    T = z.shape[0]
    nb = T // BLOCK
    qb = min(ATTN_QB, nb)
    TQ = qb * BLOCK
    kb, vb = K0 // KV_WIDTH, V0 // KV_WIDTH
    groups = [(i, k) for i in range(qb) for k in range(N_KV_HEADS)]

    def body(q_ref, kc_ref, kp_ref, vc_ref, vp_ref, b_ref, sk_ref, qg_ref, kg_ref, a_ref):
        n = pl.program_id(0)
        own = _own_block()
        kn, vv = {}, {}
        for k in range(N_KV_HEADS):
            kn[-1, k] = _head_norm(kp_ref[:, KV_COLS[k]], kg_ref[...])[0].astype(BF16)
            vv[-1, k] = vp_ref[:, KV_COLS[k]].astype(BF16)
        for i, k in groups:
            kn[i, k] = _head_norm(kc_ref[_blk(i), KV_COLS[k]], kg_ref[...])[0].astype(BF16)
            vv[i, k] = vc_ref[_blk(i), KV_COLS[k]].astype(BF16)
        qnb = {g: _head_norm(_stack_heads(q_ref, *g), qg_ref[...])[0].astype(BF16) for g in groups}
        s_own = {(i, k): _dot_nt(qnb[i, k], kn[i, k]) for i, k in groups}
        s_prev = {(i, k): _dot_nt(qnb[i, k], kn[i - 1, k]) for i, k in groups}
        p = {(i, k): _band_probs(n == 0 if i == 0 else None, own, s_own[i, k], s_prev[i, k], b_ref[k], sk_ref[k])[0]
             for i, k in groups}
        p_own = {g: jnp.where(own, p[g], 0.0).astype(BF16) for g in groups}
        p_prev = {g: jnp.where(own, 0.0, p[g]).astype(BF16) for g in groups}
        o_own = {(i, k): jnp.dot(p_own[i, k], vv[i, k], preferred_element_type=F32) for i, k in groups}
        o_prev = {(i, k): jnp.dot(p_prev[i, k], vv[i - 1, k], preferred_element_type=F32) for i, k in groups}
        for i, k in groups:
            _unstack_heads(a_ref, i, k, o_own[i, k] + o_prev[i, k])

    prev = lambda n: jnp.maximum(n * qb - 1, 0)
    return pl.pallas_call(
        body, name="attn_fwd", grid=(nb // qb,),
        in_specs=[pl.BlockSpec((TQ, ATTN_WIDTH), lambda n: (n, 0)),
                  pl.BlockSpec((TQ, KV_WIDTH), lambda n: (n, kb)),
                  pl.BlockSpec((BLOCK, KV_WIDTH), lambda n: (prev(n), kb)),
                  pl.BlockSpec((TQ, KV_WIDTH), lambda n: (n, vb)),
                  pl.BlockSpec((BLOCK, KV_WIDTH), lambda n: (prev(n), vb)),
                  pl.BlockSpec((N_KV_HEADS, GROUP_ROWS, BLOCK), lambda n: (0, 0, 0)),
                  pl.BlockSpec((N_KV_HEADS, GROUP_ROWS, 1), lambda n: (0, 0, 0)),
                  pl.BlockSpec((1, HEAD_DIM), lambda n: (0, 0)),
                  pl.BlockSpec((1, HEAD_DIM), lambda n: (0, 0))],
        out_specs=pl.BlockSpec((TQ, ATTN_WIDTH), lambda n: (n, 0)),
        out_shape=jax.ShapeDtypeStruct((T, ATTN_WIDTH), F32),
        compiler_params=_params(("parallel",)),
    )(z, z, z, z, z, biasc, sink_rows, qg, kg)


SHIFTS = 8
CONV_RC = 64


def _shifted_copies(src_ref, dst_ref, total):
    for b in range(SHIFTS):
        rows = (total - b) // SHIFTS * SHIFTS
        for r0 in range(0, rows, CONV_RC):
            nr = min(CONV_RC, rows - r0)
            dst_ref[b, pl.ds(r0, nr), :] = src_ref[pl.ds(r0 + b, nr), :]


def _tap(ref, r0, o):
    return ref[o % SHIFTS, pl.ds(r0 + (o // SHIFTS) * SHIFTS, CONV_RC), :]


def _conv_fwd(z, cw, cb):
    T = z.shape[0]
    TC = _tile(T, 512)
    ub, gb = U0 // CONV_CH, G0 // CONV_CH
    hpt = TC // HALO
    lead = HALO - (CONV_KERNEL - 1)

    def body(u_ref, g_ref, up_ref, gp_ref, w_ref, b_ref, y_ref, hp_ref, hs_ref):
        i = pl.program_id(0)
        hp_ref[pl.ds(0, HALO), :] = jnp.where(i > 0, up_ref[...] * jax.nn.sigmoid(gp_ref[...]), 0.0)
        hp_ref[pl.ds(HALO, TC), :] = u_ref[...] * jax.nn.sigmoid(g_ref[...])
        _shifted_copies(hp_ref, hs_ref, TC + HALO)
        for r0 in range(0, TC, CONV_RC):
            acc = jnp.zeros((CONV_RC, CONV_CH), F32) + b_ref[...]
            for j in range(CONV_KERNEL):
                acc = acc + _tap(hs_ref, r0, lead + j) * w_ref[pl.ds(j, 1), :]
            y_ref[pl.ds(r0, CONV_RC), :] = acc

    prev = lambda i: jnp.maximum(i * hpt - 1, 0)
    return pl.pallas_call(
        body, name="conv_fwd", grid=(T // TC, CONV_WIDTH // CONV_CH),
        in_specs=[pl.BlockSpec((TC, CONV_CH), lambda i, j: (i, ub + j)),
                  pl.BlockSpec((TC, CONV_CH), lambda i, j: (i, gb + j)),
                  pl.BlockSpec((HALO, CONV_CH), lambda i, j: (prev(i), ub + j)),
                  pl.BlockSpec((HALO, CONV_CH), lambda i, j: (prev(i), gb + j)),
                  pl.BlockSpec((CONV_ROWS, CONV_CH), lambda i, j: (0, j)),
                  pl.BlockSpec((1, CONV_CH), lambda i, j: (0, j))],
        out_specs=pl.BlockSpec((TC, CONV_CH), lambda i, j: (i, j)),
        out_shape=jax.ShapeDtypeStruct((T, CONV_WIDTH), F32),
        scratch_shapes=[pltpu.VMEM((TC + HALO, CONV_CH), F32), pltpu.VMEM((SHIFTS, TC + HALO, CONV_CH), F32)],
        compiler_params=_params(("parallel", "parallel")),
    )(z, z, z, z, cw, cb)


def _ln_silu(y, ln_g, ln_b):
    mu = jnp.mean(y, axis=-1, keepdims=True)
    yc = y - mu
    var = jnp.mean(yc * yc, axis=-1, keepdims=True)
    rstd = lax.rsqrt(var + EPS)
    yhat = yc * rstd
    yn = yhat * ln_g + ln_b
    sg = jax.nn.sigmoid(yn)
    return yn * sg, yn, sg, yhat, rstd


def _mix_norm(a, y, ln_g, ln_b, ag, cg):
    T = a.shape[0]
    TM = _tile(T, 512)

    def body(a_ref, y_ref, lg_ref, lb_ref, ag_ref, cg_ref, o_ref):
        av = a_ref[...]
        ra = lax.rsqrt(jnp.mean(av * av, axis=-1, keepdims=True) + EPS)
        o_ref[:, :ATTN_WIDTH] = (av * ra * ag_ref[...]).astype(BF16)
        c, _, _, _, _ = _ln_silu(y_ref[...], lg_ref[...], lb_ref[...])
        rc = lax.rsqrt(jnp.mean(c * c, axis=-1, keepdims=True) + EPS)
        o_ref[:, ATTN_WIDTH:] = (c * rc * cg_ref[...]).astype(BF16)

    vec = pl.BlockSpec((1, CONV_WIDTH), lambda i: (0, 0))
    return pl.pallas_call(
        body, name="mix_norm", grid=(T // TM,),
        in_specs=[pl.BlockSpec((TM, ATTN_WIDTH), lambda i: (i, 0)),
                  pl.BlockSpec((TM, CONV_WIDTH), lambda i: (i, 0)), vec, vec, vec, vec],
        out_specs=pl.BlockSpec((TM, MIX_WIDTH), lambda i: (i, 0)),
        out_shape=jax.ShapeDtypeStruct((T, MIX_WIDTH), BF16),
        compiler_params=_params(("parallel",)),
    )(a, y, ln_g, ln_b, ag, cg)


def _loss_grad(y, tgt):
    T, D = y.shape
    TM = _tile(T, 512)
    nt = T // TM

    def body(y_ref, t_ref, part_ref, dy_ref):
        diff = y_ref[...] - t_ref[...]
        dy_ref[...] = diff / D
        tok = jnp.mean(diff * diff, axis=-1, keepdims=True)
        part_ref[...] = jnp.zeros((1, LANES), F32) + 0.5 * jnp.sum(tok)

    return pl.pallas_call(
        body, name="loss_grad", grid=(nt,),
        in_specs=[pl.BlockSpec((TM, D), lambda i: (i, 0)), pl.BlockSpec((TM, D), lambda i: (i, 0))],
        out_specs=[pl.BlockSpec((None, 1, LANES), lambda i: (i, 0, 0)), pl.BlockSpec((TM, D), lambda i: (i, 0))],
        out_shape=[jax.ShapeDtypeStruct((nt, 1, LANES), F32), jax.ShapeDtypeStruct((T, D), F32)],
        compiler_params=_params(("parallel",)),
    )(y, tgt)


def _dact(g, w_all, l, up, after=None):
    T, N = g.shape
    K = w_all.shape[1]
    TM = _tile(T, 512)
    CH = _chunk(K)

    def body(g_ref, w_ref, up_ref, *rest):
        o_ref = rest[-1]
        gv = g_ref[...].astype(BF16)
        for k0 in range(0, K, CH):
            da = lax.dot_general(gv, w_ref[k0:k0 + CH, :], NT, preferred_element_type=F32)
            upv = up_ref[:, k0:k0 + CH].astype(F32)
            o_ref[:, k0:k0 + CH] = (da * (2.0 * jnp.maximum(upv, 0.0))).astype(BF16)

    return pl.pallas_call(
        body, name="mlp_dact", grid=(T // TM,),
        in_specs=[pl.BlockSpec((TM, N), lambda i: (i, 0)),
                  pl.BlockSpec((None, K, N), lambda i: (l, 0, 0)),
                  pl.BlockSpec((TM, K), lambda i: (i, 0))] + _after_spec(after),
        out_specs=pl.BlockSpec((TM, K), lambda i: (i, 0)),
        out_shape=jax.ShapeDtypeStruct((T, K), BF16),
        compiler_params=_params(("parallel",)),
    )(g, w_all, up, *_after_arg(after))


def _matmul_tn(a, b, relu2, buf, buf_shape, out_block, out_index, tm, tn, name):
    pieces = list(b) if isinstance(b, (list, tuple)) else [b]
    T, M = a.shape
    N = sum(p.shape[1] for p in pieces)
    assert len(pieces) == 1 or tn == N
    starts = np.cumsum([0] + [p.shape[1] for p in pieces])
    TK = _tile(T, 1024)
    nk = T // TK

    def body(*refs):
        a_ref, b_refs = refs[0], refs[1:1 + len(pieces)]
        o_ref = refs[-1]
        k = pl.program_id(2)
        av = a_ref[...]
        if relu2:
            av = jnp.square(jnp.maximum(av.astype(F32), 0.0)).astype(BF16)
        cs = [lax.dot_general(av, b_ref[...].astype(BF16), TN, preferred_element_type=F32) for b_ref in b_refs]

        def put(add):
            for p, c in enumerate(cs):
                if len(cs) == 1:
                    o_ref[...] = c.reshape(o_ref.shape) + (o_ref[...] if add else 0.0)
                else:
                    cols = slice(int(starts[p]), int(starts[p + 1]))
                    o_ref[:, cols] = c + (o_ref[:, cols] if add else 0.0)

        @pl.when(k == 0)
        def _():
            put(False)

        @pl.when(k > 0)
        def _():
            put(True)

    if len(pieces) == 1:
        b_specs = [pl.BlockSpec((TK, tn), lambda i, j, k: (k, j))]
    else:
        b_specs = [pl.BlockSpec((TK, p.shape[1]), lambda i, j, k: (k, 0)) for p in pieces]
    in_specs = [pl.BlockSpec((TK, tm), lambda i, j, k: (k, i))] + b_specs
    args = [a] + pieces
    aliases = {}
    if buf is not None:
        in_specs.append(ANY)
        args.append(buf)
        aliases = {len(args) - 1: 0}
    return pl.pallas_call(
        body, name=name, grid=(M // tm, N // tn, nk),
        in_specs=in_specs,
        out_specs=pl.BlockSpec(out_block, lambda i, j, k: out_index(i, j)),
        out_shape=jax.ShapeDtypeStruct(buf_shape, F32),
        input_output_aliases=aliases,
        compiler_params=_params(("parallel", "parallel", "arbitrary")),
    )(*args)


def _matmul_nt_normbwd(dz, w_all, l, x, gvec, gres, name, after=None):
    pieces = list(dz) if isinstance(dz, (list, tuple)) else [dz]
    T = pieces[0].shape[0]
    K = sum(p.shape[1] for p in pieces)
    starts = np.cumsum([0] + [p.shape[1] for p in pieces])
    D = x.shape[1]
    TM = _tile(T, 512)

    def body(*refs):
        dz_refs = refs[:len(pieces)]
        w_ref, x_ref, gv_ref, gr_ref = refs[len(pieces):len(pieces) + 4]
        o_ref, dg_ref = refs[-2:]
        i = pl.program_id(0)
        dh = jnp.zeros((TM, D), F32)
        for p, dz_ref in enumerate(dz_refs):
            width = dz_ref.shape[1]
            ch = _chunk(width, K_CHUNK)
            for k0 in range(0, width, ch):
                wk = int(starts[p]) + k0
                dh = dh + lax.dot_general(dz_ref[:, k0:k0 + ch], w_ref[:, wk:wk + ch], NT, preferred_element_type=F32)
        xv = x_ref[...]
        r = lax.rsqrt(jnp.mean(xv * xv, axis=-1, keepdims=True) + EPS)
        xhat = xv * r
        dg = jnp.sum(dh * xhat, axis=0, keepdims=True)

        @pl.when(i == 0)
        def _():
            dg_ref[...] = dg

        @pl.when(i > 0)
        def _():
            dg_ref[...] += dg

        wv = dh * gv_ref[...]
        o_ref[...] = gr_ref[...] + r * (wv - xhat * jnp.mean(wv * xhat, axis=-1, keepdims=True))

    return pl.pallas_call(
        body, name=name, grid=(T // TM,),
        in_specs=[pl.BlockSpec((TM, p.shape[1]), lambda i: (i, 0)) for p in pieces]
        + [pl.BlockSpec((None, D, K), lambda i: (l, 0, 0)),
           pl.BlockSpec((TM, D), lambda i: (i, 0)),
           pl.BlockSpec((1, D), lambda i: (0, 0)),
           pl.BlockSpec((TM, D), lambda i: (i, 0))] + _after_spec(after),
        out_specs=[pl.BlockSpec((TM, D), lambda i: (i, 0)), pl.BlockSpec((1, D), lambda i: (0, 0))],
        out_shape=[jax.ShapeDtypeStruct((T, D), F32), jax.ShapeDtypeStruct((1, D), F32)],
        compiler_params=_params(("arbitrary",)),
    )(*pieces, w_all, x, gvec, gres, *_after_arg(after))


def _mix_bwd(g1, w_all, l, a, y, ln_g, ln_b, ag, cg):
    T, D = g1.shape
    TM = _tile(T, 512)

    def body(g_ref, w_ref, a_ref, y_ref, lg_ref, lb_ref, ag_ref, cg_ref, da_ref, dy_ref, sm_ref):
        i = pl.program_id(0)
        dmix = lax.dot_general(g_ref[...].astype(BF16), w_ref[...], NT, preferred_element_type=F32)
        dma, dmc = dmix[:, :ATTN_WIDTH], dmix[:, ATTN_WIDTH:]
        av = a_ref[...]
        ra = lax.rsqrt(jnp.mean(av * av, axis=-1, keepdims=True) + EPS)
        ahat = av * ra
        d_ag = jnp.sum(dma * ahat, axis=0, keepdims=True)
        wa = dma * ag_ref[...]
        da_ref[...] = ra * (wa - ahat * jnp.mean(wa * ahat, axis=-1, keepdims=True))

        c, yn, sg, yhat, rstd = _ln_silu(y_ref[...], lg_ref[...], lb_ref[...])
        rc = lax.rsqrt(jnp.mean(c * c, axis=-1, keepdims=True) + EPS)
        chat = c * rc
        d_cg = jnp.sum(dmc * chat, axis=0, keepdims=True)
        wc = dmc * cg_ref[...]
        dc = rc * (wc - chat * jnp.mean(wc * chat, axis=-1, keepdims=True))
        dyn = dc * (sg * (1.0 + yn * (1.0 - sg)))
        d_lg = jnp.sum(dyn * yhat, axis=0, keepdims=True)
        d_lb = jnp.sum(dyn, axis=0, keepdims=True)
        dyh = dyn * lg_ref[...]
        dy = rstd * (dyh - jnp.mean(dyh, axis=-1, keepdims=True) - yhat * jnp.mean(dyh * yhat, axis=-1, keepdims=True))
        dy_ref[...] = dy
        d_cb = jnp.sum(dy, axis=0, keepdims=True)
        sums = jnp.concatenate([d_ag, d_cg, d_lg, d_lb, d_cb, jnp.zeros((3, CONV_WIDTH), F32)], axis=0)

        @pl.when(i == 0)
        def _():
            sm_ref[...] = sums

        @pl.when(i > 0)
        def _():
            sm_ref[...] += sums

    vec = pl.BlockSpec((1, CONV_WIDTH), lambda i: (0, 0))
    return pl.pallas_call(
        body, name="mix_bwd", grid=(T // TM,),
        in_specs=[pl.BlockSpec((TM, D), lambda i: (i, 0)),
                  pl.BlockSpec((None, MIX_WIDTH, D), lambda i: (l, 0, 0)),
                  pl.BlockSpec((TM, ATTN_WIDTH), lambda i: (i, 0)),
                  pl.BlockSpec((TM, CONV_WIDTH), lambda i: (i, 0)), vec, vec, vec, vec],
        out_specs=[pl.BlockSpec((TM, ATTN_WIDTH), lambda i: (i, 0)),
                   pl.BlockSpec((TM, CONV_WIDTH), lambda i: (i, 0)),
                   pl.BlockSpec((8, CONV_WIDTH), lambda i: (0, 0))],
        out_shape=[jax.ShapeDtypeStruct((T, ATTN_WIDTH), F32), jax.ShapeDtypeStruct((T, CONV_WIDTH), F32),
                   jax.ShapeDtypeStruct((8, CONV_WIDTH), F32)],
        compiler_params=_params(("arbitrary",)),
    )(g1, w_all, a, y, ln_g, ln_b, ag, cg)


def _conv_bwd(dy, z, cw):
    T = z.shape[0]
    TC = _tile(T, 512)
    nt = T // TC
    ub, gb = U0 // CONV_CH, G0 // CONV_CH
    nch = CONV_WIDTH // CONV_CH
    hpt = TC // HALO

    lead = HALO - (CONV_KERNEL - 1)

    def body(dy_ref, dyn_ref, u_ref, g_ref, up_ref, gp_ref, w_ref, du_ref, dg_ref, dw_ref,
             hp_ref, hs_ref, dyp_ref, dys_ref):
        i = pl.program_id(1)
        hp_ref[pl.ds(0, HALO), :] = jnp.where(i > 0, up_ref[...] * jax.nn.sigmoid(gp_ref[...]), 0.0)
        hp_ref[pl.ds(HALO, TC), :] = u_ref[...] * jax.nn.sigmoid(g_ref[...])
        _shifted_copies(hp_ref, hs_ref, TC + HALO)
        dyp_ref[pl.ds(0, TC), :] = dy_ref[...]
        dyp_ref[pl.ds(TC, HALO), :] = jnp.where(i < nt - 1, dyn_ref[...], 0.0)
        _shifted_copies(dyp_ref, dys_ref, TC + HALO)

        @pl.when(i == 0)
        def _():
            dw_ref[...] = jnp.zeros((CONV_ROWS, CONV_CH), F32)

        for r0 in range(0, TC, CONV_RC):
            rows = pl.ds(r0, CONV_RC)
            dh = jnp.zeros((CONV_RC, CONV_CH), F32)
            for j in range(CONV_KERNEL):
                dh = dh + _tap(dys_ref, r0, CONV_KERNEL - 1 - j) * w_ref[pl.ds(j, 1), :]
            uv = u_ref[rows, :]
            sg = jax.nn.sigmoid(g_ref[rows, :])
            du_ref[rows, :] = (dh * sg).astype(BF16)
            dg_ref[rows, :] = (dh * uv * sg * (1.0 - sg)).astype(BF16)
        for j in range(CONV_KERNEL):
            acc = jnp.zeros((SHIFTS, CONV_CH), F32)
            for r0 in range(0, TC, CONV_RC):
                prod = dy_ref[pl.ds(r0, CONV_RC), :] * _tap(hs_ref, r0, lead + j)
                acc = acc + jnp.sum(prod.reshape(CONV_RC // SHIFTS, SHIFTS, CONV_CH), axis=0)
            dw_ref[pl.ds(j, 1), :] += jnp.sum(acc, axis=0, keepdims=True)

    prev = lambda i: jnp.maximum(i * hpt - 1, 0)
    nxt = lambda i: jnp.minimum((i + 1) * hpt, T // HALO - 1)
    return pl.pallas_call(
        body, name="conv_bwd", grid=(nch, nt),
        in_specs=[pl.BlockSpec((TC, CONV_CH), lambda j, i: (i, j)),
                  pl.BlockSpec((HALO, CONV_CH), lambda j, i: (nxt(i), j)),
                  pl.BlockSpec((TC, CONV_CH), lambda j, i: (i, ub + j)),
                  pl.BlockSpec((TC, CONV_CH), lambda j, i: (i, gb + j)),
                  pl.BlockSpec((HALO, CONV_CH), lambda j, i: (prev(i), ub + j)),
                  pl.BlockSpec((HALO, CONV_CH), lambda j, i: (prev(i), gb + j)),
                  pl.BlockSpec((CONV_ROWS, CONV_CH), lambda j, i: (0, j))],
        out_specs=[pl.BlockSpec((TC, CONV_CH), lambda j, i: (i, j)),
                   pl.BlockSpec((TC, CONV_CH), lambda j, i: (i, j)),
                   pl.BlockSpec((CONV_ROWS, CONV_CH), lambda j, i: (0, j))],
        out_shape=[jax.ShapeDtypeStruct((T, CONV_WIDTH), BF16), jax.ShapeDtypeStruct((T, CONV_WIDTH), BF16),
                   jax.ShapeDtypeStruct((CONV_ROWS, CONV_WIDTH), F32)],
        scratch_shapes=[pltpu.VMEM((TC + HALO, CONV_CH), F32), pltpu.VMEM((SHIFTS, TC + HALO, CONV_CH), F32),
                        pltpu.VMEM((TC + HALO, CONV_CH), F32), pltpu.VMEM((SHIFTS, TC + HALO, CONV_CH), F32)],
        compiler_params=_params(("parallel", "arbitrary")),
    )(dy, dy, z, z, z, z, cw)


def _norm_bwd(d, that, r, g):
    w = d * g
    return r * (w - that * jnp.mean(w * that, axis=-1, keepdims=True)), jnp.sum(d * that, axis=0, keepdims=True)


def _attn_bwd(z, da, biasc, sink_rows, qg, kg):
    T = z.shape[0]
    nb = T // BLOCK
    qb = min(ATTN_QB, nb)
    TQ = qb * BLOCK
    ns = nb // qb
    kb, vb = K0 // KV_WIDTH, V0 // KV_WIDTH
    groups = [(i, k) for i in range(qb) for k in range(N_KV_HEADS)]

    def body(q_ref, kc_ref, kp_ref, vc_ref, vp_ref, da_ref, b_ref, sk_ref, qg_ref, kg_ref,
             dq_ref, dkv_ref, db_ref, sm_ref, ck_ref, cv_ref, pk_ref, pv_ref, nk_ref, nv_ref):
        n = pl.program_id(0)
        lane = lax.broadcasted_iota(jnp.int32, (1, LANES), 1)

        @pl.when(n == 0)
        def _():
            db_ref[...] = jnp.zeros(db_ref.shape, F32)
            sm_ref[...] = jnp.zeros(sm_ref.shape, F32)
            ck_ref[...] = jnp.zeros(ck_ref.shape, F32)
            cv_ref[...] = jnp.zeros(cv_ref.shape, F32)

        pk_ref[...] = jnp.zeros(pk_ref.shape, F32)
        pv_ref[...] = jnp.zeros(pv_ref.shape, F32)

        @pl.when(n < ns)
        def _():
            own = _own_block()
            knorm, kn, vv = {}, {}, {}
            for k in range(N_KV_HEADS):
                kn[-1, k] = _head_norm(kp_ref[:, KV_COLS[k]], kg_ref[...])[0].astype(BF16)
                vv[-1, k] = vp_ref[:, KV_COLS[k]].astype(BF16)
            for i, k in groups:
                knorm[i, k] = _head_norm(kc_ref[_blk(i), KV_COLS[k]], kg_ref[...])
                kn[i, k] = knorm[i, k][0].astype(BF16)
                vv[i, k] = vc_ref[_blk(i), KV_COLS[k]].astype(BF16)
            qnorm = {g: _head_norm(_stack_heads(q_ref, *g), qg_ref[...]) for g in groups}
            qnb = {g: qnorm[g][0].astype(BF16) for g in groups}
            dob = {g: _stack_heads(da_ref, *g).astype(BF16) for g in groups}
            s_own = {(i, k): _dot_nt(qnb[i, k], kn[i, k]) for i, k in groups}
            s_prev = {(i, k): _dot_nt(qnb[i, k], kn[i - 1, k]) for i, k in groups}
            dp_own = {(i, k): _dot_nt(dob[i, k], vv[i, k]) for i, k in groups}
            dp_prev = {(i, k): _dot_nt(dob[i, k], vv[i - 1, k]) for i, k in groups}
            probs = {(i, k): _band_probs(n == 0 if i == 0 else None, own, s_own[i, k], s_prev[i, k], b_ref[k], sk_ref[k])
                     for i, k in groups}
            ds_own, ds_prev, p_own, p_prev = {}, {}, {}, {}
            dsk = jnp.zeros((1, LANES), F32)
            dbias = [jnp.zeros((GROUP_ROWS, BLOCK), F32) for _ in range(N_KV_HEADS)]
            for i, k in groups:
                p, psink = probs[i, k]
                dp = jnp.where(own, dp_own[i, k], dp_prev[i, k])
                delta = jnp.sum(p * dp, axis=-1, keepdims=True)
                ds = p * (dp - delta)
                dbias[k] = dbias[k] + ds
                dsink = psink * delta
                for g in range(GQA_GROUP):
                    dsk = dsk + jnp.where(lane == k * GQA_GROUP + g, -jnp.sum(dsink[g * BLOCK:(g + 1) * BLOCK]), 0.0)
                ds_own[i, k] = jnp.where(own, ds, 0.0).astype(BF16)
                ds_prev[i, k] = jnp.where(own, 0.0, ds).astype(BF16)
                p_own[i, k] = jnp.where(own, p, 0.0).astype(BF16)
                p_prev[i, k] = jnp.where(own, 0.0, p).astype(BF16)
            for k in range(N_KV_HEADS):
                db_ref[k] += dbias[k]
            dqn_own = {(i, k): jnp.dot(ds_own[i, k], kn[i, k], preferred_element_type=F32) for i, k in groups}
            dqn_prev = {(i, k): jnp.dot(ds_prev[i, k], kn[i - 1, k], preferred_element_type=F32) for i, k in groups}
            dk_own = {g: _dot_tn(ds_own[g], qnb[g]) * SCALE for g in groups}
            dk_prev = {g: _dot_tn(ds_prev[g], qnb[g]) * SCALE for g in groups}
            dv_own = {g: _dot_tn(p_own[g], dob[g]) for g in groups}
            dv_prev = {g: _dot_tn(p_prev[g], dob[g]) for g in groups}
            dqg = jnp.zeros((1, HEAD_DIM), F32)
            dkg = jnp.zeros((1, HEAD_DIM), F32)
            for i, k in groups:
                _, qhat, rq = qnorm[i, k]
                dq, dg = _norm_bwd((dqn_own[i, k] + dqn_prev[i, k]) * SCALE, qhat, rq, qg_ref[...])
                dqg = dqg + dg
                _unstack_heads(dq_ref, i, k, dq.astype(BF16))
                if i == 0:
                    pk_ref[:, KV_COLS[k]] = dk_prev[i, k]
                    pv_ref[:, KV_COLS[k]] = dv_prev[i, k]
                if i == qb - 1:
                    nk_ref[:, KV_COLS[k]] = dk_own[i, k]
                    nv_ref[:, KV_COLS[k]] = dv_own[i, k]
                else:
                    _, khat, rk = knorm[i, k]
                    dk, dg = _norm_bwd(dk_own[i, k] + dk_prev[i + 1, k], khat, rk, kg_ref[...])
                    dkg = dkg + dg
                    dkv_ref[_blk(i + 1), KV_COLS[k]] = dk.astype(BF16)
                    dkv_ref[_blk(i + 1), pl.ds(KV_WIDTH + k * HEAD_DIM, HEAD_DIM)] = (dv_own[i, k] + dv_prev[i + 1, k]).astype(BF16)
            sm_ref[pl.ds(0, 1), pl.ds(0, HEAD_DIM)] += dqg
            sm_ref[pl.ds(1, 1), pl.ds(0, HEAD_DIM)] += dkg
            sm_ref[pl.ds(2, 1), :] += dsk

        @pl.when(n >= 1)
        def _():
            dkg = jnp.zeros((1, HEAD_DIM), F32)
            for k in range(N_KV_HEADS):
                _, khat, rk = _head_norm(kp_ref[:, KV_COLS[k]], kg_ref[...])
                dk, dg = _norm_bwd(ck_ref[:, KV_COLS[k]] + pk_ref[:, KV_COLS[k]], khat, rk, kg_ref[...])
                dkg = dkg + dg
                dkv_ref[_blk(0), KV_COLS[k]] = dk.astype(BF16)
            dkv_ref[_blk(0), pl.ds(KV_WIDTH, KV_WIDTH)] = (cv_ref[...] + pv_ref[...]).astype(BF16)
            sm_ref[pl.ds(1, 1), pl.ds(0, HEAD_DIM)] += dkg

        ck_ref[...] = nk_ref[...]
        cv_ref[...] = nv_ref[...]

    cur = lambda n: jnp.minimum(n, ns - 1)
    prev = lambda n: jnp.maximum(n * qb - 1, 0)
    carry = pltpu.VMEM((BLOCK, KV_WIDTH), F32)
    return pl.pallas_call(
        body, name="attn_bwd", grid=(ns + 1,),
        in_specs=[pl.BlockSpec((TQ, ATTN_WIDTH), lambda n: (cur(n), 0)),
                  pl.BlockSpec((TQ, KV_WIDTH), lambda n: (cur(n), kb)),
                  pl.BlockSpec((BLOCK, KV_WIDTH), lambda n: (prev(n), kb)),
                  pl.BlockSpec((TQ, KV_WIDTH), lambda n: (cur(n), vb)),
                  pl.BlockSpec((BLOCK, KV_WIDTH), lambda n: (prev(n), vb)),
                  pl.BlockSpec((TQ, ATTN_WIDTH), lambda n: (cur(n), 0)),
                  pl.BlockSpec((N_KV_HEADS, GROUP_ROWS, BLOCK), lambda n: (0, 0, 0)),
                  pl.BlockSpec((N_KV_HEADS, GROUP_ROWS, 1), lambda n: (0, 0, 0)),
                  pl.BlockSpec((1, HEAD_DIM), lambda n: (0, 0)),
                  pl.BlockSpec((1, HEAD_DIM), lambda n: (0, 0))],
        out_specs=[pl.BlockSpec((TQ, ATTN_WIDTH), lambda n: (cur(n), 0)),
                   pl.BlockSpec((TQ, 2 * KV_WIDTH), lambda n: (n, 0)),
                   pl.BlockSpec((N_KV_HEADS, GROUP_ROWS, BLOCK), lambda n: (0, 0, 0)),
                   pl.BlockSpec((8, LANES), lambda n: (0, 0))],
        out_shape=[jax.ShapeDtypeStruct((T, ATTN_WIDTH), BF16), jax.ShapeDtypeStruct(((ns + 1) * TQ, 2 * KV_WIDTH), BF16),
                   jax.ShapeDtypeStruct((N_KV_HEADS, GROUP_ROWS, BLOCK), F32), jax.ShapeDtypeStruct((8, LANES), F32)],
        scratch_shapes=[carry] * 6,
        compiler_params=_params(("arbitrary",)),
    )(z, z, z, z, z, da, biasc, sink_rows, qg, kg)


def _bucket_reduce(dbias, onehot_t):
    def body(d_ref, oh_ref, o_ref):
        d = d_ref[...]
        hi = d.astype(BF16)
        r1 = d - hi.astype(F32)
        mid = r1.astype(BF16)
        lo = (r1 - mid.astype(F32)).astype(BF16)
        oh = oh_ref[...]
        acc = lax.dot_general(lo, oh, NT, preferred_element_type=F32)
        acc = acc + lax.dot_general(mid, oh, NT, preferred_element_type=F32)
        o_ref[...] = acc + lax.dot_general(hi, oh, NT, preferred_element_type=F32)

    return pl.pallas_call(
        body, name="bucket_reduce",
        out_shape=jax.ShapeDtypeStruct((N_HEADS, LANES), F32),
        compiler_params=_params(),
    )(dbias, onehot_t)


def _adamw(w, g, m, v, name):
    R, C = w.shape
    TR = _tile(R, 512)

    def body(w_ref, g_ref, m_ref, v_ref, d_ref, nm_ref, nv_ref):
        gv = g_ref[...]
        mn = ADAM_B1 * m_ref[...] + (1.0 - ADAM_B1) * gv
        vn = ADAM_B2 * v_ref[...] + (1.0 - ADAM_B2) * jnp.square(gv)
        m_hat = mn / (1.0 - ADAM_B1 ** ADAM_STEP)
        v_hat = vn / (1.0 - ADAM_B2 ** ADAM_STEP)
        d_ref[...] = -ADAM_LR * (m_hat / (jnp.sqrt(v_hat) + ADAM_EPS) + ADAM_WD * w_ref[...])
        nm_ref[...] = mn
        nv_ref[...] = vn

    spec = pl.BlockSpec((TR, C), lambda i: (i, 0))
    shp = jax.ShapeDtypeStruct((R, C), F32)
    return pl.pallas_call(
        body, name=name, grid=(R // TR,),
        in_specs=[spec] * 4, out_specs=[spec] * 3, out_shape=[shp] * 3,
        compiler_params=_params(("parallel",)),
    )(w, g, m, v)


def _place():
    return lax.axis_index("x"), lax.axis_index("y"), lax.axis_index("c")


def _other_chips(x, y):
    return [(1 - x, y), (x, 1 - y), (1 - x, 1 - y)]


def _remote(src, dst, send_sem, recv_sem, dev):
    return pltpu.make_async_remote_copy(src_ref=src, dst_ref=dst, send_sem=send_sem, recv_sem=recv_sem,
                                        device_id=dev, device_id_type=MESH)


def _gather_shards(bufs):
    nbuf = len(bufs)

    def body(*refs):
        ins, outs = refs[:nbuf], refs[nbuf:2 * nbuf]
        send_sems, recv_sems = refs[2 * nbuf:]
        x, y, c = _place()
        me = 2 * x + y
        sib = (x, y, 1 - c)
        chips = _other_chips(x, y)
        started = []
        for b in range(nbuf):
            hh = bufs[b].shape[0] // 2
            for j, (cx, cy) in enumerate(chips):
                k = 6 * b + j
                cp = _remote(ins[b].at[pl.ds(c * hh, hh), :], outs[b].at[me, pl.ds(c * hh, hh), :],
                             send_sems.at[k], recv_sems.at[k], (cx, cy, c))
                cp.start()
                started.append(cp)
        for b in range(nbuf):
            hh = bufs[b].shape[0] // 2
            for j, (cx, cy) in enumerate(chips):
                rows = outs[b].at[2 * cx + cy, pl.ds(c * hh, hh), :]
                _remote(rows, rows, send_sems.at[6 * b + j], recv_sems.at[6 * b + j], sib).wait_recv()
                k = 6 * b + 3 + j
                cp = _remote(rows, rows, send_sems.at[k], recv_sems.at[k], sib)
                cp.start()
                started.append(cp)
        for b in range(nbuf):
            hh = bufs[b].shape[0] // 2
            for j, (cx, cy) in enumerate(chips):
                rows = outs[b].at[2 * cx + cy, pl.ds((1 - c) * hh, hh), :]
                k = 6 * b + 3 + j
                _remote(rows, rows, send_sems.at[k], recv_sems.at[k], sib).wait_recv()
        for cp in started:
            cp.wait_send()

    return pl.pallas_call(
        body, name="gather_weights",
        in_specs=[ANY] * nbuf, out_specs=[ANY] * nbuf,
        out_shape=[jax.ShapeDtypeStruct((N_CHIPS,) + b.shape, b.dtype) for b in bufs],
        scratch_shapes=[pltpu.SemaphoreType.DMA((6 * nbuf,)), pltpu.SemaphoreType.DMA((6 * nbuf,))],
        compiler_params=pltpu.CompilerParams(has_side_effects=True),
    )(*bufs)


HBM = pl.BlockSpec(memory_space=pltpu.HBM)
SEM = pl.BlockSpec(memory_space=pltpu.SEMAPHORE)
DATAFLOW = pltpu.SideEffectType.DATAFLOW_SIDE_EFFECTING


def _gather_plan(shapes):
    def plan(srcs, lands):
        x, y, c = _place()
        out = []
        for b, shp in enumerate(shapes):
            hh = shp[0] // 2
            for cx, cy in _other_chips(x, y):
                out.append((srcs[b].at[pl.ds(c * hh, hh), :], lands[b].at[2 * x + y, pl.ds(c * hh, hh), :], (cx, cy, c)))
        return out
    return plan


def _exchange_plan(shapes):
    def plan(srcs, lands):
        x, y, c = _place()
        return [(srcs[b].at[j], lands[b].at[j], (cx, cy, c))
                for b in range(len(shapes)) for j, (cx, cy) in enumerate(_other_chips(x, y))]
    return plan


def _start_copies(name, srcs, land_shapes, plan):
    n = len(srcs)
    ncopy = 3 * n

    def body(*refs):
        ins, lands = refs[:n], refs[n:2 * n]
        send_sems, recv_sems, token = refs[2 * n], refs[2 * n + 1], refs[-1]
        for k, (src, dst, dev) in enumerate(plan(ins, lands)):
            _remote(src, dst, send_sems.at[k], recv_sems.at[k], dev).start()
        token[...] = jnp.zeros_like(token)

    hbm = lambda a: pltpu.with_memory_space_constraint(a, pltpu.HBM)
    lands = [lax.empty(s, a.dtype) for s, a in zip(land_shapes, srcs)]
    outs = pl.pallas_call(
        body, name=name,
        out_shape=(pltpu.SemaphoreType.DMA((ncopy,)), pltpu.SemaphoreType.DMA((ncopy,)),
                   *[pltpu.HBM(a.shape, a.dtype) for a in srcs], *[pltpu.HBM(a.shape, a.dtype) for a in lands],
                   jax.ShapeDtypeStruct((8, LANES), F32)),
        in_specs=[HBM] * (2 * n),
        out_specs=(SEM, SEM, *([HBM] * (2 * n)), pl.BlockSpec(memory_space=pltpu.VMEM)),
        input_output_aliases={i: 2 + i for i in range(2 * n)},
        compiler_params=pltpu.CompilerParams(has_side_effects=DATAFLOW),
    )(*[hbm(a) for a in srcs], *[hbm(a) for a in lands])
    return outs[0], outs[1], list(outs[2:2 + n]), list(outs[2 + n:2 + 2 * n]), outs[-1]


def _wait_copies(name, send_sems, recv_sems, srcs, lands, plan, after):
    n = len(srcs)

    def body(*refs):
        ins, lnds = refs[:n], refs[n:2 * n]
        ssem, rsem = refs[2 * n], refs[2 * n + 1]
        for k, (src, dst, dev) in enumerate(plan(ins, lnds)):
            cp = _remote(src, dst, ssem.at[k], rsem.at[k], dev)
            cp.wait_send()
            cp.wait_recv()

    outs = pl.pallas_call(
        body, name=name,
        out_shape=(*[pltpu.HBM(a.shape, a.dtype) for a in srcs], *[pltpu.HBM(a.shape, a.dtype) for a in lands]),
        in_specs=[HBM] * (2 * n) + [SEM, SEM, ANY],
        out_specs=tuple([HBM] * (2 * n)),
        input_output_aliases={i: i for i in range(2 * n)},
        compiler_params=pltpu.CompilerParams(has_side_effects=DATAFLOW),
    )(*srcs, *lands, send_sems, recv_sems, after)
    return list(outs[n:])


def _forward_halves(bufs):
    nbuf = len(bufs)

    def body(*refs):
        outs = refs[nbuf:2 * nbuf]
        send_sems, recv_sems = refs[2 * nbuf:]
        x, y, c = _place()
        sib = (x, y, 1 - c)
        cps = []
        for b in range(nbuf):
            hh = bufs[b].shape[1] // 2
            for j, (cx, cy) in enumerate(_other_chips(x, y)):
                rows = outs[b].at[2 * cx + cy, pl.ds(c * hh, hh), :]
                cp = _remote(rows, rows, send_sems.at[3 * b + j], recv_sems.at[3 * b + j], sib)
                cp.start()
                cps.append(cp)
        for b in range(nbuf):
            hh = bufs[b].shape[1] // 2
            for j, (cx, cy) in enumerate(_other_chips(x, y)):
                rows = outs[b].at[2 * cx + cy, pl.ds((1 - c) * hh, hh), :]
                _remote(rows, rows, send_sems.at[3 * b + j], recv_sems.at[3 * b + j], sib).wait_recv()
        for cp in cps:
            cp.wait_send()

    return pl.pallas_call(
        body, name="gather_forward_halves",
        in_specs=[ANY] * nbuf, out_specs=[ANY] * nbuf,
        out_shape=[jax.ShapeDtypeStruct(b.shape, b.dtype) for b in bufs],
        input_output_aliases={b: b for b in range(nbuf)},
        scratch_shapes=[pltpu.SemaphoreType.DMA((3 * nbuf,)), pltpu.SemaphoreType.DMA((3 * nbuf,))],
        compiler_params=pltpu.CompilerParams(has_side_effects=True),
    )(*bufs)


def _swap_halves(bufs):
    nbuf = len(bufs)

    def body(*refs):
        ins, outs = refs[:nbuf], refs[nbuf:2 * nbuf]
        send_sems, recv_sems = refs[2 * nbuf:]
        x, y, c = _place()
        sib = (x, y, 1 - c)
        cps = []
        for b in range(nbuf):
            hh = bufs[b].shape[1] // 2
            cp = _remote(ins[b].at[:, pl.ds((1 - c) * hh, hh), :], outs[b], send_sems.at[b], recv_sems.at[b], sib)
            cp.start()
            cps.append(cp)
        for cp in cps:
            cp.wait()

    return pl.pallas_call(
        body, name="grad_swap_halves",
        in_specs=[ANY] * nbuf, out_specs=[ANY] * nbuf,
        out_shape=[jax.ShapeDtypeStruct((N_CHIPS, b.shape[1] // 2, b.shape[2]), b.dtype) for b in bufs],
        scratch_shapes=[pltpu.SemaphoreType.DMA((nbuf,)), pltpu.SemaphoreType.DMA((nbuf,))],
        compiler_params=pltpu.CompilerParams(has_side_effects=True),
    )(*bufs)


def _chip_sum(g, got, sel, out_dtype, name):
    _, R, C = g.shape
    hh = R // 2
    TR = _tile(hh, 512)
    nslot = sel[1].shape[0]

    def body(off_ref, sh_ref, g_ref, r_ref, o_ref):
        o_ref[...] = (g_ref[...] + r_ref[...]).astype(out_dtype)

    return pl.pallas_call(
        body, name=name,
        grid_spec=pltpu.PrefetchScalarGridSpec(
            num_scalar_prefetch=2, grid=(nslot, hh // TR),
            in_specs=[pl.BlockSpec((None, TR, C), lambda s, i, off, sh: (sh[s], off[0] + i, 0)),
                      pl.BlockSpec((None, TR, C), lambda s, i, off, sh: (sh[s], i, 0))],
            out_specs=pl.BlockSpec((None, TR, C), lambda s, i, off, sh: (s, i, 0))),
        out_shape=jax.ShapeDtypeStruct((nslot, hh, C), out_dtype),
        compiler_params=_params(("parallel", "parallel")),
    )(sel[0], sel[1], g, got)


def _exchange_chips(bufs):
    nbuf = len(bufs)

    def body(*refs):
        ins, outs = refs[:nbuf], refs[nbuf:2 * nbuf]
        send_sems, recv_sems = refs[2 * nbuf:]
        x, y, c = _place()
        cps = []
        for b in range(nbuf):
            for j, (cx, cy) in enumerate(_other_chips(x, y)):
                k = 3 * b + j
                cp = _remote(ins[b].at[j], outs[b].at[j], send_sems.at[k], recv_sems.at[k], (cx, cy, c))
                cp.start()
                cps.append(cp)
        for cp in cps:
            cp.wait()

    return pl.pallas_call(
        body, name="grad_exchange_chips",
        in_specs=[ANY] * nbuf, out_specs=[ANY] * nbuf,
        out_shape=[jax.ShapeDtypeStruct(b.shape, b.dtype) for b in bufs],
        scratch_shapes=[pltpu.SemaphoreType.DMA((3 * nbuf,)), pltpu.SemaphoreType.DMA((3 * nbuf,))],
        compiler_params=pltpu.CompilerParams(has_side_effects=True),
    )(*bufs)


def _shard_sum(own, got, off, tr, full, rows, name):
    _, hh, C = own.shape

    def body(off_ref, o_ref, r_ref, *rest):
        acc = o_ref[...]
        for j in range(N_CHIPS - 1):
            acc = acc + r_ref[j].astype(F32)
        rest[-1][...] = acc

    in_specs = [pl.BlockSpec((None, tr, C), lambda i, off: (0, i, 0)),
                pl.BlockSpec((N_CHIPS - 1, tr, C), lambda i, off: (0, i, 0))]
    args = [off, own, got]
    aliases = {}
    if full is not None:
        in_specs.append(ANY)
        args.append(full)
        aliases = {3: 0}
    return pl.pallas_call(
        body, name=name,
        grid_spec=pltpu.PrefetchScalarGridSpec(
            num_scalar_prefetch=1, grid=(hh // tr,), in_specs=in_specs,
            out_specs=pl.BlockSpec((tr, C), lambda i, off: (off[0] + i, 0))),
        out_shape=jax.ShapeDtypeStruct((rows, C), F32),
        input_output_aliases=aliases,
        compiler_params=_params(("parallel",)),
    )(*args)


def _join_halves(bufs, spans):
    nbuf = len(bufs)
    ncopy = nbuf * len(spans)

    def body(*refs):
        outs = refs[nbuf:2 * nbuf]
        send_sems, recv_sems = refs[2 * nbuf:]
        x, y, c = _place()
        sib = (x, y, 1 - c)
        cps = []
        for b in range(nbuf):
            for s, (r0, nr) in enumerate(spans[b]):
                k = b * len(spans[b]) + s
                rows = outs[b].at[pl.ds(r0 + c * (nr // 2), nr // 2), :]
                cp = _remote(rows, rows, send_sems.at[k], recv_sems.at[k], sib)
                cp.start()
                cps.append(cp)
        for b in range(nbuf):
            for s, (r0, nr) in enumerate(spans[b]):
                k = b * len(spans[b]) + s
                theirs = outs[b].at[pl.ds(r0 + (1 - c) * (nr // 2), nr // 2), :]
                _remote(theirs, theirs, send_sems.at[k], recv_sems.at[k], sib).wait_recv()
        for cp in cps:
            cp.wait_send()

    return pl.pallas_call(
        body, name="grad_join_halves",
        in_specs=[ANY] * nbuf, out_specs=[ANY] * nbuf,
        out_shape=[jax.ShapeDtypeStruct(b.shape, b.dtype) for b in bufs],
        input_output_aliases={b: b for b in range(nbuf)},
        scratch_shapes=[pltpu.SemaphoreType.DMA((ncopy,)), pltpu.SemaphoreType.DMA((ncopy,))],
        compiler_params=pltpu.CompilerParams(has_side_effects=True),
    )(*bufs)


def _sum_devices(part):
    R = part.shape[0]

    def body(p_ref, o_ref, all_ref, send_sems, recv_sems):
        x, y, c = _place()
        me = 4 * x + 2 * y + c
        all_ref[me] = p_ref[...]
        cps = []
        for k in range(1, N_DEV):
            px, py, pc = x ^ (k >> 2), y ^ ((k >> 1) & 1), c ^ (k & 1)
            cp = _remote(p_ref, all_ref.at[me], send_sems.at[k - 1], recv_sems.at[k - 1], (px, py, pc))
            cp.start()
            cps.append(cp)
        for k in range(1, N_DEV):
            peer = me ^ k
            _remote(p_ref, all_ref.at[peer], send_sems.at[k - 1], recv_sems.at[k - 1], (x, y, c)).wait_recv()
        for cp in cps:
            cp.wait_send()
        acc = all_ref[0]
        for d in range(1, N_DEV):
            acc = acc + all_ref[d]
        o_ref[...] = acc

    return pl.pallas_call(
        body, name="sum_small_grads",
        in_specs=[pl.BlockSpec(memory_space=pltpu.VMEM)],
        out_specs=pl.BlockSpec(memory_space=pltpu.VMEM),
        out_shape=jax.ShapeDtypeStruct((R, LANES), F32),
        scratch_shapes=[pltpu.VMEM((N_DEV, R, LANES), F32),
                        pltpu.SemaphoreType.DMA((N_DEV - 1,)), pltpu.SemaphoreType.DMA((N_DEV - 1,))],
        compiler_params=pltpu.CompilerParams(has_side_effects=True, vmem_limit_bytes=VMEM_LIMIT),
    )(part)


def _pack(parts):
    flat = jnp.concatenate([p.reshape(-1).astype(F32) for p in parts])
    n = flat.shape[0]
    rows = -(-n // LANES)
    rows = -(-rows // 8) * 8
    return jnp.pad(flat, (0, rows * LANES - n)).reshape(rows, LANES)


def _unpack(packed, shapes):
    flat = packed.reshape(-1)
    out, off = [], 0
    for s in shapes:
        n = int(np.prod(s))
        out.append(flat[off:off + n].reshape(s))
        off += n
    return out


def kernel(x, rel_bias, norm_mix_g, w_in, q_norm_g, k_norm_g, sinks, conv_w, conv_b, conv_ln_g, conv_ln_b, attn_out_g, conv_out_g, w_out, norm_mlp_g, w_mlp_up, w_mlp_down, loss_target, m_rel_bias, m_norm_mix_g, m_w_in, m_q_norm_g, m_k_norm_g, m_sinks, m_conv_w, m_conv_b, m_conv_ln_g, m_conv_ln_b, m_attn_out_g, m_conv_out_g, m_w_out, m_norm_mlp_g, m_w_mlp_up, m_w_mlp_down, v_rel_bias, v_norm_mix_g, v_w_in, v_q_norm_g, v_k_norm_g, v_sinks, v_conv_w, v_conv_b, v_conv_ln_g, v_conv_ln_b, v_attn_out_g, v_conv_out_g, v_w_out, v_norm_mlp_g, v_w_mlp_up, v_w_mlp_down):
    T = x.shape[1]
    L = DEPTH
    xi, yi, ci = _place()
    shard = 2 * xi + yi
    in_sh = IN_WIDTH // N_CHIPS
    out_sh = MIX_WIDTH // N_CHIPS
    ff_sh = D_FF // N_CHIPS
    cv_sh = CONV_WIDTH // N_CHIPS

    MIXING, MLP = ("w_in", "w_out", "conv_w"), ("w_mlp_up", "w_mlp_down")

    def my_shard(name, lo, hi):
        n = hi - lo
        if name == "w_in":
            return w_in[lo:hi].astype(BF16).reshape(n * D_MODEL, in_sh)
        if name == "w_out":
            return w_out[lo:hi].astype(BF16).reshape(n * out_sh, D_MODEL)
        if name == "w_mlp_up":
            return w_mlp_up[lo:hi].astype(BF16).reshape(n * D_MODEL, ff_sh)
        if name == "w_mlp_down":
            return w_mlp_down[lo:hi].astype(BF16).reshape(n * ff_sh, D_MODEL)
        cw_pad = jnp.pad(conv_w[lo:hi], ((0, 0), (0, CONV_ROWS - CONV_KERNEL), (0, 0)))
        return cw_pad.reshape(n * CONV_ROWS, cv_sh)

    def whole_weight(name, gathered, own, n):
        g = lax.dynamic_update_slice(gathered, own[None], (shard, 0, 0))
        if name == "w_in":
            return g.reshape(N_CHIPS, n, D_MODEL, in_sh).transpose(1, 2, 0, 3).reshape(n, D_MODEL, IN_WIDTH)
        if name == "w_out":
            return g.reshape(N_CHIPS, n, out_sh, D_MODEL).transpose(1, 0, 2, 3).reshape(n, MIX_WIDTH, D_MODEL)
        if name == "w_mlp_up":
            return g.reshape(N_CHIPS, n, D_MODEL, ff_sh).transpose(1, 2, 0, 3).reshape(n, D_MODEL, D_FF)
        if name == "w_mlp_down":
            return g.reshape(N_CHIPS, n, ff_sh, D_MODEL).transpose(1, 0, 2, 3).reshape(n, D_FF, D_MODEL)
        return g.reshape(N_CHIPS, n, CONV_ROWS, cv_sh).transpose(1, 2, 0, 3).reshape(n, CONV_ROWS, CONV_WIDTH)

    weight_of = {}

    def provide(entries, gathered, mine):
        for (name, lo, hi), g, own in zip(entries, gathered, mine):
            whole = whole_weight(name, g, own, hi - lo)
            for l in range(lo, hi):
                weight_of[name, l] = (whole, l - lo)

    def gather_behind(tag, entries, first):
        mine = [my_shard(*e) for e in entries]
        mine[0], _ = lax.optimization_barrier((mine[0], first))
        plan = _gather_plan([m.shape for m in mine])
        send_sems, recv_sems, srcs, lands, token = _start_copies(
            "gather_" + tag + "_start", mine, [(N_CHIPS,) + m.shape for m in mine], plan)

        def finish(after):
            got = _wait_copies("gather_" + tag + "_wait", send_sems, recv_sems, srcs, lands, plan, after)
            provide(entries, _forward_halves(got), mine)
        return token, finish

    now = [(name, 0, 1) for name in MIXING]
    early = [(name, 0, 2) for name in MLP] + [(name, 1, 2) for name in MIXING]
    late = [(name, 2, L) for name in MIXING + MLP]
    mine0 = [my_shard(*e) for e in now]
    got0 = _gather_shards(mine0)
    provide(now, got0, mine0)
    token_early, finish_early = gather_behind("early", early, got0[0])

    bucket = _band_buckets()
    bk = jnp.asarray(bucket)[None]
    biasc = jnp.zeros((N_HEADS, BLOCK, BLOCK), F32)
    for b in range(NUM_BUCKETS):
        biasc = jnp.where(bk == b, rel_bias[b][:, None, None], biasc)
    biasc = biasc.reshape(N_KV_HEADS, GROUP_ROWS, BLOCK)
    onehot_t = np.zeros((LANES, BLOCK * BLOCK), np.float32)
    onehot_t[bucket.reshape(-1), np.arange(BLOCK * BLOCK)] = 1.0
    onehot_t = jnp.asarray(onehot_t, dtype=BF16)
    sink_rows = lambda l: jnp.repeat(sinks[l], BLOCK).reshape(N_KV_HEADS, GROUP_ROWS, 1)

    row = lambda a, l: a[l][None, :]

    xs = x.reshape(T, D_MODEL)
    saved = []
    token_late = None
    for l in range(L):
        if l == 2:
            finish_late(xs)
        h, z = _norm_matmul(xs, row(norm_mix_g, l), *weight_of["w_in", l], F32, "mix_in_proj", token_early if l == 0 else None)
        a = _attn_fwd(z, biasc, sink_rows(l), row(q_norm_g, l), row(k_norm_g, l))
        cw, cl = weight_of["conv_w", l]
        yc = _conv_fwd(z, cw[cl], row(conv_b, l))
        mix = _mix_norm(a, yc, row(conv_ln_g, l), row(conv_ln_b, l), row(attn_out_g, l), row(conv_out_g, l))
        x1 = _matmul_res(mix, *weight_of["w_out", l], xs, False, "mix_out_proj")
        if l == 0:
            finish_early(x1)
            token_late, finish_late = gather_behind("late", late, weight_of["w_mlp_up", 0][0])
        h2, up = _norm_matmul(x1, row(norm_mlp_g, l), *weight_of["w_mlp_up", l], BF16, "mlp_up_proj",
                              token_late if l == 0 else None)
        x2 = _matmul_res(up, *weight_of["w_mlp_down", l], x1, True, "mlp_down_proj")
        saved.append((xs, h, z, a, yc, mix, x1, h2, up))
        xs = x2

    loss_parts, g = _loss_grad(xs, loss_target.reshape(T, D_MODEL))

    names = ["w_in", "w_out", "w_mlp_up", "w_mlp_down"]
    shard_rows = {"w_in": D_MODEL, "w_out": out_sh, "w_mlp_up": D_MODEL, "w_mlp_down": ff_sh}
    own_sel = shard.astype(jnp.int32)[None]
    send_sel = jnp.stack([shard ^ 2, shard ^ 1, shard ^ 3]).astype(jnp.int32)

    def by_shard(name, buf, n):
        if name == "w_in":
            return buf.reshape(n, D_MODEL, N_CHIPS, in_sh).transpose(2, 0, 1, 3).reshape(N_CHIPS, n * D_MODEL, in_sh)
        return buf.reshape(N_CHIPS, n * shard_rows[name], buf.shape[-1])

    def chip_sums(tag, group, n):
        order = list(group)
        G = [by_shard(name, group[name], n) for name in order]
        got = _swap_halves(G)
        owns, sends = {}, {}
        for name, g_all, g_got in zip(order, G, got):
            hh = g_all.shape[1] // 2
            off = (ci * (hh // _tile(hh, 512))).astype(jnp.int32)[None]
            owns[name] = _chip_sum(g_all, g_got, (off, own_sel), F32, "chip_sum_own_" + name + tag)
            sends[name] = _chip_sum(g_all, g_got, (off, send_sel), BF16, "chip_sum_send_" + name + tag)
        return owns, sends

    def exchange_behind(tag, group, n):
        owns, sends = chip_sums(tag, group, n)
        order = list(sends)
        bufs = [sends[name] for name in order]
        plan = _exchange_plan([b.shape for b in bufs])
        send_sems, recv_sems, srcs, lands, token = _start_copies(
            "grad_exchange" + tag + "_start", bufs, [b.shape for b in bufs], plan)

        def finish(after):
            got = _wait_copies("grad_exchange" + tag + "_wait", send_sems, recv_sems, srcs, lands, plan, after)
            return {name: (owns[name], arrived) for name, arrived in zip(order, got)}
        return token, finish

    rest = dict.fromkeys(names)
    first = dict.fromkeys(names)
    small = [None] * L
    dbias_sum = None
    token = None
    for l in reversed(range(L)):
        x0, h, z, a, yc, mix, x1, h2, up = saved[l]
        stack, n, sl = (first, 1, 0) if l == 0 else (rest, L - 1, l - 1)
        if l == 0:
            token, finish_rest_grads = exchange_behind("_rest", rest, L - 1)
        d_up = _dact(g, *weight_of["w_mlp_down", l], up, token)
        stack["w_mlp_down"] = _matmul_tn(up, g, True, stack["w_mlp_down"], (N_CHIPS, n, ff_sh, D_MODEL),
                                         (None, None, ff_sh, D_MODEL), lambda i, j: (i, sl, 0, 0), ff_sh, D_MODEL,
                                         "grad_w_mlp_down")
        stack["w_mlp_up"] = _matmul_tn(h2, d_up, False, stack["w_mlp_up"], (N_CHIPS, n, D_MODEL, ff_sh),
                                       (None, None, D_MODEL, ff_sh), lambda i, j: (j, sl, 0, 0), D_MODEL, ff_sh,
                                       "grad_w_mlp_up")
        if l == 0:
            token, finish_mlp0_grads = exchange_behind("_mlp0", {k: first[k] for k in ("w_mlp_up", "w_mlp_down")}, 1)
        g1, d_gmlp = _matmul_nt_normbwd(d_up, *weight_of["w_mlp_up", l], x1, row(norm_mlp_g, l), g, "mlp_in_bwd",
                                        token if l == 0 else None)
        d_a, d_y, sm_mix = _mix_bwd(g1, *weight_of["w_out", l], a, yc, row(conv_ln_g, l), row(conv_ln_b, l),
                                    row(attn_out_g, l), row(conv_out_g, l))
        stack["w_out"] = _matmul_tn(mix, g1, False, stack["w_out"], (N_CHIPS, n, out_sh, D_MODEL),
                                    (N_CHIPS, None, out_sh, D_MODEL), lambda i, j: (0, sl, 0, 0), MIX_WIDTH, D_MODEL,
                                    "grad_w_out")
        cw, cl = weight_of["conv_w", l]
        d_u, d_gate, d_cw = _conv_bwd(d_y, z, cw[cl])
        d_q, d_kv, dbias, sm_attn = _attn_bwd(z, d_a, biasc, sink_rows(l), row(q_norm_g, l), row(k_norm_g, l))
        dbias_sum = dbias if dbias_sum is None else dbias_sum + dbias
        d_z = [d_q, d_kv[BLOCK:BLOCK + T], d_u, d_gate]
        stack["w_in"] = _matmul_tn(h, d_z, False, stack["w_in"], (n, D_MODEL, IN_WIDTH), (None, D_MODEL, IN_WIDTH),
                                   lambda i, j: (sl, 0, 0), D_MODEL, IN_WIDTH, "grad_w_in")
        g, d_gmix = _matmul_nt_normbwd(d_z, *weight_of["w_in", l], x0, row(norm_mix_g, l), g1, "mix_in_bwd")
        small[l] = (d_gmix[0], sm_attn[0, :HEAD_DIM], sm_attn[1, :HEAD_DIM], sm_attn[2, :N_HEADS],
                    d_cw[:CONV_KERNEL], sm_mix[4], sm_mix[2], sm_mix[3], sm_mix[0], sm_mix[1], d_gmlp[0])
    grad_x = g.reshape(1, T, D_MODEL)

    d_rel = _bucket_reduce(dbias_sum.reshape(N_HEADS, BLOCK * BLOCK), onehot_t)[:, :NUM_BUCKETS].T
    stack = lambda k: jnp.stack([small[l][k] for l in range(L)])
    small_shapes = [(), (NUM_BUCKETS, N_HEADS), (L, D_MODEL), (L, HEAD_DIM), (L, HEAD_DIM), (L, N_HEADS),
                    (L, CONV_KERNEL, CONV_WIDTH), (L, CONV_WIDTH), (L, CONV_WIDTH), (L, CONV_WIDTH),
                    (L, CONV_WIDTH), (L, CONV_WIDTH), (L, D_MODEL)]
    part = _pack([jnp.sum(loss_parts[:, 0, 0]), d_rel] + [stack(k) for k in range(11)])
    tot = _unpack(_sum_devices(part), small_shapes)
    loss = tot[0]
    (g_rel, g_nmix, g_qn, g_kn, g_sk, g_cw_full, g_cb, g_lng, g_lnb, g_aog, g_cog, g_nmlp) = tot[1:]
    g_cw_sh = lax.dynamic_slice_in_dim(g_cw_full, shard * cv_sh, cv_sh, axis=2)

    small_w = [rel_bias, norm_mix_g, q_norm_g, k_norm_g, sinks, conv_w, conv_b, conv_ln_g, conv_ln_b,
               attn_out_g, conv_out_g, norm_mlp_g]
    small_m = [m_rel_bias, m_norm_mix_g, m_q_norm_g, m_k_norm_g, m_sinks, m_conv_w, m_conv_b, m_conv_ln_g,
               m_conv_ln_b, m_attn_out_g, m_conv_out_g, m_norm_mlp_g]
    small_v = [v_rel_bias, v_norm_mix_g, v_q_norm_g, v_k_norm_g, v_sinks, v_conv_w, v_conv_b, v_conv_ln_g,
               v_conv_ln_b, v_attn_out_g, v_conv_out_g, v_norm_mlp_g]
    small_g = [g_rel, g_nmix, g_qn, g_kn, g_sk, g_cw_sh, g_cb, g_lng, g_lnb, g_aog, g_cog, g_nmlp]
    shapes = [w.shape for w in small_w]
    sd, sm_, sv_ = _adamw(_pack(small_w), _pack(small_g), _pack(small_m), _pack(small_v), "adamw_small")
    small_d, small_nm, small_nv = _unpack(sd, shapes), _unpack(sm_, shapes), _unpack(sv_, shapes)

    owns_mix0, sends_mix0 = chip_sums("_mix0", {k: first[k] for k in ("w_in", "w_out")}, 1)
    arrived_mix0 = _exchange_chips([sends_mix0[k] for k in ("w_in", "w_out")])
    parts0 = {"w_in": (owns_mix0["w_in"], arrived_mix0[0]), "w_out": (owns_mix0["w_out"], arrived_mix0[1]),
              **finish_mlp0_grads(g)}
    parts1 = finish_rest_grads(g)
    grads, spans = [], []
    for name in names:
        R = shard_rows[name]
        full = None
        spans.append([(0, R), (R, (L - 1) * R)])
        for (r0, nr), (own, arrived), tag in zip(spans[-1], (parts0[name], parts1[name]), ("_first", "_rest")):
            tr = min(512, math.gcd(R, nr // 2))
            off = ((r0 + ci * (nr // 2)) // tr).astype(jnp.int32)[None]
            full = _shard_sum(own, arrived, off, tr, full, L * R, "shard_sum_" + name + tag)
        grads.append(full)
    grads = _join_halves(grads, spans)

    big_w = [w_in, w_out, w_mlp_up, w_mlp_down]
    big_m = [m_w_in, m_w_out, m_w_mlp_up, m_w_mlp_down]
    big_v = [v_w_in, v_w_out, v_w_mlp_up, v_w_mlp_down]
    big_g, big_d, big_nm, big_nv = [], [], [], []
    for b in range(4):
        shp = big_w[b].shape
        flat = lambda t: t.reshape(shp[0] * shp[1], shp[2])
        d, nm, nv = _adamw(flat(big_w[b]), grads[b], flat(big_m[b]), flat(big_v[b]), "adamw_" + names[b])
        big_g.append(grads[b].reshape(shp))
        big_d.append(d.reshape(shp))
        big_nm.append(nm.reshape(shp))
        big_nv.append(nv.reshape(shp))

    def ordered(sm, bg):
        return [sm[0], sm[1], bg[0], sm[2], sm[3], sm[4], sm[5], sm[6], sm[7], sm[8], sm[9], sm[10], bg[1], sm[11],
                bg[2], bg[3]]

    return (loss, grad_x, *ordered(small_g, big_g), *ordered(small_d, big_d), *ordered(small_nm, big_nm),
            *ordered(small_nv, big_nv))
```

```python
import math

import numpy as np
import jax
import jax.numpy as jnp
from jax import lax
from jax.experimental import pallas as pl
from jax.experimental.pallas import tpu as pltpu

F32 = jnp.float32
BF16 = jnp.bfloat16

D_MODEL = 1024
DEPTH = 4
HEAD_DIM = 64
N_HEADS = 8
N_KV_HEADS = 2
GQA_GROUP = N_HEADS // N_KV_HEADS
ATTN_WIDTH = N_HEADS * HEAD_DIM
KV_WIDTH = N_KV_HEADS * HEAD_DIM
CONV_WIDTH = D_MODEL - ATTN_WIDTH
MIX_WIDTH = ATTN_WIDTH + CONV_WIDTH
IN_WIDTH = ATTN_WIDTH + 2 * KV_WIDTH + 2 * CONV_WIDTH
BLOCK = 128
CONV_KERNEL = 31
CONV_ROWS = 32
HALO = 32
CONV_CH = 256
NUM_BUCKETS = 32
MAX_DISTANCE = 128
D_FF = 4 * D_MODEL
EPS = 1e-6
NEG = -1e30
SCALE = 1.0 / math.sqrt(HEAD_DIM)

ADAM_LR = 0.001
ADAM_B1 = 0.9
ADAM_B2 = 0.999
ADAM_EPS = 1e-08
ADAM_WD = 0.01
ADAM_STEP = 10

N_CHIPS = 4
N_DEV = 8
LANES = 128
VMEM_LIMIT = 52 * 1024 * 1024
K_CHUNK = 4096

Q0, K0, V0, U0, G0 = 0, ATTN_WIDTH, ATTN_WIDTH + KV_WIDTH, ATTN_WIDTH + 2 * KV_WIDTH, ATTN_WIDTH + 2 * KV_WIDTH + CONV_WIDTH

NT = (((1,), (1,)), ((), ()))
TN = (((0,), (0,)), ((), ()))
MESH = pl.DeviceIdType.MESH
ANY = pl.BlockSpec(memory_space=pl.ANY)


def _params(sem=None):
    return pltpu.CompilerParams(dimension_semantics=sem, vmem_limit_bytes=VMEM_LIMIT)


def _chunk(n, cap=1024):
    for c in range(cap, 0, -LANES):
        if n % c == 0:
            return c
    raise ValueError(n)


def _tile(t, want):
    return min(t, want)


def _after_spec(after):
    return [] if after is None else [pl.BlockSpec((8, LANES), lambda *_: (0, 0))]


def _after_arg(after):
    return [] if after is None else [after]


def _t5_bucket(n):
    n = np.asarray(n)
    max_exact = NUM_BUCKETS // 2
    large = max_exact + (np.log(np.maximum(n, 1) / max_exact) / np.log(MAX_DISTANCE / max_exact)
                         * (NUM_BUCKETS - max_exact)).astype(np.int32)
    large = np.minimum(large, NUM_BUCKETS - 1)
    return np.where(n < max_exact, n, large).astype(np.int32)


def _band_buckets():
    qi = np.arange(BLOCK)[:, None]
    j = np.arange(BLOCK)[None, :]
    return _t5_bucket(np.where(j <= qi, qi - j, qi + BLOCK - j))


def _norm_matmul(x, g, w_all, l, out_dtype, name, after=None):
    T, D = x.shape
    N = w_all.shape[2]
    TM = _tile(T, 512)
    CH = _chunk(N)

    def body(x_ref, g_ref, w_ref, *rest):
        h_ref, z_ref = rest[-2:]
        xv = x_ref[...]
        r = lax.rsqrt(jnp.mean(xv * xv, axis=-1, keepdims=True) + EPS)
        h = (xv * r * g_ref[...]).astype(BF16)
        h_ref[...] = h
        for c0 in range(0, N, CH):
            z_ref[:, c0:c0 + CH] = jnp.dot(h, w_ref[:, c0:c0 + CH], preferred_element_type=F32).astype(z_ref.dtype)

    return pl.pallas_call(
        body, name=name, grid=(T // TM,),
        in_specs=[pl.BlockSpec((TM, D), lambda i: (i, 0)),
                  pl.BlockSpec((1, D), lambda i: (0, 0)),
                  pl.BlockSpec((None, D, N), lambda i: (l, 0, 0))] + _after_spec(after),
        out_specs=[pl.BlockSpec((TM, D), lambda i: (i, 0)),
                   pl.BlockSpec((TM, N), lambda i: (i, 0))],
        out_shape=[jax.ShapeDtypeStruct((T, D), BF16), jax.ShapeDtypeStruct((T, N), out_dtype)],
        compiler_params=_params(("parallel",)),
    )(x, g, w_all, *_after_arg(after))


def _matmul_res(a, w_all, l, res, relu2, name):
    T, K = a.shape
    N = w_all.shape[2]
    TM = _tile(T, 512)
    CH = _chunk(K, K_CHUNK)

    def body(a_ref, w_ref, res_ref, o_ref):
        acc = res_ref[...]
        for k0 in range(0, K, CH):
            av = a_ref[:, k0:k0 + CH]
            if relu2:
                av = jnp.square(jnp.maximum(av.astype(F32), 0.0)).astype(BF16)
            acc = acc + jnp.dot(av, w_ref[k0:k0 + CH, :], preferred_element_type=F32)
        o_ref[...] = acc

    return pl.pallas_call(
        body, name=name, grid=(T // TM,),
        in_specs=[pl.BlockSpec((TM, K), lambda i: (i, 0)),
                  pl.BlockSpec((None, K, N), lambda i: (l, 0, 0)),
                  pl.BlockSpec((TM, N), lambda i: (i, 0))],
        out_specs=pl.BlockSpec((TM, N), lambda i: (i, 0)),
        out_shape=jax.ShapeDtypeStruct((T, N), F32),
        compiler_params=_params(("parallel",)),
    )(a, w_all, res)


def _head_norm(t, g):
    r = lax.rsqrt(jnp.mean(t * t, axis=-1, keepdims=True) + EPS)
    that = t * r
    return that * g, that, r


def _softmax_sink(s, sink):
    m = jnp.maximum(jnp.max(s, axis=-1, keepdims=True), sink)
    p = jnp.exp(s - m)
    es = jnp.exp(sink - m)
    den = jnp.sum(p, axis=-1, keepdims=True) + es
    return p / den, es / den


GROUP_ROWS = GQA_GROUP * BLOCK


def _own_block():
    row = lax.broadcasted_iota(jnp.int32, (GROUP_ROWS, BLOCK), 0)
    col = lax.broadcasted_iota(jnp.int32, (GROUP_ROWS, BLOCK), 1)
    return (row & (BLOCK - 1)) >= col


ATTN_QB = 4
KV_COLS = [slice(k * HEAD_DIM, (k + 1) * HEAD_DIM) for k in range(N_KV_HEADS)]


def _blk(i):
    return pl.ds(i * BLOCK, BLOCK)


def _stack_heads(ref, i, kvh):
    return jnp.concatenate([ref[_blk(i), (kvh * GQA_GROUP + g) * HEAD_DIM:(kvh * GQA_GROUP + g + 1) * HEAD_DIM]
                            for g in range(GQA_GROUP)], axis=0)


def _unstack_heads(ref, i, kvh, val):
    for g in range(GQA_GROUP):
        h = kvh * GQA_GROUP + g
        ref[_blk(i), h * HEAD_DIM:(h + 1) * HEAD_DIM] = val[g * BLOCK:(g + 1) * BLOCK]


def _band_probs(first, own, s_own, s_prev, bias, sink):
    s = jnp.where(own, s_own, s_prev) * SCALE + bias
    if first is not None:
        s = jnp.where(jnp.logical_or(own, jnp.logical_not(first)), s, NEG)
    return _softmax_sink(s, sink)


def _dot_nt(a, b):
    return lax.dot_general(a, b, NT, preferred_element_type=F32)


def _dot_tn(a, b):
    return lax.dot_general(a, b, TN, preferred_element_type=F32)


def _attn_fwd(z, biasc, sink_rows, qg, kg):
    T = z.shape[0]
    nb = T // BLOCK
    qb = min(ATTN_QB, nb)
    TQ = qb * BLOCK
    kb, vb = K0 // KV_WIDTH, V0 // KV_WIDTH
    groups = [(i, k) for i in range(qb) for k in range(N_KV_HEADS)]

    def body(q_ref, kc_ref, kp_ref, vc_ref, vp_ref, b_ref, sk_ref, qg_ref, kg_ref, a_ref):
        n = pl.program_id(0)
        own = _own_block()
        kn, vv = {}, {}
        for k in range(N_KV_HEADS):
            kn[-1, k] = _head_norm(kp_ref[:, KV_COLS[k]], kg_ref[...])[0].astype(BF16)
            vv[-1, k] = vp_ref[:, KV_COLS[k]].astype(BF16)
        for i, k in groups:
            kn[i, k] = _head_norm(kc_ref[_blk(i), KV_COLS[k]], kg_ref[...])[0].astype(BF16)
            vv[i, k] = vc_ref[_blk(i), KV_COLS[k]].astype(BF16)
        qnb = {g: _head_norm(_stack_heads(q_ref, *g), qg_ref[...])[0].astype(BF16) for g in groups}
        s_own = {(i, k): _dot_nt(qnb[i, k], kn[i, k]) for i, k in groups}
        s_prev = {(i, k): _dot_nt(qnb[i, k], kn[i - 1, k]) for i, k in groups}
        p = {(i, k): _band_probs(n == 0 if i == 0 else None, own, s_own[i, k], s_prev[i, k], b_ref[k], sk_ref[k])[0]
             for i, k in groups}
        p_own = {g: jnp.where(own, p[g], 0.0).astype(BF16) for g in groups}
        p_prev = {g: jnp.where(own, 0.0, p[g]).astype(BF16) for g in groups}
        o_own = {(i, k): jnp.dot(p_own[i, k], vv[i, k], preferred_element_type=F32) for i, k in groups}
        o_prev = {(i, k): jnp.dot(p_prev[i, k], vv[i - 1, k], preferred_element_type=F32) for i, k in groups}
        for i, k in groups:
            _unstack_heads(a_ref, i, k, o_own[i, k] + o_prev[i, k])

    prev = lambda n: jnp.maximum(n * qb - 1, 0)
    return pl.pallas_call(
        body, name="attn_fwd", grid=(nb // qb,),
        in_specs=[pl.BlockSpec((TQ, ATTN_WIDTH), lambda n: (n, 0)),
                  pl.BlockSpec((TQ, KV_WIDTH), lambda n: (n, kb)),
                  pl.BlockSpec((BLOCK, KV_WIDTH), lambda n: (prev(n), kb)),
                  pl.BlockSpec((TQ, KV_WIDTH), lambda n: (n, vb)),
                  pl.BlockSpec((BLOCK, KV_WIDTH), lambda n: (prev(n), vb)),
                  pl.BlockSpec((N_KV_HEADS, GROUP_ROWS, BLOCK), lambda n: (0, 0, 0)),
                  pl.BlockSpec((N_KV_HEADS, GROUP_ROWS, 1), lambda n: (0, 0, 0)),
                  pl.BlockSpec((1, HEAD_DIM), lambda n: (0, 0)),
                  pl.BlockSpec((1, HEAD_DIM), lambda n: (0, 0))],
        out_specs=pl.BlockSpec((TQ, ATTN_WIDTH), lambda n: (n, 0)),
        out_shape=jax.ShapeDtypeStruct((T, ATTN_WIDTH), F32),
        compiler_params=_params(("parallel",)),
    )(z, z, z, z, z, biasc, sink_rows, qg, kg)


SHIFTS = 8
CONV_RC = 64


def _shifted_copies(src_ref, dst_ref, total):
    for b in range(SHIFTS):
        rows = (total - b) // SHIFTS * SHIFTS
        for r0 in range(0, rows, CONV_RC):
            nr = min(CONV_RC, rows - r0)
            dst_ref[b, pl.ds(r0, nr), :] = src_ref[pl.ds(r0 + b, nr), :]


def _tap(ref, r0, o):
    return ref[o % SHIFTS, pl.ds(r0 + (o // SHIFTS) * SHIFTS, CONV_RC), :]


def _conv_fwd(z, cw, cb):
    T = z.shape[0]
    TC = _tile(T, 512)
    ub, gb = U0 // CONV_CH, G0 // CONV_CH
    hpt = TC // HALO
    lead = HALO - (CONV_KERNEL - 1)

    def body(u_ref, g_ref, up_ref, gp_ref, w_ref, b_ref, y_ref, hp_ref, hs_ref):
        i = pl.program_id(0)
        hp_ref[pl.ds(0, HALO), :] = jnp.where(i > 0, up_ref[...] * jax.nn.sigmoid(gp_ref[...]), 0.0)
        hp_ref[pl.ds(HALO, TC), :] = u_ref[...] * jax.nn.sigmoid(g_ref[...])
        _shifted_copies(hp_ref, hs_ref, TC + HALO)
        for r0 in range(0, TC, CONV_RC):
            acc = jnp.zeros((CONV_RC, CONV_CH), F32) + b_ref[...]
            for j in range(CONV_KERNEL):
                acc = acc + _tap(hs_ref, r0, lead + j) * w_ref[pl.ds(j, 1), :]
            y_ref[pl.ds(r0, CONV_RC), :] = acc

    prev = lambda i: jnp.maximum(i * hpt - 1, 0)
    return pl.pallas_call(
        body, name="conv_fwd", grid=(T // TC, CONV_WIDTH // CONV_CH),
        in_specs=[pl.BlockSpec((TC, CONV_CH), lambda i, j: (i, ub + j)),
                  pl.BlockSpec((TC, CONV_CH), lambda i, j: (i, gb + j)),
                  pl.BlockSpec((HALO, CONV_CH), lambda i, j: (prev(i), ub + j)),
                  pl.BlockSpec((HALO, CONV_CH), lambda i, j: (prev(i), gb + j)),
                  pl.BlockSpec((CONV_ROWS, CONV_CH), lambda i, j: (0, j)),
                  pl.BlockSpec((1, CONV_CH), lambda i, j: (0, j))],
        out_specs=pl.BlockSpec((TC, CONV_CH), lambda i, j: (i, j)),
        out_shape=jax.ShapeDtypeStruct((T, CONV_WIDTH), F32),
        scratch_shapes=[pltpu.VMEM((TC + HALO, CONV_CH), F32), pltpu.VMEM((SHIFTS, TC + HALO, CONV_CH), F32)],
        compiler_params=_params(("parallel", "parallel")),
    )(z, z, z, z, cw, cb)


def _ln_silu(y, ln_g, ln_b):
    mu = jnp.mean(y, axis=-1, keepdims=True)
    yc = y - mu
    var = jnp.mean(yc * yc, axis=-1, keepdims=True)
    rstd = lax.rsqrt(var + EPS)
    yhat = yc * rstd
    yn = yhat * ln_g + ln_b
    sg = jax.nn.sigmoid(yn)
    return yn * sg, yn, sg, yhat, rstd


def _mix_norm(a, y, ln_g, ln_b, ag, cg):
    T = a.shape[0]
    TM = _tile(T, 512)

    def body(a_ref, y_ref, lg_ref, lb_ref, ag_ref, cg_ref, o_ref):
        av = a_ref[...]
        ra = lax.rsqrt(jnp.mean(av * av, axis=-1, keepdims=True) + EPS)
        o_ref[:, :ATTN_WIDTH] = (av * ra * ag_ref[...]).astype(BF16)
        c, _, _, _, _ = _ln_silu(y_ref[...], lg_ref[...], lb_ref[...])
        rc = lax.rsqrt(jnp.mean(c * c, axis=-1, keepdims=True) + EPS)
        o_ref[:, ATTN_WIDTH:] = (c * rc * cg_ref[...]).astype(BF16)

    vec = pl.BlockSpec((1, CONV_WIDTH), lambda i: (0, 0))
    return pl.pallas_call(
        body, name="mix_norm", grid=(T // TM,),
        in_specs=[pl.BlockSpec((TM, ATTN_WIDTH), lambda i: (i, 0)),
                  pl.BlockSpec((TM, CONV_WIDTH), lambda i: (i, 0)), vec, vec, vec, vec],
        out_specs=pl.BlockSpec((TM, MIX_WIDTH), lambda i: (i, 0)),
        out_shape=jax.ShapeDtypeStruct((T, MIX_WIDTH), BF16),
        compiler_params=_params(("parallel",)),
    )(a, y, ln_g, ln_b, ag, cg)


def _loss_grad(y, tgt):
    T, D = y.shape
    TM = _tile(T, 512)
    nt = T // TM

    def body(y_ref, t_ref, part_ref, dy_ref):
        diff = y_ref[...] - t_ref[...]
        dy_ref[...] = diff / D
        tok = jnp.mean(diff * diff, axis=-1, keepdims=True)
        part_ref[...] = jnp.zeros((1, LANES), F32) + 0.5 * jnp.sum(tok)

    return pl.pallas_call(
        body, name="loss_grad", grid=(nt,),
        in_specs=[pl.BlockSpec((TM, D), lambda i: (i, 0)), pl.BlockSpec((TM, D), lambda i: (i, 0))],
        out_specs=[pl.BlockSpec((None, 1, LANES), lambda i: (i, 0, 0)), pl.BlockSpec((TM, D), lambda i: (i, 0))],
        out_shape=[jax.ShapeDtypeStruct((nt, 1, LANES), F32), jax.ShapeDtypeStruct((T, D), F32)],
        compiler_params=_params(("parallel",)),
    )(y, tgt)


def _dact(g, w_all, l, up, after=None):
    T, N = g.shape
    K = w_all.shape[1]
    TM = _tile(T, 512)
    CH = _chunk(K)

    def body(g_ref, w_ref, up_ref, *rest):
        o_ref = rest[-1]
        gv = g_ref[...].astype(BF16)
        for k0 in range(0, K, CH):
            da = lax.dot_general(gv, w_ref[k0:k0 + CH, :], NT, preferred_element_type=F32)
            upv = up_ref[:, k0:k0 + CH].astype(F32)
            o_ref[:, k0:k0 + CH] = (da * (2.0 * jnp.maximum(upv, 0.0))).astype(BF16)

    return pl.pallas_call(
        body, name="mlp_dact", grid=(T // TM,),
        in_specs=[pl.BlockSpec((TM, N), lambda i: (i, 0)),
                  pl.BlockSpec((None, K, N), lambda i: (l, 0, 0)),
                  pl.BlockSpec((TM, K), lambda i: (i, 0))] + _after_spec(after),
        out_specs=pl.BlockSpec((TM, K), lambda i: (i, 0)),
        out_shape=jax.ShapeDtypeStruct((T, K), BF16),
        compiler_params=_params(("parallel",)),
    )(g, w_all, up, *_after_arg(after))


def _matmul_tn(a, b, relu2, buf, buf_shape, out_block, out_index, tm, tn, name):
    pieces = list(b) if isinstance(b, (list, tuple)) else [b]
    T, M = a.shape
    N = sum(p.shape[1] for p in pieces)
    assert len(pieces) == 1 or tn == N
    starts = np.cumsum([0] + [p.shape[1] for p in pieces])
    TK = _tile(T, 1024)
    nk = T // TK

    def body(*refs):
        a_ref, b_refs = refs[0], refs[1:1 + len(pieces)]
        o_ref = refs[-1]
        k = pl.program_id(2)
        av = a_ref[...]
        if relu2:
            av = jnp.square(jnp.maximum(av.astype(F32), 0.0)).astype(BF16)
        cs = [lax.dot_general(av, b_ref[...].astype(BF16), TN, preferred_element_type=F32) for b_ref in b_refs]

        def put(add):
            for p, c in enumerate(cs):
                if len(cs) == 1:
                    o_ref[...] = c.reshape(o_ref.shape) + (o_ref[...] if add else 0.0)
                else:
                    cols = slice(int(starts[p]), int(starts[p + 1]))
                    o_ref[:, cols] = c + (o_ref[:, cols] if add else 0.0)

        @pl.when(k == 0)
        def _():
            put(False)

        @pl.when(k > 0)
        def _():
            put(True)

    if len(pieces) == 1:
        b_specs = [pl.BlockSpec((TK, tn), lambda i, j, k: (k, j))]
    else:
        b_specs = [pl.BlockSpec((TK, p.shape[1]), lambda i, j, k: (k, 0)) for p in pieces]
    in_specs = [pl.BlockSpec((TK, tm), lambda i, j, k: (k, i))] + b_specs
    args = [a] + pieces
    aliases = {}
    if buf is not None:
        in_specs.append(ANY)
        args.append(buf)
        aliases = {len(args) - 1: 0}
    return pl.pallas_call(
        body, name=name, grid=(M // tm, N // tn, nk),
        in_specs=in_specs,
        out_specs=pl.BlockSpec(out_block, lambda i, j, k: out_index(i, j)),
        out_shape=jax.ShapeDtypeStruct(buf_shape, F32),
        input_output_aliases=aliases,
        compiler_params=_params(("parallel", "parallel", "arbitrary")),
    )(*args)


def _matmul_nt_normbwd(dz, w_all, l, x, gvec, gres, name, after=None):
    pieces = list(dz) if isinstance(dz, (list, tuple)) else [dz]
    T = pieces[0].shape[0]
    K = sum(p.shape[1] for p in pieces)
    starts = np.cumsum([0] + [p.shape[1] for p in pieces])
    D = x.shape[1]
    TM = _tile(T, 512)

    def body(*refs):
        dz_refs = refs[:len(pieces)]
        w_ref, x_ref, gv_ref, gr_ref = refs[len(pieces):len(pieces) + 4]
        o_ref, dg_ref = refs[-2:]
        i = pl.program_id(0)
        dh = jnp.zeros((TM, D), F32)
        for p, dz_ref in enumerate(dz_refs):
            width = dz_ref.shape[1]
            ch = _chunk(width, K_CHUNK)
            for k0 in range(0, width, ch):
                wk = int(starts[p]) + k0
                dh = dh + lax.dot_general(dz_ref[:, k0:k0 + ch], w_ref[:, wk:wk + ch], NT, preferred_element_type=F32)
        xv = x_ref[...]
        r = lax.rsqrt(jnp.mean(xv * xv, axis=-1, keepdims=True) + EPS)
        xhat = xv * r
        dg = jnp.sum(dh * xhat, axis=0, keepdims=True)

        @pl.when(i == 0)
        def _():
            dg_ref[...] = dg

        @pl.when(i > 0)
        def _():
            dg_ref[...] += dg

        wv = dh * gv_ref[...]
        o_ref[...] = gr_ref[...] + r * (wv - xhat * jnp.mean(wv * xhat, axis=-1, keepdims=True))

    return pl.pallas_call(
        body, name=name, grid=(T // TM,),
        in_specs=[pl.BlockSpec((TM, p.shape[1]), lambda i: (i, 0)) for p in pieces]
        + [pl.BlockSpec((None, D, K), lambda i: (l, 0, 0)),
           pl.BlockSpec((TM, D), lambda i: (i, 0)),
           pl.BlockSpec((1, D), lambda i: (0, 0)),
           pl.BlockSpec((TM, D), lambda i: (i, 0))] + _after_spec(after),
        out_specs=[pl.BlockSpec((TM, D), lambda i: (i, 0)), pl.BlockSpec((1, D), lambda i: (0, 0))],
        out_shape=[jax.ShapeDtypeStruct((T, D), F32), jax.ShapeDtypeStruct((1, D), F32)],
        compiler_params=_params(("arbitrary",)),
    )(*pieces, w_all, x, gvec, gres, *_after_arg(after))


def _mix_bwd(g1, w_all, l, a, y, ln_g, ln_b, ag, cg):
    T, D = g1.shape
    TM = _tile(T, 512)

    def body(g_ref, w_ref, a_ref, y_ref, lg_ref, lb_ref, ag_ref, cg_ref, da_ref, dy_ref, sm_ref):
        i = pl.program_id(0)
        dmix = lax.dot_general(g_ref[...].astype(BF16), w_ref[...], NT, preferred_element_type=F32)
        dma, dmc = dmix[:, :ATTN_WIDTH], dmix[:, ATTN_WIDTH:]
        av = a_ref[...]
        ra = lax.rsqrt(jnp.mean(av * av, axis=-1, keepdims=True) + EPS)
        ahat = av * ra
        d_ag = jnp.sum(dma * ahat, axis=0, keepdims=True)
        wa = dma * ag_ref[...]
        da_ref[...] = ra * (wa - ahat * jnp.mean(wa * ahat, axis=-1, keepdims=True))

        c, yn, sg, yhat, rstd = _ln_silu(y_ref[...], lg_ref[...], lb_ref[...])
        rc = lax.rsqrt(jnp.mean(c * c, axis=-1, keepdims=True) + EPS)
        chat = c * rc
        d_cg = jnp.sum(dmc * chat, axis=0, keepdims=True)
        wc = dmc * cg_ref[...]
        dc = rc * (wc - chat * jnp.mean(wc * chat, axis=-1, keepdims=True))
        dyn = dc * (sg * (1.0 + yn * (1.0 - sg)))
        d_lg = jnp.sum(dyn * yhat, axis=0, keepdims=True)
        d_lb = jnp.sum(dyn, axis=0, keepdims=True)
        dyh = dyn * lg_ref[...]
        dy = rstd * (dyh - jnp.mean(dyh, axis=-1, keepdims=True) - yhat * jnp.mean(dyh * yhat, axis=-1, keepdims=True))
        dy_ref[...] = dy
        d_cb = jnp.sum(dy, axis=0, keepdims=True)
        sums = jnp.concatenate([d_ag, d_cg, d_lg, d_lb, d_cb, jnp.zeros((3, CONV_WIDTH), F32)], axis=0)

        @pl.when(i == 0)
        def _():
            sm_ref[...] = sums

        @pl.when(i > 0)
        def _():
            sm_ref[...] += sums

    vec = pl.BlockSpec((1, CONV_WIDTH), lambda i: (0, 0))
    return pl.pallas_call(
        body, name="mix_bwd", grid=(T // TM,),
        in_specs=[pl.BlockSpec((TM, D), lambda i: (i, 0)),
                  pl.BlockSpec((None, MIX_WIDTH, D), lambda i: (l, 0, 0)),
                  pl.BlockSpec((TM, ATTN_WIDTH), lambda i: (i, 0)),
                  pl.BlockSpec((TM, CONV_WIDTH), lambda i: (i, 0)), vec, vec, vec, vec],
        out_specs=[pl.BlockSpec((TM, ATTN_WIDTH), lambda i: (i, 0)),
                   pl.BlockSpec((TM, CONV_WIDTH), lambda i: (i, 0)),
                   pl.BlockSpec((8, CONV_WIDTH), lambda i: (0, 0))],
        out_shape=[jax.ShapeDtypeStruct((T, ATTN_WIDTH), F32), jax.ShapeDtypeStruct((T, CONV_WIDTH), F32),
                   jax.ShapeDtypeStruct((8, CONV_WIDTH), F32)],
        compiler_params=_params(("arbitrary",)),
    )(g1, w_all, a, y, ln_g, ln_b, ag, cg)


def _conv_bwd(dy, z, cw):
    T = z.shape[0]
    TC = _tile(T, 512)
    nt = T // TC
    ub, gb = U0 // CONV_CH, G0 // CONV_CH
    nch = CONV_WIDTH // CONV_CH
    hpt = TC // HALO

    lead = HALO - (CONV_KERNEL - 1)

    def body(dy_ref, dyn_ref, u_ref, g_ref, up_ref, gp_ref, w_ref, du_ref, dg_ref, dw_ref,
             hp_ref, hs_ref, dyp_ref, dys_ref):
        i = pl.program_id(1)
        hp_ref[pl.ds(0, HALO), :] = jnp.where(i > 0, up_ref[...] * jax.nn.sigmoid(gp_ref[...]), 0.0)
        hp_ref[pl.ds(HALO, TC), :] = u_ref[...] * jax.nn.sigmoid(g_ref[...])
        _shifted_copies(hp_ref, hs_ref, TC + HALO)
        dyp_ref[pl.ds(0, TC), :] = dy_ref[...]
        dyp_ref[pl.ds(TC, HALO), :] = jnp.where(i < nt - 1, dyn_ref[...], 0.0)
        _shifted_copies(dyp_ref, dys_ref, TC + HALO)

        @pl.when(i == 0)
        def _():
            dw_ref[...] = jnp.zeros((CONV_ROWS, CONV_CH), F32)

        for r0 in range(0, TC, CONV_RC):
            rows = pl.ds(r0, CONV_RC)
            dh = jnp.zeros((CONV_RC, CONV_CH), F32)
            for j in range(CONV_KERNEL):
                dh = dh + _tap(dys_ref, r0, CONV_KERNEL - 1 - j) * w_ref[pl.ds(j, 1), :]
            uv = u_ref[rows, :]
            sg = jax.nn.sigmoid(g_ref[rows, :])
            du_ref[rows, :] = (dh * sg).astype(BF16)
            dg_ref[rows, :] = (dh * uv * sg * (1.0 - sg)).astype(BF16)
        for j in range(CONV_KERNEL):
            acc = jnp.zeros((SHIFTS, CONV_CH), F32)
            for r0 in range(0, TC, CONV_RC):
                prod = dy_ref[pl.ds(r0, CONV_RC), :] * _tap(hs_ref, r0, lead + j)
                acc = acc + jnp.sum(prod.reshape(CONV_RC // SHIFTS, SHIFTS, CONV_CH), axis=0)
            dw_ref[pl.ds(j, 1), :] += jnp.sum(acc, axis=0, keepdims=True)

    prev = lambda i: jnp.maximum(i * hpt - 1, 0)
    nxt = lambda i: jnp.minimum((i + 1) * hpt, T // HALO - 1)
    return pl.pallas_call(
        body, name="conv_bwd", grid=(nch, nt),
        in_specs=[pl.BlockSpec((TC, CONV_CH), lambda j, i: (i, j)),
                  pl.BlockSpec((HALO, CONV_CH), lambda j, i: (nxt(i), j)),
                  pl.BlockSpec((TC, CONV_CH), lambda j, i: (i, ub + j)),
                  pl.BlockSpec((TC, CONV_CH), lambda j, i: (i, gb + j)),
                  pl.BlockSpec((HALO, CONV_CH), lambda j, i: (prev(i), ub + j)),
                  pl.BlockSpec((HALO, CONV_CH), lambda j, i: (prev(i), gb + j)),
                  pl.BlockSpec((CONV_ROWS, CONV_CH), lambda j, i: (0, j))],
        out_specs=[pl.BlockSpec((TC, CONV_CH), lambda j, i: (i, j)),
                   pl.BlockSpec((TC, CONV_CH), lambda j, i: (i, j)),
                   pl.BlockSpec((CONV_ROWS, CONV_CH), lambda j, i: (0, j))],
        out_shape=[jax.ShapeDtypeStruct((T, CONV_WIDTH), BF16), jax.ShapeDtypeStruct((T, CONV_WIDTH), BF16),
                   jax.ShapeDtypeStruct((CONV_ROWS, CONV_WIDTH), F32)],
        scratch_shapes=[pltpu.VMEM((TC + HALO, CONV_CH), F32), pltpu.VMEM((SHIFTS, TC + HALO, CONV_CH), F32),
                        pltpu.VMEM((TC + HALO, CONV_CH), F32), pltpu.VMEM((SHIFTS, TC + HALO, CONV_CH), F32)],
        compiler_params=_params(("parallel", "arbitrary")),
    )(dy, dy, z, z, z, z, cw)


def _norm_bwd(d, that, r, g):
    w = d * g
    return r * (w - that * jnp.mean(w * that, axis=-1, keepdims=True)), jnp.sum(d * that, axis=0, keepdims=True)


def _attn_bwd(z, da, biasc, sink_rows, qg, kg):
    T = z.shape[0]
    nb = T // BLOCK
    qb = min(ATTN_QB, nb)
    TQ = qb * BLOCK
    ns = nb // qb
    kb, vb = K0 // KV_WIDTH, V0 // KV_WIDTH
    groups = [(i, k) for i in range(qb) for k in range(N_KV_HEADS)]

    def body(q_ref, kc_ref, kp_ref, vc_ref, vp_ref, da_ref, b_ref, sk_ref, qg_ref, kg_ref,
             dq_ref, dkv_ref, db_ref, sm_ref, ck_ref, cv_ref, pk_ref, pv_ref, nk_ref, nv_ref):
        n = pl.program_id(0)
        lane = lax.broadcasted_iota(jnp.int32, (1, LANES), 1)

        @pl.when(n == 0)
        def _():
            db_ref[...] = jnp.zeros(db_ref.shape, F32)
            sm_ref[...] = jnp.zeros(sm_ref.shape, F32)
            ck_ref[...] = jnp.zeros(ck_ref.shape, F32)
            cv_ref[...] = jnp.zeros(cv_ref.shape, F32)

        pk_ref[...] = jnp.zeros(pk_ref.shape, F32)
        pv_ref[...] = jnp.zeros(pv_ref.shape, F32)

        @pl.when(n < ns)
        def _():
            own = _own_block()
            knorm, kn, vv = {}, {}, {}
            for k in range(N_KV_HEADS):
                kn[-1, k] = _head_norm(kp_ref[:, KV_COLS[k]], kg_ref[...])[0].astype(BF16)
                vv[-1, k] = vp_ref[:, KV_COLS[k]].astype(BF16)
            for i, k in groups:
                knorm[i, k] = _head_norm(kc_ref[_blk(i), KV_COLS[k]], kg_ref[...])
                kn[i, k] = knorm[i, k][0].astype(BF16)
                vv[i, k] = vc_ref[_blk(i), KV_COLS[k]].astype(BF16)
            qnorm = {g: _head_norm(_stack_heads(q_ref, *g), qg_ref[...]) for g in groups}
            qnb = {g: qnorm[g][0].astype(BF16) for g in groups}
            dob = {g: _stack_heads(da_ref, *g).astype(BF16) for g in groups}
            s_own = {(i, k): _dot_nt(qnb[i, k], kn[i, k]) for i, k in groups}
            s_prev = {(i, k): _dot_nt(qnb[i, k], kn[i - 1, k]) for i, k in groups}
            dp_own = {(i, k): _dot_nt(dob[i, k], vv[i, k]) for i, k in groups}
            dp_prev = {(i, k): _dot_nt(dob[i, k], vv[i - 1, k]) for i, k in groups}
            probs = {(i, k): _band_probs(n == 0 if i == 0 else None, own, s_own[i, k], s_prev[i, k], b_ref[k], sk_ref[k])
                     for i, k in groups}
            ds_own, ds_prev, p_own, p_prev = {}, {}, {}, {}
            dsk = jnp.zeros((1, LANES), F32)
            dbias = [jnp.zeros((GROUP_ROWS, BLOCK), F32) for _ in range(N_KV_HEADS)]
            for i, k in groups:
                p, psink = probs[i, k]
                dp = jnp.where(own, dp_own[i, k], dp_prev[i, k])
                delta = jnp.sum(p * dp, axis=-1, keepdims=True)
                ds = p * (dp - delta)
                dbias[k] = dbias[k] + ds
                dsink = psink * delta
                for g in range(GQA_GROUP):
                    dsk = dsk + jnp.where(lane == k * GQA_GROUP + g, -jnp.sum(dsink[g * BLOCK:(g + 1) * BLOCK]), 0.0)
                ds_own[i, k] = jnp.where(own, ds, 0.0).astype(BF16)
                ds_prev[i, k] = jnp.where(own, 0.0, ds).astype(BF16)
                p_own[i, k] = jnp.where(own, p, 0.0).astype(BF16)
                p_prev[i, k] = jnp.where(own, 0.0, p).astype(BF16)
            for k in range(N_KV_HEADS):
                db_ref[k] += dbias[k]
            dqn_own = {(i, k): jnp.dot(ds_own[i, k], kn[i, k], preferred_element_type=F32) for i, k in groups}
            dqn_prev = {(i, k): jnp.dot(ds_prev[i, k], kn[i - 1, k], preferred_element_type=F32) for i, k in groups}
            dk_own = {g: _dot_tn(ds_own[g], qnb[g]) * SCALE for g in groups}
            dk_prev = {g: _dot_tn(ds_prev[g], qnb[g]) * SCALE for g in groups}
            dv_own = {g: _dot_tn(p_own[g], dob[g]) for g in groups}
            dv_prev = {g: _dot_tn(p_prev[g], dob[g]) for g in groups}
            dqg = jnp.zeros((1, HEAD_DIM), F32)
            dkg = jnp.zeros((1, HEAD_DIM), F32)
            for i, k in groups:
                _, qhat, rq = qnorm[i, k]
                dq, dg = _norm_bwd((dqn_own[i, k] + dqn_prev[i, k]) * SCALE, qhat, rq, qg_ref[...])
                dqg = dqg + dg
                _unstack_heads(dq_ref, i, k, dq.astype(BF16))
                if i == 0:
                    pk_ref[:, KV_COLS[k]] = dk_prev[i, k]
                    pv_ref[:, KV_COLS[k]] = dv_prev[i, k]
                if i == qb - 1:
                    nk_ref[:, KV_COLS[k]] = dk_own[i, k]
                    nv_ref[:, KV_COLS[k]] = dv_own[i, k]
                else:
                    _, khat, rk = knorm[i, k]
                    dk, dg = _norm_bwd(dk_own[i, k] + dk_prev[i + 1, k], khat, rk, kg_ref[...])
                    dkg = dkg + dg
                    dkv_ref[_blk(i + 1), KV_COLS[k]] = dk.astype(BF16)
                    dkv_ref[_blk(i + 1), pl.ds(KV_WIDTH + k * HEAD_DIM, HEAD_DIM)] = (dv_own[i, k] + dv_prev[i + 1, k]).astype(BF16)
            sm_ref[pl.ds(0, 1), pl.ds(0, HEAD_DIM)] += dqg
            sm_ref[pl.ds(1, 1), pl.ds(0, HEAD_DIM)] += dkg
            sm_ref[pl.ds(2, 1), :] += dsk

        @pl.when(n >= 1)
        def _():
            dkg = jnp.zeros((1, HEAD_DIM), F32)
            for k in range(N_KV_HEADS):
                _, khat, rk = _head_norm(kp_ref[:, KV_COLS[k]], kg_ref[...])
                dk, dg = _norm_bwd(ck_ref[:, KV_COLS[k]] + pk_ref[:, KV_COLS[k]], khat, rk, kg_ref[...])
                dkg = dkg + dg
                dkv_ref[_blk(0), KV_COLS[k]] = dk.astype(BF16)
            dkv_ref[_blk(0), pl.ds(KV_WIDTH, KV_WIDTH)] = (cv_ref[...] + pv_ref[...]).astype(BF16)
            sm_ref[pl.ds(1, 1), pl.ds(0, HEAD_DIM)] += dkg

        ck_ref[...] = nk_ref[...]
        cv_ref[...] = nv_ref[...]

    cur = lambda n: jnp.minimum(n, ns - 1)
    prev = lambda n: jnp.maximum(n * qb - 1, 0)
    carry = pltpu.VMEM((BLOCK, KV_WIDTH), F32)
    return pl.pallas_call(
        body, name="attn_bwd", grid=(ns + 1,),
        in_specs=[pl.BlockSpec((TQ, ATTN_WIDTH), lambda n: (cur(n), 0)),
                  pl.BlockSpec((TQ, KV_WIDTH), lambda n: (cur(n), kb)),
                  pl.BlockSpec((BLOCK, KV_WIDTH), lambda n: (prev(n), kb)),
                  pl.BlockSpec((TQ, KV_WIDTH), lambda n: (cur(n), vb)),
                  pl.BlockSpec((BLOCK, KV_WIDTH), lambda n: (prev(n), vb)),
                  pl.BlockSpec((TQ, ATTN_WIDTH), lambda n: (cur(n), 0)),
                  pl.BlockSpec((N_KV_HEADS, GROUP_ROWS, BLOCK), lambda n: (0, 0, 0)),
                  pl.BlockSpec((N_KV_HEADS, GROUP_ROWS, 1), lambda n: (0, 0, 0)),
                  pl.BlockSpec((1, HEAD_DIM), lambda n: (0, 0)),
                  pl.BlockSpec((1, HEAD_DIM), lambda n: (0, 0))],
        out_specs=[pl.BlockSpec((TQ, ATTN_WIDTH), lambda n: (cur(n), 0)),
                   pl.BlockSpec((TQ, 2 * KV_WIDTH), lambda n: (n, 0)),
                   pl.BlockSpec((N_KV_HEADS, GROUP_ROWS, BLOCK), lambda n: (0, 0, 0)),
                   pl.BlockSpec((8, LANES), lambda n: (0, 0))],
        out_shape=[jax.ShapeDtypeStruct((T, ATTN_WIDTH), BF16), jax.ShapeDtypeStruct(((ns + 1) * TQ, 2 * KV_WIDTH), BF16),
                   jax.ShapeDtypeStruct((N_KV_HEADS, GROUP_ROWS, BLOCK), F32), jax.ShapeDtypeStruct((8, LANES), F32)],
        scratch_shapes=[carry] * 6,
        compiler_params=_params(("arbitrary",)),
    )(z, z, z, z, z, da, biasc, sink_rows, qg, kg)


def _bucket_reduce(dbias, onehot_t):
    def body(d_ref, oh_ref, o_ref):
        d = d_ref[...]
        hi = d.astype(BF16)
        r1 = d - hi.astype(F32)
        mid = r1.astype(BF16)
        lo = (r1 - mid.astype(F32)).astype(BF16)
        oh = oh_ref[...]
        acc = lax.dot_general(lo, oh, NT, preferred_element_type=F32)
        acc = acc + lax.dot_general(mid, oh, NT, preferred_element_type=F32)
        o_ref[...] = acc + lax.dot_general(hi, oh, NT, preferred_element_type=F32)

    return pl.pallas_call(
        body, name="bucket_reduce",
        out_shape=jax.ShapeDtypeStruct((N_HEADS, LANES), F32),
        compiler_params=_params(),
    )(dbias, onehot_t)


def _adamw(w, g, m, v, name):
    R, C = w.shape
    TR = _tile(R, 512)

    def body(w_ref, g_ref, m_ref, v_ref, d_ref, nm_ref, nv_ref):
        gv = g_ref[...]
        mn = ADAM_B1 * m_ref[...] + (1.0 - ADAM_B1) * gv
        vn = ADAM_B2 * v_ref[...] + (1.0 - ADAM_B2) * jnp.square(gv)
        m_hat = mn / (1.0 - ADAM_B1 ** ADAM_STEP)
        v_hat = vn / (1.0 - ADAM_B2 ** ADAM_STEP)
        d_ref[...] = -ADAM_LR * (m_hat / (jnp.sqrt(v_hat) + ADAM_EPS) + ADAM_WD * w_ref[...])
        nm_ref[...] = mn
        nv_ref[...] = vn

    spec = pl.BlockSpec((TR, C), lambda i: (i, 0))
    shp = jax.ShapeDtypeStruct((R, C), F32)
    return pl.pallas_call(
        body, name=name, grid=(R // TR,),
        in_specs=[spec] * 4, out_specs=[spec] * 3, out_shape=[shp] * 3,
        compiler_params=_params(("parallel",)),
    )(w, g, m, v)


def _place():
    return lax.axis_index("x"), lax.axis_index("y"), lax.axis_index("c")


def _other_chips(x, y):
    return [(1 - x, y), (x, 1 - y), (1 - x, 1 - y)]


def _remote(src, dst, send_sem, recv_sem, dev):
    return pltpu.make_async_remote_copy(src_ref=src, dst_ref=dst, send_sem=send_sem, recv_sem=recv_sem,
                                        device_id=dev, device_id_type=MESH)


def _gather_shards(bufs):
    nbuf = len(bufs)

    def body(*refs):
        ins, outs = refs[:nbuf], refs[nbuf:2 * nbuf]
        send_sems, recv_sems = refs[2 * nbuf:]
        x, y, c = _place()
        me = 2 * x + y
        sib = (x, y, 1 - c)
        chips = _other_chips(x, y)
        started = []
        for b in range(nbuf):
            hh = bufs[b].shape[0] // 2
            for j, (cx, cy) in enumerate(chips):
                k = 6 * b + j
                cp = _remote(ins[b].at[pl.ds(c * hh, hh), :], outs[b].at[me, pl.ds(c * hh, hh), :],
                             send_sems.at[k], recv_sems.at[k], (cx, cy, c))
                cp.start()
                started.append(cp)
        for b in range(nbuf):
            hh = bufs[b].shape[0] // 2
            for j, (cx, cy) in enumerate(chips):
                rows = outs[b].at[2 * cx + cy, pl.ds(c * hh, hh), :]
                _remote(rows, rows, send_sems.at[6 * b + j], recv_sems.at[6 * b + j], sib).wait_recv()
                k = 6 * b + 3 + j
                cp = _remote(rows, rows, send_sems.at[k], recv_sems.at[k], sib)
                cp.start()
                started.append(cp)
        for b in range(nbuf):
            hh = bufs[b].shape[0] // 2
            for j, (cx, cy) in enumerate(chips):
                rows = outs[b].at[2 * cx + cy, pl.ds((1 - c) * hh, hh), :]
                k = 6 * b + 3 + j
                _remote(rows, rows, send_sems.at[k], recv_sems.at[k], sib).wait_recv()
        for cp in started:
            cp.wait_send()

    return pl.pallas_call(
        body, name="gather_weights",
        in_specs=[ANY] * nbuf, out_specs=[ANY] * nbuf,
        out_shape=[jax.ShapeDtypeStruct((N_CHIPS,) + b.shape, b.dtype) for b in bufs],
        scratch_shapes=[pltpu.SemaphoreType.DMA((6 * nbuf,)), pltpu.SemaphoreType.DMA((6 * nbuf,))],
        compiler_params=pltpu.CompilerParams(has_side_effects=True),
    )(*bufs)


HBM = pl.BlockSpec(memory_space=pltpu.HBM)
SEM = pl.BlockSpec(memory_space=pltpu.SEMAPHORE)
DATAFLOW = pltpu.SideEffectType.DATAFLOW_SIDE_EFFECTING


def _gather_plan(shapes):
    def plan(srcs, lands):
        x, y, c = _place()
        out = []
        for b, shp in enumerate(shapes):
            hh = shp[0] // 2
            for cx, cy in _other_chips(x, y):
                out.append((srcs[b].at[pl.ds(c * hh, hh), :], lands[b].at[2 * x + y, pl.ds(c * hh, hh), :], (cx, cy, c)))
        return out
    return plan


def _exchange_plan(shapes):
    def plan(srcs, lands):
        x, y, c = _place()
        return [(srcs[b].at[j], lands[b].at[j], (cx, cy, c))
                for b in range(len(shapes)) for j, (cx, cy) in enumerate(_other_chips(x, y))]
    return plan


def _swap_plan(shapes):
    def plan(srcs, lands):
        x, y, c = _place()
        return [(srcs[b].at[:, pl.ds((1 - c) * (shp[1] // 2), shp[1] // 2), :], lands[b], (x, y, 1 - c))
                for b, shp in enumerate(shapes)]
    return plan


def _start_copies(name, srcs, land_shapes, plan, per_buffer=N_CHIPS - 1):
    n = len(srcs)
    ncopy = per_buffer * n

    def body(*refs):
        ins, lands = refs[:n], refs[n:2 * n]
        send_sems, recv_sems, token = refs[2 * n], refs[2 * n + 1], refs[-1]
        for k, (src, dst, dev) in enumerate(plan(ins, lands)):
            _remote(src, dst, send_sems.at[k], recv_sems.at[k], dev).start()
        token[...] = jnp.zeros_like(token)

    hbm = lambda a: pltpu.with_memory_space_constraint(a, pltpu.HBM)
    lands = [lax.empty(s, a.dtype) for s, a in zip(land_shapes, srcs)]
    outs = pl.pallas_call(
        body, name=name,
        out_shape=(pltpu.SemaphoreType.DMA((ncopy,)), pltpu.SemaphoreType.DMA((ncopy,)),
                   *[pltpu.HBM(a.shape, a.dtype) for a in srcs], *[pltpu.HBM(a.shape, a.dtype) for a in lands],
                   jax.ShapeDtypeStruct((8, LANES), F32)),
        in_specs=[HBM] * (2 * n),
        out_specs=(SEM, SEM, *([HBM] * (2 * n)), pl.BlockSpec(memory_space=pltpu.VMEM)),
        input_output_aliases={i: 2 + i for i in range(2 * n)},
        compiler_params=pltpu.CompilerParams(has_side_effects=DATAFLOW),
    )(*[hbm(a) for a in srcs], *[hbm(a) for a in lands])
    return outs[0], outs[1], list(outs[2:2 + n]), list(outs[2 + n:2 + 2 * n]), outs[-1]


def _wait_copies(name, send_sems, recv_sems, srcs, lands, plan, after):
    n = len(srcs)

    def body(*refs):
        ins, lnds = refs[:n], refs[n:2 * n]
        ssem, rsem = refs[2 * n], refs[2 * n + 1]
        for k, (src, dst, dev) in enumerate(plan(ins, lnds)):
            cp = _remote(src, dst, ssem.at[k], rsem.at[k], dev)
            cp.wait_send()
            cp.wait_recv()

    outs = pl.pallas_call(
        body, name=name,
        out_shape=(*[pltpu.HBM(a.shape, a.dtype) for a in srcs], *[pltpu.HBM(a.shape, a.dtype) for a in lands]),
        in_specs=[HBM] * (2 * n) + [SEM, SEM, ANY],
        out_specs=tuple([HBM] * (2 * n)),
        input_output_aliases={i: i for i in range(2 * n)},
        compiler_params=pltpu.CompilerParams(has_side_effects=DATAFLOW),
    )(*srcs, *lands, send_sems, recv_sems, after)
    return list(outs[:n]), list(outs[n:])


def _forward_halves(bufs):
    nbuf = len(bufs)

    def body(*refs):
        outs = refs[nbuf:2 * nbuf]
        send_sems, recv_sems = refs[2 * nbuf:]
        x, y, c = _place()
        sib = (x, y, 1 - c)
        cps = []
        for b in range(nbuf):
            hh = bufs[b].shape[1] // 2
            for j, (cx, cy) in enumerate(_other_chips(x, y)):
                rows = outs[b].at[2 * cx + cy, pl.ds(c * hh, hh), :]
                cp = _remote(rows, rows, send_sems.at[3 * b + j], recv_sems.at[3 * b + j], sib)
                cp.start()
                cps.append(cp)
        for b in range(nbuf):
            hh = bufs[b].shape[1] // 2
            for j, (cx, cy) in enumerate(_other_chips(x, y)):
                rows = outs[b].at[2 * cx + cy, pl.ds((1 - c) * hh, hh), :]
                _remote(rows, rows, send_sems.at[3 * b + j], recv_sems.at[3 * b + j], sib).wait_recv()
        for cp in cps:
            cp.wait_send()

    return pl.pallas_call(
        body, name="gather_forward_halves",
        in_specs=[ANY] * nbuf, out_specs=[ANY] * nbuf,
        out_shape=[jax.ShapeDtypeStruct(b.shape, b.dtype) for b in bufs],
        input_output_aliases={b: b for b in range(nbuf)},
        scratch_shapes=[pltpu.SemaphoreType.DMA((3 * nbuf,)), pltpu.SemaphoreType.DMA((3 * nbuf,))],
        compiler_params=pltpu.CompilerParams(has_side_effects=True),
    )(*bufs)


def _swap_halves(bufs):
    nbuf = len(bufs)

    def body(*refs):
        ins, outs = refs[:nbuf], refs[nbuf:2 * nbuf]
        send_sems, recv_sems = refs[2 * nbuf:]
        x, y, c = _place()
        sib = (x, y, 1 - c)
        cps = []
        for b in range(nbuf):
            hh = bufs[b].shape[1] // 2
            cp = _remote(ins[b].at[:, pl.ds((1 - c) * hh, hh), :], outs[b], send_sems.at[b], recv_sems.at[b], sib)
            cp.start()
            cps.append(cp)
        for cp in cps:
            cp.wait()

    return pl.pallas_call(
        body, name="grad_swap_halves",
        in_specs=[ANY] * nbuf, out_specs=[ANY] * nbuf,
        out_shape=[jax.ShapeDtypeStruct((N_CHIPS, b.shape[1] // 2, b.shape[2]), b.dtype) for b in bufs],
        scratch_shapes=[pltpu.SemaphoreType.DMA((nbuf,)), pltpu.SemaphoreType.DMA((nbuf,))],
        compiler_params=pltpu.CompilerParams(has_side_effects=True),
    )(*bufs)


def _chip_sum(g, got, sel, out_dtype, name):
    _, R, C = g.shape
    hh = R // 2
    TR = _tile(hh, 512)
    nslot = sel[1].shape[0]

    def body(off_ref, sh_ref, g_ref, r_ref, o_ref):
        o_ref[...] = (g_ref[...] + r_ref[...]).astype(out_dtype)

    return pl.pallas_call(
        body, name=name,
        grid_spec=pltpu.PrefetchScalarGridSpec(
            num_scalar_prefetch=2, grid=(nslot, hh // TR),
            in_specs=[pl.BlockSpec((None, TR, C), lambda s, i, off, sh: (sh[s], off[0] + i, 0)),
                      pl.BlockSpec((None, TR, C), lambda s, i, off, sh: (sh[s], i, 0))],
            out_specs=pl.BlockSpec((None, TR, C), lambda s, i, off, sh: (s, i, 0))),
        out_shape=jax.ShapeDtypeStruct((nslot, hh, C), out_dtype),
        compiler_params=_params(("parallel", "parallel")),
    )(sel[0], sel[1], g, got)


def _exchange_chips(bufs):
    nbuf = len(bufs)

    def body(*refs):
        ins, outs = refs[:nbuf], refs[nbuf:2 * nbuf]
        send_sems, recv_sems = refs[2 * nbuf:]
        x, y, c = _place()
        cps = []
        for b in range(nbuf):
            for j, (cx, cy) in enumerate(_other_chips(x, y)):
                k = 3 * b + j
                cp = _remote(ins[b].at[j], outs[b].at[j], send_sems.at[k], recv_sems.at[k], (cx, cy, c))
                cp.start()
                cps.append(cp)
        for cp in cps:
            cp.wait()

    return pl.pallas_call(
        body, name="grad_exchange_chips",
        in_specs=[ANY] * nbuf, out_specs=[ANY] * nbuf,
        out_shape=[jax.ShapeDtypeStruct(b.shape, b.dtype) for b in bufs],
        scratch_shapes=[pltpu.SemaphoreType.DMA((3 * nbuf,)), pltpu.SemaphoreType.DMA((3 * nbuf,))],
        compiler_params=pltpu.CompilerParams(has_side_effects=True),
    )(*bufs)


def _shard_sum(own, got, off, tr, full, rows, name):
    _, hh, C = own.shape

    def body(off_ref, o_ref, r_ref, *rest):
        acc = o_ref[...]
        for j in range(N_CHIPS - 1):
            acc = acc + r_ref[j].astype(F32)
        rest[-1][...] = acc

    in_specs = [pl.BlockSpec((None, tr, C), lambda i, off: (0, i, 0)),
                pl.BlockSpec((N_CHIPS - 1, tr, C), lambda i, off: (0, i, 0))]
    args = [off, own, got]
    aliases = {}
    if full is not None:
        in_specs.append(ANY)
        args.append(full)
        aliases = {3: 0}
    return pl.pallas_call(
        body, name=name,
        grid_spec=pltpu.PrefetchScalarGridSpec(
            num_scalar_prefetch=1, grid=(hh // tr,), in_specs=in_specs,
            out_specs=pl.BlockSpec((tr, C), lambda i, off: (off[0] + i, 0))),
        out_shape=jax.ShapeDtypeStruct((rows, C), F32),
        input_output_aliases=aliases,
        compiler_params=_params(("parallel",)),
    )(*args)


def _join_halves(bufs, spans):
    nbuf = len(bufs)
    ncopy = nbuf * len(spans)

    def body(*refs):
        outs = refs[nbuf:2 * nbuf]
        send_sems, recv_sems = refs[2 * nbuf:]
        x, y, c = _place()
        sib = (x, y, 1 - c)
        cps = []
        for b in range(nbuf):
            for s, (r0, nr) in enumerate(spans[b]):
                k = b * len(spans[b]) + s
                rows = outs[b].at[pl.ds(r0 + c * (nr // 2), nr // 2), :]
                cp = _remote(rows, rows, send_sems.at[k], recv_sems.at[k], sib)
                cp.start()
                cps.append(cp)
        for b in range(nbuf):
            for s, (r0, nr) in enumerate(spans[b]):
                k = b * len(spans[b]) + s
                theirs = outs[b].at[pl.ds(r0 + (1 - c) * (nr // 2), nr // 2), :]
                _remote(theirs, theirs, send_sems.at[k], recv_sems.at[k], sib).wait_recv()
        for cp in cps:
            cp.wait_send()

    return pl.pallas_call(
        body, name="grad_join_halves",
        in_specs=[ANY] * nbuf, out_specs=[ANY] * nbuf,
        out_shape=[jax.ShapeDtypeStruct(b.shape, b.dtype) for b in bufs],
        input_output_aliases={b: b for b in range(nbuf)},
        scratch_shapes=[pltpu.SemaphoreType.DMA((ncopy,)), pltpu.SemaphoreType.DMA((ncopy,))],
        compiler_params=pltpu.CompilerParams(has_side_effects=True),
    )(*bufs)


def _sum_devices(part):
    R = part.shape[0]

    def body(p_ref, o_ref, all_ref, send_sems, recv_sems):
        x, y, c = _place()
        me = 4 * x + 2 * y + c
        all_ref[me] = p_ref[...]
        cps = []
        for k in range(1, N_DEV):
            px, py, pc = x ^ (k >> 2), y ^ ((k >> 1) & 1), c ^ (k & 1)
            cp = _remote(p_ref, all_ref.at[me], send_sems.at[k - 1], recv_sems.at[k - 1], (px, py, pc))
            cp.start()
            cps.append(cp)
        for k in range(1, N_DEV):
            peer = me ^ k
            _remote(p_ref, all_ref.at[peer], send_sems.at[k - 1], recv_sems.at[k - 1], (x, y, c)).wait_recv()
        for cp in cps:
            cp.wait_send()
        acc = all_ref[0]
        for d in range(1, N_DEV):
            acc = acc + all_ref[d]
        o_ref[...] = acc

    return pl.pallas_call(
        body, name="sum_small_grads",
        in_specs=[pl.BlockSpec(memory_space=pltpu.VMEM)],
        out_specs=pl.BlockSpec(memory_space=pltpu.VMEM),
        out_shape=jax.ShapeDtypeStruct((R, LANES), F32),
        scratch_shapes=[pltpu.VMEM((N_DEV, R, LANES), F32),
                        pltpu.SemaphoreType.DMA((N_DEV - 1,)), pltpu.SemaphoreType.DMA((N_DEV - 1,))],
        compiler_params=pltpu.CompilerParams(has_side_effects=True, vmem_limit_bytes=VMEM_LIMIT),
    )(part)


def _pack(parts):
    flat = jnp.concatenate([p.reshape(-1).astype(F32) for p in parts])
    n = flat.shape[0]
    rows = -(-n // LANES)
    rows = -(-rows // 8) * 8
    return jnp.pad(flat, (0, rows * LANES - n)).reshape(rows, LANES)


def _unpack(packed, shapes):
    flat = packed.reshape(-1)
    out, off = [], 0
    for s in shapes:
        n = int(np.prod(s))
        out.append(flat[off:off + n].reshape(s))
        off += n
    return out


def kernel(x, rel_bias, norm_mix_g, w_in, q_norm_g, k_norm_g, sinks, conv_w, conv_b, conv_ln_g, conv_ln_b, attn_out_g, conv_out_g, w_out, norm_mlp_g, w_mlp_up, w_mlp_down, loss_target, m_rel_bias, m_norm_mix_g, m_w_in, m_q_norm_g, m_k_norm_g, m_sinks, m_conv_w, m_conv_b, m_conv_ln_g, m_conv_ln_b, m_attn_out_g, m_conv_out_g, m_w_out, m_norm_mlp_g, m_w_mlp_up, m_w_mlp_down, v_rel_bias, v_norm_mix_g, v_w_in, v_q_norm_g, v_k_norm_g, v_sinks, v_conv_w, v_conv_b, v_conv_ln_g, v_conv_ln_b, v_attn_out_g, v_conv_out_g, v_w_out, v_norm_mlp_g, v_w_mlp_up, v_w_mlp_down):
    T = x.shape[1]
    L = DEPTH
    xi, yi, ci = _place()
    shard = 2 * xi + yi
    in_sh = IN_WIDTH // N_CHIPS
    out_sh = MIX_WIDTH // N_CHIPS
    ff_sh = D_FF // N_CHIPS
    cv_sh = CONV_WIDTH // N_CHIPS

    MIXING, MLP = ("w_in", "w_out", "conv_w"), ("w_mlp_up", "w_mlp_down")

    def my_shard(name, lo, hi):
        n = hi - lo
        if name == "w_in":
            return w_in[lo:hi].astype(BF16).reshape(n * D_MODEL, in_sh)
        if name == "w_out":
            return w_out[lo:hi].astype(BF16).reshape(n * out_sh, D_MODEL)
        if name == "w_mlp_up":
            return w_mlp_up[lo:hi].astype(BF16).reshape(n * D_MODEL, ff_sh)
        if name == "w_mlp_down":
            return w_mlp_down[lo:hi].astype(BF16).reshape(n * ff_sh, D_MODEL)
        cw_pad = jnp.pad(conv_w[lo:hi], ((0, 0), (0, CONV_ROWS - CONV_KERNEL), (0, 0)))
        return cw_pad.reshape(n * CONV_ROWS, cv_sh)

    def whole_weight(name, gathered, own, n):
        g = lax.dynamic_update_slice(gathered, own[None], (shard, 0, 0))
        if name == "w_in":
            return g.reshape(N_CHIPS, n, D_MODEL, in_sh).transpose(1, 2, 0, 3).reshape(n, D_MODEL, IN_WIDTH)
        if name == "w_out":
            return g.reshape(N_CHIPS, n, out_sh, D_MODEL).transpose(1, 0, 2, 3).reshape(n, MIX_WIDTH, D_MODEL)
        if name == "w_mlp_up":
            return g.reshape(N_CHIPS, n, D_MODEL, ff_sh).transpose(1, 2, 0, 3).reshape(n, D_MODEL, D_FF)
        if name == "w_mlp_down":
            return g.reshape(N_CHIPS, n, ff_sh, D_MODEL).transpose(1, 0, 2, 3).reshape(n, D_FF, D_MODEL)
        return g.reshape(N_CHIPS, n, CONV_ROWS, cv_sh).transpose(1, 2, 0, 3).reshape(n, CONV_ROWS, CONV_WIDTH)

    weight_of = {}

    def provide(entries, gathered, mine):
        for (name, lo, hi), g, own in zip(entries, gathered, mine):
            whole = whole_weight(name, g, own, hi - lo)
            for l in range(lo, hi):
                weight_of[name, l] = (whole, l - lo)

    def gather_behind(tag, entries, first):
        mine = [my_shard(*e) for e in entries]
        mine[0], _ = lax.optimization_barrier((mine[0], first))
        plan = _gather_plan([m.shape for m in mine])
        send_sems, recv_sems, srcs, lands, token = _start_copies(
            "gather_" + tag + "_start", mine, [(N_CHIPS,) + m.shape for m in mine], plan)

        def finish(after):
            _, got = _wait_copies("gather_" + tag + "_wait", send_sems, recv_sems, srcs, lands, plan, after)
            provide(entries, _forward_halves(got), mine)
        return token, finish

    now = [(name, 0, 1) for name in MIXING]
    early = [(name, 0, 2) for name in MLP] + [(name, 1, 2) for name in MIXING]
    late = [(name, 2, L) for name in MIXING + MLP]
    mine0 = [my_shard(*e) for e in now]
    got0 = _gather_shards(mine0)
    provide(now, got0, mine0)
    token_early, finish_early = gather_behind("early", early, got0[0])

    bucket = _band_buckets()
    bk = jnp.asarray(bucket)[None]
    biasc = jnp.zeros((N_HEADS, BLOCK, BLOCK), F32)
    for b in range(NUM_BUCKETS):
        biasc = jnp.where(bk == b, rel_bias[b][:, None, None], biasc)
    biasc = biasc.reshape(N_KV_HEADS, GROUP_ROWS, BLOCK)
    onehot_t = np.zeros((LANES, BLOCK * BLOCK), np.float32)
    onehot_t[bucket.reshape(-1), np.arange(BLOCK * BLOCK)] = 1.0
    onehot_t = jnp.asarray(onehot_t, dtype=BF16)
    sink_rows = lambda l: jnp.repeat(sinks[l], BLOCK).reshape(N_KV_HEADS, GROUP_ROWS, 1)

    row = lambda a, l: a[l][None, :]

    xs = x.reshape(T, D_MODEL)
    saved = []
    token_late = None
    for l in range(L):
        if l == 2:
            finish_late(xs)
        h, z = _norm_matmul(xs, row(norm_mix_g, l), *weight_of["w_in", l], F32, "mix_in_proj", token_early if l == 0 else None)
        a = _attn_fwd(z, biasc, sink_rows(l), row(q_norm_g, l), row(k_norm_g, l))
        cw, cl = weight_of["conv_w", l]
        yc = _conv_fwd(z, cw[cl], row(conv_b, l))
        mix = _mix_norm(a, yc, row(conv_ln_g, l), row(conv_ln_b, l), row(attn_out_g, l), row(conv_out_g, l))
        x1 = _matmul_res(mix, *weight_of["w_out", l], xs, False, "mix_out_proj")
        if l == 0:
            finish_early(x1)
            token_late, finish_late = gather_behind("late", late, weight_of["w_mlp_up", 0][0])
        h2, up = _norm_matmul(x1, row(norm_mlp_g, l), *weight_of["w_mlp_up", l], BF16, "mlp_up_proj",
                              token_late if l == 0 else None)
        x2 = _matmul_res(up, *weight_of["w_mlp_down", l], x1, True, "mlp_down_proj")
        saved.append((xs, h, z, a, yc, mix, x1, h2, up))
        xs = x2

    loss_parts, g = _loss_grad(xs, loss_target.reshape(T, D_MODEL))

    names = ["w_in", "w_out", "w_mlp_up", "w_mlp_down"]
    shard_rows = {"w_in": D_MODEL, "w_out": out_sh, "w_mlp_up": D_MODEL, "w_mlp_down": ff_sh}
    own_sel = shard.astype(jnp.int32)[None]
    send_sel = jnp.stack([shard ^ 2, shard ^ 1, shard ^ 3]).astype(jnp.int32)

    def by_shard(name, buf, n):
        if name == "w_in":
            return buf.reshape(n, D_MODEL, N_CHIPS, in_sh).transpose(2, 0, 1, 3).reshape(N_CHIPS, n * D_MODEL, in_sh)
        return buf.reshape(N_CHIPS, n * shard_rows[name], buf.shape[-1])

    def chip_sums(tag, group, n, swapped=None):
        order = list(group)
        G, got = swapped if swapped else (None, None)
        if not swapped:
            G = [by_shard(name, group[name], n) for name in order]
            got = _swap_halves(G)
        owns, sends = {}, {}
        for name, g_all, g_got in zip(order, G, got):
            hh = g_all.shape[1] // 2
            off = (ci * (hh // _tile(hh, 512))).astype(jnp.int32)[None]
            owns[name] = _chip_sum(g_all, g_got, (off, own_sel), F32, "chip_sum_own_" + name + tag)
            sends[name] = _chip_sum(g_all, g_got, (off, send_sel), BF16, "chip_sum_send_" + name + tag)
        return owns, sends

    def swap_behind(tag, group, n):
        G = [by_shard(name, group[name], n) for name in group]
        plan = _swap_plan([g_all.shape for g_all in G])
        send_sems, recv_sems, srcs, lands, token = _start_copies(
            "grad_swap" + tag + "_start", G, [(N_CHIPS, g_all.shape[1] // 2, g_all.shape[2]) for g_all in G], plan, 1)

        def finish(after):
            return _wait_copies("grad_swap" + tag + "_wait", send_sems, recv_sems, srcs, lands, plan, after)
        return token, finish

    def exchange_behind(tag, group, n, swapped=None):
        owns, sends = chip_sums(tag, group, n, swapped)
        order = list(sends)
        bufs = [sends[name] for name in order]
        plan = _exchange_plan([b.shape for b in bufs])
        send_sems, recv_sems, srcs, lands, token = _start_copies(
            "grad_exchange" + tag + "_start", bufs, [b.shape for b in bufs], plan)

        def finish(after):
            _, got = _wait_copies("grad_exchange" + tag + "_wait", send_sems, recv_sems, srcs, lands, plan, after)
            return {name: (owns[name], arrived) for name, arrived in zip(order, got)}
        return token, finish

    rest = dict.fromkeys(names)
    first = dict.fromkeys(names)
    small = [None] * L
    dbias_sum = None
    token = None
    for l in reversed(range(L)):
        x0, h, z, a, yc, mix, x1, h2, up = saved[l]
        stack, n, sl = (first, 1, 0) if l == 0 else (rest, L - 1, l - 1)
        if l == 0:
            token, finish_rest_swap = swap_behind("_rest", rest, L - 1)
        d_up = _dact(g, *weight_of["w_mlp_down", l], up, token)
        stack["w_mlp_down"] = _matmul_tn(up, g, True, stack["w_mlp_down"], (N_CHIPS, n, ff_sh, D_MODEL),
                                         (None, None, ff_sh, D_MODEL), lambda i, j: (i, sl, 0, 0), ff_sh, D_MODEL,
                                         "grad_w_mlp_down")
        stack["w_mlp_up"] = _matmul_tn(h2, d_up, False, stack["w_mlp_up"], (N_CHIPS, n, D_MODEL, ff_sh),
                                       (None, None, D_MODEL, ff_sh), lambda i, j: (j, sl, 0, 0), D_MODEL, ff_sh,
                                       "grad_w_mlp_up")
        if l == 0:
            token_rest, finish_rest_grads = exchange_behind("_rest", rest, L - 1, finish_rest_swap(stack["w_mlp_up"]))
            token, finish_mlp0_grads = exchange_behind("_mlp0", {k: first[k] for k in ("w_mlp_up", "w_mlp_down")}, 1)
            token = token + token_rest
        g1, d_gmlp = _matmul_nt_normbwd(d_up, *weight_of["w_mlp_up", l], x1, row(norm_mlp_g, l), g, "mlp_in_bwd",
                                        token if l == 0 else None)
        d_a, d_y, sm_mix = _mix_bwd(g1, *weight_of["w_out", l], a, yc, row(conv_ln_g, l), row(conv_ln_b, l),
                                    row(attn_out_g, l), row(conv_out_g, l))
        stack["w_out"] = _matmul_tn(mix, g1, False, stack["w_out"], (N_CHIPS, n, out_sh, D_MODEL),
                                    (N_CHIPS, None, out_sh, D_MODEL), lambda i, j: (0, sl, 0, 0), MIX_WIDTH, D_MODEL,
                                    "grad_w_out")
        cw, cl = weight_of["conv_w", l]
        d_u, d_gate, d_cw = _conv_bwd(d_y, z, cw[cl])
        d_q, d_kv, dbias, sm_attn = _attn_bwd(z, d_a, biasc, sink_rows(l), row(q_norm_g, l), row(k_norm_g, l))
        dbias_sum = dbias if dbias_sum is None else dbias_sum + dbias
        d_z = [d_q, d_kv[BLOCK:BLOCK + T], d_u, d_gate]
        stack["w_in"] = _matmul_tn(h, d_z, False, stack["w_in"], (n, D_MODEL, IN_WIDTH), (None, D_MODEL, IN_WIDTH),
                                   lambda i, j: (sl, 0, 0), D_MODEL, IN_WIDTH, "grad_w_in")
        g, d_gmix = _matmul_nt_normbwd(d_z, *weight_of["w_in", l], x0, row(norm_mix_g, l), g1, "mix_in_bwd")
        small[l] = (d_gmix[0], sm_attn[0, :HEAD_DIM], sm_attn[1, :HEAD_DIM], sm_attn[2, :N_HEADS],
                    d_cw[:CONV_KERNEL], sm_mix[4], sm_mix[2], sm_mix[3], sm_mix[0], sm_mix[1], d_gmlp[0])
    grad_x = g.reshape(1, T, D_MODEL)

    d_rel = _bucket_reduce(dbias_sum.reshape(N_HEADS, BLOCK * BLOCK), onehot_t)[:, :NUM_BUCKETS].T
    stack = lambda k: jnp.stack([small[l][k] for l in range(L)])
    small_shapes = [(), (NUM_BUCKETS, N_HEADS), (L, D_MODEL), (L, HEAD_DIM), (L, HEAD_DIM), (L, N_HEADS),
                    (L, CONV_KERNEL, CONV_WIDTH), (L, CONV_WIDTH), (L, CONV_WIDTH), (L, CONV_WIDTH),
                    (L, CONV_WIDTH), (L, CONV_WIDTH), (L, D_MODEL)]
    part = _pack([jnp.sum(loss_parts[:, 0, 0]), d_rel] + [stack(k) for k in range(11)])
    tot = _unpack(_sum_devices(part), small_shapes)
    loss = tot[0]
    (g_rel, g_nmix, g_qn, g_kn, g_sk, g_cw_full, g_cb, g_lng, g_lnb, g_aog, g_cog, g_nmlp) = tot[1:]
    g_cw_sh = lax.dynamic_slice_in_dim(g_cw_full, shard * cv_sh, cv_sh, axis=2)

    small_w = [rel_bias, norm_mix_g, q_norm_g, k_norm_g, sinks, conv_w, conv_b, conv_ln_g, conv_ln_b,
               attn_out_g, conv_out_g, norm_mlp_g]
    small_m = [m_rel_bias, m_norm_mix_g, m_q_norm_g, m_k_norm_g, m_sinks, m_conv_w, m_conv_b, m_conv_ln_g,
               m_conv_ln_b, m_attn_out_g, m_conv_out_g, m_norm_mlp_g]
    small_v = [v_rel_bias, v_norm_mix_g, v_q_norm_g, v_k_norm_g, v_sinks, v_conv_w, v_conv_b, v_conv_ln_g,
               v_conv_ln_b, v_attn_out_g, v_conv_out_g, v_norm_mlp_g]
    small_g = [g_rel, g_nmix, g_qn, g_kn, g_sk, g_cw_sh, g_cb, g_lng, g_lnb, g_aog, g_cog, g_nmlp]
    shapes = [w.shape for w in small_w]
    sd, sm_, sv_ = _adamw(_pack(small_w), _pack(small_g), _pack(small_m), _pack(small_v), "adamw_small")
    small_d, small_nm, small_nv = _unpack(sd, shapes), _unpack(sm_, shapes), _unpack(sv_, shapes)

    owns_mix0, sends_mix0 = chip_sums("_mix0", {k: first[k] for k in ("w_in", "w_out")}, 1)
    arrived_mix0 = _exchange_chips([sends_mix0[k] for k in ("w_in", "w_out")])
    parts0 = {"w_in": (owns_mix0["w_in"], arrived_mix0[0]), "w_out": (owns_mix0["w_out"], arrived_mix0[1]),
              **finish_mlp0_grads(g)}
    parts1 = finish_rest_grads(g)
    grads, spans = [], []
    for name in names:
        R = shard_rows[name]
        full = None
        spans.append([(0, R), (R, (L - 1) * R)])
        for (r0, nr), (own, arrived), tag in zip(spans[-1], (parts0[name], parts1[name]), ("_first", "_rest")):
            tr = min(512, math.gcd(R, nr // 2))
            off = ((r0 + ci * (nr // 2)) // tr).astype(jnp.int32)[None]
            full = _shard_sum(own, arrived, off, tr, full, L * R, "shard_sum_" + name + tag)
        grads.append(full)
    grads = _join_halves(grads, spans)

    big_w = [w_in, w_out, w_mlp_up, w_mlp_down]
    big_m = [m_w_in, m_w_out, m_w_mlp_up, m_w_mlp_down]
    big_v = [v_w_in, v_w_out, v_w_mlp_up, v_w_mlp_down]
    big_g, big_d, big_nm, big_nv = [], [], [], []
    for b in range(4):
        shp = big_w[b].shape
        flat = lambda t: t.reshape(shp[0] * shp[1], shp[2])
        d, nm, nv = _adamw(flat(big_w[b]), grads[b], flat(big_m[b]), flat(big_v[b]), "adamw_" + names[b])
        big_g.append(grads[b].reshape(shp))
        big_d.append(d.reshape(shp))
        big_nm.append(nm.reshape(shp))
        big_nv.append(nv.reshape(shp))

    def ordered(sm, bg):
        return [sm[0], sm[1], bg[0], sm[2], sm[3], sm[4], sm[5], sm[6], sm[7], sm[8], sm[9], sm[10], bg[1], sm[11],
                bg[2], bg[3]]

    return (loss, grad_x, *ordered(small_g, big_g), *ordered(small_d, big_d), *ordered(small_nm, big_nm),
            *ordered(small_nv, big_nv))
```

```python
import math

import numpy as np
import jax
import jax.numpy as jnp
from jax import lax
from jax.experimental import pallas as pl
from jax.experimental.pallas import tpu as pltpu

F32 = jnp.float32
BF16 = jnp.bfloat16

D_MODEL = 1024
DEPTH = 4
HEAD_DIM = 64
N_HEADS = 8
N_KV_HEADS = 2
GQA_GROUP = N_HEADS // N_KV_HEADS
ATTN_WIDTH = N_HEADS * HEAD_DIM
KV_WIDTH = N_KV_HEADS * HEAD_DIM
CONV_WIDTH = D_MODEL - ATTN_WIDTH
MIX_WIDTH = ATTN_WIDTH + CONV_WIDTH
IN_WIDTH = ATTN_WIDTH + 2 * KV_WIDTH + 2 * CONV_WIDTH
BLOCK = 128
CONV_KERNEL = 31
CONV_ROWS = 32
HALO = 32
CONV_CH = 256
NUM_BUCKETS = 32
MAX_DISTANCE = 128
D_FF = 4 * D_MODEL
EPS = 1e-6
NEG = -1e30
SCALE = 1.0 / math.sqrt(HEAD_DIM)

ADAM_LR = 0.001
ADAM_B1 = 0.9
ADAM_B2 = 0.999
ADAM_EPS = 1e-08
ADAM_WD = 0.01
ADAM_STEP = 10

N_CHIPS = 4
N_DEV = 8
LANES = 128
VMEM_LIMIT = 52 * 1024 * 1024
K_CHUNK = 4096

Q0, K0, V0, U0, G0 = 0, ATTN_WIDTH, ATTN_WIDTH + KV_WIDTH, ATTN_WIDTH + 2 * KV_WIDTH, ATTN_WIDTH + 2 * KV_WIDTH + CONV_WIDTH

NT = (((1,), (1,)), ((), ()))
TN = (((0,), (0,)), ((), ()))
MESH = pl.DeviceIdType.MESH
ANY = pl.BlockSpec(memory_space=pl.ANY)


def _params(sem=None):
    return pltpu.CompilerParams(dimension_semantics=sem, vmem_limit_bytes=VMEM_LIMIT)


def _chunk(n, cap=1024):
    for c in range(cap, 0, -LANES):
        if n % c == 0:
            return c
    raise ValueError(n)


def _tile(t, want):
    return min(t, want)


def _after_spec(after):
    return [] if after is None else [pl.BlockSpec((8, LANES), lambda *_: (0, 0))]


def _after_arg(after):
    return [] if after is None else [after]


def _weight_spec(w_all, l):
    if w_all.ndim == 4:
        return pl.BlockSpec((N_CHIPS, None) + w_all.shape[2:], lambda *_: (0, l, 0, 0))
    return pl.BlockSpec((None,) + w_all.shape[1:], lambda *_: (l, 0, 0))


def _t5_bucket(n):
    n = np.asarray(n)
    max_exact = NUM_BUCKETS // 2
    large = max_exact + (np.log(np.maximum(n, 1) / max_exact) / np.log(MAX_DISTANCE / max_exact)
                         * (NUM_BUCKETS - max_exact)).astype(np.int32)
    large = np.minimum(large, NUM_BUCKETS - 1)
    return np.where(n < max_exact, n, large).astype(np.int32)


def _band_buckets():
    qi = np.arange(BLOCK)[:, None]
    j = np.arange(BLOCK)[None, :]
    return _t5_bucket(np.where(j <= qi, qi - j, qi + BLOCK - j))


def _norm_matmul(x, g, w_all, l, out_dtype, name, after=None):
    T, D = x.shape
    sharded = w_all.ndim == 4
    N = w_all.shape[-1] * (N_CHIPS if sharded else 1)
    TM = _tile(T, 512)
    CH = w_all.shape[-1] if sharded else _chunk(N)

    def body(x_ref, g_ref, w_ref, *rest):
        h_ref, z_ref = rest[-2:]
        xv = x_ref[...]
        r = lax.rsqrt(jnp.mean(xv * xv, axis=-1, keepdims=True) + EPS)
        h = (xv * r * g_ref[...]).astype(BF16)
        h_ref[...] = h
        for c0 in range(0, N, CH):
            wc = w_ref[c0 // CH] if sharded else w_ref[:, c0:c0 + CH]
            z_ref[:, c0:c0 + CH] = jnp.dot(h, wc, preferred_element_type=F32).astype(z_ref.dtype)

    return pl.pallas_call(
        body, name=name, grid=(T // TM,),
        in_specs=[pl.BlockSpec((TM, D), lambda i: (i, 0)),
                  pl.BlockSpec((1, D), lambda i: (0, 0)),
                  _weight_spec(w_all, l)] + _after_spec(after),
        out_specs=[pl.BlockSpec((TM, D), lambda i: (i, 0)),
                   pl.BlockSpec((TM, N), lambda i: (i, 0))],
        out_shape=[jax.ShapeDtypeStruct((T, D), BF16), jax.ShapeDtypeStruct((T, N), out_dtype)],
        compiler_params=_params(("parallel",)),
    )(x, g, w_all, *_after_arg(after))


def _matmul_res(a, w_all, l, res, relu2, name):
    T, K = a.shape
    sharded = w_all.ndim == 4
    N = w_all.shape[-1]
    TM = _tile(T, 512)
    CH = w_all.shape[2] if sharded else _chunk(K, K_CHUNK)

    def body(a_ref, w_ref, res_ref, o_ref):
        acc = res_ref[...]
        for k0 in range(0, K, CH):
            av = a_ref[:, k0:k0 + CH]
            if relu2:
                av = jnp.square(jnp.maximum(av.astype(F32), 0.0)).astype(BF16)
            acc = acc + jnp.dot(av, w_ref[k0 // CH] if sharded else w_ref[k0:k0 + CH, :], preferred_element_type=F32)
        o_ref[...] = acc

    return pl.pallas_call(
        body, name=name, grid=(T // TM,),
        in_specs=[pl.BlockSpec((TM, K), lambda i: (i, 0)),
                  _weight_spec(w_all, l),
                  pl.BlockSpec((TM, N), lambda i: (i, 0))],
        out_specs=pl.BlockSpec((TM, N), lambda i: (i, 0)),
        out_shape=jax.ShapeDtypeStruct((T, N), F32),
        compiler_params=_params(("parallel",)),
    )(a, w_all, res)


def _head_norm(t, g):
    r = lax.rsqrt(jnp.mean(t * t, axis=-1, keepdims=True) + EPS)
    that = t * r
    return that * g, that, r


def _softmax_sink(s, sink):
    m = jnp.maximum(jnp.max(s, axis=-1, keepdims=True), sink)
    p = jnp.exp(s - m)
    es = jnp.exp(sink - m)
    den = jnp.sum(p, axis=-1, keepdims=True) + es
    return p / den, es / den


GROUP_ROWS = GQA_GROUP * BLOCK


def _own_block():
    row = lax.broadcasted_iota(jnp.int32, (GROUP_ROWS, BLOCK), 0)
    col = lax.broadcasted_iota(jnp.int32, (GROUP_ROWS, BLOCK), 1)
    return (row & (BLOCK - 1)) >= col


ATTN_QB = 4
KV_COLS = [slice(k * HEAD_DIM, (k + 1) * HEAD_DIM) for k in range(N_KV_HEADS)]


def _blk(i):
    return pl.ds(i * BLOCK, BLOCK)


def _stack_heads(ref, i, kvh):
    return jnp.concatenate([ref[_blk(i), (kvh * GQA_GROUP + g) * HEAD_DIM:(kvh * GQA_GROUP + g + 1) * HEAD_DIM]
                            for g in range(GQA_GROUP)], axis=0)


def _unstack_heads(ref, i, kvh, val):
    for g in range(GQA_GROUP):
        h = kvh * GQA_GROUP + g
        ref[_blk(i), h * HEAD_DIM:(h + 1) * HEAD_DIM] = val[g * BLOCK:(g + 1) * BLOCK]


def _band_probs(first, own, s_own, s_prev, bias, sink):
    s = jnp.where(own, s_own, s_prev) * SCALE + bias
    if first is not None:
        s = jnp.where(jnp.logical_or(own, jnp.logical_not(first)), s, NEG)
    return _softmax_sink(s, sink)


def _dot_nt(a, b):
    return lax.dot_general(a, b, NT, preferred_element_type=F32)


def _dot_tn(a, b):
    return lax.dot_general(a, b, TN, preferred_element_type=F32)


def _attn_fwd(z, biasc, sink_rows, qg, kg):
    T = z.shape[0]
    nb = T // BLOCK
    qb = min(ATTN_QB, nb)
    TQ = qb * BLOCK
    kb, vb = K0 // KV_WIDTH, V0 // KV_WIDTH
    groups = [(i, k) for i in range(qb) for k in range(N_KV_HEADS)]

    def body(q_ref, kc_ref, kp_ref, vc_ref, vp_ref, b_ref, sk_ref, qg_ref, kg_ref, a_ref):
        n = pl.program_id(0)
        own = _own_block()
        kn, vv = {}, {}
        for k in range(N_KV_HEADS):
            kn[-1, k] = _head_norm(kp_ref[:, KV_COLS[k]], kg_ref[...])[0].astype(BF16)
            vv[-1, k] = vp_ref[:, KV_COLS[k]].astype(BF16)
        for i, k in groups:
            kn[i, k] = _head_norm(kc_ref[_blk(i), KV_COLS[k]], kg_ref[...])[0].astype(BF16)
            vv[i, k] = vc_ref[_blk(i), KV_COLS[k]].astype(BF16)
        qnb = {g: _head_norm(_stack_heads(q_ref, *g), qg_ref[...])[0].astype(BF16) for g in groups}
        s_own = {(i, k): _dot_nt(qnb[i, k], kn[i, k]) for i, k in groups}
        s_prev = {(i, k): _dot_nt(qnb[i, k], kn[i - 1, k]) for i, k in groups}
        p = {(i, k): _band_probs(n == 0 if i == 0 else None, own, s_own[i, k], s_prev[i, k], b_ref[k], sk_ref[k])[0]
             for i, k in groups}
        p_own = {g: jnp.where(own, p[g], 0.0).astype(BF16) for g in groups}
        p_prev = {g: jnp.where(own, 0.0, p[g]).astype(BF16) for g in groups}
        o_own = {(i, k): jnp.dot(p_own[i, k], vv[i, k], preferred_element_type=F32) for i, k in groups}
        o_prev = {(i, k): jnp.dot(p_prev[i, k], vv[i - 1, k], preferred_element_type=F32) for i, k in groups}
        for i, k in groups:
            _unstack_heads(a_ref, i, k, o_own[i, k] + o_prev[i, k])

    prev = lambda n: jnp.maximum(n * qb - 1, 0)
    return pl.pallas_call(
        body, name="attn_fwd", grid=(nb // qb,),
        in_specs=[pl.BlockSpec((TQ, ATTN_WIDTH), lambda n: (n, 0)),
                  pl.BlockSpec((TQ, KV_WIDTH), lambda n: (n, kb)),
                  pl.BlockSpec((BLOCK, KV_WIDTH), lambda n: (prev(n), kb)),
                  pl.BlockSpec((TQ, KV_WIDTH), lambda n: (n, vb)),
                  pl.BlockSpec((BLOCK, KV_WIDTH), lambda n: (prev(n), vb)),
                  pl.BlockSpec((N_KV_HEADS, GROUP_ROWS, BLOCK), lambda n: (0, 0, 0)),
                  pl.BlockSpec((N_KV_HEADS, GROUP_ROWS, 1), lambda n: (0, 0, 0)),
                  pl.BlockSpec((1, HEAD_DIM), lambda n: (0, 0)),
                  pl.BlockSpec((1, HEAD_DIM), lambda n: (0, 0))],
        out_specs=pl.BlockSpec((TQ, ATTN_WIDTH), lambda n: (n, 0)),
        out_shape=jax.ShapeDtypeStruct((T, ATTN_WIDTH), F32),
        compiler_params=_params(("parallel",)),
    )(z, z, z, z, z, biasc, sink_rows, qg, kg)


SHIFTS = 8
CONV_RC = 64


def _shifted_copies(src_ref, dst_ref, total):
    for b in range(SHIFTS):
        rows = (total - b) // SHIFTS * SHIFTS
        for r0 in range(0, rows, CONV_RC):
            nr = min(CONV_RC, rows - r0)
            dst_ref[b, pl.ds(r0, nr), :] = src_ref[pl.ds(r0 + b, nr), :]


def _tap(ref, r0, o):
    return ref[o % SHIFTS, pl.ds(r0 + (o // SHIFTS) * SHIFTS, CONV_RC), :]


def _conv_fwd(z, cw, cb):
    T = z.shape[0]
    TC = _tile(T, 512)
    ub, gb = U0 // CONV_CH, G0 // CONV_CH
    hpt = TC // HALO
    lead = HALO - (CONV_KERNEL - 1)

    def body(u_ref, g_ref, up_ref, gp_ref, w_ref, b_ref, y_ref, hp_ref, hs_ref):
        i = pl.program_id(0)
        hp_ref[pl.ds(0, HALO), :] = jnp.where(i > 0, up_ref[...] * jax.nn.sigmoid(gp_ref[...]), 0.0)
        hp_ref[pl.ds(HALO, TC), :] = u_ref[...] * jax.nn.sigmoid(g_ref[...])
        _shifted_copies(hp_ref, hs_ref, TC + HALO)
        for r0 in range(0, TC, CONV_RC):
            acc = jnp.zeros((CONV_RC, CONV_CH), F32) + b_ref[...]
            for j in range(CONV_KERNEL):
                acc = acc + _tap(hs_ref, r0, lead + j) * w_ref[pl.ds(j, 1), :]
            y_ref[pl.ds(r0, CONV_RC), :] = acc

    prev = lambda i: jnp.maximum(i * hpt - 1, 0)
    return pl.pallas_call(
        body, name="conv_fwd", grid=(T // TC, CONV_WIDTH // CONV_CH),
        in_specs=[pl.BlockSpec((TC, CONV_CH), lambda i, j: (i, ub + j)),
                  pl.BlockSpec((TC, CONV_CH), lambda i, j: (i, gb + j)),
                  pl.BlockSpec((HALO, CONV_CH), lambda i, j: (prev(i), ub + j)),
                  pl.BlockSpec((HALO, CONV_CH), lambda i, j: (prev(i), gb + j)),
                  pl.BlockSpec((CONV_ROWS, CONV_CH), lambda i, j: (0, j)),
                  pl.BlockSpec((1, CONV_CH), lambda i, j: (0, j))],
        out_specs=pl.BlockSpec((TC, CONV_CH), lambda i, j: (i, j)),
        out_shape=jax.ShapeDtypeStruct((T, CONV_WIDTH), F32),
        scratch_shapes=[pltpu.VMEM((TC + HALO, CONV_CH), F32), pltpu.VMEM((SHIFTS, TC + HALO, CONV_CH), F32)],
        compiler_params=_params(("parallel", "parallel")),
    )(z, z, z, z, cw, cb)


def _ln_silu(y, ln_g, ln_b):
    mu = jnp.mean(y, axis=-1, keepdims=True)
    yc = y - mu
    var = jnp.mean(yc * yc, axis=-1, keepdims=True)
    rstd = lax.rsqrt(var + EPS)
    yhat = yc * rstd
    yn = yhat * ln_g + ln_b
    sg = jax.nn.sigmoid(yn)
    return yn * sg, yn, sg, yhat, rstd


def _mix_norm(a, y, ln_g, ln_b, ag, cg):
    T = a.shape[0]
    TM = _tile(T, 512)

    def body(a_ref, y_ref, lg_ref, lb_ref, ag_ref, cg_ref, o_ref):
        av = a_ref[...]
        ra = lax.rsqrt(jnp.mean(av * av, axis=-1, keepdims=True) + EPS)
        o_ref[:, :ATTN_WIDTH] = (av * ra * ag_ref[...]).astype(BF16)
        c, _, _, _, _ = _ln_silu(y_ref[...], lg_ref[...], lb_ref[...])
        rc = lax.rsqrt(jnp.mean(c * c, axis=-1, keepdims=True) + EPS)
        o_ref[:, ATTN_WIDTH:] = (c * rc * cg_ref[...]).astype(BF16)

    vec = pl.BlockSpec((1, CONV_WIDTH), lambda i: (0, 0))
    return pl.pallas_call(
        body, name="mix_norm", grid=(T // TM,),
        in_specs=[pl.BlockSpec((TM, ATTN_WIDTH), lambda i: (i, 0)),
                  pl.BlockSpec((TM, CONV_WIDTH), lambda i: (i, 0)), vec, vec, vec, vec],
        out_specs=pl.BlockSpec((TM, MIX_WIDTH), lambda i: (i, 0)),
        out_shape=jax.ShapeDtypeStruct((T, MIX_WIDTH), BF16),
        compiler_params=_params(("parallel",)),
    )(a, y, ln_g, ln_b, ag, cg)


def _loss_grad(y, tgt):
    T, D = y.shape
    TM = _tile(T, 512)
    nt = T // TM

    def body(y_ref, t_ref, part_ref, dy_ref):
        diff = y_ref[...] - t_ref[...]
        dy_ref[...] = diff / D
        tok = jnp.mean(diff * diff, axis=-1, keepdims=True)
        part_ref[...] = jnp.zeros((1, LANES), F32) + 0.5 * jnp.sum(tok)

    return pl.pallas_call(
        body, name="loss_grad", grid=(nt,),
        in_specs=[pl.BlockSpec((TM, D), lambda i: (i, 0)), pl.BlockSpec((TM, D), lambda i: (i, 0))],
        out_specs=[pl.BlockSpec((None, 1, LANES), lambda i: (i, 0, 0)), pl.BlockSpec((TM, D), lambda i: (i, 0))],
        out_shape=[jax.ShapeDtypeStruct((nt, 1, LANES), F32), jax.ShapeDtypeStruct((T, D), F32)],
        compiler_params=_params(("parallel",)),
    )(y, tgt)


def _dact(g, w_all, l, up, after=None):
    T, N = g.shape
    sharded = w_all.ndim == 4
    K = w_all.shape[2] * N_CHIPS if sharded else w_all.shape[1]
    TM = _tile(T, 512)
    CH = w_all.shape[2] if sharded else _chunk(K)

    def body(g_ref, w_ref, up_ref, *rest):
        o_ref = rest[-1]
        gv = g_ref[...].astype(BF16)
        for k0 in range(0, K, CH):
            da = lax.dot_general(gv, w_ref[k0 // CH] if sharded else w_ref[k0:k0 + CH, :], NT, preferred_element_type=F32)
            upv = up_ref[:, k0:k0 + CH].astype(F32)
            o_ref[:, k0:k0 + CH] = (da * (2.0 * jnp.maximum(upv, 0.0))).astype(BF16)

    return pl.pallas_call(
        body, name="mlp_dact", grid=(T // TM,),
        in_specs=[pl.BlockSpec((TM, N), lambda i: (i, 0)),
                  _weight_spec(w_all, l),
                  pl.BlockSpec((TM, K), lambda i: (i, 0))] + _after_spec(after),
        out_specs=pl.BlockSpec((TM, K), lambda i: (i, 0)),
        out_shape=jax.ShapeDtypeStruct((T, K), BF16),
        compiler_params=_params(("parallel",)),
    )(g, w_all, up, *_after_arg(after))


def _matmul_tn(a, b, relu2, buf, buf_shape, out_block, out_index, tm, tn, name):
    pieces = list(b) if isinstance(b, (list, tuple)) else [b]
    T, M = a.shape
    N = sum(p.shape[1] for p in pieces)
    assert len(pieces) == 1 or tn == N
    starts = np.cumsum([0] + [p.shape[1] for p in pieces])
    TK = _tile(T, 1024)
    nk = T // TK

    def body(*refs):
        a_ref, b_refs = refs[0], refs[1:1 + len(pieces)]
        o_ref = refs[-1]
        k = pl.program_id(2)
        av = a_ref[...]
        if relu2:
            av = jnp.square(jnp.maximum(av.astype(F32), 0.0)).astype(BF16)
        cs = [lax.dot_general(av, b_ref[...].astype(BF16), TN, preferred_element_type=F32) for b_ref in b_refs]

        def put(add):
            for p, c in enumerate(cs):
                if len(cs) == 1:
                    o_ref[...] = c.reshape(o_ref.shape) + (o_ref[...] if add else 0.0)
                else:
                    cols = slice(int(starts[p]), int(starts[p + 1]))
                    o_ref[:, cols] = c + (o_ref[:, cols] if add else 0.0)

        @pl.when(k == 0)
        def _():
            put(False)

        @pl.when(k > 0)
        def _():
            put(True)

    if len(pieces) == 1:
        b_specs = [pl.BlockSpec((TK, tn), lambda i, j, k: (k, j))]
    else:
        b_specs = [pl.BlockSpec((TK, p.shape[1]), lambda i, j, k: (k, 0)) for p in pieces]
    in_specs = [pl.BlockSpec((TK, tm), lambda i, j, k: (k, i))] + b_specs
    args = [a] + pieces
    aliases = {}
    if buf is not None:
        in_specs.append(ANY)
        args.append(buf)
        aliases = {len(args) - 1: 0}
    return pl.pallas_call(
        body, name=name, grid=(M // tm, N // tn, nk),
        in_specs=in_specs,
        out_specs=pl.BlockSpec(out_block, lambda i, j, k: out_index(i, j)),
        out_shape=jax.ShapeDtypeStruct(buf_shape, F32),
        input_output_aliases=aliases,
        compiler_params=_params(("parallel", "parallel", "arbitrary")),
    )(*args)


def _matmul_nt_normbwd(dz, w_all, l, x, gvec, gres, name, after=None):
    pieces = list(dz) if isinstance(dz, (list, tuple)) else [dz]
    T = pieces[0].shape[0]
    K = sum(p.shape[1] for p in pieces)
    starts = np.cumsum([0] + [p.shape[1] for p in pieces])
    D = x.shape[1]
    TM = _tile(T, 512)
    sharded = w_all.ndim == 4

    def body(*refs):
        dz_refs = refs[:len(pieces)]
        w_ref, x_ref, gv_ref, gr_ref = refs[len(pieces):len(pieces) + 4]
        o_ref, dg_ref = refs[-2:]
        i = pl.program_id(0)
        dh = jnp.zeros((TM, D), F32)
        for p, dz_ref in enumerate(dz_refs):
            width = dz_ref.shape[1]
            ch = w_all.shape[-1] if sharded else _chunk(width, K_CHUNK)
            for k0 in range(0, width, ch):
                wk = int(starts[p]) + k0
                wc = w_ref[wk // ch] if sharded else w_ref[:, wk:wk + ch]
                dh = dh + lax.dot_general(dz_ref[:, k0:k0 + ch], wc, NT, preferred_element_type=F32)
        xv = x_ref[...]
        r = lax.rsqrt(jnp.mean(xv * xv, axis=-1, keepdims=True) + EPS)
        xhat = xv * r
        dg = jnp.sum(dh * xhat, axis=0, keepdims=True)

        @pl.when(i == 0)
        def _():
            dg_ref[...] = dg

        @pl.when(i > 0)
        def _():
            dg_ref[...] += dg

        wv = dh * gv_ref[...]
        o_ref[...] = gr_ref[...] + r * (wv - xhat * jnp.mean(wv * xhat, axis=-1, keepdims=True))

    return pl.pallas_call(
        body, name=name, grid=(T // TM,),
        in_specs=[pl.BlockSpec((TM, p.shape[1]), lambda i: (i, 0)) for p in pieces]
        + [_weight_spec(w_all, l),
           pl.BlockSpec((TM, D), lambda i: (i, 0)),
           pl.BlockSpec((1, D), lambda i: (0, 0)),
           pl.BlockSpec((TM, D), lambda i: (i, 0))] + _after_spec(after),
        out_specs=[pl.BlockSpec((TM, D), lambda i: (i, 0)), pl.BlockSpec((1, D), lambda i: (0, 0))],
        out_shape=[jax.ShapeDtypeStruct((T, D), F32), jax.ShapeDtypeStruct((1, D), F32)],
        compiler_params=_params(("arbitrary",)),
    )(*pieces, w_all, x, gvec, gres, *_after_arg(after))


def _mix_bwd(g1, w_all, l, a, y, ln_g, ln_b, ag, cg):
    T, D = g1.shape
    TM = _tile(T, 512)

    def body(g_ref, w_ref, a_ref, y_ref, lg_ref, lb_ref, ag_ref, cg_ref, da_ref, dy_ref, sm_ref):
        i = pl.program_id(0)
        gb = g_ref[...].astype(BF16)
        dm = [lax.dot_general(gb, w_ref[s], NT, preferred_element_type=F32) for s in range(N_CHIPS)]
        dma = jnp.concatenate(dm[:N_CHIPS // 2], axis=1)
        dmc = jnp.concatenate(dm[N_CHIPS // 2:], axis=1)
        av = a_ref[...]
        ra = lax.rsqrt(jnp.mean(av * av, axis=-1, keepdims=True) + EPS)
        ahat = av * ra
        d_ag = jnp.sum(dma * ahat, axis=0, keepdims=True)
        wa = dma * ag_ref[...]
        da_ref[...] = ra * (wa - ahat * jnp.mean(wa * ahat, axis=-1, keepdims=True))

        c, yn, sg, yhat, rstd = _ln_silu(y_ref[...], lg_ref[...], lb_ref[...])
        rc = lax.rsqrt(jnp.mean(c * c, axis=-1, keepdims=True) + EPS)
        chat = c * rc
        d_cg = jnp.sum(dmc * chat, axis=0, keepdims=True)
        wc = dmc * cg_ref[...]
        dc = rc * (wc - chat * jnp.mean(wc * chat, axis=-1, keepdims=True))
        dyn = dc * (sg * (1.0 + yn * (1.0 - sg)))
        d_lg = jnp.sum(dyn * yhat, axis=0, keepdims=True)
        d_lb = jnp.sum(dyn, axis=0, keepdims=True)
        dyh = dyn * lg_ref[...]
        dy = rstd * (dyh - jnp.mean(dyh, axis=-1, keepdims=True) - yhat * jnp.mean(dyh * yhat, axis=-1, keepdims=True))
        dy_ref[...] = dy
        d_cb = jnp.sum(dy, axis=0, keepdims=True)
        sums = jnp.concatenate([d_ag, d_cg, d_lg, d_lb, d_cb, jnp.zeros((3, CONV_WIDTH), F32)], axis=0)

        @pl.when(i == 0)
        def _():
            sm_ref[...] = sums

        @pl.when(i > 0)
        def _():
            sm_ref[...] += sums

    vec = pl.BlockSpec((1, CONV_WIDTH), lambda i: (0, 0))
    return pl.pallas_call(
        body, name="mix_bwd", grid=(T // TM,),
        in_specs=[pl.BlockSpec((TM, D), lambda i: (i, 0)),
                  _weight_spec(w_all, l),
                  pl.BlockSpec((TM, ATTN_WIDTH), lambda i: (i, 0)),
                  pl.BlockSpec((TM, CONV_WIDTH), lambda i: (i, 0)), vec, vec, vec, vec],
        out_specs=[pl.BlockSpec((TM, ATTN_WIDTH), lambda i: (i, 0)),
                   pl.BlockSpec((TM, CONV_WIDTH), lambda i: (i, 0)),
                   pl.BlockSpec((8, CONV_WIDTH), lambda i: (0, 0))],
        out_shape=[jax.ShapeDtypeStruct((T, ATTN_WIDTH), F32), jax.ShapeDtypeStruct((T, CONV_WIDTH), F32),
                   jax.ShapeDtypeStruct((8, CONV_WIDTH), F32)],
        compiler_params=_params(("arbitrary",)),
    )(g1, w_all, a, y, ln_g, ln_b, ag, cg)


def _conv_bwd(dy, z, cw):
    T = z.shape[0]
    TC = _tile(T, 512)
    nt = T // TC
    ub, gb = U0 // CONV_CH, G0 // CONV_CH
    nch = CONV_WIDTH // CONV_CH
    hpt = TC // HALO

    lead = HALO - (CONV_KERNEL - 1)

    def body(dy_ref, dyn_ref, u_ref, g_ref, up_ref, gp_ref, w_ref, du_ref, dg_ref, dw_ref,
             hp_ref, hs_ref, dyp_ref, dys_ref):
        i = pl.program_id(1)
        hp_ref[pl.ds(0, HALO), :] = jnp.where(i > 0, up_ref[...] * jax.nn.sigmoid(gp_ref[...]), 0.0)
        hp_ref[pl.ds(HALO, TC), :] = u_ref[...] * jax.nn.sigmoid(g_ref[...])
        _shifted_copies(hp_ref, hs_ref, TC + HALO)
        dyp_ref[pl.ds(0, TC), :] = dy_ref[...]
        dyp_ref[pl.ds(TC, HALO), :] = jnp.where(i < nt - 1, dyn_ref[...], 0.0)
        _shifted_copies(dyp_ref, dys_ref, TC + HALO)

        @pl.when(i == 0)
        def _():
            dw_ref[...] = jnp.zeros((CONV_ROWS, CONV_CH), F32)

        for r0 in range(0, TC, CONV_RC):
            rows = pl.ds(r0, CONV_RC)
            dh = jnp.zeros((CONV_RC, CONV_CH), F32)
            for j in range(CONV_KERNEL):
                dh = dh + _tap(dys_ref, r0, CONV_KERNEL - 1 - j) * w_ref[pl.ds(j, 1), :]
            uv = u_ref[rows, :]
            sg = jax.nn.sigmoid(g_ref[rows, :])
            du_ref[rows, :] = (dh * sg).astype(BF16)
            dg_ref[rows, :] = (dh * uv * sg * (1.0 - sg)).astype(BF16)
        for j in range(CONV_KERNEL):
            acc = jnp.zeros((SHIFTS, CONV_CH), F32)
            for r0 in range(0, TC, CONV_RC):
                prod = dy_ref[pl.ds(r0, CONV_RC), :] * _tap(hs_ref, r0, lead + j)
                acc = acc + jnp.sum(prod.reshape(CONV_RC // SHIFTS, SHIFTS, CONV_CH), axis=0)
            dw_ref[pl.ds(j, 1), :] += jnp.sum(acc, axis=0, keepdims=True)

    prev = lambda i: jnp.maximum(i * hpt - 1, 0)
    nxt = lambda i: jnp.minimum((i + 1) * hpt, T // HALO - 1)
    return pl.pallas_call(
        body, name="conv_bwd", grid=(nch, nt),
        in_specs=[pl.BlockSpec((TC, CONV_CH), lambda j, i: (i, j)),
                  pl.BlockSpec((HALO, CONV_CH), lambda j, i: (nxt(i), j)),
                  pl.BlockSpec((TC, CONV_CH), lambda j, i: (i, ub + j)),
                  pl.BlockSpec((TC, CONV_CH), lambda j, i: (i, gb + j)),
                  pl.BlockSpec((HALO, CONV_CH), lambda j, i: (prev(i), ub + j)),
                  pl.BlockSpec((HALO, CONV_CH), lambda j, i: (prev(i), gb + j)),
                  pl.BlockSpec((CONV_ROWS, CONV_CH), lambda j, i: (0, j))],
        out_specs=[pl.BlockSpec((TC, CONV_CH), lambda j, i: (i, j)),
                   pl.BlockSpec((TC, CONV_CH), lambda j, i: (i, j)),
                   pl.BlockSpec((CONV_ROWS, CONV_CH), lambda j, i: (0, j))],
        out_shape=[jax.ShapeDtypeStruct((T, CONV_WIDTH), BF16), jax.ShapeDtypeStruct((T, CONV_WIDTH), BF16),
                   jax.ShapeDtypeStruct((CONV_ROWS, CONV_WIDTH), F32)],
        scratch_shapes=[pltpu.VMEM((TC + HALO, CONV_CH), F32), pltpu.VMEM((SHIFTS, TC + HALO, CONV_CH), F32),
                        pltpu.VMEM((TC + HALO, CONV_CH), F32), pltpu.VMEM((SHIFTS, TC + HALO, CONV_CH), F32)],
        compiler_params=_params(("parallel", "arbitrary")),
    )(dy, dy, z, z, z, z, cw)


def _norm_bwd(d, that, r, g):
    w = d * g
    return r * (w - that * jnp.mean(w * that, axis=-1, keepdims=True)), jnp.sum(d * that, axis=0, keepdims=True)


def _attn_bwd(z, da, biasc, sink_rows, qg, kg):
    T = z.shape[0]
    nb = T // BLOCK
    qb = min(ATTN_QB, nb)
    TQ = qb * BLOCK
    ns = nb // qb
    kb, vb = K0 // KV_WIDTH, V0 // KV_WIDTH
    groups = [(i, k) for i in range(qb) for k in range(N_KV_HEADS)]

    def body(q_ref, kc_ref, kp_ref, vc_ref, vp_ref, da_ref, b_ref, sk_ref, qg_ref, kg_ref,
             dq_ref, dkv_ref, db_ref, sm_ref, ck_ref, cv_ref, pk_ref, pv_ref, nk_ref, nv_ref):
        n = pl.program_id(0)
        lane = lax.broadcasted_iota(jnp.int32, (1, LANES), 1)

        @pl.when(n == 0)
        def _():
            db_ref[...] = jnp.zeros(db_ref.shape, F32)
            sm_ref[...] = jnp.zeros(sm_ref.shape, F32)
            ck_ref[...] = jnp.zeros(ck_ref.shape, F32)
            cv_ref[...] = jnp.zeros(cv_ref.shape, F32)

        pk_ref[...] = jnp.zeros(pk_ref.shape, F32)
        pv_ref[...] = jnp.zeros(pv_ref.shape, F32)

        @pl.when(n < ns)
        def _():
            own = _own_block()
            knorm, kn, vv = {}, {}, {}
            for k in range(N_KV_HEADS):
                kn[-1, k] = _head_norm(kp_ref[:, KV_COLS[k]], kg_ref[...])[0].astype(BF16)
                vv[-1, k] = vp_ref[:, KV_COLS[k]].astype(BF16)
            for i, k in groups:
                knorm[i, k] = _head_norm(kc_ref[_blk(i), KV_COLS[k]], kg_ref[...])
                kn[i, k] = knorm[i, k][0].astype(BF16)
                vv[i, k] = vc_ref[_blk(i), KV_COLS[k]].astype(BF16)
            qnorm = {g: _head_norm(_stack_heads(q_ref, *g), qg_ref[...]) for g in groups}
            qnb = {g: qnorm[g][0].astype(BF16) for g in groups}
            dob = {g: _stack_heads(da_ref, *g).astype(BF16) for g in groups}
            s_own = {(i, k): _dot_nt(qnb[i, k], kn[i, k]) for i, k in groups}
            s_prev = {(i, k): _dot_nt(qnb[i, k], kn[i - 1, k]) for i, k in groups}
            dp_own = {(i, k): _dot_nt(dob[i, k], vv[i, k]) for i, k in groups}
            dp_prev = {(i, k): _dot_nt(dob[i, k], vv[i - 1, k]) for i, k in groups}
            probs = {(i, k): _band_probs(n == 0 if i == 0 else None, own, s_own[i, k], s_prev[i, k], b_ref[k], sk_ref[k])
                     for i, k in groups}
            ds_own, ds_prev, p_own, p_prev = {}, {}, {}, {}
            dsk = jnp.zeros((1, LANES), F32)
            dbias = [jnp.zeros((GROUP_ROWS, BLOCK), F32) for _ in range(N_KV_HEADS)]
            for i, k in groups:
                p, psink = probs[i, k]
                dp = jnp.where(own, dp_own[i, k], dp_prev[i, k])
                delta = jnp.sum(p * dp, axis=-1, keepdims=True)
                ds = p * (dp - delta)
                dbias[k] = dbias[k] + ds
                dsink = psink * delta
                for g in range(GQA_GROUP):
                    dsk = dsk + jnp.where(lane == k * GQA_GROUP + g, -jnp.sum(dsink[g * BLOCK:(g + 1) * BLOCK]), 0.0)
                ds_own[i, k] = jnp.where(own, ds, 0.0).astype(BF16)
                ds_prev[i, k] = jnp.where(own, 0.0, ds).astype(BF16)
                p_own[i, k] = jnp.where(own, p, 0.0).astype(BF16)
                p_prev[i, k] = jnp.where(own, 0.0, p).astype(BF16)
            for k in range(N_KV_HEADS):
                db_ref[k] += dbias[k]
            dqn_own = {(i, k): jnp.dot(ds_own[i, k], kn[i, k], preferred_element_type=F32) for i, k in groups}
            dqn_prev = {(i, k): jnp.dot(ds_prev[i, k], kn[i - 1, k], preferred_element_type=F32) for i, k in groups}
            dk_own = {g: _dot_tn(ds_own[g], qnb[g]) * SCALE for g in groups}
            dk_prev = {g: _dot_tn(ds_prev[g], qnb[g]) * SCALE for g in groups}
            dv_own = {g: _dot_tn(p_own[g], dob[g]) for g in groups}
            dv_prev = {g: _dot_tn(p_prev[g], dob[g]) for g in groups}
            dqg = jnp.zeros((1, HEAD_DIM), F32)
            dkg = jnp.zeros((1, HEAD_DIM), F32)
            for i, k in groups:
                _, qhat, rq = qnorm[i, k]
                dq, dg = _norm_bwd((dqn_own[i, k] + dqn_prev[i, k]) * SCALE, qhat, rq, qg_ref[...])
                dqg = dqg + dg
                _unstack_heads(dq_ref, i, k, dq.astype(BF16))
                if i == 0:
                    pk_ref[:, KV_COLS[k]] = dk_prev[i, k]
                    pv_ref[:, KV_COLS[k]] = dv_prev[i, k]
                if i == qb - 1:
                    nk_ref[:, KV_COLS[k]] = dk_own[i, k]
                    nv_ref[:, KV_COLS[k]] = dv_own[i, k]
                else:
                    _, khat, rk = knorm[i, k]
                    dk, dg = _norm_bwd(dk_own[i, k] + dk_prev[i + 1, k], khat, rk, kg_ref[...])
                    dkg = dkg + dg
                    dkv_ref[_blk(i + 1), KV_COLS[k]] = dk.astype(BF16)
                    dkv_ref[_blk(i + 1), pl.ds(KV_WIDTH + k * HEAD_DIM, HEAD_DIM)] = (dv_own[i, k] + dv_prev[i + 1, k]).astype(BF16)
            sm_ref[pl.ds(0, 1), pl.ds(0, HEAD_DIM)] += dqg
            sm_ref[pl.ds(1, 1), pl.ds(0, HEAD_DIM)] += dkg
            sm_ref[pl.ds(2, 1), :] += dsk

        @pl.when(n >= 1)
        def _():
            dkg = jnp.zeros((1, HEAD_DIM), F32)
            for k in range(N_KV_HEADS):
                _, khat, rk = _head_norm(kp_ref[:, KV_COLS[k]], kg_ref[...])
                dk, dg = _norm_bwd(ck_ref[:, KV_COLS[k]] + pk_ref[:, KV_COLS[k]], khat, rk, kg_ref[...])
                dkg = dkg + dg
                dkv_ref[_blk(0), KV_COLS[k]] = dk.astype(BF16)
            dkv_ref[_blk(0), pl.ds(KV_WIDTH, KV_WIDTH)] = (cv_ref[...] + pv_ref[...]).astype(BF16)
            sm_ref[pl.ds(1, 1), pl.ds(0, HEAD_DIM)] += dkg

        ck_ref[...] = nk_ref[...]
        cv_ref[...] = nv_ref[...]

    cur = lambda n: jnp.minimum(n, ns - 1)
    prev = lambda n: jnp.maximum(n * qb - 1, 0)
    carry = pltpu.VMEM((BLOCK, KV_WIDTH), F32)
    return pl.pallas_call(
        body, name="attn_bwd", grid=(ns + 1,),
        in_specs=[pl.BlockSpec((TQ, ATTN_WIDTH), lambda n: (cur(n), 0)),
                  pl.BlockSpec((TQ, KV_WIDTH), lambda n: (cur(n), kb)),
                  pl.BlockSpec((BLOCK, KV_WIDTH), lambda n: (prev(n), kb)),
                  pl.BlockSpec((TQ, KV_WIDTH), lambda n: (cur(n), vb)),
                  pl.BlockSpec((BLOCK, KV_WIDTH), lambda n: (prev(n), vb)),
                  pl.BlockSpec((TQ, ATTN_WIDTH), lambda n: (cur(n), 0)),
                  pl.BlockSpec((N_KV_HEADS, GROUP_ROWS, BLOCK), lambda n: (0, 0, 0)),
                  pl.BlockSpec((N_KV_HEADS, GROUP_ROWS, 1), lambda n: (0, 0, 0)),
                  pl.BlockSpec((1, HEAD_DIM), lambda n: (0, 0)),
                  pl.BlockSpec((1, HEAD_DIM), lambda n: (0, 0))],
        out_specs=[pl.BlockSpec((TQ, ATTN_WIDTH), lambda n: (cur(n), 0)),
                   pl.BlockSpec((TQ, 2 * KV_WIDTH), lambda n: (n, 0)),
                   pl.BlockSpec((N_KV_HEADS, GROUP_ROWS, BLOCK), lambda n: (0, 0, 0)),
                   pl.BlockSpec((8, LANES), lambda n: (0, 0))],
        out_shape=[jax.ShapeDtypeStruct((T, ATTN_WIDTH), BF16), jax.ShapeDtypeStruct(((ns + 1) * TQ, 2 * KV_WIDTH), BF16),
                   jax.ShapeDtypeStruct((N_KV_HEADS, GROUP_ROWS, BLOCK), F32), jax.ShapeDtypeStruct((8, LANES), F32)],
        scratch_shapes=[carry] * 6,
        compiler_params=_params(("arbitrary",)),
    )(z, z, z, z, z, da, biasc, sink_rows, qg, kg)


def _bucket_reduce(dbias, onehot_t):
    def body(d_ref, oh_ref, o_ref):
        d = d_ref[...]
        hi = d.astype(BF16)
        r1 = d - hi.astype(F32)
        mid = r1.astype(BF16)
        lo = (r1 - mid.astype(F32)).astype(BF16)
        oh = oh_ref[...]
        acc = lax.dot_general(lo, oh, NT, preferred_element_type=F32)
        acc = acc + lax.dot_general(mid, oh, NT, preferred_element_type=F32)
        o_ref[...] = acc + lax.dot_general(hi, oh, NT, preferred_element_type=F32)

    return pl.pallas_call(
        body, name="bucket_reduce",
        out_shape=jax.ShapeDtypeStruct((N_HEADS, LANES), F32),
        compiler_params=_params(),
    )(dbias, onehot_t)


def _adamw(w, g, m, v, name):
    R, C = w.shape
    TR = _tile(R, 512)

    def body(w_ref, g_ref, m_ref, v_ref, d_ref, nm_ref, nv_ref):
        gv = g_ref[...]
        mn = ADAM_B1 * m_ref[...] + (1.0 - ADAM_B1) * gv
        vn = ADAM_B2 * v_ref[...] + (1.0 - ADAM_B2) * jnp.square(gv)
        m_hat = mn / (1.0 - ADAM_B1 ** ADAM_STEP)
        v_hat = vn / (1.0 - ADAM_B2 ** ADAM_STEP)
        d_ref[...] = -ADAM_LR * (m_hat / (jnp.sqrt(v_hat) + ADAM_EPS) + ADAM_WD * w_ref[...])
        nm_ref[...] = mn
        nv_ref[...] = vn

    spec = pl.BlockSpec((TR, C), lambda i: (i, 0))
    shp = jax.ShapeDtypeStruct((R, C), F32)
    return pl.pallas_call(
        body, name=name, grid=(R // TR,),
        in_specs=[spec] * 4, out_specs=[spec] * 3, out_shape=[shp] * 3,
        compiler_params=_params(("parallel",)),
    )(w, g, m, v)


def _place():
    return lax.axis_index("x"), lax.axis_index("y"), lax.axis_index("c")


def _other_chips(x, y):
    return [(1 - x, y), (x, 1 - y), (1 - x, 1 - y)]


def _remote(src, dst, send_sem, recv_sem, dev):
    return pltpu.make_async_remote_copy(src_ref=src, dst_ref=dst, send_sem=send_sem, recv_sem=recv_sem,
                                        device_id=dev, device_id_type=MESH)


def _gather_shards(bufs):
    nbuf = len(bufs)

    def body(*refs):
        ins, outs = refs[:nbuf], refs[nbuf:2 * nbuf]
        send_sems, recv_sems = refs[2 * nbuf:]
        x, y, c = _place()
        me = 2 * x + y
        sib = (x, y, 1 - c)
        chips = _other_chips(x, y)
        started = []
        for b in range(nbuf):
            hh = bufs[b].shape[0] // 2
            for j, (cx, cy) in enumerate(chips):
                k = 6 * b + j
                cp = _remote(ins[b].at[pl.ds(c * hh, hh), :], outs[b].at[me, pl.ds(c * hh, hh), :],
                             send_sems.at[k], recv_sems.at[k], (cx, cy, c))
                cp.start()
                started.append(cp)
        for b in range(nbuf):
            hh = bufs[b].shape[0] // 2
            for j, (cx, cy) in enumerate(chips):
                rows = outs[b].at[2 * cx + cy, pl.ds(c * hh, hh), :]
                _remote(rows, rows, send_sems.at[6 * b + j], recv_sems.at[6 * b + j], sib).wait_recv()
                k = 6 * b + 3 + j
                cp = _remote(rows, rows, send_sems.at[k], recv_sems.at[k], sib)
                cp.start()
                started.append(cp)
        for b in range(nbuf):
            hh = bufs[b].shape[0] // 2
            for j, (cx, cy) in enumerate(chips):
                rows = outs[b].at[2 * cx + cy, pl.ds((1 - c) * hh, hh), :]
                k = 6 * b + 3 + j
                _remote(rows, rows, send_sems.at[k], recv_sems.at[k], sib).wait_recv()
        for cp in started:
            cp.wait_send()

    return pl.pallas_call(
        body, name="gather_weights",
        in_specs=[ANY] * nbuf, out_specs=[ANY] * nbuf,
        out_shape=[jax.ShapeDtypeStruct((N_CHIPS,) + b.shape, b.dtype) for b in bufs],
        scratch_shapes=[pltpu.SemaphoreType.DMA((6 * nbuf,)), pltpu.SemaphoreType.DMA((6 * nbuf,))],
        compiler_params=pltpu.CompilerParams(has_side_effects=True),
    )(*bufs)


HBM = pl.BlockSpec(memory_space=pltpu.HBM)
SEM = pl.BlockSpec(memory_space=pltpu.SEMAPHORE)
DATAFLOW = pltpu.SideEffectType.DATAFLOW_SIDE_EFFECTING


def _gather_plan(shapes):
    def plan(srcs, lands):
        x, y, c = _place()
        out = []
        for b, shp in enumerate(shapes):
            hh = shp[0] // 2
            for cx, cy in _other_chips(x, y):
                out.append((srcs[b].at[pl.ds(c * hh, hh), :], lands[b].at[2 * x + y, pl.ds(c * hh, hh), :], (cx, cy, c)))
        return out
    return plan


def _exchange_plan(shapes):
    def plan(srcs, lands):
        x, y, c = _place()
        return [(srcs[b].at[j], lands[b].at[j], (cx, cy, c))
                for b in range(len(shapes)) for j, (cx, cy) in enumerate(_other_chips(x, y))]
    return plan


def _swap_plan(shapes):
    def plan(srcs, lands):
        x, y, c = _place()
        return [(srcs[b].at[:, pl.ds((1 - c) * (shp[1] // 2), shp[1] // 2), :], lands[b], (x, y, 1 - c))
                for b, shp in enumerate(shapes)]
    return plan


def _start_copies(name, srcs, land_shapes, plan, per_buffer=N_CHIPS - 1):
    n = len(srcs)
    ncopy = per_buffer * n

    def body(*refs):
        ins, lands = refs[:n], refs[n:2 * n]
        send_sems, recv_sems, token = refs[2 * n], refs[2 * n + 1], refs[-1]
        for k, (src, dst, dev) in enumerate(plan(ins, lands)):
            _remote(src, dst, send_sems.at[k], recv_sems.at[k], dev).start()
        token[...] = jnp.zeros_like(token)

    hbm = lambda a: pltpu.with_memory_space_constraint(a, pltpu.HBM)
    lands = [lax.empty(s, a.dtype) for s, a in zip(land_shapes, srcs)]
    outs = pl.pallas_call(
        body, name=name,
        out_shape=(pltpu.SemaphoreType.DMA((ncopy,)), pltpu.SemaphoreType.DMA((ncopy,)),
                   *[pltpu.HBM(a.shape, a.dtype) for a in srcs], *[pltpu.HBM(a.shape, a.dtype) for a in lands],
                   jax.ShapeDtypeStruct((8, LANES), F32)),
        in_specs=[HBM] * (2 * n),
        out_specs=(SEM, SEM, *([HBM] * (2 * n)), pl.BlockSpec(memory_space=pltpu.VMEM)),
        input_output_aliases={i: 2 + i for i in range(2 * n)},
        compiler_params=pltpu.CompilerParams(has_side_effects=DATAFLOW),
    )(*[hbm(a) for a in srcs], *[hbm(a) for a in lands])
    return outs[0], outs[1], list(outs[2:2 + n]), list(outs[2 + n:2 + 2 * n]), outs[-1]


def _wait_copies(name, send_sems, recv_sems, srcs, lands, plan, after):
    n = len(srcs)

    def body(*refs):
        ins, lnds = refs[:n], refs[n:2 * n]
        ssem, rsem = refs[2 * n], refs[2 * n + 1]
        for k, (src, dst, dev) in enumerate(plan(ins, lnds)):
            cp = _remote(src, dst, ssem.at[k], rsem.at[k], dev)
            cp.wait_send()
            cp.wait_recv()

    outs = pl.pallas_call(
        body, name=name,
        out_shape=(*[pltpu.HBM(a.shape, a.dtype) for a in srcs], *[pltpu.HBM(a.shape, a.dtype) for a in lands]),
        in_specs=[HBM] * (2 * n) + [SEM, SEM, ANY],
        out_specs=tuple([HBM] * (2 * n)),
        input_output_aliases={i: i for i in range(2 * n)},
        compiler_params=pltpu.CompilerParams(has_side_effects=DATAFLOW),
    )(*srcs, *lands, send_sems, recv_sems, after)
    return list(outs[:n]), list(outs[n:])


def _forward_halves(bufs):
    nbuf = len(bufs)

    def body(*refs):
        outs = refs[nbuf:2 * nbuf]
        send_sems, recv_sems = refs[2 * nbuf:]
        x, y, c = _place()
        sib = (x, y, 1 - c)
        cps = []
        for b in range(nbuf):
            hh = bufs[b].shape[1] // 2
            for j, (cx, cy) in enumerate(_other_chips(x, y)):
                rows = outs[b].at[2 * cx + cy, pl.ds(c * hh, hh), :]
                cp = _remote(rows, rows, send_sems.at[3 * b + j], recv_sems.at[3 * b + j], sib)
                cp.start()
                cps.append(cp)
        for b in range(nbuf):
            hh = bufs[b].shape[1] // 2
            for j, (cx, cy) in enumerate(_other_chips(x, y)):
                rows = outs[b].at[2 * cx + cy, pl.ds((1 - c) * hh, hh), :]
                _remote(rows, rows, send_sems.at[3 * b + j], recv_sems.at[3 * b + j], sib).wait_recv()
        for cp in cps:
            cp.wait_send()

    return pl.pallas_call(
        body, name="gather_forward_halves",
        in_specs=[ANY] * nbuf, out_specs=[ANY] * nbuf,
        out_shape=[jax.ShapeDtypeStruct(b.shape, b.dtype) for b in bufs],
        input_output_aliases={b: b for b in range(nbuf)},
        scratch_shapes=[pltpu.SemaphoreType.DMA((3 * nbuf,)), pltpu.SemaphoreType.DMA((3 * nbuf,))],
        compiler_params=pltpu.CompilerParams(has_side_effects=True),
    )(*bufs)


def _swap_halves(bufs):
    nbuf = len(bufs)

    def body(*refs):
        ins, outs = refs[:nbuf], refs[nbuf:2 * nbuf]
        send_sems, recv_sems = refs[2 * nbuf:]
        x, y, c = _place()
        sib = (x, y, 1 - c)
        cps = []
        for b in range(nbuf):
            hh = bufs[b].shape[1] // 2
            cp = _remote(ins[b].at[:, pl.ds((1 - c) * hh, hh), :], outs[b], send_sems.at[b], recv_sems.at[b], sib)
            cp.start()
            cps.append(cp)
        for cp in cps:
            cp.wait()

    return pl.pallas_call(
        body, name="grad_swap_halves",
        in_specs=[ANY] * nbuf, out_specs=[ANY] * nbuf,
        out_shape=[jax.ShapeDtypeStruct((N_CHIPS, b.shape[1] // 2, b.shape[2]), b.dtype) for b in bufs],
        scratch_shapes=[pltpu.SemaphoreType.DMA((nbuf,)), pltpu.SemaphoreType.DMA((nbuf,))],
        compiler_params=pltpu.CompilerParams(has_side_effects=True),
    )(*bufs)


def _chip_sum(g, got, sel, out_dtype, name):
    _, R, C = g.shape
    hh = R // 2
    TR = _tile(hh, 512)
    nslot = sel[1].shape[0]

    def body(off_ref, sh_ref, g_ref, r_ref, o_ref):
        o_ref[...] = (g_ref[...] + r_ref[...]).astype(out_dtype)

    return pl.pallas_call(
        body, name=name,
        grid_spec=pltpu.PrefetchScalarGridSpec(
            num_scalar_prefetch=2, grid=(nslot, hh // TR),
            in_specs=[pl.BlockSpec((None, TR, C), lambda s, i, off, sh: (sh[s], off[0] + i, 0)),
                      pl.BlockSpec((None, TR, C), lambda s, i, off, sh: (sh[s], i, 0))],
            out_specs=pl.BlockSpec((None, TR, C), lambda s, i, off, sh: (s, i, 0))),
        out_shape=jax.ShapeDtypeStruct((nslot, hh, C), out_dtype),
        compiler_params=_params(("parallel", "parallel")),
    )(sel[0], sel[1], g, got)


def _exchange_chips(bufs):
    nbuf = len(bufs)

    def body(*refs):
        ins, outs = refs[:nbuf], refs[nbuf:2 * nbuf]
        send_sems, recv_sems = refs[2 * nbuf:]
        x, y, c = _place()
        cps = []
        for b in range(nbuf):
            for j, (cx, cy) in enumerate(_other_chips(x, y)):
                k = 3 * b + j
                cp = _remote(ins[b].at[j], outs[b].at[j], send_sems.at[k], recv_sems.at[k], (cx, cy, c))
                cp.start()
                cps.append(cp)
        for cp in cps:
            cp.wait()

    return pl.pallas_call(
        body, name="grad_exchange_chips",
        in_specs=[ANY] * nbuf, out_specs=[ANY] * nbuf,
        out_shape=[jax.ShapeDtypeStruct(b.shape, b.dtype) for b in bufs],
        scratch_shapes=[pltpu.SemaphoreType.DMA((3 * nbuf,)), pltpu.SemaphoreType.DMA((3 * nbuf,))],
        compiler_params=pltpu.CompilerParams(has_side_effects=True),
    )(*bufs)


def _shard_sum(own, got, off, tr, full, rows, name):
    _, hh, C = own.shape

    def body(off_ref, o_ref, r_ref, *rest):
        acc = o_ref[...]
        for j in range(N_CHIPS - 1):
            acc = acc + r_ref[j].astype(F32)
        rest[-1][...] = acc

    in_specs = [pl.BlockSpec((None, tr, C), lambda i, off: (0, i, 0)),
                pl.BlockSpec((N_CHIPS - 1, tr, C), lambda i, off: (0, i, 0))]
    args = [off, own, got]
    aliases = {}
    if full is not None:
        in_specs.append(ANY)
        args.append(full)
        aliases = {3: 0}
    return pl.pallas_call(
        body, name=name,
        grid_spec=pltpu.PrefetchScalarGridSpec(
            num_scalar_prefetch=1, grid=(hh // tr,), in_specs=in_specs,
            out_specs=pl.BlockSpec((tr, C), lambda i, off: (off[0] + i, 0))),
        out_shape=jax.ShapeDtypeStruct((rows, C), F32),
        input_output_aliases=aliases,
        compiler_params=_params(("parallel",)),
    )(*args)


def _join_halves(bufs, spans):
    nbuf = len(bufs)
    ncopy = nbuf * len(spans)

    def body(*refs):
        outs = refs[nbuf:2 * nbuf]
        send_sems, recv_sems = refs[2 * nbuf:]
        x, y, c = _place()
        sib = (x, y, 1 - c)
        cps = []
        for b in range(nbuf):
            for s, (r0, nr) in enumerate(spans[b]):
                k = b * len(spans[b]) + s
                rows = outs[b].at[pl.ds(r0 + c * (nr // 2), nr // 2), :]
                cp = _remote(rows, rows, send_sems.at[k], recv_sems.at[k], sib)
                cp.start()
                cps.append(cp)
        for b in range(nbuf):
            for s, (r0, nr) in enumerate(spans[b]):
                k = b * len(spans[b]) + s
                theirs = outs[b].at[pl.ds(r0 + (1 - c) * (nr // 2), nr // 2), :]
                _remote(theirs, theirs, send_sems.at[k], recv_sems.at[k], sib).wait_recv()
        for cp in cps:
            cp.wait_send()

    return pl.pallas_call(
        body, name="grad_join_halves",
        in_specs=[ANY] * nbuf, out_specs=[ANY] * nbuf,
        out_shape=[jax.ShapeDtypeStruct(b.shape, b.dtype) for b in bufs],
        input_output_aliases={b: b for b in range(nbuf)},
        scratch_shapes=[pltpu.SemaphoreType.DMA((ncopy,)), pltpu.SemaphoreType.DMA((ncopy,))],
        compiler_params=pltpu.CompilerParams(has_side_effects=True),
    )(*bufs)


def _sum_devices(part):
    R = part.shape[0]

    def body(p_ref, o_ref, all_ref, send_sems, recv_sems):
        x, y, c = _place()
        me = 4 * x + 2 * y + c
        all_ref[me] = p_ref[...]
        cps = []
        for k in range(1, N_DEV):
            px, py, pc = x ^ (k >> 2), y ^ ((k >> 1) & 1), c ^ (k & 1)
            cp = _remote(p_ref, all_ref.at[me], send_sems.at[k - 1], recv_sems.at[k - 1], (px, py, pc))
            cp.start()
            cps.append(cp)
        for k in range(1, N_DEV):
            peer = me ^ k
            _remote(p_ref, all_ref.at[peer], send_sems.at[k - 1], recv_sems.at[k - 1], (x, y, c)).wait_recv()
        for cp in cps:
            cp.wait_send()
        acc = all_ref[0]
        for d in range(1, N_DEV):
            acc = acc + all_ref[d]
        o_ref[...] = acc

    return pl.pallas_call(
        body, name="sum_small_grads",
        in_specs=[pl.BlockSpec(memory_space=pltpu.VMEM)],
        out_specs=pl.BlockSpec(memory_space=pltpu.VMEM),
        out_shape=jax.ShapeDtypeStruct((R, LANES), F32),
        scratch_shapes=[pltpu.VMEM((N_DEV, R, LANES), F32),
                        pltpu.SemaphoreType.DMA((N_DEV - 1,)), pltpu.SemaphoreType.DMA((N_DEV - 1,))],
        compiler_params=pltpu.CompilerParams(has_side_effects=True, vmem_limit_bytes=VMEM_LIMIT),
    )(part)


def _pack(parts):
    flat = jnp.concatenate([p.reshape(-1).astype(F32) for p in parts])
    n = flat.shape[0]
    rows = -(-n // LANES)
    rows = -(-rows // 8) * 8
    return jnp.pad(flat, (0, rows * LANES - n)).reshape(rows, LANES)


def _unpack(packed, shapes):
    flat = packed.reshape(-1)
    out, off = [], 0
    for s in shapes:
        n = int(np.prod(s))
        out.append(flat[off:off + n].reshape(s))
        off += n
    return out


def kernel(x, rel_bias, norm_mix_g, w_in, q_norm_g, k_norm_g, sinks, conv_w, conv_b, conv_ln_g, conv_ln_b, attn_out_g, conv_out_g, w_out, norm_mlp_g, w_mlp_up, w_mlp_down, loss_target, m_rel_bias, m_norm_mix_g, m_w_in, m_q_norm_g, m_k_norm_g, m_sinks, m_conv_w, m_conv_b, m_conv_ln_g, m_conv_ln_b, m_attn_out_g, m_conv_out_g, m_w_out, m_norm_mlp_g, m_w_mlp_up, m_w_mlp_down, v_rel_bias, v_norm_mix_g, v_w_in, v_q_norm_g, v_k_norm_g, v_sinks, v_conv_w, v_conv_b, v_conv_ln_g, v_conv_ln_b, v_attn_out_g, v_conv_out_g, v_w_out, v_norm_mlp_g, v_w_mlp_up, v_w_mlp_down):
    T = x.shape[1]
    L = DEPTH
    xi, yi, ci = _place()
    shard = 2 * xi + yi
    in_sh = IN_WIDTH // N_CHIPS
    out_sh = MIX_WIDTH // N_CHIPS
    ff_sh = D_FF // N_CHIPS
    cv_sh = CONV_WIDTH // N_CHIPS

    MIXING, MLP = ("w_in", "w_out", "conv_w"), ("w_mlp_up", "w_mlp_down")

    def my_shard(name, lo, hi):
        n = hi - lo
        if name == "w_in":
            return w_in[lo:hi].astype(BF16).reshape(n * D_MODEL, in_sh)
        if name == "w_out":
            return w_out[lo:hi].astype(BF16).reshape(n * out_sh, D_MODEL)
        if name == "w_mlp_up":
            return w_mlp_up[lo:hi].astype(BF16).reshape(n * D_MODEL, ff_sh)
        if name == "w_mlp_down":
            return w_mlp_down[lo:hi].astype(BF16).reshape(n * ff_sh, D_MODEL)
        cw_pad = jnp.pad(conv_w[lo:hi], ((0, 0), (0, CONV_ROWS - CONV_KERNEL), (0, 0)))
        return cw_pad.reshape(n * CONV_ROWS, cv_sh)

    def whole_weight(name, gathered, own, n):
        g = lax.dynamic_update_slice(gathered, own[None], (shard, 0, 0))
        if name == "w_in":
            return g.reshape(N_CHIPS, n, D_MODEL, in_sh).transpose(1, 2, 0, 3).reshape(n, D_MODEL, IN_WIDTH)
        if name in ("w_out", "w_mlp_up", "w_mlp_down"):
            return g.reshape(N_CHIPS, n, g.shape[1] // n, g.shape[2])
        return g.reshape(N_CHIPS, n, CONV_ROWS, cv_sh).transpose(1, 2, 0, 3).reshape(n, CONV_ROWS, CONV_WIDTH)

    weight_of = {}

    def provide(entries, gathered, mine):
        for (name, lo, hi), g, own in zip(entries, gathered, mine):
            whole = whole_weight(name, g, own, hi - lo)
            for l in range(lo, hi):
                weight_of[name, l] = (whole, l - lo)

    def gather_behind(tag, entries, first):
        mine = [my_shard(*e) for e in entries]
        mine[0], _ = lax.optimization_barrier((mine[0], first))
        plan = _gather_plan([m.shape for m in mine])
        send_sems, recv_sems, srcs, lands, token = _start_copies(
            "gather_" + tag + "_start", mine, [(N_CHIPS,) + m.shape for m in mine], plan)

        def finish(after):
            _, got = _wait_copies("gather_" + tag + "_wait", send_sems, recv_sems, srcs, lands, plan, after)
            provide(entries, _forward_halves(got), mine)
        return token, finish

    now = [(name, 0, 1) for name in MIXING]
    early = [(name, 0, 2) for name in MLP] + [(name, 1, 2) for name in MIXING]
    late = [(name, 2, L) for name in MIXING + MLP]
    mine0 = [my_shard(*e) for e in now]
    got0 = _gather_shards(mine0)
    provide(now, got0, mine0)
    token_early, finish_early = gather_behind("early", early, got0[0])

    bucket = _band_buckets()
    bk = jnp.asarray(bucket)[None]
    biasc = jnp.zeros((N_HEADS, BLOCK, BLOCK), F32)
    for b in range(NUM_BUCKETS):
        biasc = jnp.where(bk == b, rel_bias[b][:, None, None], biasc)
    biasc = biasc.reshape(N_KV_HEADS, GROUP_ROWS, BLOCK)
    onehot_t = np.zeros((LANES, BLOCK * BLOCK), np.float32)
    onehot_t[bucket.reshape(-1), np.arange(BLOCK * BLOCK)] = 1.0
    onehot_t = jnp.asarray(onehot_t, dtype=BF16)
    sink_rows = lambda l: jnp.repeat(sinks[l], BLOCK).reshape(N_KV_HEADS, GROUP_ROWS, 1)

    row = lambda a, l: a[l][None, :]

    xs = x.reshape(T, D_MODEL)
    saved = []
    token_late = None
    for l in range(L):
        if l == 2:
            finish_late(xs)
        h, z = _norm_matmul(xs, row(norm_mix_g, l), *weight_of["w_in", l], F32, "mix_in_proj", token_early if l == 0 else None)
        a = _attn_fwd(z, biasc, sink_rows(l), row(q_norm_g, l), row(k_norm_g, l))
        cw, cl = weight_of["conv_w", l]
        yc = _conv_fwd(z, cw[cl], row(conv_b, l))
        mix = _mix_norm(a, yc, row(conv_ln_g, l), row(conv_ln_b, l), row(attn_out_g, l), row(conv_out_g, l))
        x1 = _matmul_res(mix, *weight_of["w_out", l], xs, False, "mix_out_proj")
        if l == 0:
            finish_early(x1)
            token_late, finish_late = gather_behind("late", late, weight_of["w_mlp_up", 0][0])
        h2, up = _norm_matmul(x1, row(norm_mlp_g, l), *weight_of["w_mlp_up", l], BF16, "mlp_up_proj",
                              token_late if l == 0 else None)
        x2 = _matmul_res(up, *weight_of["w_mlp_down", l], x1, True, "mlp_down_proj")
        saved.append((xs, h, z, a, yc, mix, x1, h2, up))
        xs = x2

    loss_parts, g = _loss_grad(xs, loss_target.reshape(T, D_MODEL))

    names = ["w_in", "w_out", "w_mlp_up", "w_mlp_down"]
    shard_rows = {"w_in": D_MODEL, "w_out": out_sh, "w_mlp_up": D_MODEL, "w_mlp_down": ff_sh}
    own_sel = shard.astype(jnp.int32)[None]
    send_sel = jnp.stack([shard ^ 2, shard ^ 1, shard ^ 3]).astype(jnp.int32)

    def by_shard(name, buf, n):
        if name == "w_in":
            return buf.reshape(n, D_MODEL, N_CHIPS, in_sh).transpose(2, 0, 1, 3).reshape(N_CHIPS, n * D_MODEL, in_sh)
        return buf.reshape(N_CHIPS, n * shard_rows[name], buf.shape[-1])

    def chip_sums(tag, group, n, swapped=None):
        order = list(group)
        G, got = swapped if swapped else (None, None)
        if not swapped:
            G = [by_shard(name, group[name], n) for name in order]
            got = _swap_halves(G)
        owns, sends = {}, {}
        for name, g_all, g_got in zip(order, G, got):
            hh = g_all.shape[1] // 2
            off = (ci * (hh // _tile(hh, 512))).astype(jnp.int32)[None]
            owns[name] = _chip_sum(g_all, g_got, (off, own_sel), F32, "chip_sum_own_" + name + tag)
            sends[name] = _chip_sum(g_all, g_got, (off, send_sel), BF16, "chip_sum_send_" + name + tag)
        return owns, sends

    def swap_behind(tag, group, n):
        G = [by_shard(name, group[name], n) for name in group]
        plan = _swap_plan([g_all.shape for g_all in G])
        send_sems, recv_sems, srcs, lands, token = _start_copies(
            "grad_swap" + tag + "_start", G, [(N_CHIPS, g_all.shape[1] // 2, g_all.shape[2]) for g_all in G], plan, 1)

        def finish(after):
            return _wait_copies("grad_swap" + tag + "_wait", send_sems, recv_sems, srcs, lands, plan, after)
        return token, finish

    def exchange_behind(tag, group, n, swapped=None):
        owns, sends = chip_sums(tag, group, n, swapped)
        order = list(sends)
        bufs = [sends[name] for name in order]
        plan = _exchange_plan([b.shape for b in bufs])
        send_sems, recv_sems, srcs, lands, token = _start_copies(
            "grad_exchange" + tag + "_start", bufs, [b.shape for b in bufs], plan)

        def finish(after):
            _, got = _wait_copies("grad_exchange" + tag + "_wait", send_sems, recv_sems, srcs, lands, plan, after)
            return {name: (owns[name], arrived) for name, arrived in zip(order, got)}
        return token, finish

    rest = dict.fromkeys(names)
    first = dict.fromkeys(names)
    small = [None] * L
    dbias_sum = None
    token = None
    for l in reversed(range(L)):
        x0, h, z, a, yc, mix, x1, h2, up = saved[l]
        stack, n, sl = (first, 1, 0) if l == 0 else (rest, L - 1, l - 1)
        if l == 0:
            token, finish_rest_swap = swap_behind("_rest", rest, L - 1)
        d_up = _dact(g, *weight_of["w_mlp_down", l], up, token)
        stack["w_mlp_down"] = _matmul_tn(up, g, True, stack["w_mlp_down"], (N_CHIPS, n, ff_sh, D_MODEL),
                                         (None, None, ff_sh, D_MODEL), lambda i, j: (i, sl, 0, 0), ff_sh, D_MODEL,
                                         "grad_w_mlp_down")
        stack["w_mlp_up"] = _matmul_tn(h2, d_up, False, stack["w_mlp_up"], (N_CHIPS, n, D_MODEL, ff_sh),
                                       (None, None, D_MODEL, ff_sh), lambda i, j: (j, sl, 0, 0), D_MODEL, ff_sh,
                                       "grad_w_mlp_up")
        if l == 0:
            token_rest, finish_rest_grads = exchange_behind("_rest", rest, L - 1, finish_rest_swap(stack["w_mlp_up"]))
            token, finish_mlp0_grads = exchange_behind("_mlp0", {k: first[k] for k in ("w_mlp_up", "w_mlp_down")}, 1)
            token = token + token_rest
        g1, d_gmlp = _matmul_nt_normbwd(d_up, *weight_of["w_mlp_up", l], x1, row(norm_mlp_g, l), g, "mlp_in_bwd",
                                        token if l == 0 else None)
        d_a, d_y, sm_mix = _mix_bwd(g1, *weight_of["w_out", l], a, yc, row(conv_ln_g, l), row(conv_ln_b, l),
                                    row(attn_out_g, l), row(conv_out_g, l))
        stack["w_out"] = _matmul_tn(mix, g1, False, stack["w_out"], (N_CHIPS, n, out_sh, D_MODEL),
                                    (N_CHIPS, None, out_sh, D_MODEL), lambda i, j: (0, sl, 0, 0), MIX_WIDTH, D_MODEL,
                                    "grad_w_out")
        cw, cl = weight_of["conv_w", l]
        d_u, d_gate, d_cw = _conv_bwd(d_y, z, cw[cl])
        d_q, d_kv, dbias, sm_attn = _attn_bwd(z, d_a, biasc, sink_rows(l), row(q_norm_g, l), row(k_norm_g, l))
        dbias_sum = dbias if dbias_sum is None else dbias_sum + dbias
        d_z = [d_q, d_kv[BLOCK:BLOCK + T], d_u, d_gate]
        stack["w_in"] = _matmul_tn(h, d_z, False, stack["w_in"], (n, D_MODEL, IN_WIDTH), (None, D_MODEL, IN_WIDTH),
                                   lambda i, j: (sl, 0, 0), D_MODEL, IN_WIDTH, "grad_w_in")
        g, d_gmix = _matmul_nt_normbwd(d_z, *weight_of["w_in", l], x0, row(norm_mix_g, l), g1, "mix_in_bwd")
        small[l] = (d_gmix[0], sm_attn[0, :HEAD_DIM], sm_attn[1, :HEAD_DIM], sm_attn[2, :N_HEADS],
                    d_cw[:CONV_KERNEL], sm_mix[4], sm_mix[2], sm_mix[3], sm_mix[0], sm_mix[1], d_gmlp[0])
    grad_x = g.reshape(1, T, D_MODEL)

    d_rel = _bucket_reduce(dbias_sum.reshape(N_HEADS, BLOCK * BLOCK), onehot_t)[:, :NUM_BUCKETS].T
    stack = lambda k: jnp.stack([small[l][k] for l in range(L)])
    small_shapes = [(), (NUM_BUCKETS, N_HEADS), (L, D_MODEL), (L, HEAD_DIM), (L, HEAD_DIM), (L, N_HEADS),
                    (L, CONV_KERNEL, CONV_WIDTH), (L, CONV_WIDTH), (L, CONV_WIDTH), (L, CONV_WIDTH),
                    (L, CONV_WIDTH), (L, CONV_WIDTH), (L, D_MODEL)]
    part = _pack([jnp.sum(loss_parts[:, 0, 0]), d_rel] + [stack(k) for k in range(11)])
    tot = _unpack(_sum_devices(part), small_shapes)
    loss = tot[0]
    (g_rel, g_nmix, g_qn, g_kn, g_sk, g_cw_full, g_cb, g_lng, g_lnb, g_aog, g_cog, g_nmlp) = tot[1:]
    g_cw_sh = lax.dynamic_slice_in_dim(g_cw_full, shard * cv_sh, cv_sh, axis=2)

    small_w = [rel_bias, norm_mix_g, q_norm_g, k_norm_g, sinks, conv_w, conv_b, conv_ln_g, conv_ln_b,
               attn_out_g, conv_out_g, norm_mlp_g]
    small_m = [m_rel_bias, m_norm_mix_g, m_q_norm_g, m_k_norm_g, m_sinks, m_conv_w, m_conv_b, m_conv_ln_g,
               m_conv_ln_b, m_attn_out_g, m_conv_out_g, m_norm_mlp_g]
    small_v = [v_rel_bias, v_norm_mix_g, v_q_norm_g, v_k_norm_g, v_sinks, v_conv_w, v_conv_b, v_conv_ln_g,
               v_conv_ln_b, v_attn_out_g, v_conv_out_g, v_norm_mlp_g]
    small_g = [g_rel, g_nmix, g_qn, g_kn, g_sk, g_cw_sh, g_cb, g_lng, g_lnb, g_aog, g_cog, g_nmlp]
    shapes = [w.shape for w in small_w]
    sd, sm_, sv_ = _adamw(_pack(small_w), _pack(small_g), _pack(small_m), _pack(small_v), "adamw_small")
    small_d, small_nm, small_nv = _unpack(sd, shapes), _unpack(sm_, shapes), _unpack(sv_, shapes)

    owns_mix0, sends_mix0 = chip_sums("_mix0", {k: first[k] for k in ("w_in", "w_out")}, 1)
    arrived_mix0 = _exchange_chips([sends_mix0[k] for k in ("w_in", "w_out")])
    parts0 = {"w_in": (owns_mix0["w_in"], arrived_mix0[0]), "w_out": (owns_mix0["w_out"], arrived_mix0[1]),
              **finish_mlp0_grads(g)}
    parts1 = finish_rest_grads(g)
    grads, spans = [], []
    for name in names:
        R = shard_rows[name]
        full = None
        spans.append([(0, R), (R, (L - 1) * R)])
        for (r0, nr), (own, arrived), tag in zip(spans[-1], (parts0[name], parts1[name]), ("_first", "_rest")):
            tr = min(512, math.gcd(R, nr // 2))
            off = ((r0 + ci * (nr // 2)) // tr).astype(jnp.int32)[None]
            full = _shard_sum(own, arrived, off, tr, full, L * R, "shard_sum_" + name + tag)
        grads.append(full)
    grads = _join_halves(grads, spans)

    big_w = [w_in, w_out, w_mlp_up, w_mlp_down]
    big_m = [m_w_in, m_w_out, m_w_mlp_up, m_w_mlp_down]
    big_v = [v_w_in, v_w_out, v_w_mlp_up, v_w_mlp_down]
    big_g, big_d, big_nm, big_nv = [], [], [], []
    for b in range(4):
        shp = big_w[b].shape
        flat = lambda t: t.reshape(shp[0] * shp[1], shp[2])
        d, nm, nv = _adamw(flat(big_w[b]), grads[b], flat(big_m[b]), flat(big_v[b]), "adamw_" + names[b])
        big_g.append(grads[b].reshape(shp))
        big_d.append(d.reshape(shp))
        big_nm.append(nm.reshape(shp))
        big_nv.append(nv.reshape(shp))

    def ordered(sm, bg):
        return [sm[0], sm[1], bg[0], sm[2], sm[3], sm[4], sm[5], sm[6], sm[7], sm[8], sm[9], sm[10], bg[1], sm[11],
                bg[2], bg[3]]

    return (loss, grad_x, *ordered(small_g, big_g), *ordered(small_d, big_d), *ordered(small_nm, big_nm),
            *ordered(small_nv, big_nv))
```

```python
import math

import numpy as np
import jax
import jax.numpy as jnp
from jax import lax
from jax.experimental import pallas as pl
from jax.experimental.pallas import tpu as pltpu

F32 = jnp.float32
BF16 = jnp.bfloat16

D_MODEL = 1024
DEPTH = 4
HEAD_DIM = 64
N_HEADS = 8
N_KV_HEADS = 2
GQA_GROUP = N_HEADS // N_KV_HEADS
ATTN_WIDTH = N_HEADS * HEAD_DIM
KV_WIDTH = N_KV_HEADS * HEAD_DIM
CONV_WIDTH = D_MODEL - ATTN_WIDTH
MIX_WIDTH = ATTN_WIDTH + CONV_WIDTH
IN_WIDTH = ATTN_WIDTH + 2 * KV_WIDTH + 2 * CONV_WIDTH
BLOCK = 128
CONV_KERNEL = 31
CONV_ROWS = 32
HALO = 32
CONV_CH = 256
NUM_BUCKETS = 32
MAX_DISTANCE = 128
D_FF = 4 * D_MODEL
EPS = 1e-6
NEG = -1e30
SCALE = 1.0 / math.sqrt(HEAD_DIM)

ADAM_LR = 0.001
ADAM_B1 = 0.9
ADAM_B2 = 0.999
ADAM_EPS = 1e-08
ADAM_WD = 0.01
ADAM_STEP = 10

N_CHIPS = 4
N_DEV = 8
LANES = 128
VMEM_LIMIT = 52 * 1024 * 1024
K_CHUNK = 4096

Q0, K0, V0, U0, G0 = 0, ATTN_WIDTH, ATTN_WIDTH + KV_WIDTH, ATTN_WIDTH + 2 * KV_WIDTH, ATTN_WIDTH + 2 * KV_WIDTH + CONV_WIDTH

NT = (((1,), (1,)), ((), ()))
TN = (((0,), (0,)), ((), ()))
MESH = pl.DeviceIdType.MESH
ANY = pl.BlockSpec(memory_space=pl.ANY)


def _params(sem=None):
    return pltpu.CompilerParams(dimension_semantics=sem, vmem_limit_bytes=VMEM_LIMIT)


def _chunk(n, cap=1024):
    for c in range(cap, 0, -LANES):
        if n % c == 0:
            return c
    raise ValueError(n)


def _tile(t, want):
    return min(t, want)


def _after_spec(after):
    return [] if after is None else [pl.BlockSpec((8, LANES), lambda *_: (0, 0))]


def _after_arg(after):
    return [] if after is None else [after]


def _weight_spec(w_all, l):
    if w_all.ndim == 4:
        return pl.BlockSpec((N_CHIPS, None) + w_all.shape[2:], lambda *_: (0, l, 0, 0))
    return pl.BlockSpec((None,) + w_all.shape[1:], lambda *_: (l, 0, 0))


def _t5_bucket(n):
    n = np.asarray(n)
    max_exact = NUM_BUCKETS // 2
    large = max_exact + (np.log(np.maximum(n, 1) / max_exact) / np.log(MAX_DISTANCE / max_exact)
                         * (NUM_BUCKETS - max_exact)).astype(np.int32)
    large = np.minimum(large, NUM_BUCKETS - 1)
    return np.where(n < max_exact, n, large).astype(np.int32)


def _band_buckets():
    qi = np.arange(BLOCK)[:, None]
    j = np.arange(BLOCK)[None, :]
    return _t5_bucket(np.where(j <= qi, qi - j, qi + BLOCK - j))


def _norm_matmul(x, g, w_all, l, out_dtype, name, after=None):
    T, D = x.shape
    sharded = w_all.ndim == 4
    N = w_all.shape[-1] * (N_CHIPS if sharded else 1)
    TM = _tile(T, 512)
    CH = w_all.shape[-1] if sharded else _chunk(N)

    def body(x_ref, g_ref, w_ref, *rest):
        h_ref, z_ref = rest[-2:]
        xv = x_ref[...]
        r = lax.rsqrt(jnp.mean(xv * xv, axis=-1, keepdims=True) + EPS)
        h = (xv * r * g_ref[...]).astype(BF16)
        h_ref[...] = h
        for c0 in range(0, N, CH):
            wc = w_ref[c0 // CH] if sharded else w_ref[:, c0:c0 + CH]
            z_ref[:, c0:c0 + CH] = jnp.dot(h, wc, preferred_element_type=F32).astype(z_ref.dtype)

    return pl.pallas_call(
        body, name=name, grid=(T // TM,),
        in_specs=[pl.BlockSpec((TM, D), lambda i: (i, 0)),
                  pl.BlockSpec((1, D), lambda i: (0, 0)),
                  _weight_spec(w_all, l)] + _after_spec(after),
        out_specs=[pl.BlockSpec((TM, D), lambda i: (i, 0)),
                   pl.BlockSpec((TM, N), lambda i: (i, 0))],
        out_shape=[jax.ShapeDtypeStruct((T, D), BF16), jax.ShapeDtypeStruct((T, N), out_dtype)],
        compiler_params=_params(("parallel",)),
    )(x, g, w_all, *_after_arg(after))


def _matmul_res(a, w_all, l, res, relu2, name):
    T, K = a.shape
    sharded = w_all.ndim == 4
    N = w_all.shape[-1]
    TM = _tile(T, 512)
    CH = w_all.shape[2] if sharded else _chunk(K, K_CHUNK)

    def body(a_ref, w_ref, res_ref, o_ref):
        acc = res_ref[...]
        for k0 in range(0, K, CH):
            av = a_ref[:, k0:k0 + CH]
            if relu2:
                av = jnp.square(jnp.maximum(av.astype(F32), 0.0)).astype(BF16)
            acc = acc + jnp.dot(av, w_ref[k0 // CH] if sharded else w_ref[k0:k0 + CH, :], preferred_element_type=F32)
        o_ref[...] = acc

    return pl.pallas_call(
        body, name=name, grid=(T // TM,),
        in_specs=[pl.BlockSpec((TM, K), lambda i: (i, 0)),
                  _weight_spec(w_all, l),
                  pl.BlockSpec((TM, N), lambda i: (i, 0))],
        out_specs=pl.BlockSpec((TM, N), lambda i: (i, 0)),
        out_shape=jax.ShapeDtypeStruct((T, N), F32),
        compiler_params=_params(("parallel",)),
    )(a, w_all, res)


def _head_norm(t, g):
    r = lax.rsqrt(jnp.mean(t * t, axis=-1, keepdims=True) + EPS)
    that = t * r
    return that * g, that, r


def _softmax_sink(s, sink):
    m = jnp.maximum(jnp.max(s, axis=-1, keepdims=True), sink)
    p = jnp.exp(s - m)
    es = jnp.exp(sink - m)
    den = jnp.sum(p, axis=-1, keepdims=True) + es
    return p / den, es / den


GROUP_ROWS = GQA_GROUP * BLOCK


def _own_block():
    row = lax.broadcasted_iota(jnp.int32, (GROUP_ROWS, BLOCK), 0)
    col = lax.broadcasted_iota(jnp.int32, (GROUP_ROWS, BLOCK), 1)
    return (row & (BLOCK - 1)) >= col


ATTN_QB = 4
KV_COLS = [slice(k * HEAD_DIM, (k + 1) * HEAD_DIM) for k in range(N_KV_HEADS)]


def _blk(i):
    return pl.ds(i * BLOCK, BLOCK)


def _stack_heads(ref, i, kvh):
    return jnp.concatenate([ref[_blk(i), (kvh * GQA_GROUP + g) * HEAD_DIM:(kvh * GQA_GROUP + g + 1) * HEAD_DIM]
                            for g in range(GQA_GROUP)], axis=0)


def _unstack_heads(ref, i, kvh, val):
    for g in range(GQA_GROUP):
        h = kvh * GQA_GROUP + g
        ref[_blk(i), h * HEAD_DIM:(h + 1) * HEAD_DIM] = val[g * BLOCK:(g + 1) * BLOCK]


def _band_probs(first, own, s_own, s_prev, bias, sink):
    s = jnp.where(own, s_own, s_prev) * SCALE + bias
    if first is not None:
        s = jnp.where(jnp.logical_or(own, jnp.logical_not(first)), s, NEG)
    return _softmax_sink(s, sink)


def _dot_nt(a, b):
    return lax.dot_general(a, b, NT, preferred_element_type=F32)


def _dot_tn(a, b):
    return lax.dot_general(a, b, TN, preferred_element_type=F32)


def _attn_fwd(z, biasc, sink_rows, qg, kg):
    T = z.shape[0]
    nb = T // BLOCK
    qb = min(ATTN_QB, nb)
    TQ = qb * BLOCK
    kb, vb = K0 // KV_WIDTH, V0 // KV_WIDTH
    groups = [(i, k) for i in range(qb) for k in range(N_KV_HEADS)]

    def body(q_ref, kc_ref, kp_ref, vc_ref, vp_ref, b_ref, sk_ref, qg_ref, kg_ref, a_ref):
        n = pl.program_id(0)
        own = _own_block()
        kn, vv = {}, {}
        for k in range(N_KV_HEADS):
            kn[-1, k] = _head_norm(kp_ref[:, KV_COLS[k]], kg_ref[...])[0].astype(BF16)
            vv[-1, k] = vp_ref[:, KV_COLS[k]].astype(BF16)
        for i, k in groups:
            kn[i, k] = _head_norm(kc_ref[_blk(i), KV_COLS[k]], kg_ref[...])[0].astype(BF16)
            vv[i, k] = vc_ref[_blk(i), KV_COLS[k]].astype(BF16)
        qnb = {g: _head_norm(_stack_heads(q_ref, *g), qg_ref[...])[0].astype(BF16) for g in groups}
        s_own = {(i, k): _dot_nt(qnb[i, k], kn[i, k]) for i, k in groups}
        s_prev = {(i, k): _dot_nt(qnb[i, k], kn[i - 1, k]) for i, k in groups}
        p = {(i, k): _band_probs(n == 0 if i == 0 else None, own, s_own[i, k], s_prev[i, k], b_ref[k], sk_ref[k])[0]
             for i, k in groups}
        p_own = {g: jnp.where(own, p[g], 0.0).astype(BF16) for g in groups}
        p_prev = {g: jnp.where(own, 0.0, p[g]).astype(BF16) for g in groups}
        o_own = {(i, k): jnp.dot(p_own[i, k], vv[i, k], preferred_element_type=F32) for i, k in groups}
        o_prev = {(i, k): jnp.dot(p_prev[i, k], vv[i - 1, k], preferred_element_type=F32) for i, k in groups}
        for i, k in groups:
            _unstack_heads(a_ref, i, k, o_own[i, k] + o_prev[i, k])

    prev = lambda n: jnp.maximum(n * qb - 1, 0)
    return pl.pallas_call(
        body, name="attn_fwd", grid=(nb // qb,),
        in_specs=[pl.BlockSpec((TQ, ATTN_WIDTH), lambda n: (n, 0)),
                  pl.BlockSpec((TQ, KV_WIDTH), lambda n: (n, kb)),
                  pl.BlockSpec((BLOCK, KV_WIDTH), lambda n: (prev(n), kb)),
                  pl.BlockSpec((TQ, KV_WIDTH), lambda n: (n, vb)),
                  pl.BlockSpec((BLOCK, KV_WIDTH), lambda n: (prev(n), vb)),
                  pl.BlockSpec((N_KV_HEADS, GROUP_ROWS, BLOCK), lambda n: (0, 0, 0)),
                  pl.BlockSpec((N_KV_HEADS, GROUP_ROWS, 1), lambda n: (0, 0, 0)),
                  pl.BlockSpec((1, HEAD_DIM), lambda n: (0, 0)),
                  pl.BlockSpec((1, HEAD_DIM), lambda n: (0, 0))],
        out_specs=pl.BlockSpec((TQ, ATTN_WIDTH), lambda n: (n, 0)),
        out_shape=jax.ShapeDtypeStruct((T, ATTN_WIDTH), F32),
        compiler_params=_params(("parallel",)),
    )(z, z, z, z, z, biasc, sink_rows, qg, kg)


SHIFTS = 8
CONV_RC = 64


def _shifted_copies(src_ref, dst_ref, total):
    for b in range(SHIFTS):
        rows = (total - b) // SHIFTS * SHIFTS
        for r0 in range(0, rows, CONV_RC):
            nr = min(CONV_RC, rows - r0)
            dst_ref[b, pl.ds(r0, nr), :] = src_ref[pl.ds(r0 + b, nr), :]


def _tap(ref, r0, o):
    return ref[o % SHIFTS, pl.ds(r0 + (o // SHIFTS) * SHIFTS, CONV_RC), :]


def _conv_fwd(z, cw, cb):
    T = z.shape[0]
    TC = _tile(T, 512)
    ub, gb = U0 // CONV_CH, G0 // CONV_CH
    hpt = TC // HALO
    lead = HALO - (CONV_KERNEL - 1)

    def body(u_ref, g_ref, up_ref, gp_ref, w_ref, b_ref, y_ref, hp_ref, hs_ref):
        i = pl.program_id(0)
        hp_ref[pl.ds(0, HALO), :] = jnp.where(i > 0, up_ref[...] * jax.nn.sigmoid(gp_ref[...]), 0.0)
        hp_ref[pl.ds(HALO, TC), :] = u_ref[...] * jax.nn.sigmoid(g_ref[...])
        _shifted_copies(hp_ref, hs_ref, TC + HALO)
        for r0 in range(0, TC, CONV_RC):
            acc = jnp.zeros((CONV_RC, CONV_CH), F32) + b_ref[...]
            for j in range(CONV_KERNEL):
                acc = acc + _tap(hs_ref, r0, lead + j) * w_ref[pl.ds(j, 1), :]
            y_ref[pl.ds(r0, CONV_RC), :] = acc

    prev = lambda i: jnp.maximum(i * hpt - 1, 0)
    return pl.pallas_call(
        body, name="conv_fwd", grid=(T // TC, CONV_WIDTH // CONV_CH),
        in_specs=[pl.BlockSpec((TC, CONV_CH), lambda i, j: (i, ub + j)),
                  pl.BlockSpec((TC, CONV_CH), lambda i, j: (i, gb + j)),
                  pl.BlockSpec((HALO, CONV_CH), lambda i, j: (prev(i), ub + j)),
                  pl.BlockSpec((HALO, CONV_CH), lambda i, j: (prev(i), gb + j)),
                  pl.BlockSpec((CONV_ROWS, CONV_CH), lambda i, j: (0, j)),
                  pl.BlockSpec((1, CONV_CH), lambda i, j: (0, j))],
        out_specs=pl.BlockSpec((TC, CONV_CH), lambda i, j: (i, j)),
        out_shape=jax.ShapeDtypeStruct((T, CONV_WIDTH), F32),
        scratch_shapes=[pltpu.VMEM((TC + HALO, CONV_CH), F32), pltpu.VMEM((SHIFTS, TC + HALO, CONV_CH), F32)],
        compiler_params=_params(("parallel", "parallel")),
    )(z, z, z, z, cw, cb)


def _ln_silu(y, ln_g, ln_b):
    mu = jnp.mean(y, axis=-1, keepdims=True)
    yc = y - mu
    var = jnp.mean(yc * yc, axis=-1, keepdims=True)
    rstd = lax.rsqrt(var + EPS)
    yhat = yc * rstd
    yn = yhat * ln_g + ln_b
    sg = jax.nn.sigmoid(yn)
    return yn * sg, yn, sg, yhat, rstd


def _mix_norm(a, y, ln_g, ln_b, ag, cg):
    T = a.shape[0]
    TM = _tile(T, 512)

    def body(a_ref, y_ref, lg_ref, lb_ref, ag_ref, cg_ref, o_ref):
        av = a_ref[...]
        ra = lax.rsqrt(jnp.mean(av * av, axis=-1, keepdims=True) + EPS)
        o_ref[:, :ATTN_WIDTH] = (av * ra * ag_ref[...]).astype(BF16)
        c, _, _, _, _ = _ln_silu(y_ref[...], lg_ref[...], lb_ref[...])
        rc = lax.rsqrt(jnp.mean(c * c, axis=-1, keepdims=True) + EPS)
        o_ref[:, ATTN_WIDTH:] = (c * rc * cg_ref[...]).astype(BF16)

    vec = pl.BlockSpec((1, CONV_WIDTH), lambda i: (0, 0))
    return pl.pallas_call(
        body, name="mix_norm", grid=(T // TM,),
        in_specs=[pl.BlockSpec((TM, ATTN_WIDTH), lambda i: (i, 0)),
                  pl.BlockSpec((TM, CONV_WIDTH), lambda i: (i, 0)), vec, vec, vec, vec],
        out_specs=pl.BlockSpec((TM, MIX_WIDTH), lambda i: (i, 0)),
        out_shape=jax.ShapeDtypeStruct((T, MIX_WIDTH), BF16),
        compiler_params=_params(("parallel",)),
    )(a, y, ln_g, ln_b, ag, cg)


def _loss_grad(y, tgt):
    T, D = y.shape
    TM = _tile(T, 512)
    nt = T // TM

    def body(y_ref, t_ref, part_ref, dy_ref):
        diff = y_ref[...] - t_ref[...]
        dy_ref[...] = diff / D
        tok = jnp.mean(diff * diff, axis=-1, keepdims=True)
        part_ref[...] = jnp.zeros((1, LANES), F32) + 0.5 * jnp.sum(tok)

    return pl.pallas_call(
        body, name="loss_grad", grid=(nt,),
        in_specs=[pl.BlockSpec((TM, D), lambda i: (i, 0)), pl.BlockSpec((TM, D), lambda i: (i, 0))],
        out_specs=[pl.BlockSpec((None, 1, LANES), lambda i: (i, 0, 0)), pl.BlockSpec((TM, D), lambda i: (i, 0))],
        out_shape=[jax.ShapeDtypeStruct((nt, 1, LANES), F32), jax.ShapeDtypeStruct((T, D), F32)],
        compiler_params=_params(("parallel",)),
    )(y, tgt)


def _dact(g, w_all, l, up, after=None):
    T, N = g.shape
    sharded = w_all.ndim == 4
    K = w_all.shape[2] * N_CHIPS if sharded else w_all.shape[1]
    TM = _tile(T, 512)
    CH = w_all.shape[2] if sharded else _chunk(K)

    def body(g_ref, w_ref, up_ref, *rest):
        o_ref = rest[-1]
        gv = g_ref[...].astype(BF16)
        for k0 in range(0, K, CH):
            da = lax.dot_general(gv, w_ref[k0 // CH] if sharded else w_ref[k0:k0 + CH, :], NT, preferred_element_type=F32)
            upv = up_ref[:, k0:k0 + CH].astype(F32)
            o_ref[:, k0:k0 + CH] = (da * (2.0 * jnp.maximum(upv, 0.0))).astype(BF16)

    return pl.pallas_call(
        body, name="mlp_dact", grid=(T // TM,),
        in_specs=[pl.BlockSpec((TM, N), lambda i: (i, 0)),
                  _weight_spec(w_all, l),
                  pl.BlockSpec((TM, K), lambda i: (i, 0))] + _after_spec(after),
        out_specs=pl.BlockSpec((TM, K), lambda i: (i, 0)),
        out_shape=jax.ShapeDtypeStruct((T, K), BF16),
        compiler_params=_params(("parallel",)),
    )(g, w_all, up, *_after_arg(after))


def _matmul_tn(a, b, relu2, buf, buf_shape, out_block, out_index, tm, tn, name):
    pieces = list(b) if isinstance(b, (list, tuple)) else [b]
    T, M = a.shape
    N = sum(p.shape[1] for p in pieces)
    assert len(pieces) == 1 or tn == N
    starts = np.cumsum([0] + [p.shape[1] for p in pieces])
    TK = _tile(T, 2048)
    nk = T // TK

    def body(*refs):
        a_ref, b_refs = refs[0], refs[1:1 + len(pieces)]
        o_ref = refs[-1]
        k = pl.program_id(2)
        av = a_ref[...]
        if relu2:
            av = jnp.square(jnp.maximum(av.astype(F32), 0.0)).astype(BF16)
        cs = [lax.dot_general(av, b_ref[...].astype(BF16), TN, preferred_element_type=F32) for b_ref in b_refs]

        def put(add):
            for p, c in enumerate(cs):
                if len(cs) == 1:
                    o_ref[...] = c.reshape(o_ref.shape) + (o_ref[...] if add else 0.0)
                else:
                    cols = slice(int(starts[p]), int(starts[p + 1]))
                    o_ref[:, cols] = c + (o_ref[:, cols] if add else 0.0)

        @pl.when(k == 0)
        def _():
            put(False)

        @pl.when(k > 0)
        def _():
            put(True)

    if len(pieces) == 1:
        b_specs = [pl.BlockSpec((TK, tn), lambda i, j, k: (k, j))]
    else:
        b_specs = [pl.BlockSpec((TK, p.shape[1]), lambda i, j, k: (k, 0)) for p in pieces]
    in_specs = [pl.BlockSpec((TK, tm), lambda i, j, k: (k, i))] + b_specs
    args = [a] + pieces
    aliases = {}
    if buf is not None:
        in_specs.append(ANY)
        args.append(buf)
        aliases = {len(args) - 1: 0}
    return pl.pallas_call(
        body, name=name, grid=(M // tm, N // tn, nk),
        in_specs=in_specs,
        out_specs=pl.BlockSpec(out_block, lambda i, j, k: out_index(i, j)),
        out_shape=jax.ShapeDtypeStruct(buf_shape, F32),
        input_output_aliases=aliases,
        compiler_params=_params(("parallel", "parallel", "arbitrary")),
    )(*args)


def _matmul_nt_normbwd(dz, w_all, l, x, gvec, gres, name, after=None):
    pieces = list(dz) if isinstance(dz, (list, tuple)) else [dz]
    T = pieces[0].shape[0]
    K = sum(p.shape[1] for p in pieces)
    starts = np.cumsum([0] + [p.shape[1] for p in pieces])
    D = x.shape[1]
    TM = _tile(T, 512)
    sharded = w_all.ndim == 4

    def body(*refs):
        dz_refs = refs[:len(pieces)]
        w_ref, x_ref, gv_ref, gr_ref = refs[len(pieces):len(pieces) + 4]
        o_ref, dg_ref = refs[-2:]
        i = pl.program_id(0)
        dh = jnp.zeros((TM, D), F32)
        for p, dz_ref in enumerate(dz_refs):
            width = dz_ref.shape[1]
            ch = w_all.shape[-1] if sharded else _chunk(width, K_CHUNK)
            for k0 in range(0, width, ch):
                wk = int(starts[p]) + k0
                wc = w_ref[wk // ch] if sharded else w_ref[:, wk:wk + ch]
                dh = dh + lax.dot_general(dz_ref[:, k0:k0 + ch], wc, NT, preferred_element_type=F32)
        xv = x_ref[...]
        r = lax.rsqrt(jnp.mean(xv * xv, axis=-1, keepdims=True) + EPS)
        xhat = xv * r
        dg = jnp.sum(dh * xhat, axis=0, keepdims=True)

        @pl.when(i == 0)
        def _():
            dg_ref[...] = dg

        @pl.when(i > 0)
        def _():
            dg_ref[...] += dg

        wv = dh * gv_ref[...]
        o_ref[...] = gr_ref[...] + r * (wv - xhat * jnp.mean(wv * xhat, axis=-1, keepdims=True))

    return pl.pallas_call(
        body, name=name, grid=(T // TM,),
        in_specs=[pl.BlockSpec((TM, p.shape[1]), lambda i: (i, 0)) for p in pieces]
        + [_weight_spec(w_all, l),
           pl.BlockSpec((TM, D), lambda i: (i, 0)),
           pl.BlockSpec((1, D), lambda i: (0, 0)),
           pl.BlockSpec((TM, D), lambda i: (i, 0))] + _after_spec(after),
        out_specs=[pl.BlockSpec((TM, D), lambda i: (i, 0)), pl.BlockSpec((1, D), lambda i: (0, 0))],
        out_shape=[jax.ShapeDtypeStruct((T, D), F32), jax.ShapeDtypeStruct((1, D), F32)],
        compiler_params=_params(("arbitrary",)),
    )(*pieces, w_all, x, gvec, gres, *_after_arg(after))


def _mix_bwd(g1, w_all, l, a, y, ln_g, ln_b, ag, cg):
    T, D = g1.shape
    TM = _tile(T, 512)

    def body(g_ref, w_ref, a_ref, y_ref, lg_ref, lb_ref, ag_ref, cg_ref, da_ref, dy_ref, sm_ref):
        i = pl.program_id(0)
        gb = g_ref[...].astype(BF16)
        dm = [lax.dot_general(gb, w_ref[s], NT, preferred_element_type=F32) for s in range(N_CHIPS)]
        dma = jnp.concatenate(dm[:N_CHIPS // 2], axis=1)
        dmc = jnp.concatenate(dm[N_CHIPS // 2:], axis=1)
        av = a_ref[...]
        ra = lax.rsqrt(jnp.mean(av * av, axis=-1, keepdims=True) + EPS)
        ahat = av * ra
        d_ag = jnp.sum(dma * ahat, axis=0, keepdims=True)
        wa = dma * ag_ref[...]
        da_ref[...] = ra * (wa - ahat * jnp.mean(wa * ahat, axis=-1, keepdims=True))

        c, yn, sg, yhat, rstd = _ln_silu(y_ref[...], lg_ref[...], lb_ref[...])
        rc = lax.rsqrt(jnp.mean(c * c, axis=-1, keepdims=True) + EPS)
        chat = c * rc
        d_cg = jnp.sum(dmc * chat, axis=0, keepdims=True)
        wc = dmc * cg_ref[...]
        dc = rc * (wc - chat * jnp.mean(wc * chat, axis=-1, keepdims=True))
        dyn = dc * (sg * (1.0 + yn * (1.0 - sg)))
        d_lg = jnp.sum(dyn * yhat, axis=0, keepdims=True)
        d_lb = jnp.sum(dyn, axis=0, keepdims=True)
        dyh = dyn * lg_ref[...]
        dy = rstd * (dyh - jnp.mean(dyh, axis=-1, keepdims=True) - yhat * jnp.mean(dyh * yhat, axis=-1, keepdims=True))
        dy_ref[...] = dy
        d_cb = jnp.sum(dy, axis=0, keepdims=True)
        sums = jnp.concatenate([d_ag, d_cg, d_lg, d_lb, d_cb, jnp.zeros((3, CONV_WIDTH), F32)], axis=0)

        @pl.when(i == 0)
        def _():
            sm_ref[...] = sums

        @pl.when(i > 0)
        def _():
            sm_ref[...] += sums

    vec = pl.BlockSpec((1, CONV_WIDTH), lambda i: (0, 0))
    return pl.pallas_call(
        body, name="mix_bwd", grid=(T // TM,),
        in_specs=[pl.BlockSpec((TM, D), lambda i: (i, 0)),
                  _weight_spec(w_all, l),
                  pl.BlockSpec((TM, ATTN_WIDTH), lambda i: (i, 0)),
                  pl.BlockSpec((TM, CONV_WIDTH), lambda i: (i, 0)), vec, vec, vec, vec],
        out_specs=[pl.BlockSpec((TM, ATTN_WIDTH), lambda i: (i, 0)),
                   pl.BlockSpec((TM, CONV_WIDTH), lambda i: (i, 0)),
                   pl.BlockSpec((8, CONV_WIDTH), lambda i: (0, 0))],
        out_shape=[jax.ShapeDtypeStruct((T, ATTN_WIDTH), F32), jax.ShapeDtypeStruct((T, CONV_WIDTH), F32),
                   jax.ShapeDtypeStruct((8, CONV_WIDTH), F32)],
        compiler_params=_params(("arbitrary",)),
    )(g1, w_all, a, y, ln_g, ln_b, ag, cg)


def _conv_bwd(dy, z, cw):
    T = z.shape[0]
    TC = _tile(T, 512)
    nt = T // TC
    ub, gb = U0 // CONV_CH, G0 // CONV_CH
    nch = CONV_WIDTH // CONV_CH
    hpt = TC // HALO

    lead = HALO - (CONV_KERNEL - 1)

    def body(dy_ref, dyn_ref, u_ref, g_ref, up_ref, gp_ref, w_ref, du_ref, dg_ref, dw_ref,
             hp_ref, hs_ref, dyp_ref, dys_ref):
        i = pl.program_id(1)
        hp_ref[pl.ds(0, HALO), :] = jnp.where(i > 0, up_ref[...] * jax.nn.sigmoid(gp_ref[...]), 0.0)
        hp_ref[pl.ds(HALO, TC), :] = u_ref[...] * jax.nn.sigmoid(g_ref[...])
        _shifted_copies(hp_ref, hs_ref, TC + HALO)
        dyp_ref[pl.ds(0, TC), :] = dy_ref[...]
        dyp_ref[pl.ds(TC, HALO), :] = jnp.where(i < nt - 1, dyn_ref[...], 0.0)
        _shifted_copies(dyp_ref, dys_ref, TC + HALO)

        @pl.when(i == 0)
        def _():
            dw_ref[...] = jnp.zeros((CONV_ROWS, CONV_CH), F32)

        for r0 in range(0, TC, CONV_RC):
            rows = pl.ds(r0, CONV_RC)
            dh = jnp.zeros((CONV_RC, CONV_CH), F32)
            for j in range(CONV_KERNEL):
                dh = dh + _tap(dys_ref, r0, CONV_KERNEL - 1 - j) * w_ref[pl.ds(j, 1), :]
            uv = u_ref[rows, :]
            sg = jax.nn.sigmoid(g_ref[rows, :])
            du_ref[rows, :] = (dh * sg).astype(BF16)
            dg_ref[rows, :] = (dh * uv * sg * (1.0 - sg)).astype(BF16)
        for j in range(CONV_KERNEL):
            acc = jnp.zeros((SHIFTS, CONV_CH), F32)
            for r0 in range(0, TC, CONV_RC):
                prod = dy_ref[pl.ds(r0, CONV_RC), :] * _tap(hs_ref, r0, lead + j)
                acc = acc + jnp.sum(prod.reshape(CONV_RC // SHIFTS, SHIFTS, CONV_CH), axis=0)
            dw_ref[pl.ds(j, 1), :] += jnp.sum(acc, axis=0, keepdims=True)

    prev = lambda i: jnp.maximum(i * hpt - 1, 0)
    nxt = lambda i: jnp.minimum((i + 1) * hpt, T // HALO - 1)
    return pl.pallas_call(
        body, name="conv_bwd", grid=(nch, nt),
        in_specs=[pl.BlockSpec((TC, CONV_CH), lambda j, i: (i, j)),
                  pl.BlockSpec((HALO, CONV_CH), lambda j, i: (nxt(i), j)),
                  pl.BlockSpec((TC, CONV_CH), lambda j, i: (i, ub + j)),
                  pl.BlockSpec((TC, CONV_CH), lambda j, i: (i, gb + j)),
                  pl.BlockSpec((HALO, CONV_CH), lambda j, i: (prev(i), ub + j)),
                  pl.BlockSpec((HALO, CONV_CH), lambda j, i: (prev(i), gb + j)),
                  pl.BlockSpec((CONV_ROWS, CONV_CH), lambda j, i: (0, j))],
        out_specs=[pl.BlockSpec((TC, CONV_CH), lambda j, i: (i, j)),
                   pl.BlockSpec((TC, CONV_CH), lambda j, i: (i, j)),
                   pl.BlockSpec((CONV_ROWS, CONV_CH), lambda j, i: (0, j))],
        out_shape=[jax.ShapeDtypeStruct((T, CONV_WIDTH), BF16), jax.ShapeDtypeStruct((T, CONV_WIDTH), BF16),
                   jax.ShapeDtypeStruct((CONV_ROWS, CONV_WIDTH), F32)],
        scratch_shapes=[pltpu.VMEM((TC + HALO, CONV_CH), F32), pltpu.VMEM((SHIFTS, TC + HALO, CONV_CH), F32),
                        pltpu.VMEM((TC + HALO, CONV_CH), F32), pltpu.VMEM((SHIFTS, TC + HALO, CONV_CH), F32)],
        compiler_params=_params(("parallel", "arbitrary")),
    )(dy, dy, z, z, z, z, cw)


def _norm_bwd(d, that, r, g):
    w = d * g
    return r * (w - that * jnp.mean(w * that, axis=-1, keepdims=True)), jnp.sum(d * that, axis=0, keepdims=True)


def _attn_bwd(z, da, biasc, sink_rows, qg, kg):
    T = z.shape[0]
    nb = T // BLOCK
    qb = min(ATTN_QB, nb)
    TQ = qb * BLOCK
    ns = nb // qb
    kb, vb = K0 // KV_WIDTH, V0 // KV_WIDTH
    groups = [(i, k) for i in range(qb) for k in range(N_KV_HEADS)]

    def body(q_ref, kc_ref, kp_ref, vc_ref, vp_ref, da_ref, b_ref, sk_ref, qg_ref, kg_ref,
             dq_ref, dkv_ref, db_ref, sm_ref, ck_ref, cv_ref, pk_ref, pv_ref, nk_ref, nv_ref):
        n = pl.program_id(0)
        lane = lax.broadcasted_iota(jnp.int32, (1, LANES), 1)

        @pl.when(n == 0)
        def _():
            db_ref[...] = jnp.zeros(db_ref.shape, F32)
            sm_ref[...] = jnp.zeros(sm_ref.shape, F32)
            ck_ref[...] = jnp.zeros(ck_ref.shape, F32)
            cv_ref[...] = jnp.zeros(cv_ref.shape, F32)

        pk_ref[...] = jnp.zeros(pk_ref.shape, F32)
        pv_ref[...] = jnp.zeros(pv_ref.shape, F32)

        @pl.when(n < ns)
        def _():
            own = _own_block()
            knorm, kn, vv = {}, {}, {}
            for k in range(N_KV_HEADS):
                kn[-1, k] = _head_norm(kp_ref[:, KV_COLS[k]], kg_ref[...])[0].astype(BF16)
                vv[-1, k] = vp_ref[:, KV_COLS[k]].astype(BF16)
            for i, k in groups:
                knorm[i, k] = _head_norm(kc_ref[_blk(i), KV_COLS[k]], kg_ref[...])
                kn[i, k] = knorm[i, k][0].astype(BF16)
                vv[i, k] = vc_ref[_blk(i), KV_COLS[k]].astype(BF16)
            qnorm = {g: _head_norm(_stack_heads(q_ref, *g), qg_ref[...]) for g in groups}
            qnb = {g: qnorm[g][0].astype(BF16) for g in groups}
            dob = {g: _stack_heads(da_ref, *g).astype(BF16) for g in groups}
            s_own = {(i, k): _dot_nt(qnb[i, k], kn[i, k]) for i, k in groups}
            s_prev = {(i, k): _dot_nt(qnb[i, k], kn[i - 1, k]) for i, k in groups}
            dp_own = {(i, k): _dot_nt(dob[i, k], vv[i, k]) for i, k in groups}
            dp_prev = {(i, k): _dot_nt(dob[i, k], vv[i - 1, k]) for i, k in groups}
            probs = {(i, k): _band_probs(n == 0 if i == 0 else None, own, s_own[i, k], s_prev[i, k], b_ref[k], sk_ref[k])
                     for i, k in groups}
            ds_own, ds_prev, p_own, p_prev = {}, {}, {}, {}
            dsk = jnp.zeros((1, LANES), F32)
            dbias = [jnp.zeros((GROUP_ROWS, BLOCK), F32) for _ in range(N_KV_HEADS)]
            for i, k in groups:
                p, psink = probs[i, k]
                dp = jnp.where(own, dp_own[i, k], dp_prev[i, k])
                delta = jnp.sum(p * dp, axis=-1, keepdims=True)
                ds = p * (dp - delta)
                dbias[k] = dbias[k] + ds
                dsink = psink * delta
                for g in range(GQA_GROUP):
                    dsk = dsk + jnp.where(lane == k * GQA_GROUP + g, -jnp.sum(dsink[g * BLOCK:(g + 1) * BLOCK]), 0.0)
                ds_own[i, k] = jnp.where(own, ds, 0.0).astype(BF16)
                ds_prev[i, k] = jnp.where(own, 0.0, ds).astype(BF16)
                p_own[i, k] = jnp.where(own, p, 0.0).astype(BF16)
                p_prev[i, k] = jnp.where(own, 0.0, p).astype(BF16)
            for k in range(N_KV_HEADS):
                db_ref[k] += dbias[k]
            dqn_own = {(i, k): jnp.dot(ds_own[i, k], kn[i, k], preferred_element_type=F32) for i, k in groups}
            dqn_prev = {(i, k): jnp.dot(ds_prev[i, k], kn[i - 1, k], preferred_element_type=F32) for i, k in groups}
            dk_own = {g: _dot_tn(ds_own[g], qnb[g]) * SCALE for g in groups}
            dk_prev = {g: _dot_tn(ds_prev[g], qnb[g]) * SCALE for g in groups}
            dv_own = {g: _dot_tn(p_own[g], dob[g]) for g in groups}
            dv_prev = {g: _dot_tn(p_prev[g], dob[g]) for g in groups}
            dqg = jnp.zeros((1, HEAD_DIM), F32)
            dkg = jnp.zeros((1, HEAD_DIM), F32)
            for i, k in groups:
                _, qhat, rq = qnorm[i, k]
                dq, dg = _norm_bwd((dqn_own[i, k] + dqn_prev[i, k]) * SCALE, qhat, rq, qg_ref[...])
                dqg = dqg + dg
                _unstack_heads(dq_ref, i, k, dq.astype(BF16))
                if i == 0:
                    pk_ref[:, KV_COLS[k]] = dk_prev[i, k]
                    pv_ref[:, KV_COLS[k]] = dv_prev[i, k]
                if i == qb - 1:
                    nk_ref[:, KV_COLS[k]] = dk_own[i, k]
                    nv_ref[:, KV_COLS[k]] = dv_own[i, k]
                else:
                    _, khat, rk = knorm[i, k]
                    dk, dg = _norm_bwd(dk_own[i, k] + dk_prev[i + 1, k], khat, rk, kg_ref[...])
                    dkg = dkg + dg
                    dkv_ref[_blk(i + 1), KV_COLS[k]] = dk.astype(BF16)
                    dkv_ref[_blk(i + 1), pl.ds(KV_WIDTH + k * HEAD_DIM, HEAD_DIM)] = (dv_own[i, k] + dv_prev[i + 1, k]).astype(BF16)
            sm_ref[pl.ds(0, 1), pl.ds(0, HEAD_DIM)] += dqg
            sm_ref[pl.ds(1, 1), pl.ds(0, HEAD_DIM)] += dkg
            sm_ref[pl.ds(2, 1), :] += dsk

        @pl.when(n >= 1)
        def _():
            dkg = jnp.zeros((1, HEAD_DIM), F32)
            for k in range(N_KV_HEADS):
                _, khat, rk = _head_norm(kp_ref[:, KV_COLS[k]], kg_ref[...])
                dk, dg = _norm_bwd(ck_ref[:, KV_COLS[k]] + pk_ref[:, KV_COLS[k]], khat, rk, kg_ref[...])
                dkg = dkg + dg
                dkv_ref[_blk(0), KV_COLS[k]] = dk.astype(BF16)
            dkv_ref[_blk(0), pl.ds(KV_WIDTH, KV_WIDTH)] = (cv_ref[...] + pv_ref[...]).astype(BF16)
            sm_ref[pl.ds(1, 1), pl.ds(0, HEAD_DIM)] += dkg

        ck_ref[...] = nk_ref[...]
        cv_ref[...] = nv_ref[...]

    cur = lambda n: jnp.minimum(n, ns - 1)
    prev = lambda n: jnp.maximum(n * qb - 1, 0)
    carry = pltpu.VMEM((BLOCK, KV_WIDTH), F32)
    return pl.pallas_call(
        body, name="attn_bwd", grid=(ns + 1,),
        in_specs=[pl.BlockSpec((TQ, ATTN_WIDTH), lambda n: (cur(n), 0)),
                  pl.BlockSpec((TQ, KV_WIDTH), lambda n: (cur(n), kb)),
                  pl.BlockSpec((BLOCK, KV_WIDTH), lambda n: (prev(n), kb)),
                  pl.BlockSpec((TQ, KV_WIDTH), lambda n: (cur(n), vb)),
                  pl.BlockSpec((BLOCK, KV_WIDTH), lambda n: (prev(n), vb)),
                  pl.BlockSpec((TQ, ATTN_WIDTH), lambda n: (cur(n), 0)),
                  pl.BlockSpec((N_KV_HEADS, GROUP_ROWS, BLOCK), lambda n: (0, 0, 0)),
                  pl.BlockSpec((N_KV_HEADS, GROUP_ROWS, 1), lambda n: (0, 0, 0)),
                  pl.BlockSpec((1, HEAD_DIM), lambda n: (0, 0)),
                  pl.BlockSpec((1, HEAD_DIM), lambda n: (0, 0))],
        out_specs=[pl.BlockSpec((TQ, ATTN_WIDTH), lambda n: (cur(n), 0)),
                   pl.BlockSpec((TQ, 2 * KV_WIDTH), lambda n: (n, 0)),
                   pl.BlockSpec((N_KV_HEADS, GROUP_ROWS, BLOCK), lambda n: (0, 0, 0)),
                   pl.BlockSpec((8, LANES), lambda n: (0, 0))],
        out_shape=[jax.ShapeDtypeStruct((T, ATTN_WIDTH), BF16), jax.ShapeDtypeStruct(((ns + 1) * TQ, 2 * KV_WIDTH), BF16),
                   jax.ShapeDtypeStruct((N_KV_HEADS, GROUP_ROWS, BLOCK), F32), jax.ShapeDtypeStruct((8, LANES), F32)],
        scratch_shapes=[carry] * 6,
        compiler_params=_params(("arbitrary",)),
    )(z, z, z, z, z, da, biasc, sink_rows, qg, kg)


def _bucket_reduce(dbias, onehot_t):
    def body(d_ref, oh_ref, o_ref):
        d = d_ref[...]
        hi = d.astype(BF16)
        r1 = d - hi.astype(F32)
        mid = r1.astype(BF16)
        lo = (r1 - mid.astype(F32)).astype(BF16)
        oh = oh_ref[...]
        acc = lax.dot_general(lo, oh, NT, preferred_element_type=F32)
        acc = acc + lax.dot_general(mid, oh, NT, preferred_element_type=F32)
        o_ref[...] = acc + lax.dot_general(hi, oh, NT, preferred_element_type=F32)

    return pl.pallas_call(
        body, name="bucket_reduce",
        out_shape=jax.ShapeDtypeStruct((N_HEADS, LANES), F32),
        compiler_params=_params(),
    )(dbias, onehot_t)


def _adamw(w, g, m, v, name):
    R, C = w.shape
    TR = _tile(R, 512)

    def body(w_ref, g_ref, m_ref, v_ref, d_ref, nm_ref, nv_ref):
        gv = g_ref[...]
        mn = ADAM_B1 * m_ref[...] + (1.0 - ADAM_B1) * gv
        vn = ADAM_B2 * v_ref[...] + (1.0 - ADAM_B2) * jnp.square(gv)
        m_hat = mn / (1.0 - ADAM_B1 ** ADAM_STEP)
        v_hat = vn / (1.0 - ADAM_B2 ** ADAM_STEP)
        d_ref[...] = -ADAM_LR * (m_hat / (jnp.sqrt(v_hat) + ADAM_EPS) + ADAM_WD * w_ref[...])
        nm_ref[...] = mn
        nv_ref[...] = vn

    spec = pl.BlockSpec((TR, C), lambda i: (i, 0))
    shp = jax.ShapeDtypeStruct((R, C), F32)
    return pl.pallas_call(
        body, name=name, grid=(R // TR,),
        in_specs=[spec] * 4, out_specs=[spec] * 3, out_shape=[shp] * 3,
        compiler_params=_params(("parallel",)),
    )(w, g, m, v)


def _place():
    return lax.axis_index("x"), lax.axis_index("y"), lax.axis_index("c")


def _other_chips(x, y):
    return [(1 - x, y), (x, 1 - y), (1 - x, 1 - y)]


def _remote(src, dst, send_sem, recv_sem, dev):
    return pltpu.make_async_remote_copy(src_ref=src, dst_ref=dst, send_sem=send_sem, recv_sem=recv_sem,
                                        device_id=dev, device_id_type=MESH)


def _gather_shards(bufs):
    nbuf = len(bufs)

    def body(*refs):
        ins, outs = refs[:nbuf], refs[nbuf:2 * nbuf]
        send_sems, recv_sems = refs[2 * nbuf:]
        x, y, c = _place()
        me = 2 * x + y
        sib = (x, y, 1 - c)
        chips = _other_chips(x, y)
        started = []
        for b in range(nbuf):
            hh = bufs[b].shape[0] // 2
            for j, (cx, cy) in enumerate(chips):
                k = 6 * b + j
                cp = _remote(ins[b].at[pl.ds(c * hh, hh), :], outs[b].at[me, pl.ds(c * hh, hh), :],
                             send_sems.at[k], recv_sems.at[k], (cx, cy, c))
                cp.start()
                started.append(cp)
        for b in range(nbuf):
            hh = bufs[b].shape[0] // 2
            for j, (cx, cy) in enumerate(chips):
                rows = outs[b].at[2 * cx + cy, pl.ds(c * hh, hh), :]
                _remote(rows, rows, send_sems.at[6 * b + j], recv_sems.at[6 * b + j], sib).wait_recv()
                k = 6 * b + 3 + j
                cp = _remote(rows, rows, send_sems.at[k], recv_sems.at[k], sib)
                cp.start()
                started.append(cp)
        for b in range(nbuf):
            hh = bufs[b].shape[0] // 2
            for j, (cx, cy) in enumerate(chips):
                rows = outs[b].at[2 * cx + cy, pl.ds((1 - c) * hh, hh), :]
                k = 6 * b + 3 + j
                _remote(rows, rows, send_sems.at[k], recv_sems.at[k], sib).wait_recv()
        for cp in started:
            cp.wait_send()

    return pl.pallas_call(
        body, name="gather_weights",
        in_specs=[ANY] * nbuf, out_specs=[ANY] * nbuf,
        out_shape=[jax.ShapeDtypeStruct((N_CHIPS,) + b.shape, b.dtype) for b in bufs],
        scratch_shapes=[pltpu.SemaphoreType.DMA((6 * nbuf,)), pltpu.SemaphoreType.DMA((6 * nbuf,))],
        compiler_params=pltpu.CompilerParams(has_side_effects=True),
    )(*bufs)


HBM = pl.BlockSpec(memory_space=pltpu.HBM)
SEM = pl.BlockSpec(memory_space=pltpu.SEMAPHORE)
DATAFLOW = pltpu.SideEffectType.DATAFLOW_SIDE_EFFECTING


def _gather_plan(shapes):
    def plan(srcs, lands):
        x, y, c = _place()
        out = []
        for b, shp in enumerate(shapes):
            hh = shp[0] // 2
            for cx, cy in _other_chips(x, y):
                out.append((srcs[b].at[pl.ds(c * hh, hh), :], lands[b].at[2 * x + y, pl.ds(c * hh, hh), :], (cx, cy, c)))
        return out
    return plan


def _exchange_plan(shapes):
    def plan(srcs, lands):
        x, y, c = _place()
        return [(srcs[b].at[j], lands[b].at[j], (cx, cy, c))
                for b in range(len(shapes)) for j, (cx, cy) in enumerate(_other_chips(x, y))]
    return plan


def _swap_plan(shapes):
    def plan(srcs, lands):
        x, y, c = _place()
        return [(srcs[b].at[:, pl.ds((1 - c) * (shp[1] // 2), shp[1] // 2), :], lands[b], (x, y, 1 - c))
                for b, shp in enumerate(shapes)]
    return plan


def _start_copies(name, srcs, land_shapes, plan, per_buffer=N_CHIPS - 1):
    n = len(srcs)
    ncopy = per_buffer * n

    def body(*refs):
        ins, lands = refs[:n], refs[n:2 * n]
        send_sems, recv_sems, token = refs[2 * n], refs[2 * n + 1], refs[-1]
        for k, (src, dst, dev) in enumerate(plan(ins, lands)):
            _remote(src, dst, send_sems.at[k], recv_sems.at[k], dev).start()
        token[...] = jnp.zeros_like(token)

    hbm = lambda a: pltpu.with_memory_space_constraint(a, pltpu.HBM)
    lands = [lax.empty(s, a.dtype) for s, a in zip(land_shapes, srcs)]
    outs = pl.pallas_call(
        body, name=name,
        out_shape=(pltpu.SemaphoreType.DMA((ncopy,)), pltpu.SemaphoreType.DMA((ncopy,)),
                   *[pltpu.HBM(a.shape, a.dtype) for a in srcs], *[pltpu.HBM(a.shape, a.dtype) for a in lands],
                   jax.ShapeDtypeStruct((8, LANES), F32)),
        in_specs=[HBM] * (2 * n),
        out_specs=(SEM, SEM, *([HBM] * (2 * n)), pl.BlockSpec(memory_space=pltpu.VMEM)),
        input_output_aliases={i: 2 + i for i in range(2 * n)},
        compiler_params=pltpu.CompilerParams(has_side_effects=DATAFLOW),
    )(*[hbm(a) for a in srcs], *[hbm(a) for a in lands])
    return outs[0], outs[1], list(outs[2:2 + n]), list(outs[2 + n:2 + 2 * n]), outs[-1]


def _wait_copies(name, send_sems, recv_sems, srcs, lands, plan, after):
    n = len(srcs)

    def body(*refs):
        ins, lnds = refs[:n], refs[n:2 * n]
        ssem, rsem = refs[2 * n], refs[2 * n + 1]
        for k, (src, dst, dev) in enumerate(plan(ins, lnds)):
            cp = _remote(src, dst, ssem.at[k], rsem.at[k], dev)
            cp.wait_send()
            cp.wait_recv()

    outs = pl.pallas_call(
        body, name=name,
        out_shape=(*[pltpu.HBM(a.shape, a.dtype) for a in srcs], *[pltpu.HBM(a.shape, a.dtype) for a in lands]),
        in_specs=[HBM] * (2 * n) + [SEM, SEM, ANY],
        out_specs=tuple([HBM] * (2 * n)),
        input_output_aliases={i: i for i in range(2 * n)},
        compiler_params=pltpu.CompilerParams(has_side_effects=DATAFLOW),
    )(*srcs, *lands, send_sems, recv_sems, after)
    return list(outs[:n]), list(outs[n:])


def _forward_halves(bufs):
    nbuf = len(bufs)

    def body(*refs):
        outs = refs[nbuf:2 * nbuf]
        send_sems, recv_sems = refs[2 * nbuf:]
        x, y, c = _place()
        sib = (x, y, 1 - c)
        cps = []
        for b in range(nbuf):
            hh = bufs[b].shape[1] // 2
            for j, (cx, cy) in enumerate(_other_chips(x, y)):
                rows = outs[b].at[2 * cx + cy, pl.ds(c * hh, hh), :]
                cp = _remote(rows, rows, send_sems.at[3 * b + j], recv_sems.at[3 * b + j], sib)
                cp.start()
                cps.append(cp)
        for b in range(nbuf):
            hh = bufs[b].shape[1] // 2
            for j, (cx, cy) in enumerate(_other_chips(x, y)):
                rows = outs[b].at[2 * cx + cy, pl.ds((1 - c) * hh, hh), :]
                _remote(rows, rows, send_sems.at[3 * b + j], recv_sems.at[3 * b + j], sib).wait_recv()
        for cp in cps:
            cp.wait_send()

    return pl.pallas_call(
        body, name="gather_forward_halves",
        in_specs=[ANY] * nbuf, out_specs=[ANY] * nbuf,
        out_shape=[jax.ShapeDtypeStruct(b.shape, b.dtype) for b in bufs],
        input_output_aliases={b: b for b in range(nbuf)},
        scratch_shapes=[pltpu.SemaphoreType.DMA((3 * nbuf,)), pltpu.SemaphoreType.DMA((3 * nbuf,))],
        compiler_params=pltpu.CompilerParams(has_side_effects=True),
    )(*bufs)


def _swap_halves(bufs):
    nbuf = len(bufs)

    def body(*refs):
        ins, outs = refs[:nbuf], refs[nbuf:2 * nbuf]
        send_sems, recv_sems = refs[2 * nbuf:]
        x, y, c = _place()
        sib = (x, y, 1 - c)
        cps = []
        for b in range(nbuf):
            hh = bufs[b].shape[1] // 2
            cp = _remote(ins[b].at[:, pl.ds((1 - c) * hh, hh), :], outs[b], send_sems.at[b], recv_sems.at[b], sib)
            cp.start()
            cps.append(cp)
        for cp in cps:
            cp.wait()

    return pl.pallas_call(
        body, name="grad_swap_halves",
        in_specs=[ANY] * nbuf, out_specs=[ANY] * nbuf,
        out_shape=[jax.ShapeDtypeStruct((N_CHIPS, b.shape[1] // 2, b.shape[2]), b.dtype) for b in bufs],
        scratch_shapes=[pltpu.SemaphoreType.DMA((nbuf,)), pltpu.SemaphoreType.DMA((nbuf,))],
        compiler_params=pltpu.CompilerParams(has_side_effects=True),
    )(*bufs)


def _chip_sum(g, got, sel, out_dtype, name):
    _, R, C = g.shape
    hh = R // 2
    TR = _tile(hh, 512)
    nslot = sel[1].shape[0]

    def body(off_ref, sh_ref, g_ref, r_ref, o_ref):
        o_ref[...] = (g_ref[...] + r_ref[...]).astype(out_dtype)

    return pl.pallas_call(
        body, name=name,
        grid_spec=pltpu.PrefetchScalarGridSpec(
            num_scalar_prefetch=2, grid=(nslot, hh // TR),
            in_specs=[pl.BlockSpec((None, TR, C), lambda s, i, off, sh: (sh[s], off[0] + i, 0)),
                      pl.BlockSpec((None, TR, C), lambda s, i, off, sh: (sh[s], i, 0))],
            out_specs=pl.BlockSpec((None, TR, C), lambda s, i, off, sh: (s, i, 0))),
        out_shape=jax.ShapeDtypeStruct((nslot, hh, C), out_dtype),
        compiler_params=_params(("parallel", "parallel")),
    )(sel[0], sel[1], g, got)


def _exchange_chips(bufs):
    nbuf = len(bufs)

    def body(*refs):
        ins, outs = refs[:nbuf], refs[nbuf:2 * nbuf]
        send_sems, recv_sems = refs[2 * nbuf:]
        x, y, c = _place()
        cps = []
        for b in range(nbuf):
            for j, (cx, cy) in enumerate(_other_chips(x, y)):
                k = 3 * b + j
                cp = _remote(ins[b].at[j], outs[b].at[j], send_sems.at[k], recv_sems.at[k], (cx, cy, c))
                cp.start()
                cps.append(cp)
        for cp in cps:
            cp.wait()

    return pl.pallas_call(
        body, name="grad_exchange_chips",
        in_specs=[ANY] * nbuf, out_specs=[ANY] * nbuf,
        out_shape=[jax.ShapeDtypeStruct(b.shape, b.dtype) for b in bufs],
        scratch_shapes=[pltpu.SemaphoreType.DMA((3 * nbuf,)), pltpu.SemaphoreType.DMA((3 * nbuf,))],
        compiler_params=pltpu.CompilerParams(has_side_effects=True),
    )(*bufs)


def _shard_sum(own, got, off, tr, full, rows, name):
    _, hh, C = own.shape

    def body(off_ref, o_ref, r_ref, *rest):
        acc = o_ref[...]
        for j in range(N_CHIPS - 1):
            acc = acc + r_ref[j].astype(F32)
        rest[-1][...] = acc

    in_specs = [pl.BlockSpec((None, tr, C), lambda i, off: (0, i, 0)),
                pl.BlockSpec((N_CHIPS - 1, tr, C), lambda i, off: (0, i, 0))]
    args = [off, own, got]
    aliases = {}
    if full is not None:
        in_specs.append(ANY)
        args.append(full)
        aliases = {3: 0}
    return pl.pallas_call(
        body, name=name,
        grid_spec=pltpu.PrefetchScalarGridSpec(
            num_scalar_prefetch=1, grid=(hh // tr,), in_specs=in_specs,
            out_specs=pl.BlockSpec((tr, C), lambda i, off: (off[0] + i, 0))),
        out_shape=jax.ShapeDtypeStruct((rows, C), F32),
        input_output_aliases=aliases,
        compiler_params=_params(("parallel",)),
    )(*args)


def _join_halves(bufs, spans):
    nbuf = len(bufs)
    ncopy = nbuf * len(spans)

    def body(*refs):
        outs = refs[nbuf:2 * nbuf]
        send_sems, recv_sems = refs[2 * nbuf:]
        x, y, c = _place()
        sib = (x, y, 1 - c)
        cps = []
        for b in range(nbuf):
            for s, (r0, nr) in enumerate(spans[b]):
                k = b * len(spans[b]) + s
                rows = outs[b].at[pl.ds(r0 + c * (nr // 2), nr // 2), :]
                cp = _remote(rows, rows, send_sems.at[k], recv_sems.at[k], sib)
                cp.start()
                cps.append(cp)
        for b in range(nbuf):
            for s, (r0, nr) in enumerate(spans[b]):
                k = b * len(spans[b]) + s
                theirs = outs[b].at[pl.ds(r0 + (1 - c) * (nr // 2), nr // 2), :]
                _remote(theirs, theirs, send_sems.at[k], recv_sems.at[k], sib).wait_recv()
        for cp in cps:
            cp.wait_send()

    return pl.pallas_call(
        body, name="grad_join_halves",
        in_specs=[ANY] * nbuf, out_specs=[ANY] * nbuf,
        out_shape=[jax.ShapeDtypeStruct(b.shape, b.dtype) for b in bufs],
        input_output_aliases={b: b for b in range(nbuf)},
        scratch_shapes=[pltpu.SemaphoreType.DMA((ncopy,)), pltpu.SemaphoreType.DMA((ncopy,))],
        compiler_params=pltpu.CompilerParams(has_side_effects=True),
    )(*bufs)


def _sum_devices(part):
    R = part.shape[0]

    def body(p_ref, o_ref, all_ref, send_sems, recv_sems):
        x, y, c = _place()
        me = 4 * x + 2 * y + c
        all_ref[me] = p_ref[...]
        cps = []
        for k in range(1, N_DEV):
            px, py, pc = x ^ (k >> 2), y ^ ((k >> 1) & 1), c ^ (k & 1)
            cp = _remote(p_ref, all_ref.at[me], send_sems.at[k - 1], recv_sems.at[k - 1], (px, py, pc))
            cp.start()
            cps.append(cp)
        for k in range(1, N_DEV):
            peer = me ^ k
            _remote(p_ref, all_ref.at[peer], send_sems.at[k - 1], recv_sems.at[k - 1], (x, y, c)).wait_recv()
        for cp in cps:
            cp.wait_send()
        acc = all_ref[0]
        for d in range(1, N_DEV):
            acc = acc + all_ref[d]
        o_ref[...] = acc

    return pl.pallas_call(
        body, name="sum_small_grads",
        in_specs=[pl.BlockSpec(memory_space=pltpu.VMEM)],
        out_specs=pl.BlockSpec(memory_space=pltpu.VMEM),
        out_shape=jax.ShapeDtypeStruct((R, LANES), F32),
        scratch_shapes=[pltpu.VMEM((N_DEV, R, LANES), F32),
                        pltpu.SemaphoreType.DMA((N_DEV - 1,)), pltpu.SemaphoreType.DMA((N_DEV - 1,))],
        compiler_params=pltpu.CompilerParams(has_side_effects=True, vmem_limit_bytes=VMEM_LIMIT),
    )(part)


def _pack(parts):
    flat = jnp.concatenate([p.reshape(-1).astype(F32) for p in parts])
    n = flat.shape[0]
    rows = -(-n // LANES)
    rows = -(-rows // 8) * 8
    return jnp.pad(flat, (0, rows * LANES - n)).reshape(rows, LANES)


def _unpack(packed, shapes):
    flat = packed.reshape(-1)
    out, off = [], 0
    for s in shapes:
        n = int(np.prod(s))
        out.append(flat[off:off + n].reshape(s))
        off += n
    return out


def kernel(x, rel_bias, norm_mix_g, w_in, q_norm_g, k_norm_g, sinks, conv_w, conv_b, conv_ln_g, conv_ln_b, attn_out_g, conv_out_g, w_out, norm_mlp_g, w_mlp_up, w_mlp_down, loss_target, m_rel_bias, m_norm_mix_g, m_w_in, m_q_norm_g, m_k_norm_g, m_sinks, m_conv_w, m_conv_b, m_conv_ln_g, m_conv_ln_b, m_attn_out_g, m_conv_out_g, m_w_out, m_norm_mlp_g, m_w_mlp_up, m_w_mlp_down, v_rel_bias, v_norm_mix_g, v_w_in, v_q_norm_g, v_k_norm_g, v_sinks, v_conv_w, v_conv_b, v_conv_ln_g, v_conv_ln_b, v_attn_out_g, v_conv_out_g, v_w_out, v_norm_mlp_g, v_w_mlp_up, v_w_mlp_down):
    T = x.shape[1]
    L = DEPTH
    xi, yi, ci = _place()
    shard = 2 * xi + yi
    in_sh = IN_WIDTH // N_CHIPS
    out_sh = MIX_WIDTH // N_CHIPS
    ff_sh = D_FF // N_CHIPS
    cv_sh = CONV_WIDTH // N_CHIPS

    MIXING, MLP = ("w_in", "w_out", "conv_w"), ("w_mlp_up", "w_mlp_down")

    def my_shard(name, lo, hi):
        n = hi - lo
        if name == "w_in":
            return w_in[lo:hi].astype(BF16).reshape(n * D_MODEL, in_sh)
        if name == "w_out":
            return w_out[lo:hi].astype(BF16).reshape(n * out_sh, D_MODEL)
        if name == "w_mlp_up":
            return w_mlp_up[lo:hi].astype(BF16).reshape(n * D_MODEL, ff_sh)
        if name == "w_mlp_down":
            return w_mlp_down[lo:hi].astype(BF16).reshape(n * ff_sh, D_MODEL)
        cw_pad = jnp.pad(conv_w[lo:hi], ((0, 0), (0, CONV_ROWS - CONV_KERNEL), (0, 0)))
        return cw_pad.reshape(n * CONV_ROWS, cv_sh)

    def whole_weight(name, gathered, own, n):
        g = lax.dynamic_update_slice(gathered, own[None], (shard, 0, 0))
        if name == "w_in":
            return g.reshape(N_CHIPS, n, D_MODEL, in_sh).transpose(1, 2, 0, 3).reshape(n, D_MODEL, IN_WIDTH)
        if name in ("w_out", "w_mlp_up", "w_mlp_down"):
            return g.reshape(N_CHIPS, n, g.shape[1] // n, g.shape[2])
        return g.reshape(N_CHIPS, n, CONV_ROWS, cv_sh).transpose(1, 2, 0, 3).reshape(n, CONV_ROWS, CONV_WIDTH)

    weight_of = {}

    def provide(entries, gathered, mine):
        for (name, lo, hi), g, own in zip(entries, gathered, mine):
            whole = whole_weight(name, g, own, hi - lo)
            for l in range(lo, hi):
                weight_of[name, l] = (whole, l - lo)

    def gather_behind(tag, entries, first):
        mine = [my_shard(*e) for e in entries]
        mine[0], _ = lax.optimization_barrier((mine[0], first))
        plan = _gather_plan([m.shape for m in mine])
        send_sems, recv_sems, srcs, lands, token = _start_copies(
            "gather_" + tag + "_start", mine, [(N_CHIPS,) + m.shape for m in mine], plan)

        def finish(after):
            _, got = _wait_copies("gather_" + tag + "_wait", send_sems, recv_sems, srcs, lands, plan, after)
            provide(entries, _forward_halves(got), mine)
        return token, finish

    now = [(name, 0, 1) for name in MIXING]
    early = [(name, 0, 2) for name in MLP] + [(name, 1, 2) for name in MIXING]
    late = [(name, 2, L) for name in MIXING + MLP]
    mine0 = [my_shard(*e) for e in now]
    got0 = _gather_shards(mine0)
    provide(now, got0, mine0)
    token_early, finish_early = gather_behind("early", early, got0[0])

    bucket = _band_buckets()
    bk = jnp.asarray(bucket)[None]
    biasc = jnp.zeros((N_HEADS, BLOCK, BLOCK), F32)
    for b in range(NUM_BUCKETS):
        biasc = jnp.where(bk == b, rel_bias[b][:, None, None], biasc)
    biasc = biasc.reshape(N_KV_HEADS, GROUP_ROWS, BLOCK)
    onehot_t = np.zeros((LANES, BLOCK * BLOCK), np.float32)
    onehot_t[bucket.reshape(-1), np.arange(BLOCK * BLOCK)] = 1.0
    onehot_t = jnp.asarray(onehot_t, dtype=BF16)
    sink_rows = lambda l: jnp.repeat(sinks[l], BLOCK).reshape(N_KV_HEADS, GROUP_ROWS, 1)

    row = lambda a, l: a[l][None, :]

    xs = x.reshape(T, D_MODEL)
    saved = []
    token_late = None
    for l in range(L):
        if l == 2:
            finish_late(xs)
        h, z = _norm_matmul(xs, row(norm_mix_g, l), *weight_of["w_in", l], F32, "mix_in_proj", token_early if l == 0 else None)
        a = _attn_fwd(z, biasc, sink_rows(l), row(q_norm_g, l), row(k_norm_g, l))
        cw, cl = weight_of["conv_w", l]
        yc = _conv_fwd(z, cw[cl], row(conv_b, l))
        mix = _mix_norm(a, yc, row(conv_ln_g, l), row(conv_ln_b, l), row(attn_out_g, l), row(conv_out_g, l))
        x1 = _matmul_res(mix, *weight_of["w_out", l], xs, False, "mix_out_proj")
        if l == 0:
            finish_early(x1)
            token_late, finish_late = gather_behind("late", late, weight_of["w_mlp_up", 0][0])
        h2, up = _norm_matmul(x1, row(norm_mlp_g, l), *weight_of["w_mlp_up", l], BF16, "mlp_up_proj",
                              token_late if l == 0 else None)
        x2 = _matmul_res(up, *weight_of["w_mlp_down", l], x1, True, "mlp_down_proj")
        saved.append((xs, h, z, a, yc, mix, x1, h2, up))
        xs = x2

    loss_parts, g = _loss_grad(xs, loss_target.reshape(T, D_MODEL))

    names = ["w_in", "w_out", "w_mlp_up", "w_mlp_down"]
    shard_rows = {"w_in": D_MODEL, "w_out": out_sh, "w_mlp_up": D_MODEL, "w_mlp_down": ff_sh}
    own_sel = shard.astype(jnp.int32)[None]
    send_sel = jnp.stack([shard ^ 2, shard ^ 1, shard ^ 3]).astype(jnp.int32)

    def by_shard(name, buf, n):
        if name == "w_in":
            return buf.reshape(n, D_MODEL, N_CHIPS, in_sh).transpose(2, 0, 1, 3).reshape(N_CHIPS, n * D_MODEL, in_sh)
        return buf.reshape(N_CHIPS, n * shard_rows[name], buf.shape[-1])

    def chip_sums(tag, group, n, swapped=None):
        order = list(group)
        G, got = swapped if swapped else (None, None)
        if not swapped:
            G = [by_shard(name, group[name], n) for name in order]
            got = _swap_halves(G)
        owns, sends = {}, {}
        for name, g_all, g_got in zip(order, G, got):
            hh = g_all.shape[1] // 2
            off = (ci * (hh // _tile(hh, 512))).astype(jnp.int32)[None]
            owns[name] = _chip_sum(g_all, g_got, (off, own_sel), F32, "chip_sum_own_" + name + tag)
            sends[name] = _chip_sum(g_all, g_got, (off, send_sel), BF16, "chip_sum_send_" + name + tag)
        return owns, sends

    def swap_behind(tag, group, n):
        G = [by_shard(name, group[name], n) for name in group]
        plan = _swap_plan([g_all.shape for g_all in G])
        send_sems, recv_sems, srcs, lands, token = _start_copies(
            "grad_swap" + tag + "_start", G, [(N_CHIPS, g_all.shape[1] // 2, g_all.shape[2]) for g_all in G], plan, 1)

        def finish(after):
            return _wait_copies("grad_swap" + tag + "_wait", send_sems, recv_sems, srcs, lands, plan, after)
        return token, finish

    def exchange_behind(tag, group, n, swapped=None):
        owns, sends = chip_sums(tag, group, n, swapped)
        order = list(sends)
        bufs = [sends[name] for name in order]
        plan = _exchange_plan([b.shape for b in bufs])
        send_sems, recv_sems, srcs, lands, token = _start_copies(
            "grad_exchange" + tag + "_start", bufs, [b.shape for b in bufs], plan)

        def finish(after):
            _, got = _wait_copies("grad_exchange" + tag + "_wait", send_sems, recv_sems, srcs, lands, plan, after)
            return {name: (owns[name], arrived) for name, arrived in zip(order, got)}
        return token, finish

    rest = dict.fromkeys(names)
    first = dict.fromkeys(names)
    small = [None] * L
    dbias_sum = None
    token = None
    for l in reversed(range(L)):
        x0, h, z, a, yc, mix, x1, h2, up = saved[l]
        stack, n, sl = (first, 1, 0) if l == 0 else (rest, L - 1, l - 1)
        if l == 0:
            token, finish_rest_swap = swap_behind("_rest", rest, L - 1)
        d_up = _dact(g, *weight_of["w_mlp_down", l], up, token)
        stack["w_mlp_down"] = _matmul_tn(up, g, True, stack["w_mlp_down"], (N_CHIPS, n, ff_sh, D_MODEL),
                                         (None, None, ff_sh, D_MODEL), lambda i, j: (i, sl, 0, 0), ff_sh, D_MODEL,
                                         "grad_w_mlp_down")
        stack["w_mlp_up"] = _matmul_tn(h2, d_up, False, stack["w_mlp_up"], (N_CHIPS, n, D_MODEL, ff_sh),
                                       (None, None, D_MODEL, ff_sh), lambda i, j: (j, sl, 0, 0), D_MODEL, ff_sh,
                                       "grad_w_mlp_up")
        if l == 0:
            token_rest, finish_rest_grads = exchange_behind("_rest", rest, L - 1, finish_rest_swap(stack["w_mlp_up"]))
            token, finish_mlp0_grads = exchange_behind("_mlp0", {k: first[k] for k in ("w_mlp_up", "w_mlp_down")}, 1)
            token = token + token_rest
        g1, d_gmlp = _matmul_nt_normbwd(d_up, *weight_of["w_mlp_up", l], x1, row(norm_mlp_g, l), g, "mlp_in_bwd",
                                        token if l == 0 else None)
        d_a, d_y, sm_mix = _mix_bwd(g1, *weight_of["w_out", l], a, yc, row(conv_ln_g, l), row(conv_ln_b, l),
                                    row(attn_out_g, l), row(conv_out_g, l))
        stack["w_out"] = _matmul_tn(mix, g1, False, stack["w_out"], (N_CHIPS, n, out_sh, D_MODEL),
                                    (N_CHIPS, None, out_sh, D_MODEL), lambda i, j: (0, sl, 0, 0), MIX_WIDTH, D_MODEL,
                                    "grad_w_out")
        cw, cl = weight_of["conv_w", l]
        d_u, d_gate, d_cw = _conv_bwd(d_y, z, cw[cl])
        d_q, d_kv, dbias, sm_attn = _attn_bwd(z, d_a, biasc, sink_rows(l), row(q_norm_g, l), row(k_norm_g, l))
        dbias_sum = dbias if dbias_sum is None else dbias_sum + dbias
        d_z = [d_q, d_kv[BLOCK:BLOCK + T], d_u, d_gate]
        stack["w_in"] = _matmul_tn(h, d_z, False, stack["w_in"], (n, D_MODEL, IN_WIDTH), (None, D_MODEL, IN_WIDTH),
                                   lambda i, j: (sl, 0, 0), D_MODEL, IN_WIDTH, "grad_w_in")
        g, d_gmix = _matmul_nt_normbwd(d_z, *weight_of["w_in", l], x0, row(norm_mix_g, l), g1, "mix_in_bwd")
        small[l] = (d_gmix[0], sm_attn[0, :HEAD_DIM], sm_attn[1, :HEAD_DIM], sm_attn[2, :N_HEADS],
                    d_cw[:CONV_KERNEL], sm_mix[4], sm_mix[2], sm_mix[3], sm_mix[0], sm_mix[1], d_gmlp[0])
    grad_x = g.reshape(1, T, D_MODEL)

    d_rel = _bucket_reduce(dbias_sum.reshape(N_HEADS, BLOCK * BLOCK), onehot_t)[:, :NUM_BUCKETS].T
    stack = lambda k: jnp.stack([small[l][k] for l in range(L)])
    small_shapes = [(), (NUM_BUCKETS, N_HEADS), (L, D_MODEL), (L, HEAD_DIM), (L, HEAD_DIM), (L, N_HEADS),
                    (L, CONV_KERNEL, CONV_WIDTH), (L, CONV_WIDTH), (L, CONV_WIDTH), (L, CONV_WIDTH),
                    (L, CONV_WIDTH), (L, CONV_WIDTH), (L, D_MODEL)]
    part = _pack([jnp.sum(loss_parts[:, 0, 0]), d_rel] + [stack(k) for k in range(11)])
    tot = _unpack(_sum_devices(part), small_shapes)
    loss = tot[0]
    (g_rel, g_nmix, g_qn, g_kn, g_sk, g_cw_full, g_cb, g_lng, g_lnb, g_aog, g_cog, g_nmlp) = tot[1:]
    g_cw_sh = lax.dynamic_slice_in_dim(g_cw_full, shard * cv_sh, cv_sh, axis=2)

    small_w = [rel_bias, norm_mix_g, q_norm_g, k_norm_g, sinks, conv_w, conv_b, conv_ln_g, conv_ln_b,
               attn_out_g, conv_out_g, norm_mlp_g]
    small_m = [m_rel_bias, m_norm_mix_g, m_q_norm_g, m_k_norm_g, m_sinks, m_conv_w, m_conv_b, m_conv_ln_g,
               m_conv_ln_b, m_attn_out_g, m_conv_out_g, m_norm_mlp_g]
    small_v = [v_rel_bias, v_norm_mix_g, v_q_norm_g, v_k_norm_g, v_sinks, v_conv_w, v_conv_b, v_conv_ln_g,
               v_conv_ln_b, v_attn_out_g, v_conv_out_g, v_norm_mlp_g]
    small_g = [g_rel, g_nmix, g_qn, g_kn, g_sk, g_cw_sh, g_cb, g_lng, g_lnb, g_aog, g_cog, g_nmlp]
    shapes = [w.shape for w in small_w]
    sd, sm_, sv_ = _adamw(_pack(small_w), _pack(small_g), _pack(small_m), _pack(small_v), "adamw_small")
    small_d, small_nm, small_nv = _unpack(sd, shapes), _unpack(sm_, shapes), _unpack(sv_, shapes)

    owns_mix0, sends_mix0 = chip_sums("_mix0", {k: first[k] for k in ("w_in", "w_out")}, 1)
    arrived_mix0 = _exchange_chips([sends_mix0[k] for k in ("w_in", "w_out")])
    parts0 = {"w_in": (owns_mix0["w_in"], arrived_mix0[0]), "w_out": (owns_mix0["w_out"], arrived_mix0[1]),
              **finish_mlp0_grads(g)}
    parts1 = finish_rest_grads(g)
    grads, spans = [], []
    for name in names:
        R = shard_rows[name]
        full = None
        spans.append([(0, R), (R, (L - 1) * R)])
        for (r0, nr), (own, arrived), tag in zip(spans[-1], (parts0[name], parts1[name]), ("_first", "_rest")):
            tr = min(512, math.gcd(R, nr // 2))
            off = ((r0 + ci * (nr // 2)) // tr).astype(jnp.int32)[None]
            full = _shard_sum(own, arrived, off, tr, full, L * R, "shard_sum_" + name + tag)
        grads.append(full)
    grads = _join_halves(grads, spans)

    big_w = [w_in, w_out, w_mlp_up, w_mlp_down]
    big_m = [m_w_in, m_w_out, m_w_mlp_up, m_w_mlp_down]
    big_v = [v_w_in, v_w_out, v_w_mlp_up, v_w_mlp_down]
    big_g, big_d, big_nm, big_nv = [], [], [], []
    for b in range(4):
        shp = big_w[b].shape
        flat = lambda t: t.reshape(shp[0] * shp[1], shp[2])
        d, nm, nv = _adamw(flat(big_w[b]), grads[b], flat(big_m[b]), flat(big_v[b]), "adamw_" + names[b])
        big_g.append(grads[b].reshape(shp))
        big_d.append(d.reshape(shp))
        big_nm.append(nm.reshape(shp))
        big_nv.append(nv.reshape(shp))

    def ordered(sm, bg):
        return [sm[0], sm[1], bg[0], sm[2], sm[3], sm[4], sm[5], sm[6], sm[7], sm[8], sm[9], sm[10], bg[1], sm[11],
                bg[2], bg[3]]

    return (loss, grad_x, *ordered(small_g, big_g), *ordered(small_d, big_d), *ordered(small_nm, big_nm),
            *ordered(small_nv, big_nv))
```

```python
import math

import numpy as np
import jax
import jax.numpy as jnp
from jax import lax
from jax.experimental import pallas as pl
from jax.experimental.pallas import tpu as pltpu

F32 = jnp.float32
BF16 = jnp.bfloat16

D_MODEL = 1024
DEPTH = 4
HEAD_DIM = 64
N_HEADS = 8
N_KV_HEADS = 2
GQA_GROUP = N_HEADS // N_KV_HEADS
ATTN_WIDTH = N_HEADS * HEAD_DIM
KV_WIDTH = N_KV_HEADS * HEAD_DIM
CONV_WIDTH = D_MODEL - ATTN_WIDTH
MIX_WIDTH = ATTN_WIDTH + CONV_WIDTH
IN_WIDTH = ATTN_WIDTH + 2 * KV_WIDTH + 2 * CONV_WIDTH
BLOCK = 128
CONV_KERNEL = 31
CONV_ROWS = 32
HALO = 32
CONV_CH = 256
NUM_BUCKETS = 32
MAX_DISTANCE = 128
D_FF = 4 * D_MODEL
EPS = 1e-6
NEG = -1e30
SCALE = 1.0 / math.sqrt(HEAD_DIM)

ADAM_LR = 0.001
ADAM_B1 = 0.9
ADAM_B2 = 0.999
ADAM_EPS = 1e-08
ADAM_WD = 0.01
ADAM_STEP = 10

N_CHIPS = 4
N_DEV = 8
LANES = 128
VMEM_LIMIT = 52 * 1024 * 1024
K_CHUNK = 4096

Q0, K0, V0, U0, G0 = 0, ATTN_WIDTH, ATTN_WIDTH + KV_WIDTH, ATTN_WIDTH + 2 * KV_WIDTH, ATTN_WIDTH + 2 * KV_WIDTH + CONV_WIDTH

NT = (((1,), (1,)), ((), ()))
TN = (((0,), (0,)), ((), ()))
MESH = pl.DeviceIdType.MESH
ANY = pl.BlockSpec(memory_space=pl.ANY)


def _params(sem=None):
    return pltpu.CompilerParams(dimension_semantics=sem, vmem_limit_bytes=VMEM_LIMIT)


def _chunk(n, cap=1024):
    for c in range(cap, 0, -LANES):
        if n % c == 0:
            return c
    raise ValueError(n)


def _tile(t, want):
    return min(t, want)


def _after_spec(after):
    return [] if after is None else [pl.BlockSpec((8, LANES), lambda *_: (0, 0))]


def _after_arg(after):
    return [] if after is None else [after]


def _weight_spec(w_all, l):
    if w_all.ndim == 4:
        return pl.BlockSpec((N_CHIPS, None) + w_all.shape[2:], lambda *_: (0, l, 0, 0))
    return pl.BlockSpec((None,) + w_all.shape[1:], lambda *_: (l, 0, 0))


def _t5_bucket(n):
    n = np.asarray(n)
    max_exact = NUM_BUCKETS // 2
    large = max_exact + (np.log(np.maximum(n, 1) / max_exact) / np.log(MAX_DISTANCE / max_exact)
                         * (NUM_BUCKETS - max_exact)).astype(np.int32)
    large = np.minimum(large, NUM_BUCKETS - 1)
    return np.where(n < max_exact, n, large).astype(np.int32)


def _band_buckets():
    qi = np.arange(BLOCK)[:, None]
    j = np.arange(BLOCK)[None, :]
    return _t5_bucket(np.where(j <= qi, qi - j, qi + BLOCK - j))


def _norm_matmul(x, g, w_all, l, out_dtype, name, after=None):
    T, D = x.shape
    sharded = w_all.ndim == 4
    N = w_all.shape[-1] * (N_CHIPS if sharded else 1)
    TM = _tile(T, 512)
    CH = w_all.shape[-1] if sharded else _chunk(N)

    def body(x_ref, g_ref, w_ref, *rest):
        h_ref, z_ref = rest[-2:]
        xv = x_ref[...]
        r = lax.rsqrt(jnp.mean(xv * xv, axis=-1, keepdims=True) + EPS)
        h = (xv * r * g_ref[...]).astype(BF16)
        h_ref[...] = h
        for c0 in range(0, N, CH):
            wc = w_ref[c0 // CH] if sharded else w_ref[:, c0:c0 + CH]
            z_ref[:, c0:c0 + CH] = jnp.dot(h, wc, preferred_element_type=F32).astype(z_ref.dtype)

    return pl.pallas_call(
        body, name=name, grid=(T // TM,),
        in_specs=[pl.BlockSpec((TM, D), lambda i: (i, 0)),
                  pl.BlockSpec((1, D), lambda i: (0, 0)),
                  _weight_spec(w_all, l)] + _after_spec(after),
        out_specs=[pl.BlockSpec((TM, D), lambda i: (i, 0)),
                   pl.BlockSpec((TM, N), lambda i: (i, 0))],
        out_shape=[jax.ShapeDtypeStruct((T, D), BF16), jax.ShapeDtypeStruct((T, N), out_dtype)],
        compiler_params=_params(("parallel",)),
    )(x, g, w_all, *_after_arg(after))


def _matmul_res(a, w_all, l, res, relu2, name):
    T, K = a.shape
    sharded = w_all.ndim == 4
    N = w_all.shape[-1]
    TM = _tile(T, 512)
    CH = w_all.shape[2] if sharded else _chunk(K, K_CHUNK)

    def body(a_ref, w_ref, res_ref, o_ref):
        acc = res_ref[...]
        for k0 in range(0, K, CH):
            av = a_ref[:, k0:k0 + CH]
            if relu2:
                av = jnp.square(jnp.maximum(av.astype(F32), 0.0)).astype(BF16)
            acc = acc + jnp.dot(av, w_ref[k0 // CH] if sharded else w_ref[k0:k0 + CH, :], preferred_element_type=F32)
        o_ref[...] = acc

    return pl.pallas_call(
        body, name=name, grid=(T // TM,),
        in_specs=[pl.BlockSpec((TM, K), lambda i: (i, 0)),
                  _weight_spec(w_all, l),
                  pl.BlockSpec((TM, N), lambda i: (i, 0))],
        out_specs=pl.BlockSpec((TM, N), lambda i: (i, 0)),
        out_shape=jax.ShapeDtypeStruct((T, N), F32),
        compiler_params=_params(("parallel",)),
    )(a, w_all, res)


def _head_norm(t, g):
    r = lax.rsqrt(jnp.mean(t * t, axis=-1, keepdims=True) + EPS)
    that = t * r
    return that * g, that, r


def _softmax_sink(s, sink):
    m = jnp.maximum(jnp.max(s, axis=-1, keepdims=True), sink)
    p = jnp.exp(s - m)
    es = jnp.exp(sink - m)
    den = jnp.sum(p, axis=-1, keepdims=True) + es
    return p / den, es / den


GROUP_ROWS = GQA_GROUP * BLOCK


def _own_block():
    row = lax.broadcasted_iota(jnp.int32, (GROUP_ROWS, BLOCK), 0)
    col = lax.broadcasted_iota(jnp.int32, (GROUP_ROWS, BLOCK), 1)
    return (row & (BLOCK - 1)) >= col


ATTN_QB = 4
KV_COLS = [slice(k * HEAD_DIM, (k + 1) * HEAD_DIM) for k in range(N_KV_HEADS)]


def _blk(i):
    return pl.ds(i * BLOCK, BLOCK)


def _stack_heads(ref, i, kvh):
    return jnp.concatenate([ref[_blk(i), (kvh * GQA_GROUP + g) * HEAD_DIM:(kvh * GQA_GROUP + g + 1) * HEAD_DIM]
                            for g in range(GQA_GROUP)], axis=0)


def _unstack_heads(ref, i, kvh, val):
    for g in range(GQA_GROUP):
        h = kvh * GQA_GROUP + g
        ref[_blk(i), h * HEAD_DIM:(h + 1) * HEAD_DIM] = val[g * BLOCK:(g + 1) * BLOCK]


def _band_probs(first, own, s_own, s_prev, bias, sink):
    s = jnp.where(own, s_own, s_prev) * SCALE + bias
    if first is not None:
        s = jnp.where(jnp.logical_or(own, jnp.logical_not(first)), s, NEG)
    return _softmax_sink(s, sink)


def _dot_nt(a, b):
    return lax.dot_general(a, b, NT, preferred_element_type=F32)


def _dot_tn(a, b):
    return lax.dot_general(a, b, TN, preferred_element_type=F32)


def _attn_fwd(z, biasc, sink_rows, qg, kg):
    T = z.shape[0]
    nb = T // BLOCK
    qb = min(ATTN_QB, nb)
    TQ = qb * BLOCK
    kb, vb = K0 // KV_WIDTH, V0 // KV_WIDTH
    groups = [(i, k) for i in range(qb) for k in range(N_KV_HEADS)]

    def body(q_ref, kc_ref, kp_ref, vc_ref, vp_ref, b_ref, sk_ref, qg_ref, kg_ref, a_ref):
        n = pl.program_id(0)
        own = _own_block()
        kn, vv = {}, {}
        for k in range(N_KV_HEADS):
            kn[-1, k] = _head_norm(kp_ref[:, KV_COLS[k]], kg_ref[...])[0].astype(BF16)
            vv[-1, k] = vp_ref[:, KV_COLS[k]].astype(BF16)
        for i, k in groups:
            kn[i, k] = _head_norm(kc_ref[_blk(i), KV_COLS[k]], kg_ref[...])[0].astype(BF16)
            vv[i, k] = vc_ref[_blk(i), KV_COLS[k]].astype(BF16)
        qnb = {g: _head_norm(_stack_heads(q_ref, *g), qg_ref[...])[0].astype(BF16) for g in groups}
        s_own = {(i, k): _dot_nt(qnb[i, k], kn[i, k]) for i, k in groups}
        s_prev = {(i, k): _dot_nt(qnb[i, k], kn[i - 1, k]) for i, k in groups}
        p = {(i, k): _band_probs(n == 0 if i == 0 else None, own, s_own[i, k], s_prev[i, k], b_ref[k], sk_ref[k])[0]
             for i, k in groups}
        p_own = {g: jnp.where(own, p[g], 0.0).astype(BF16) for g in groups}
        p_prev = {g: jnp.where(own, 0.0, p[g]).astype(BF16) for g in groups}
        o_own = {(i, k): jnp.dot(p_own[i, k], vv[i, k], preferred_element_type=F32) for i, k in groups}
        o_prev = {(i, k): jnp.dot(p_prev[i, k], vv[i - 1, k], preferred_element_type=F32) for i, k in groups}
        for i, k in groups:
            _unstack_heads(a_ref, i, k, o_own[i, k] + o_prev[i, k])

    prev = lambda n: jnp.maximum(n * qb - 1, 0)
    return pl.pallas_call(
        body, name="attn_fwd", grid=(nb // qb,),
        in_specs=[pl.BlockSpec((TQ, ATTN_WIDTH), lambda n: (n, 0)),
                  pl.BlockSpec((TQ, KV_WIDTH), lambda n: (n, kb)),
                  pl.BlockSpec((BLOCK, KV_WIDTH), lambda n: (prev(n), kb)),
                  pl.BlockSpec((TQ, KV_WIDTH), lambda n: (n, vb)),
                  pl.BlockSpec((BLOCK, KV_WIDTH), lambda n: (prev(n), vb)),
                  pl.BlockSpec((N_KV_HEADS, GROUP_ROWS, BLOCK), lambda n: (0, 0, 0)),
                  pl.BlockSpec((N_KV_HEADS, GROUP_ROWS, 1), lambda n: (0, 0, 0)),
                  pl.BlockSpec((1, HEAD_DIM), lambda n: (0, 0)),
                  pl.BlockSpec((1, HEAD_DIM), lambda n: (0, 0))],
        out_specs=pl.BlockSpec((TQ, ATTN_WIDTH), lambda n: (n, 0)),
        out_shape=jax.ShapeDtypeStruct((T, ATTN_WIDTH), F32),
        compiler_params=_params(("parallel",)),
    )(z, z, z, z, z, biasc, sink_rows, qg, kg)


SHIFTS = 8
CONV_RC = 64


def _shifted_copies(src_ref, dst_ref, total):
    for b in range(SHIFTS):
        rows = (total - b) // SHIFTS * SHIFTS
        for r0 in range(0, rows, CONV_RC):
            nr = min(CONV_RC, rows - r0)
            dst_ref[b, pl.ds(r0, nr), :] = src_ref[pl.ds(r0 + b, nr), :]


def _tap(ref, r0, o):
    return ref[o % SHIFTS, pl.ds(r0 + (o // SHIFTS) * SHIFTS, CONV_RC), :]


def _conv_fwd(z, cw, cb):
    T = z.shape[0]
    TC = _tile(T, 512)
    ub, gb = U0 // CONV_CH, G0 // CONV_CH
    hpt = TC // HALO
    lead = HALO - (CONV_KERNEL - 1)

    def body(u_ref, g_ref, up_ref, gp_ref, w_ref, b_ref, y_ref, hp_ref, hs_ref):
        i = pl.program_id(0)
        hp_ref[pl.ds(0, HALO), :] = jnp.where(i > 0, up_ref[...] * jax.nn.sigmoid(gp_ref[...]), 0.0)
        hp_ref[pl.ds(HALO, TC), :] = u_ref[...] * jax.nn.sigmoid(g_ref[...])
        _shifted_copies(hp_ref, hs_ref, TC + HALO)
        for r0 in range(0, TC, CONV_RC):
            acc = jnp.zeros((CONV_RC, CONV_CH), F32) + b_ref[...]
            for j in range(CONV_KERNEL):
                acc = acc + _tap(hs_ref, r0, lead + j) * w_ref[pl.ds(j, 1), :]
            y_ref[pl.ds(r0, CONV_RC), :] = acc

    prev = lambda i: jnp.maximum(i * hpt - 1, 0)
    return pl.pallas_call(
        body, name="conv_fwd", grid=(T // TC, CONV_WIDTH // CONV_CH),
        in_specs=[pl.BlockSpec((TC, CONV_CH), lambda i, j: (i, ub + j)),
                  pl.BlockSpec((TC, CONV_CH), lambda i, j: (i, gb + j)),
                  pl.BlockSpec((HALO, CONV_CH), lambda i, j: (prev(i), ub + j)),
                  pl.BlockSpec((HALO, CONV_CH), lambda i, j: (prev(i), gb + j)),
                  pl.BlockSpec((CONV_ROWS, CONV_CH), lambda i, j: (0, j)),
                  pl.BlockSpec((1, CONV_CH), lambda i, j: (0, j))],
        out_specs=pl.BlockSpec((TC, CONV_CH), lambda i, j: (i, j)),
        out_shape=jax.ShapeDtypeStruct((T, CONV_WIDTH), F32),
        scratch_shapes=[pltpu.VMEM((TC + HALO, CONV_CH), F32), pltpu.VMEM((SHIFTS, TC + HALO, CONV_CH), F32)],
        compiler_params=_params(("parallel", "parallel")),
    )(z, z, z, z, cw, cb)


def _ln_silu(y, ln_g, ln_b):
    mu = jnp.mean(y, axis=-1, keepdims=True)
    yc = y - mu
    var = jnp.mean(yc * yc, axis=-1, keepdims=True)
    rstd = lax.rsqrt(var + EPS)
    yhat = yc * rstd
    yn = yhat * ln_g + ln_b
    sg = jax.nn.sigmoid(yn)
    return yn * sg, yn, sg, yhat, rstd


def _mix_norm(a, y, ln_g, ln_b, ag, cg):
    T = a.shape[0]
    TM = _tile(T, 512)

    def body(a_ref, y_ref, lg_ref, lb_ref, ag_ref, cg_ref, o_ref):
        av = a_ref[...]
        ra = lax.rsqrt(jnp.mean(av * av, axis=-1, keepdims=True) + EPS)
        o_ref[:, :ATTN_WIDTH] = (av * ra * ag_ref[...]).astype(BF16)
        c, _, _, _, _ = _ln_silu(y_ref[...], lg_ref[...], lb_ref[...])
        rc = lax.rsqrt(jnp.mean(c * c, axis=-1, keepdims=True) + EPS)
        o_ref[:, ATTN_WIDTH:] = (c * rc * cg_ref[...]).astype(BF16)

    vec = pl.BlockSpec((1, CONV_WIDTH), lambda i: (0, 0))
    return pl.pallas_call(
        body, name="mix_norm", grid=(T // TM,),
        in_specs=[pl.BlockSpec((TM, ATTN_WIDTH), lambda i: (i, 0)),
                  pl.BlockSpec((TM, CONV_WIDTH), lambda i: (i, 0)), vec, vec, vec, vec],
        out_specs=pl.BlockSpec((TM, MIX_WIDTH), lambda i: (i, 0)),
        out_shape=jax.ShapeDtypeStruct((T, MIX_WIDTH), BF16),
        compiler_params=_params(("parallel",)),
    )(a, y, ln_g, ln_b, ag, cg)


def _loss_grad(y, tgt):
    T, D = y.shape
    TM = _tile(T, 512)
    nt = T // TM

    def body(y_ref, t_ref, part_ref, dy_ref, dyb_ref):
        diff = y_ref[...] - t_ref[...]
        dy_ref[...] = diff / D
        dyb_ref[...] = (diff / D).astype(BF16)
        tok = jnp.mean(diff * diff, axis=-1, keepdims=True)
        part_ref[...] = jnp.zeros((1, LANES), F32) + 0.5 * jnp.sum(tok)

    return pl.pallas_call(
        body, name="loss_grad", grid=(nt,),
        in_specs=[pl.BlockSpec((TM, D), lambda i: (i, 0)), pl.BlockSpec((TM, D), lambda i: (i, 0))],
        out_specs=[pl.BlockSpec((None, 1, LANES), lambda i: (i, 0, 0)), pl.BlockSpec((TM, D), lambda i: (i, 0)),
                   pl.BlockSpec((TM, D), lambda i: (i, 0))],
        out_shape=[jax.ShapeDtypeStruct((nt, 1, LANES), F32), jax.ShapeDtypeStruct((T, D), F32),
                   jax.ShapeDtypeStruct((T, D), BF16)],
        compiler_params=_params(("parallel",)),
    )(y, tgt)


def _dact(g, w_all, l, up, after=None):
    T, N = g.shape
    sharded = w_all.ndim == 4
    K = w_all.shape[2] * N_CHIPS if sharded else w_all.shape[1]
    TM = _tile(T, 512)
    CH = w_all.shape[2] if sharded else _chunk(K)

    def body(g_ref, w_ref, up_ref, *rest):
        o_ref = rest[-1]
        gv = g_ref[...].astype(BF16)
        for k0 in range(0, K, CH):
            da = lax.dot_general(gv, w_ref[k0 // CH] if sharded else w_ref[k0:k0 + CH, :], NT, preferred_element_type=F32)
            upv = up_ref[:, k0:k0 + CH].astype(F32)
            o_ref[:, k0:k0 + CH] = (da * (2.0 * jnp.maximum(upv, 0.0))).astype(BF16)

    return pl.pallas_call(
        body, name="mlp_dact", grid=(T // TM,),
        in_specs=[pl.BlockSpec((TM, N), lambda i: (i, 0)),
                  _weight_spec(w_all, l),
                  pl.BlockSpec((TM, K), lambda i: (i, 0))] + _after_spec(after),
        out_specs=pl.BlockSpec((TM, K), lambda i: (i, 0)),
        out_shape=jax.ShapeDtypeStruct((T, K), BF16),
        compiler_params=_params(("parallel",)),
    )(g, w_all, up, *_after_arg(after))


def _matmul_tn(a, b, relu2, buf, buf_shape, out_block, out_index, tm, tn, name):
    pieces = list(b) if isinstance(b, (list, tuple)) else [b]
    T, M = a.shape
    N = sum(p.shape[1] for p in pieces)
    assert len(pieces) == 1 or tn == N
    starts = np.cumsum([0] + [p.shape[1] for p in pieces])
    TK = _tile(T, 2048)
    nk = T // TK

    def body(*refs):
        a_ref, b_refs = refs[0], refs[1:1 + len(pieces)]
        o_ref = refs[-1]
        k = pl.program_id(2)
        av = a_ref[...]
        if relu2:
            av = jnp.square(jnp.maximum(av.astype(F32), 0.0)).astype(BF16)
        cs = [lax.dot_general(av, b_ref[...].astype(BF16), TN, preferred_element_type=F32) for b_ref in b_refs]

        def put(add):
            for p, c in enumerate(cs):
                if len(cs) == 1:
                    o_ref[...] = c.reshape(o_ref.shape) + (o_ref[...] if add else 0.0)
                else:
                    cols = slice(int(starts[p]), int(starts[p + 1]))
                    o_ref[:, cols] = c + (o_ref[:, cols] if add else 0.0)

        @pl.when(k == 0)
        def _():
            put(False)

        @pl.when(k > 0)
        def _():
            put(True)

    if len(pieces) == 1:
        b_specs = [pl.BlockSpec((TK, tn), lambda i, j, k: (k, j))]
    else:
        b_specs = [pl.BlockSpec((TK, p.shape[1]), lambda i, j, k: (k, 0)) for p in pieces]
    in_specs = [pl.BlockSpec((TK, tm), lambda i, j, k: (k, i))] + b_specs
    args = [a] + pieces
    aliases = {}
    if buf is not None:
        in_specs.append(ANY)
        args.append(buf)
        aliases = {len(args) - 1: 0}
    return pl.pallas_call(
        body, name=name, grid=(M // tm, N // tn, nk),
        in_specs=in_specs,
        out_specs=pl.BlockSpec(out_block, lambda i, j, k: out_index(i, j)),
        out_shape=jax.ShapeDtypeStruct(buf_shape, F32),
        input_output_aliases=aliases,
        compiler_params=_params(("parallel", "parallel", "arbitrary")),
    )(*args)


def _matmul_nt_normbwd(dz, w_all, l, x, gvec, gres, name, after=None):
    pieces = list(dz) if isinstance(dz, (list, tuple)) else [dz]
    T = pieces[0].shape[0]
    K = sum(p.shape[1] for p in pieces)
    starts = np.cumsum([0] + [p.shape[1] for p in pieces])
    D = x.shape[1]
    TM = _tile(T, 512)
    sharded = w_all.ndim == 4

    def body(*refs):
        dz_refs = refs[:len(pieces)]
        w_ref, x_ref, gv_ref, gr_ref = refs[len(pieces):len(pieces) + 4]
        o_ref, ob_ref, dg_ref = refs[-3:]
        i = pl.program_id(0)
        dh = jnp.zeros((TM, D), F32)
        for p, dz_ref in enumerate(dz_refs):
            width = dz_ref.shape[1]
            ch = w_all.shape[-1] if sharded else _chunk(width, K_CHUNK)
            for k0 in range(0, width, ch):
                wk = int(starts[p]) + k0
                wc = w_ref[wk // ch] if sharded else w_ref[:, wk:wk + ch]
                dh = dh + lax.dot_general(dz_ref[:, k0:k0 + ch], wc, NT, preferred_element_type=F32)
        xv = x_ref[...]
        r = lax.rsqrt(jnp.mean(xv * xv, axis=-1, keepdims=True) + EPS)
        xhat = xv * r
        dg = jnp.sum(dh * xhat, axis=0, keepdims=True)

        @pl.when(i == 0)
        def _():
            dg_ref[...] = dg

        @pl.when(i > 0)
        def _():
            dg_ref[...] += dg

        wv = dh * gv_ref[...]
        out = gr_ref[...] + r * (wv - xhat * jnp.mean(wv * xhat, axis=-1, keepdims=True))
        o_ref[...] = out
        ob_ref[...] = out.astype(BF16)

    return pl.pallas_call(
        body, name=name, grid=(T // TM,),
        in_specs=[pl.BlockSpec((TM, p.shape[1]), lambda i: (i, 0)) for p in pieces]
        + [_weight_spec(w_all, l),
           pl.BlockSpec((TM, D), lambda i: (i, 0)),
           pl.BlockSpec((1, D), lambda i: (0, 0)),
           pl.BlockSpec((TM, D), lambda i: (i, 0))] + _after_spec(after),
        out_specs=[pl.BlockSpec((TM, D), lambda i: (i, 0)), pl.BlockSpec((TM, D), lambda i: (i, 0)),
                   pl.BlockSpec((1, D), lambda i: (0, 0))],
        out_shape=[jax.ShapeDtypeStruct((T, D), F32), jax.ShapeDtypeStruct((T, D), BF16),
                   jax.ShapeDtypeStruct((1, D), F32)],
        compiler_params=_params(("arbitrary",)),
    )(*pieces, w_all, x, gvec, gres, *_after_arg(after))


def _mix_bwd(g1, w_all, l, a, y, ln_g, ln_b, ag, cg):
    T, D = g1.shape
    TM = _tile(T, 512)

    def body(g_ref, w_ref, a_ref, y_ref, lg_ref, lb_ref, ag_ref, cg_ref, da_ref, dy_ref, sm_ref):
        i = pl.program_id(0)
        gb = g_ref[...].astype(BF16)
        dm = [lax.dot_general(gb, w_ref[s], NT, preferred_element_type=F32) for s in range(N_CHIPS)]
        dma = jnp.concatenate(dm[:N_CHIPS // 2], axis=1)
        dmc = jnp.concatenate(dm[N_CHIPS // 2:], axis=1)
        av = a_ref[...]
        ra = lax.rsqrt(jnp.mean(av * av, axis=-1, keepdims=True) + EPS)
        ahat = av * ra
        d_ag = jnp.sum(dma * ahat, axis=0, keepdims=True)
        wa = dma * ag_ref[...]
        da_ref[...] = ra * (wa - ahat * jnp.mean(wa * ahat, axis=-1, keepdims=True))

        c, yn, sg, yhat, rstd = _ln_silu(y_ref[...], lg_ref[...], lb_ref[...])
        rc = lax.rsqrt(jnp.mean(c * c, axis=-1, keepdims=True) + EPS)
        chat = c * rc
        d_cg = jnp.sum(dmc * chat, axis=0, keepdims=True)
        wc = dmc * cg_ref[...]
        dc = rc * (wc - chat * jnp.mean(wc * chat, axis=-1, keepdims=True))
        dyn = dc * (sg * (1.0 + yn * (1.0 - sg)))
        d_lg = jnp.sum(dyn * yhat, axis=0, keepdims=True)
        d_lb = jnp.sum(dyn, axis=0, keepdims=True)
        dyh = dyn * lg_ref[...]
        dy = rstd * (dyh - jnp.mean(dyh, axis=-1, keepdims=True) - yhat * jnp.mean(dyh * yhat, axis=-1, keepdims=True))
        dy_ref[...] = dy
        d_cb = jnp.sum(dy, axis=0, keepdims=True)
        sums = jnp.concatenate([d_ag, d_cg, d_lg, d_lb, d_cb, jnp.zeros((3, CONV_WIDTH), F32)], axis=0)

        @pl.when(i == 0)
        def _():
            sm_ref[...] = sums

        @pl.when(i > 0)
        def _():
            sm_ref[...] += sums

    vec = pl.BlockSpec((1, CONV_WIDTH), lambda i: (0, 0))
    return pl.pallas_call(
        body, name="mix_bwd", grid=(T // TM,),
        in_specs=[pl.BlockSpec((TM, D), lambda i: (i, 0)),
                  _weight_spec(w_all, l),
                  pl.BlockSpec((TM, ATTN_WIDTH), lambda i: (i, 0)),
                  pl.BlockSpec((TM, CONV_WIDTH), lambda i: (i, 0)), vec, vec, vec, vec],
        out_specs=[pl.BlockSpec((TM, ATTN_WIDTH), lambda i: (i, 0)),
                   pl.BlockSpec((TM, CONV_WIDTH), lambda i: (i, 0)),
                   pl.BlockSpec((8, CONV_WIDTH), lambda i: (0, 0))],
        out_shape=[jax.ShapeDtypeStruct((T, ATTN_WIDTH), F32), jax.ShapeDtypeStruct((T, CONV_WIDTH), F32),
                   jax.ShapeDtypeStruct((8, CONV_WIDTH), F32)],
        compiler_params=_params(("arbitrary",)),
    )(g1, w_all, a, y, ln_g, ln_b, ag, cg)


def _conv_bwd(dy, z, cw):
    T = z.shape[0]
    TC = _tile(T, 512)
    nt = T // TC
    ub, gb = U0 // CONV_CH, G0 // CONV_CH
    nch = CONV_WIDTH // CONV_CH
    hpt = TC // HALO

    lead = HALO - (CONV_KERNEL - 1)

    def body(dy_ref, dyn_ref, u_ref, g_ref, up_ref, gp_ref, w_ref, du_ref, dg_ref, dw_ref,
             hp_ref, hs_ref, dyp_ref, dys_ref):
        i = pl.program_id(1)
        hp_ref[pl.ds(0, HALO), :] = jnp.where(i > 0, up_ref[...] * jax.nn.sigmoid(gp_ref[...]), 0.0)
        hp_ref[pl.ds(HALO, TC), :] = u_ref[...] * jax.nn.sigmoid(g_ref[...])
        _shifted_copies(hp_ref, hs_ref, TC + HALO)
        dyp_ref[pl.ds(0, TC), :] = dy_ref[...]
        dyp_ref[pl.ds(TC, HALO), :] = jnp.where(i < nt - 1, dyn_ref[...], 0.0)
        _shifted_copies(dyp_ref, dys_ref, TC + HALO)

        @pl.when(i == 0)
        def _():
            dw_ref[...] = jnp.zeros((CONV_ROWS, CONV_CH), F32)

        for r0 in range(0, TC, CONV_RC):
            rows = pl.ds(r0, CONV_RC)
            dh = jnp.zeros((CONV_RC, CONV_CH), F32)
            for j in range(CONV_KERNEL):
                dh = dh + _tap(dys_ref, r0, CONV_KERNEL - 1 - j) * w_ref[pl.ds(j, 1), :]
            uv = u_ref[rows, :]
            sg = jax.nn.sigmoid(g_ref[rows, :])
            du_ref[rows, :] = (dh * sg).astype(BF16)
            dg_ref[rows, :] = (dh * uv * sg * (1.0 - sg)).astype(BF16)
        for j in range(CONV_KERNEL):
            acc = jnp.zeros((SHIFTS, CONV_CH), F32)
            for r0 in range(0, TC, CONV_RC):
                prod = dy_ref[pl.ds(r0, CONV_RC), :] * _tap(hs_ref, r0, lead + j)
                acc = acc + jnp.sum(prod.reshape(CONV_RC // SHIFTS, SHIFTS, CONV_CH), axis=0)
            dw_ref[pl.ds(j, 1), :] += jnp.sum(acc, axis=0, keepdims=True)

    prev = lambda i: jnp.maximum(i * hpt - 1, 0)
    nxt = lambda i: jnp.minimum((i + 1) * hpt, T // HALO - 1)
    return pl.pallas_call(
        body, name="conv_bwd", grid=(nch, nt),
        in_specs=[pl.BlockSpec((TC, CONV_CH), lambda j, i: (i, j)),
                  pl.BlockSpec((HALO, CONV_CH), lambda j, i: (nxt(i), j)),
                  pl.BlockSpec((TC, CONV_CH), lambda j, i: (i, ub + j)),
                  pl.BlockSpec((TC, CONV_CH), lambda j, i: (i, gb + j)),
                  pl.BlockSpec((HALO, CONV_CH), lambda j, i: (prev(i), ub + j)),
                  pl.BlockSpec((HALO, CONV_CH), lambda j, i: (prev(i), gb + j)),
                  pl.BlockSpec((CONV_ROWS, CONV_CH), lambda j, i: (0, j))],
        out_specs=[pl.BlockSpec((TC, CONV_CH), lambda j, i: (i, j)),
                   pl.BlockSpec((TC, CONV_CH), lambda j, i: (i, j)),
                   pl.BlockSpec((CONV_ROWS, CONV_CH), lambda j, i: (0, j))],
        out_shape=[jax.ShapeDtypeStruct((T, CONV_WIDTH), BF16), jax.ShapeDtypeStruct((T, CONV_WIDTH), BF16),
                   jax.ShapeDtypeStruct((CONV_ROWS, CONV_WIDTH), F32)],
        scratch_shapes=[pltpu.VMEM((TC + HALO, CONV_CH), F32), pltpu.VMEM((SHIFTS, TC + HALO, CONV_CH), F32),
                        pltpu.VMEM((TC + HALO, CONV_CH), F32), pltpu.VMEM((SHIFTS, TC + HALO, CONV_CH), F32)],
        compiler_params=_params(("parallel", "arbitrary")),
    )(dy, dy, z, z, z, z, cw)


def _norm_bwd(d, that, r, g):
    w = d * g
    return r * (w - that * jnp.mean(w * that, axis=-1, keepdims=True)), jnp.sum(d * that, axis=0, keepdims=True)


def _attn_bwd(z, da, biasc, sink_rows, qg, kg):
    T = z.shape[0]
    nb = T // BLOCK
    qb = min(ATTN_QB, nb)
    TQ = qb * BLOCK
    ns = nb // qb
    kb, vb = K0 // KV_WIDTH, V0 // KV_WIDTH
    groups = [(i, k) for i in range(qb) for k in range(N_KV_HEADS)]

    def body(q_ref, kc_ref, kp_ref, vc_ref, vp_ref, da_ref, b_ref, sk_ref, qg_ref, kg_ref,
             dq_ref, dkv_ref, db_ref, sm_ref, ck_ref, cv_ref, pk_ref, pv_ref, nk_ref, nv_ref):
        n = pl.program_id(0)
        lane = lax.broadcasted_iota(jnp.int32, (1, LANES), 1)

        @pl.when(n == 0)
        def _():
            db_ref[...] = jnp.zeros(db_ref.shape, F32)
            sm_ref[...] = jnp.zeros(sm_ref.shape, F32)
            ck_ref[...] = jnp.zeros(ck_ref.shape, F32)
            cv_ref[...] = jnp.zeros(cv_ref.shape, F32)

        pk_ref[...] = jnp.zeros(pk_ref.shape, F32)
        pv_ref[...] = jnp.zeros(pv_ref.shape, F32)

        @pl.when(n < ns)
        def _():
            own = _own_block()
            knorm, kn, vv = {}, {}, {}
            for k in range(N_KV_HEADS):
                kn[-1, k] = _head_norm(kp_ref[:, KV_COLS[k]], kg_ref[...])[0].astype(BF16)
                vv[-1, k] = vp_ref[:, KV_COLS[k]].astype(BF16)
            for i, k in groups:
                knorm[i, k] = _head_norm(kc_ref[_blk(i), KV_COLS[k]], kg_ref[...])
                kn[i, k] = knorm[i, k][0].astype(BF16)
                vv[i, k] = vc_ref[_blk(i), KV_COLS[k]].astype(BF16)
            qnorm = {g: _head_norm(_stack_heads(q_ref, *g), qg_ref[...]) for g in groups}
            qnb = {g: qnorm[g][0].astype(BF16) for g in groups}
            dob = {g: _stack_heads(da_ref, *g).astype(BF16) for g in groups}
            s_own = {(i, k): _dot_nt(qnb[i, k], kn[i, k]) for i, k in groups}
            s_prev = {(i, k): _dot_nt(qnb[i, k], kn[i - 1, k]) for i, k in groups}
            dp_own = {(i, k): _dot_nt(dob[i, k], vv[i, k]) for i, k in groups}
            dp_prev = {(i, k): _dot_nt(dob[i, k], vv[i - 1, k]) for i, k in groups}
            probs = {(i, k): _band_probs(n == 0 if i == 0 else None, own, s_own[i, k], s_prev[i, k], b_ref[k], sk_ref[k])
                     for i, k in groups}
            ds_own, ds_prev, p_own, p_prev = {}, {}, {}, {}
            dsk = jnp.zeros((1, LANES), F32)
            dbias = [jnp.zeros((GROUP_ROWS, BLOCK), F32) for _ in range(N_KV_HEADS)]
            for i, k in groups:
                p, psink = probs[i, k]
                dp = jnp.where(own, dp_own[i, k], dp_prev[i, k])
                delta = jnp.sum(p * dp, axis=-1, keepdims=True)
                ds = p * (dp - delta)
                dbias[k] = dbias[k] + ds
                dsink = psink * delta
                for g in range(GQA_GROUP):
                    dsk = dsk + jnp.where(lane == k * GQA_GROUP + g, -jnp.sum(dsink[g * BLOCK:(g + 1) * BLOCK]), 0.0)
                ds_own[i, k] = jnp.where(own, ds, 0.0).astype(BF16)
                ds_prev[i, k] = jnp.where(own, 0.0, ds).astype(BF16)
                p_own[i, k] = jnp.where(own, p, 0.0).astype(BF16)
                p_prev[i, k] = jnp.where(own, 0.0, p).astype(BF16)
            for k in range(N_KV_HEADS):
                db_ref[k] += dbias[k]
            dqn_own = {(i, k): jnp.dot(ds_own[i, k], kn[i, k], preferred_element_type=F32) for i, k in groups}
            dqn_prev = {(i, k): jnp.dot(ds_prev[i, k], kn[i - 1, k], preferred_element_type=F32) for i, k in groups}
            dk_own = {g: _dot_tn(ds_own[g], qnb[g]) * SCALE for g in groups}
            dk_prev = {g: _dot_tn(ds_prev[g], qnb[g]) * SCALE for g in groups}
            dv_own = {g: _dot_tn(p_own[g], dob[g]) for g in groups}
            dv_prev = {g: _dot_tn(p_prev[g], dob[g]) for g in groups}
            dqg = jnp.zeros((1, HEAD_DIM), F32)
            dkg = jnp.zeros((1, HEAD_DIM), F32)
            for i, k in groups:
                _, qhat, rq = qnorm[i, k]
                dq, dg = _norm_bwd((dqn_own[i, k] + dqn_prev[i, k]) * SCALE, qhat, rq, qg_ref[...])
                dqg = dqg + dg
                _unstack_heads(dq_ref, i, k, dq.astype(BF16))
                if i == 0:
                    pk_ref[:, KV_COLS[k]] = dk_prev[i, k]
                    pv_ref[:, KV_COLS[k]] = dv_prev[i, k]
                if i == qb - 1:
                    nk_ref[:, KV_COLS[k]] = dk_own[i, k]
                    nv_ref[:, KV_COLS[k]] = dv_own[i, k]
                else:
                    _, khat, rk = knorm[i, k]
                    dk, dg = _norm_bwd(dk_own[i, k] + dk_prev[i + 1, k], khat, rk, kg_ref[...])
                    dkg = dkg + dg
                    dkv_ref[_blk(i + 1), KV_COLS[k]] = dk.astype(BF16)
                    dkv_ref[_blk(i + 1), pl.ds(KV_WIDTH + k * HEAD_DIM, HEAD_DIM)] = (dv_own[i, k] + dv_prev[i + 1, k]).astype(BF16)
            sm_ref[pl.ds(0, 1), pl.ds(0, HEAD_DIM)] += dqg
            sm_ref[pl.ds(1, 1), pl.ds(0, HEAD_DIM)] += dkg
            sm_ref[pl.ds(2, 1), :] += dsk

        @pl.when(n >= 1)
        def _():
            dkg = jnp.zeros((1, HEAD_DIM), F32)
            for k in range(N_KV_HEADS):
                _, khat, rk = _head_norm(kp_ref[:, KV_COLS[k]], kg_ref[...])
                dk, dg = _norm_bwd(ck_ref[:, KV_COLS[k]] + pk_ref[:, KV_COLS[k]], khat, rk, kg_ref[...])
                dkg = dkg + dg
                dkv_ref[_blk(0), KV_COLS[k]] = dk.astype(BF16)
            dkv_ref[_blk(0), pl.ds(KV_WIDTH, KV_WIDTH)] = (cv_ref[...] + pv_ref[...]).astype(BF16)
            sm_ref[pl.ds(1, 1), pl.ds(0, HEAD_DIM)] += dkg

        ck_ref[...] = nk_ref[...]
        cv_ref[...] = nv_ref[...]

    cur = lambda n: jnp.minimum(n, ns - 1)
    prev = lambda n: jnp.maximum(n * qb - 1, 0)
    carry = pltpu.VMEM((BLOCK, KV_WIDTH), F32)
    return pl.pallas_call(
        body, name="attn_bwd", grid=(ns + 1,),
        in_specs=[pl.BlockSpec((TQ, ATTN_WIDTH), lambda n: (cur(n), 0)),
                  pl.BlockSpec((TQ, KV_WIDTH), lambda n: (cur(n), kb)),
                  pl.BlockSpec((BLOCK, KV_WIDTH), lambda n: (prev(n), kb)),
                  pl.BlockSpec((TQ, KV_WIDTH), lambda n: (cur(n), vb)),
                  pl.BlockSpec((BLOCK, KV_WIDTH), lambda n: (prev(n), vb)),
                  pl.BlockSpec((TQ, ATTN_WIDTH), lambda n: (cur(n), 0)),
                  pl.BlockSpec((N_KV_HEADS, GROUP_ROWS, BLOCK), lambda n: (0, 0, 0)),
                  pl.BlockSpec((N_KV_HEADS, GROUP_ROWS, 1), lambda n: (0, 0, 0)),
                  pl.BlockSpec((1, HEAD_DIM), lambda n: (0, 0)),
                  pl.BlockSpec((1, HEAD_DIM), lambda n: (0, 0))],
        out_specs=[pl.BlockSpec((TQ, ATTN_WIDTH), lambda n: (cur(n), 0)),
                   pl.BlockSpec((TQ, 2 * KV_WIDTH), lambda n: (n, 0)),
                   pl.BlockSpec((N_KV_HEADS, GROUP_ROWS, BLOCK), lambda n: (0, 0, 0)),
                   pl.BlockSpec((8, LANES), lambda n: (0, 0))],
        out_shape=[jax.ShapeDtypeStruct((T, ATTN_WIDTH), BF16), jax.ShapeDtypeStruct(((ns + 1) * TQ, 2 * KV_WIDTH), BF16),
                   jax.ShapeDtypeStruct((N_KV_HEADS, GROUP_ROWS, BLOCK), F32), jax.ShapeDtypeStruct((8, LANES), F32)],
        scratch_shapes=[carry] * 6,
        compiler_params=_params(("arbitrary",)),
    )(z, z, z, z, z, da, biasc, sink_rows, qg, kg)


def _bucket_reduce(dbias, onehot_t):
    def body(d_ref, oh_ref, o_ref):
        d = d_ref[...]
        hi = d.astype(BF16)
        r1 = d - hi.astype(F32)
        mid = r1.astype(BF16)
        lo = (r1 - mid.astype(F32)).astype(BF16)
        oh = oh_ref[...]
        acc = lax.dot_general(lo, oh, NT, preferred_element_type=F32)
        acc = acc + lax.dot_general(mid, oh, NT, preferred_element_type=F32)
        o_ref[...] = acc + lax.dot_general(hi, oh, NT, preferred_element_type=F32)

    return pl.pallas_call(
        body, name="bucket_reduce",
        out_shape=jax.ShapeDtypeStruct((N_HEADS, LANES), F32),
        compiler_params=_params(),
    )(dbias, onehot_t)


def _adamw(w, g, m, v, name):
    R, C = w.shape
    TR = _tile(R, 512)

    def body(w_ref, g_ref, m_ref, v_ref, d_ref, nm_ref, nv_ref):
        gv = g_ref[...]
        mn = ADAM_B1 * m_ref[...] + (1.0 - ADAM_B1) * gv
        vn = ADAM_B2 * v_ref[...] + (1.0 - ADAM_B2) * jnp.square(gv)
        m_hat = mn / (1.0 - ADAM_B1 ** ADAM_STEP)
        v_hat = vn / (1.0 - ADAM_B2 ** ADAM_STEP)
        d_ref[...] = -ADAM_LR * (m_hat / (jnp.sqrt(v_hat) + ADAM_EPS) + ADAM_WD * w_ref[...])
        nm_ref[...] = mn
        nv_ref[...] = vn

    spec = pl.BlockSpec((TR, C), lambda i: (i, 0))
    shp = jax.ShapeDtypeStruct((R, C), F32)
    return pl.pallas_call(
        body, name=name, grid=(R // TR,),
        in_specs=[spec] * 4, out_specs=[spec] * 3, out_shape=[shp] * 3,
        compiler_params=_params(("parallel",)),
    )(w, g, m, v)


def _place():
    return lax.axis_index("x"), lax.axis_index("y"), lax.axis_index("c")


def _other_chips(x, y):
    return [(1 - x, y), (x, 1 - y), (1 - x, 1 - y)]


def _remote(src, dst, send_sem, recv_sem, dev):
    return pltpu.make_async_remote_copy(src_ref=src, dst_ref=dst, send_sem=send_sem, recv_sem=recv_sem,
                                        device_id=dev, device_id_type=MESH)


def _gather_shards(bufs):
    nbuf = len(bufs)

    def body(*refs):
        ins, outs = refs[:nbuf], refs[nbuf:2 * nbuf]
        send_sems, recv_sems = refs[2 * nbuf:]
        x, y, c = _place()
        me = 2 * x + y
        sib = (x, y, 1 - c)
        chips = _other_chips(x, y)
        started = []
        for b in range(nbuf):
            hh = bufs[b].shape[0] // 2
            for j, (cx, cy) in enumerate(chips):
                k = 6 * b + j
                cp = _remote(ins[b].at[pl.ds(c * hh, hh), :], outs[b].at[me, pl.ds(c * hh, hh), :],
                             send_sems.at[k], recv_sems.at[k], (cx, cy, c))
                cp.start()
                started.append(cp)
        for b in range(nbuf):
            hh = bufs[b].shape[0] // 2
            for j, (cx, cy) in enumerate(chips):
                rows = outs[b].at[2 * cx + cy, pl.ds(c * hh, hh), :]
                _remote(rows, rows, send_sems.at[6 * b + j], recv_sems.at[6 * b + j], sib).wait_recv()
                k = 6 * b + 3 + j
                cp = _remote(rows, rows, send_sems.at[k], recv_sems.at[k], sib)
                cp.start()
                started.append(cp)
        for b in range(nbuf):
            hh = bufs[b].shape[0] // 2
            for j, (cx, cy) in enumerate(chips):
                rows = outs[b].at[2 * cx + cy, pl.ds((1 - c) * hh, hh), :]
                k = 6 * b + 3 + j
                _remote(rows, rows, send_sems.at[k], recv_sems.at[k], sib).wait_recv()
        for cp in started:
            cp.wait_send()

    return pl.pallas_call(
        body, name="gather_weights",
        in_specs=[ANY] * nbuf, out_specs=[ANY] * nbuf,
        out_shape=[jax.ShapeDtypeStruct((N_CHIPS,) + b.shape, b.dtype) for b in bufs],
        scratch_shapes=[pltpu.SemaphoreType.DMA((6 * nbuf,)), pltpu.SemaphoreType.DMA((6 * nbuf,))],
        compiler_params=pltpu.CompilerParams(has_side_effects=True),
    )(*bufs)


HBM = pl.BlockSpec(memory_space=pltpu.HBM)
SEM = pl.BlockSpec(memory_space=pltpu.SEMAPHORE)
DATAFLOW = pltpu.SideEffectType.DATAFLOW_SIDE_EFFECTING


def _gather_plan(shapes):
    def plan(srcs, lands):
        x, y, c = _place()
        out = []
        for b, shp in enumerate(shapes):
            hh = shp[0] // 2
            for cx, cy in _other_chips(x, y):
                out.append((srcs[b].at[pl.ds(c * hh, hh), :], lands[b].at[2 * x + y, pl.ds(c * hh, hh), :], (cx, cy, c)))
        return out
    return plan


def _exchange_plan(shapes):
    def plan(srcs, lands):
        x, y, c = _place()
        return [(srcs[b].at[j], lands[b].at[j], (cx, cy, c))
                for b in range(len(shapes)) for j, (cx, cy) in enumerate(_other_chips(x, y))]
    return plan


def _swap_plan(shapes):
    def plan(srcs, lands):
        x, y, c = _place()
        return [(srcs[b].at[:, pl.ds((1 - c) * (shp[1] // 2), shp[1] // 2), :], lands[b], (x, y, 1 - c))
                for b, shp in enumerate(shapes)]
    return plan


def _start_copies(name, srcs, land_shapes, plan, per_buffer=N_CHIPS - 1):
    n = len(srcs)
    ncopy = per_buffer * n

    def body(*refs):
        ins, lands = refs[:n], refs[n:2 * n]
        send_sems, recv_sems, token = refs[2 * n], refs[2 * n + 1], refs[-1]
        for k, (src, dst, dev) in enumerate(plan(ins, lands)):
            _remote(src, dst, send_sems.at[k], recv_sems.at[k], dev).start()
        token[...] = jnp.zeros_like(token)

    hbm = lambda a: pltpu.with_memory_space_constraint(a, pltpu.HBM)
    lands = [lax.empty(s, a.dtype) for s, a in zip(land_shapes, srcs)]
    outs = pl.pallas_call(
        body, name=name,
        out_shape=(pltpu.SemaphoreType.DMA((ncopy,)), pltpu.SemaphoreType.DMA((ncopy,)),
                   *[pltpu.HBM(a.shape, a.dtype) for a in srcs], *[pltpu.HBM(a.shape, a.dtype) for a in lands],
                   jax.ShapeDtypeStruct((8, LANES), F32)),
        in_specs=[HBM] * (2 * n),
        out_specs=(SEM, SEM, *([HBM] * (2 * n)), pl.BlockSpec(memory_space=pltpu.VMEM)),
        input_output_aliases={i: 2 + i for i in range(2 * n)},
        compiler_params=pltpu.CompilerParams(has_side_effects=DATAFLOW),
    )(*[hbm(a) for a in srcs], *[hbm(a) for a in lands])
    return outs[0], outs[1], list(outs[2:2 + n]), list(outs[2 + n:2 + 2 * n]), outs[-1]


def _wait_copies(name, send_sems, recv_sems, srcs, lands, plan, after):
    n = len(srcs)

    def body(*refs):
        ins, lnds = refs[:n], refs[n:2 * n]
        ssem, rsem = refs[2 * n], refs[2 * n + 1]
        for k, (src, dst, dev) in enumerate(plan(ins, lnds)):
            cp = _remote(src, dst, ssem.at[k], rsem.at[k], dev)
            cp.wait_send()
            cp.wait_recv()

    outs = pl.pallas_call(
        body, name=name,
        out_shape=(*[pltpu.HBM(a.shape, a.dtype) for a in srcs], *[pltpu.HBM(a.shape, a.dtype) for a in lands]),
        in_specs=[HBM] * (2 * n) + [SEM, SEM, ANY],
        out_specs=tuple([HBM] * (2 * n)),
        input_output_aliases={i: i for i in range(2 * n)},
        compiler_params=pltpu.CompilerParams(has_side_effects=DATAFLOW),
    )(*srcs, *lands, send_sems, recv_sems, after)
    return list(outs[:n]), list(outs[n:])


def _forward_halves(bufs):
    nbuf = len(bufs)

    def body(*refs):
        outs = refs[nbuf:2 * nbuf]
        send_sems, recv_sems = refs[2 * nbuf:]
        x, y, c = _place()
        sib = (x, y, 1 - c)
        cps = []
        for b in range(nbuf):
            hh = bufs[b].shape[1] // 2
            for j, (cx, cy) in enumerate(_other_chips(x, y)):
                rows = outs[b].at[2 * cx + cy, pl.ds(c * hh, hh), :]
                cp = _remote(rows, rows, send_sems.at[3 * b + j], recv_sems.at[3 * b + j], sib)
                cp.start()
                cps.append(cp)
        for b in range(nbuf):
            hh = bufs[b].shape[1] // 2
            for j, (cx, cy) in enumerate(_other_chips(x, y)):
                rows = outs[b].at[2 * cx + cy, pl.ds((1 - c) * hh, hh), :]
                _remote(rows, rows, send_sems.at[3 * b + j], recv_sems.at[3 * b + j], sib).wait_recv()
        for cp in cps:
            cp.wait_send()

    return pl.pallas_call(
        body, name="gather_forward_halves",
        in_specs=[ANY] * nbuf, out_specs=[ANY] * nbuf,
        out_shape=[jax.ShapeDtypeStruct(b.shape, b.dtype) for b in bufs],
        input_output_aliases={b: b for b in range(nbuf)},
        scratch_shapes=[pltpu.SemaphoreType.DMA((3 * nbuf,)), pltpu.SemaphoreType.DMA((3 * nbuf,))],
        compiler_params=pltpu.CompilerParams(has_side_effects=True),
    )(*bufs)


def _swap_halves(bufs):
    nbuf = len(bufs)

    def body(*refs):
        ins, outs = refs[:nbuf], refs[nbuf:2 * nbuf]
        send_sems, recv_sems = refs[2 * nbuf:]
        x, y, c = _place()
        sib = (x, y, 1 - c)
        cps = []
        for b in range(nbuf):
            hh = bufs[b].shape[1] // 2
            cp = _remote(ins[b].at[:, pl.ds((1 - c) * hh, hh), :], outs[b], send_sems.at[b], recv_sems.at[b], sib)
            cp.start()
            cps.append(cp)
        for cp in cps:
            cp.wait()

    return pl.pallas_call(
        body, name="grad_swap_halves",
        in_specs=[ANY] * nbuf, out_specs=[ANY] * nbuf,
        out_shape=[jax.ShapeDtypeStruct((N_CHIPS, b.shape[1] // 2, b.shape[2]), b.dtype) for b in bufs],
        scratch_shapes=[pltpu.SemaphoreType.DMA((nbuf,)), pltpu.SemaphoreType.DMA((nbuf,))],
        compiler_params=pltpu.CompilerParams(has_side_effects=True),
    )(*bufs)


def _chip_sum(g, got, sel, out_dtype, name):
    _, R, C = g.shape
    hh = R // 2
    TR = _tile(hh, 512)
    nslot = sel[1].shape[0]

    def body(off_ref, sh_ref, g_ref, r_ref, o_ref):
        o_ref[...] = (g_ref[...] + r_ref[...]).astype(out_dtype)

    return pl.pallas_call(
        body, name=name,
        grid_spec=pltpu.PrefetchScalarGridSpec(
            num_scalar_prefetch=2, grid=(nslot, hh // TR),
            in_specs=[pl.BlockSpec((None, TR, C), lambda s, i, off, sh: (sh[s], off[0] + i, 0)),
                      pl.BlockSpec((None, TR, C), lambda s, i, off, sh: (sh[s], i, 0))],
            out_specs=pl.BlockSpec((None, TR, C), lambda s, i, off, sh: (s, i, 0))),
        out_shape=jax.ShapeDtypeStruct((nslot, hh, C), out_dtype),
        compiler_params=_params(("parallel", "parallel")),
    )(sel[0], sel[1], g, got)


def _exchange_chips(bufs):
    nbuf = len(bufs)

    def body(*refs):
        ins, outs = refs[:nbuf], refs[nbuf:2 * nbuf]
        send_sems, recv_sems = refs[2 * nbuf:]
        x, y, c = _place()
        cps = []
        for b in range(nbuf):
            for j, (cx, cy) in enumerate(_other_chips(x, y)):
                k = 3 * b + j
                cp = _remote(ins[b].at[j], outs[b].at[j], send_sems.at[k], recv_sems.at[k], (cx, cy, c))
                cp.start()
                cps.append(cp)
        for cp in cps:
            cp.wait()

    return pl.pallas_call(
        body, name="grad_exchange_chips",
        in_specs=[ANY] * nbuf, out_specs=[ANY] * nbuf,
        out_shape=[jax.ShapeDtypeStruct(b.shape, b.dtype) for b in bufs],
        scratch_shapes=[pltpu.SemaphoreType.DMA((3 * nbuf,)), pltpu.SemaphoreType.DMA((3 * nbuf,))],
        compiler_params=pltpu.CompilerParams(has_side_effects=True),
    )(*bufs)


def _shard_sum(own, got, off, tr, full, rows, name):
    _, hh, C = own.shape

    def body(off_ref, o_ref, r_ref, *rest):
        acc = o_ref[...]
        for j in range(N_CHIPS - 1):
            acc = acc + r_ref[j].astype(F32)
        rest[-1][...] = acc

    in_specs = [pl.BlockSpec((None, tr, C), lambda i, off: (0, i, 0)),
                pl.BlockSpec((N_CHIPS - 1, tr, C), lambda i, off: (0, i, 0))]
    args = [off, own, got]
    aliases = {}
    if full is not None:
        in_specs.append(ANY)
        args.append(full)
        aliases = {3: 0}
    return pl.pallas_call(
        body, name=name,
        grid_spec=pltpu.PrefetchScalarGridSpec(
            num_scalar_prefetch=1, grid=(hh // tr,), in_specs=in_specs,
            out_specs=pl.BlockSpec((tr, C), lambda i, off: (off[0] + i, 0))),
        out_shape=jax.ShapeDtypeStruct((rows, C), F32),
        input_output_aliases=aliases,
        compiler_params=_params(("parallel",)),
    )(*args)


def _join_halves(bufs, spans):
    nbuf = len(bufs)
    ncopy = nbuf * len(spans)

    def body(*refs):
        outs = refs[nbuf:2 * nbuf]
        send_sems, recv_sems = refs[2 * nbuf:]
        x, y, c = _place()
        sib = (x, y, 1 - c)
        cps = []
        for b in range(nbuf):
            for s, (r0, nr) in enumerate(spans[b]):
                k = b * len(spans[b]) + s
                rows = outs[b].at[pl.ds(r0 + c * (nr // 2), nr // 2), :]
                cp = _remote(rows, rows, send_sems.at[k], recv_sems.at[k], sib)
                cp.start()
                cps.append(cp)
        for b in range(nbuf):
            for s, (r0, nr) in enumerate(spans[b]):
                k = b * len(spans[b]) + s
                theirs = outs[b].at[pl.ds(r0 + (1 - c) * (nr // 2), nr // 2), :]
                _remote(theirs, theirs, send_sems.at[k], recv_sems.at[k], sib).wait_recv()
        for cp in cps:
            cp.wait_send()

    return pl.pallas_call(
        body, name="grad_join_halves",
        in_specs=[ANY] * nbuf, out_specs=[ANY] * nbuf,
        out_shape=[jax.ShapeDtypeStruct(b.shape, b.dtype) for b in bufs],
        input_output_aliases={b: b for b in range(nbuf)},
        scratch_shapes=[pltpu.SemaphoreType.DMA((ncopy,)), pltpu.SemaphoreType.DMA((ncopy,))],
        compiler_params=pltpu.CompilerParams(has_side_effects=True),
    )(*bufs)


def _sum_devices(part):
    R = part.shape[0]

    def body(p_ref, o_ref, all_ref, send_sems, recv_sems):
        x, y, c = _place()
        me = 4 * x + 2 * y + c
        all_ref[me] = p_ref[...]
        cps = []
        for k in range(1, N_DEV):
            px, py, pc = x ^ (k >> 2), y ^ ((k >> 1) & 1), c ^ (k & 1)
            cp = _remote(p_ref, all_ref.at[me], send_sems.at[k - 1], recv_sems.at[k - 1], (px, py, pc))
            cp.start()
            cps.append(cp)
        for k in range(1, N_DEV):
            peer = me ^ k
            _remote(p_ref, all_ref.at[peer], send_sems.at[k - 1], recv_sems.at[k - 1], (x, y, c)).wait_recv()
        for cp in cps:
            cp.wait_send()
        acc = all_ref[0]
        for d in range(1, N_DEV):
            acc = acc + all_ref[d]
        o_ref[...] = acc

    return pl.pallas_call(
        body, name="sum_small_grads",
        in_specs=[pl.BlockSpec(memory_space=pltpu.VMEM)],
        out_specs=pl.BlockSpec(memory_space=pltpu.VMEM),
        out_shape=jax.ShapeDtypeStruct((R, LANES), F32),
        scratch_shapes=[pltpu.VMEM((N_DEV, R, LANES), F32),
                        pltpu.SemaphoreType.DMA((N_DEV - 1,)), pltpu.SemaphoreType.DMA((N_DEV - 1,))],
        compiler_params=pltpu.CompilerParams(has_side_effects=True, vmem_limit_bytes=VMEM_LIMIT),
    )(part)


def _pack(parts):
    flat = jnp.concatenate([p.reshape(-1).astype(F32) for p in parts])
    n = flat.shape[0]
    rows = -(-n // LANES)
    rows = -(-rows // 8) * 8
    return jnp.pad(flat, (0, rows * LANES - n)).reshape(rows, LANES)


def _unpack(packed, shapes):
    flat = packed.reshape(-1)
    out, off = [], 0
    for s in shapes:
        n = int(np.prod(s))
        out.append(flat[off:off + n].reshape(s))
        off += n
    return out


def kernel(x, rel_bias, norm_mix_g, w_in, q_norm_g, k_norm_g, sinks, conv_w, conv_b, conv_ln_g, conv_ln_b, attn_out_g, conv_out_g, w_out, norm_mlp_g, w_mlp_up, w_mlp_down, loss_target, m_rel_bias, m_norm_mix_g, m_w_in, m_q_norm_g, m_k_norm_g, m_sinks, m_conv_w, m_conv_b, m_conv_ln_g, m_conv_ln_b, m_attn_out_g, m_conv_out_g, m_w_out, m_norm_mlp_g, m_w_mlp_up, m_w_mlp_down, v_rel_bias, v_norm_mix_g, v_w_in, v_q_norm_g, v_k_norm_g, v_sinks, v_conv_w, v_conv_b, v_conv_ln_g, v_conv_ln_b, v_attn_out_g, v_conv_out_g, v_w_out, v_norm_mlp_g, v_w_mlp_up, v_w_mlp_down):
    T = x.shape[1]
    L = DEPTH
    xi, yi, ci = _place()
    shard = 2 * xi + yi
    in_sh = IN_WIDTH // N_CHIPS
    out_sh = MIX_WIDTH // N_CHIPS
    ff_sh = D_FF // N_CHIPS
    cv_sh = CONV_WIDTH // N_CHIPS

    MIXING, MLP = ("w_in", "w_out", "conv_w"), ("w_mlp_up", "w_mlp_down")

    def my_shard(name, lo, hi):
        n = hi - lo
        if name == "w_in":
            return w_in[lo:hi].astype(BF16).reshape(n * D_MODEL, in_sh)
        if name == "w_out":
            return w_out[lo:hi].astype(BF16).reshape(n * out_sh, D_MODEL)
        if name == "w_mlp_up":
            return w_mlp_up[lo:hi].astype(BF16).reshape(n * D_MODEL, ff_sh)
        if name == "w_mlp_down":
            return w_mlp_down[lo:hi].astype(BF16).reshape(n * ff_sh, D_MODEL)
        cw_pad = jnp.pad(conv_w[lo:hi], ((0, 0), (0, CONV_ROWS - CONV_KERNEL), (0, 0)))
        return cw_pad.reshape(n * CONV_ROWS, cv_sh)

    def whole_weight(name, gathered, own, n):
        g = lax.dynamic_update_slice(gathered, own[None], (shard, 0, 0))
        if name == "w_in":
            return g.reshape(N_CHIPS, n, D_MODEL, in_sh).transpose(1, 2, 0, 3).reshape(n, D_MODEL, IN_WIDTH)
        if name in ("w_out", "w_mlp_up", "w_mlp_down"):
            return g.reshape(N_CHIPS, n, g.shape[1] // n, g.shape[2])
        return g.reshape(N_CHIPS, n, CONV_ROWS, cv_sh).transpose(1, 2, 0, 3).reshape(n, CONV_ROWS, CONV_WIDTH)

    weight_of = {}

    def provide(entries, gathered, mine):
        for (name, lo, hi), g, own in zip(entries, gathered, mine):
            whole = whole_weight(name, g, own, hi - lo)
            for l in range(lo, hi):
                weight_of[name, l] = (whole, l - lo)

    def gather_behind(tag, entries, first):
        mine = [my_shard(*e) for e in entries]
        mine[0], _ = lax.optimization_barrier((mine[0], first))
        plan = _gather_plan([m.shape for m in mine])
        send_sems, recv_sems, srcs, lands, token = _start_copies(
            "gather_" + tag + "_start", mine, [(N_CHIPS,) + m.shape for m in mine], plan)

        def finish(after):
            _, got = _wait_copies("gather_" + tag + "_wait", send_sems, recv_sems, srcs, lands, plan, after)
            provide(entries, _forward_halves(got), mine)
        return token, finish

    now = [(name, 0, 1) for name in MIXING]
    early = [(name, 0, 2) for name in MLP] + [(name, 1, 2) for name in MIXING]
    late = [(name, 2, L) for name in MIXING + MLP]
    mine0 = [my_shard(*e) for e in now]
    got0 = _gather_shards(mine0)
    provide(now, got0, mine0)
    token_early, finish_early = gather_behind("early", early, got0[0])

    bucket = _band_buckets()
    bk = jnp.asarray(bucket)[None]
    biasc = jnp.zeros((N_HEADS, BLOCK, BLOCK), F32)
    for b in range(NUM_BUCKETS):
        biasc = jnp.where(bk == b, rel_bias[b][:, None, None], biasc)
    biasc = biasc.reshape(N_KV_HEADS, GROUP_ROWS, BLOCK)
    onehot_t = np.zeros((LANES, BLOCK * BLOCK), np.float32)
    onehot_t[bucket.reshape(-1), np.arange(BLOCK * BLOCK)] = 1.0
    onehot_t = jnp.asarray(onehot_t, dtype=BF16)
    sink_rows = lambda l: jnp.repeat(sinks[l], BLOCK).reshape(N_KV_HEADS, GROUP_ROWS, 1)

    row = lambda a, l: a[l][None, :]

    xs = x.reshape(T, D_MODEL)
    saved = []
    token_late = None
    for l in range(L):
        if l == 2:
            finish_late(xs)
        h, z = _norm_matmul(xs, row(norm_mix_g, l), *weight_of["w_in", l], F32, "mix_in_proj", token_early if l == 0 else None)
        a = _attn_fwd(z, biasc, sink_rows(l), row(q_norm_g, l), row(k_norm_g, l))
        cw, cl = weight_of["conv_w", l]
        yc = _conv_fwd(z, cw[cl], row(conv_b, l))
        mix = _mix_norm(a, yc, row(conv_ln_g, l), row(conv_ln_b, l), row(attn_out_g, l), row(conv_out_g, l))
        x1 = _matmul_res(mix, *weight_of["w_out", l], xs, False, "mix_out_proj")
        if l == 0:
            finish_early(x1)
            token_late, finish_late = gather_behind("late", late, weight_of["w_mlp_up", 0][0])
        h2, up = _norm_matmul(x1, row(norm_mlp_g, l), *weight_of["w_mlp_up", l], BF16, "mlp_up_proj",
                              token_late if l == 0 else None)
        x2 = _matmul_res(up, *weight_of["w_mlp_down", l], x1, True, "mlp_down_proj")
        saved.append((xs, h, z, a, yc, mix, x1, h2, up))
        xs = x2

    loss_parts, g, g_bf = _loss_grad(xs, loss_target.reshape(T, D_MODEL))

    names = ["w_in", "w_out", "w_mlp_up", "w_mlp_down"]
    shard_rows = {"w_in": D_MODEL, "w_out": out_sh, "w_mlp_up": D_MODEL, "w_mlp_down": ff_sh}
    own_sel = shard.astype(jnp.int32)[None]
    send_sel = jnp.stack([shard ^ 2, shard ^ 1, shard ^ 3]).astype(jnp.int32)

    def by_shard(name, buf, n):
        if name == "w_in":
            return buf.reshape(n, D_MODEL, N_CHIPS, in_sh).transpose(2, 0, 1, 3).reshape(N_CHIPS, n * D_MODEL, in_sh)
        return buf.reshape(N_CHIPS, n * shard_rows[name], buf.shape[-1])

    def chip_sums(tag, group, n, swapped=None):
        order = list(group)
        G, got = swapped if swapped else (None, None)
        if not swapped:
            G = [by_shard(name, group[name], n) for name in order]
            got = _swap_halves(G)
        owns, sends = {}, {}
        for name, g_all, g_got in zip(order, G, got):
            hh = g_all.shape[1] // 2
            off = (ci * (hh // _tile(hh, 512))).astype(jnp.int32)[None]
            owns[name] = _chip_sum(g_all, g_got, (off, own_sel), F32, "chip_sum_own_" + name + tag)
            sends[name] = _chip_sum(g_all, g_got, (off, send_sel), BF16, "chip_sum_send_" + name + tag)
        return owns, sends

    def swap_behind(tag, group, n):
        G = [by_shard(name, group[name], n) for name in group]
        plan = _swap_plan([g_all.shape for g_all in G])
        send_sems, recv_sems, srcs, lands, token = _start_copies(
            "grad_swap" + tag + "_start", G, [(N_CHIPS, g_all.shape[1] // 2, g_all.shape[2]) for g_all in G], plan, 1)

        def finish(after):
            return _wait_copies("grad_swap" + tag + "_wait", send_sems, recv_sems, srcs, lands, plan, after)
        return token, finish

    def exchange_behind(tag, group, n, swapped=None):
        owns, sends = chip_sums(tag, group, n, swapped)
        order = list(sends)
        bufs = [sends[name] for name in order]
        plan = _exchange_plan([b.shape for b in bufs])
        send_sems, recv_sems, srcs, lands, token = _start_copies(
            "grad_exchange" + tag + "_start", bufs, [b.shape for b in bufs], plan)

        def finish(after):
            _, got = _wait_copies("grad_exchange" + tag + "_wait", send_sems, recv_sems, srcs, lands, plan, after)
            return {name: (owns[name], arrived) for name, arrived in zip(order, got)}
        return token, finish

    rest = dict.fromkeys(names)
    first = dict.fromkeys(names)
    small = [None] * L
    dbias_sum = None
    token = None
    for l in reversed(range(L)):
        x0, h, z, a, yc, mix, x1, h2, up = saved[l]
        stack, n, sl = (first, 1, 0) if l == 0 else (rest, L - 1, l - 1)
        if l == 0:
            token, finish_rest_swap = swap_behind("_rest", rest, L - 1)
        d_up = _dact(g_bf, *weight_of["w_mlp_down", l], up, token)
        stack["w_mlp_down"] = _matmul_tn(up, g_bf, True, stack["w_mlp_down"], (N_CHIPS, n, ff_sh, D_MODEL),
                                         (None, None, ff_sh, D_MODEL), lambda i, j: (i, sl, 0, 0), ff_sh, D_MODEL,
                                         "grad_w_mlp_down")
        stack["w_mlp_up"] = _matmul_tn(h2, d_up, False, stack["w_mlp_up"], (N_CHIPS, n, D_MODEL, ff_sh),
                                       (None, None, D_MODEL, ff_sh), lambda i, j: (j, sl, 0, 0), D_MODEL, ff_sh,
                                       "grad_w_mlp_up")
        if l == 0:
            token_rest, finish_rest_grads = exchange_behind("_rest", rest, L - 1, finish_rest_swap(stack["w_mlp_up"]))
            token, finish_mlp0_grads = exchange_behind("_mlp0", {k: first[k] for k in ("w_mlp_up", "w_mlp_down")}, 1)
            token = token + token_rest
        g1, g1_bf, d_gmlp = _matmul_nt_normbwd(d_up, *weight_of["w_mlp_up", l], x1, row(norm_mlp_g, l), g, "mlp_in_bwd",
                                               token if l == 0 else None)
        d_a, d_y, sm_mix = _mix_bwd(g1_bf, *weight_of["w_out", l], a, yc, row(conv_ln_g, l), row(conv_ln_b, l),
                                    row(attn_out_g, l), row(conv_out_g, l))
        stack["w_out"] = _matmul_tn(mix, g1_bf, False, stack["w_out"], (N_CHIPS, n, out_sh, D_MODEL),
                                    (N_CHIPS, None, out_sh, D_MODEL), lambda i, j: (0, sl, 0, 0), MIX_WIDTH, D_MODEL,
                                    "grad_w_out")
        cw, cl = weight_of["conv_w", l]
        d_u, d_gate, d_cw = _conv_bwd(d_y, z, cw[cl])
        d_q, d_kv, dbias, sm_attn = _attn_bwd(z, d_a, biasc, sink_rows(l), row(q_norm_g, l), row(k_norm_g, l))
        dbias_sum = dbias if dbias_sum is None else dbias_sum + dbias
        d_z = [d_q, d_kv[BLOCK:BLOCK + T], d_u, d_gate]
        stack["w_in"] = _matmul_tn(h, d_z, False, stack["w_in"], (n, D_MODEL, IN_WIDTH), (None, D_MODEL, IN_WIDTH),
                                   lambda i, j: (sl, 0, 0), D_MODEL, IN_WIDTH, "grad_w_in")
        g, g_bf, d_gmix = _matmul_nt_normbwd(d_z, *weight_of["w_in", l], x0, row(norm_mix_g, l), g1, "mix_in_bwd")
        small[l] = (d_gmix[0], sm_attn[0, :HEAD_DIM], sm_attn[1, :HEAD_DIM], sm_attn[2, :N_HEADS],
                    d_cw[:CONV_KERNEL], sm_mix[4], sm_mix[2], sm_mix[3], sm_mix[0], sm_mix[1], d_gmlp[0])
    grad_x = g.reshape(1, T, D_MODEL)

    d_rel = _bucket_reduce(dbias_sum.reshape(N_HEADS, BLOCK * BLOCK), onehot_t)[:, :NUM_BUCKETS].T
    stack = lambda k: jnp.stack([small[l][k] for l in range(L)])
    small_shapes = [(), (NUM_BUCKETS, N_HEADS), (L, D_MODEL), (L, HEAD_DIM), (L, HEAD_DIM), (L, N_HEADS),
                    (L, CONV_KERNEL, CONV_WIDTH), (L, CONV_WIDTH), (L, CONV_WIDTH), (L, CONV_WIDTH),
                    (L, CONV_WIDTH), (L, CONV_WIDTH), (L, D_MODEL)]
    part = _pack([jnp.sum(loss_parts[:, 0, 0]), d_rel] + [stack(k) for k in range(11)])
    tot = _unpack(_sum_devices(part), small_shapes)
    loss = tot[0]
    (g_rel, g_nmix, g_qn, g_kn, g_sk, g_cw_full, g_cb, g_lng, g_lnb, g_aog, g_cog, g_nmlp) = tot[1:]
    g_cw_sh = lax.dynamic_slice_in_dim(g_cw_full, shard * cv_sh, cv_sh, axis=2)

    small_w = [rel_bias, norm_mix_g, q_norm_g, k_norm_g, sinks, conv_w, conv_b, conv_ln_g, conv_ln_b,
               attn_out_g, conv_out_g, norm_mlp_g]
    small_m = [m_rel_bias, m_norm_mix_g, m_q_norm_g, m_k_norm_g, m_sinks, m_conv_w, m_conv_b, m_conv_ln_g,
               m_conv_ln_b, m_attn_out_g, m_conv_out_g, m_norm_mlp_g]
    small_v = [v_rel_bias, v_norm_mix_g, v_q_norm_g, v_k_norm_g, v_sinks, v_conv_w, v_conv_b, v_conv_ln_g,
               v_conv_ln_b, v_attn_out_g, v_conv_out_g, v_norm_mlp_g]
    small_g = [g_rel, g_nmix, g_qn, g_kn, g_sk, g_cw_sh, g_cb, g_lng, g_lnb, g_aog, g_cog, g_nmlp]
    shapes = [w.shape for w in small_w]
    sd, sm_, sv_ = _adamw(_pack(small_w), _pack(small_g), _pack(small_m), _pack(small_v), "adamw_small")
    small_d, small_nm, small_nv = _unpack(sd, shapes), _unpack(sm_, shapes), _unpack(sv_, shapes)

    owns_mix0, sends_mix0 = chip_sums("_mix0", {k: first[k] for k in ("w_in", "w_out")}, 1)
    arrived_mix0 = _exchange_chips([sends_mix0[k] for k in ("w_in", "w_out")])
    parts0 = {"w_in": (owns_mix0["w_in"], arrived_mix0[0]), "w_out": (owns_mix0["w_out"], arrived_mix0[1]),
              **finish_mlp0_grads(g)}
    parts1 = finish_rest_grads(g)
    grads, spans = [], []
    for name in names:
        R = shard_rows[name]
        full = None
        spans.append([(0, R), (R, (L - 1) * R)])
        for (r0, nr), (own, arrived), tag in zip(spans[-1], (parts0[name], parts1[name]), ("_first", "_rest")):
            tr = min(512, math.gcd(R, nr // 2))
            off = ((r0 + ci * (nr // 2)) // tr).astype(jnp.int32)[None]
            full = _shard_sum(own, arrived, off, tr, full, L * R, "shard_sum_" + name + tag)
        grads.append(full)
    grads = _join_halves(grads, spans)

    big_w = [w_in, w_out, w_mlp_up, w_mlp_down]
    big_m = [m_w_in, m_w_out, m_w_mlp_up, m_w_mlp_down]
    big_v = [v_w_in, v_w_out, v_w_mlp_up, v_w_mlp_down]
    big_g, big_d, big_nm, big_nv = [], [], [], []
    for b in range(4):
        shp = big_w[b].shape
        flat = lambda t: t.reshape(shp[0] * shp[1], shp[2])
        d, nm, nv = _adamw(flat(big_w[b]), grads[b], flat(big_m[b]), flat(big_v[b]), "adamw_" + names[b])
        big_g.append(grads[b].reshape(shp))
        big_d.append(d.reshape(shp))
        big_nm.append(nm.reshape(shp))
        big_nv.append(nv.reshape(shp))

    def ordered(sm, bg):
        return [sm[0], sm[1], bg[0], sm[2], sm[3], sm[4], sm[5], sm[6], sm[7], sm[8], sm[9], sm[10], bg[1], sm[11],
                bg[2], bg[3]]

    return (loss, grad_x, *ordered(small_g, big_g), *ordered(small_d, big_d), *ordered(small_nm, big_nm),
            *ordered(small_nv, big_nv))
```
